```python
import math
import jax, jax.numpy as jnp
from jax import lax
import numpy as np

D_MODEL = 1024
BATCH = 8
SEQ = 2048
DEPTH = 4

HEAD_DIM = 64
ATT_WIDTH = D_MODEL // 2
RWKV_WIDTH = D_MODEL - ATT_WIDTH
N_ATT_HEADS = ATT_WIDTH // HEAD_DIM
N_RWKV_HEADS = RWKV_WIDTH // HEAD_DIM
DILATED_GROUPS = ((128, 1), (512, 4), (2048, 16))
Q_BLOCK = 128
N_BUCKETS = 32
MAX_DISTANCE = 2048
NEG_INF = -1e30
D_DECAY_LORA = 64
D_AAA_LORA = 64
D_MV_LORA = 32
D_GATE_LORA = 160
RWKV_COLS = 3 * RWKV_WIDTH + D_DECAY_LORA + D_AAA_LORA + D_GATE_LORA
RWKV_SPLITS = (RWKV_WIDTH, 2 * RWKV_WIDTH, 3 * RWKV_WIDTH,
               3 * RWKV_WIDTH + D_DECAY_LORA, 3 * RWKV_WIDTH + D_DECAY_LORA + D_AAA_LORA)
N_IN = 3 * ATT_WIDTH + RWKV_COLS
GN_EPS = HEAD_DIM * 1e-5
RMS_EPS = 1e-6
D_FF_DENSE = 2816
N_EXPERTS = 8
TOP_K = 2
D_FF_EXPERT = 3584
MOE_BLOCK = 128
N_DENSE = (DEPTH + 1) // 2
N_MOE = DEPTH // 2

kernel_name = "hymba_dilated_rwkv7_moe_adaln"


def rms_norm(x, g):
    xf = x.astype(jnp.float32)
    y = xf * lax.rsqrt(jnp.mean(xf * xf, axis=-1, keepdims=True) + RMS_EPS)
    return (y * g.astype(jnp.float32)).astype(x.dtype)


def t5_bucket(dist):
    n = np.asarray(dist)
    max_exact = N_BUCKETS // 2
    large = max_exact + (np.log(np.maximum(n, 1) / max_exact) / np.log(MAX_DISTANCE / max_exact)
                         * (N_BUCKETS - max_exact)).astype(np.int32)
    large = np.minimum(large, N_BUCKETS - 1)
    return np.where(n < max_exact, n, large).astype(np.int32)


def swiglu(x, wg, wu, wd):
    return (jax.nn.silu(x @ wg) * (x @ wu)) @ wd


def dilated_attention(q, k, v, rel_bias):
    B, S, H, hd = q.shape
    n_blk = S // Q_BLOCK
    scale = 1.0 / math.sqrt(hd)
    offsets = [np.arange(0, w + 1, d, dtype=np.int32) for (w, d) in DILATED_GROUPS]
    biases = [rel_bias[t5_bucket(off)].T.astype(jnp.float32) for off in offsets]

    def block(i):
        t = i * Q_BLOCK + jnp.arange(Q_BLOCK, dtype=jnp.int32)
        qb = lax.dynamic_slice_in_dim(q, i * Q_BLOCK, Q_BLOCK, axis=1) * scale
        outs, lses = [], []
        for off, bias in zip(offsets, biases):
            idx = t[:, None] - off[None, :]
            valid = idx >= 0
            idxc = jnp.maximum(idx, 0)
            kg = jnp.take(k, idxc, axis=1)
            vg = jnp.take(v, idxc, axis=1)
            s = jnp.einsum('bqhd,bqnhd->bhqn', qb, kg).astype(jnp.float32) + bias[None, :, None, :]
            s = jnp.where(valid[None, None], s, NEG_INF)
            lse = jax.nn.logsumexp(s, axis=-1)
            p = jnp.exp(s - lse[..., None])
            outs.append(jnp.einsum('bhqn,bqnhd->bqhd', p.astype(v.dtype), vg))
            lses.append(lse)
        wts = jax.nn.softmax(jnp.stack(lses), axis=0)
        wts = jnp.transpose(wts, (0, 1, 3, 2))[..., None]
        return jnp.sum(wts.astype(v.dtype) * jnp.stack(outs), axis=0)

    out = lax.map(block, jnp.arange(n_blk, dtype=jnp.int32))
    return jnp.transpose(out, (1, 0, 2, 3, 4)).reshape(B, S, H * hd)


def rwkv7_time_mix(p, v_first, v_res, mu, w0, w2, a0, a2, g2, k_k, k_a, r_k, ln_w, ln_b):
    B, S, _ = p.shape
    p_prev = jnp.pad(p, ((0, 0), (1, 0), (0, 0)))[:, :S]
    p = p + mu * (p_prev - p)
    r, k, v, wd, ad, gd = jnp.split(p, RWKV_SPLITS, axis=-1)
    w = -jax.nn.softplus(-(w0 + jnp.tanh(wd) @ w2)) - 0.5
    a = jax.nn.sigmoid(a0 + ad @ a2)
    g = jax.nn.sigmoid(gd) @ g2
    if v_res is None:
        v_first = v
    else:
        v0, v1, v2 = v_res
        v = v + (v_first - v) * jax.nn.sigmoid(v0 + (v @ v1) @ v2)

    def heads(t):
        return t.reshape(B, S, N_RWKV_HEADS, HEAD_DIM).astype(jnp.float32)

    kk = heads(k * k_k)
    kk = kk / jnp.maximum(jnp.linalg.norm(kk, axis=-1, keepdims=True), 1e-12)
    k = k * (1.0 + (a - 1.0) * k_a)
    rh, kh, vh, ah = heads(r), heads(k), heads(v), heads(a)
    decay = jnp.exp(-jnp.exp(heads(w)))

    def step(state, inp):
        r_t, w_t, k_t, v_t, kk_t, a_t = inp
        sa = jnp.einsum('bhvk,bhk->bhv', state, -kk_t)
        state = (state * w_t[:, :, None, :] + sa[..., None] * (kk_t * a_t)[:, :, None, :]
                 + v_t[..., None] * k_t[:, :, None, :])
        return state, jnp.einsum('bhvk,bhk->bhv', state, r_t)

    tm = lambda t: jnp.moveaxis(t, 1, 0)
    s0 = jnp.zeros((B, N_RWKV_HEADS, HEAD_DIM, HEAD_DIM), jnp.float32)
    _, y = lax.scan(step, s0, (tm(rh), tm(decay), tm(kh), tm(vh), tm(kk), tm(ah)))
    y = jnp.moveaxis(y, 0, 1)
    mean = jnp.mean(y, axis=-1, keepdims=True)
    var = jnp.mean(jnp.square(y - mean), axis=-1, keepdims=True)
    yn = ((y - mean) * lax.rsqrt(var + GN_EPS)).reshape(B, S, RWKV_WIDTH)
    yn = yn * ln_w.astype(jnp.float32) + ln_b.astype(jnp.float32)
    bonus = (jnp.sum(rh * kh * r_k.astype(jnp.float32), axis=-1, keepdims=True) * vh).reshape(B, S, RWKV_WIDTH)
    out = (yn + bonus) * g.astype(jnp.float32)
    return out.astype(p.dtype), v_first


def moe_swiglu(h, router_w, router_b, w_gate, w_up, w_down):
    B, S, D = h.shape
    T = B * S
    hf = h.reshape(T, D)
    logits = (hf @ router_w).astype(jnp.float32) + router_b.astype(jnp.float32)
    top_logits, top_idx = lax.top_k(logits, TOP_K)
    gates = jax.nn.softmax(top_logits, axis=-1)
    TK = T * TOP_K
    flat_e = top_idx.reshape(TK)
    flat_tok = jnp.arange(TK, dtype=jnp.int32) // TOP_K
    flat_gate = gates.reshape(TK)
    order = jnp.argsort(flat_e)
    sorted_e = flat_e[order]
    counts = jnp.zeros((N_EXPERTS,), jnp.int32).at[flat_e].add(1)
    starts = jnp.cumsum(counts) - counts
    padded = (counts + MOE_BLOCK - 1) // MOE_BLOCK * MOE_BLOCK
    pad_ends = jnp.cumsum(padded)
    pad_starts = pad_ends - padded
    dest = pad_starts[sorted_e] + jnp.arange(TK, dtype=jnp.int32) - starts[sorted_e]
    n_blocks = -(-TK // MOE_BLOCK) + N_EXPERTS
    n_rows = n_blocks * MOE_BLOCK
    row_tok = jnp.full((n_rows,), T, jnp.int32).at[dest].set(flat_tok[order])
    row_gate = jnp.zeros((n_rows,), jnp.float32).at[dest].set(flat_gate[order])
    block_start = jnp.arange(n_blocks, dtype=jnp.int32) * MOE_BLOCK
    block_e = jnp.minimum(jnp.searchsorted(pad_ends, block_start, side='right'), N_EXPERTS - 1)
    x_rows = jnp.concatenate([hf, jnp.zeros((1, D), hf.dtype)], axis=0)[row_tok]
    x_rows = x_rows.reshape(n_blocks, MOE_BLOCK, D)

    def expert_block(args):
        xb, e = args
        return swiglu(xb, w_gate[e], w_up[e], w_down[e])

    y_rows = lax.map(expert_block, (x_rows, block_e)).reshape(n_rows, D)
    y_rows = (y_rows.astype(jnp.float32) * row_gate[:, None]).astype(h.dtype)
    out = jnp.zeros((T + 1, D), h.dtype).at[row_tok].add(y_rows)
    return out[:T].reshape(B, S, D)


def setup_inputs(seed: int = 0) -> dict:
    key = jax.random.key(seed)
    ks = iter(jax.random.split(key, 40))
    f32 = jnp.float32

    def nrm(shape, scale):
        return jax.random.normal(next(ks), shape, f32) * scale

    def unif(shape, lo, hi):
        return jax.random.uniform(next(ks), shape, f32, lo, hi)

    D, RW, L = D_MODEL, RWKV_WIDTH, DEPTH
    return {
        "x": nrm((BATCH, SEQ, D), 1.0),
        "c": nrm((BATCH, D), 1.0),
        "w_ada": nrm((L, D, 6 * D), 0.5 * D ** -0.5),
        "b_ada": nrm((L, 6 * D), 0.02),
        "norm1_g": 1.0 + nrm((L, D), 0.05),
        "norm2_g": 1.0 + nrm((L, D), 0.05),
        "final_g": 1.0 + nrm((D,), 0.05),
        "w_in": nrm((L, D, N_IN), D ** -0.5),
        "w_out": nrm((L, D, D), D ** -0.5),
        "rel_bias": nrm((N_BUCKETS, N_ATT_HEADS), 0.3),
        "rwkv_mu": unif((L, RWKV_COLS), 0.0, 1.0),
        "rwkv_w0": unif((L, RW), -5.0, -0.5),
        "rwkv_w2": nrm((L, D_DECAY_LORA, RW), 0.3 * D_DECAY_LORA ** -0.5),
        "rwkv_a0": nrm((L, RW), 0.3),
        "rwkv_a2": nrm((L, D_AAA_LORA, RW), 0.3 * D_AAA_LORA ** -0.5),
        "rwkv_g2": nrm((L, D_GATE_LORA, RW), D_GATE_LORA ** -0.5),
        "rwkv_k_k": 0.85 + nrm((L, RW), 0.05),
        "rwkv_k_a": 1.0 + nrm((L, RW), 0.05),
        "rwkv_r_k": nrm((L, N_RWKV_HEADS, HEAD_DIM), 0.1),
        "rwkv_ln_w": 1.0 + nrm((L, RW), 0.05),
        "rwkv_ln_b": nrm((L, RW), 0.02),
        "rwkv_v0": nrm((L - 1, RW), 0.3),
        "rwkv_v1": nrm((L - 1, RW, D_MV_LORA), RW ** -0.5),
        "rwkv_v2": nrm((L - 1, D_MV_LORA, RW), 0.3 * D_MV_LORA ** -0.5),
        "ffn_w_gate": nrm((N_DENSE, D, D_FF_DENSE), D ** -0.5),
        "ffn_w_up": nrm((N_DENSE, D, D_FF_DENSE), D ** -0.5),
        "ffn_w_down": nrm((N_DENSE, D_FF_DENSE, D), D_FF_DENSE ** -0.5),
        "moe_router_w": nrm((N_MOE, D, N_EXPERTS), D ** -0.5),
        "moe_router_b": nrm((N_MOE, N_EXPERTS), 0.01),
        "moe_w_gate": nrm((N_MOE, N_EXPERTS, D, D_FF_EXPERT), D ** -0.5),
        "moe_w_up": nrm((N_MOE, N_EXPERTS, D, D_FF_EXPERT), D ** -0.5),
        "moe_w_down": nrm((N_MOE, N_EXPERTS, D_FF_EXPERT, D), D_FF_EXPERT ** -0.5),
    }


def reference(x, c, w_ada, b_ada, norm1_g, norm2_g, final_g, w_in, w_out, rel_bias,
              rwkv_mu, rwkv_w0, rwkv_w2, rwkv_a0, rwkv_a2, rwkv_g2, rwkv_k_k, rwkv_k_a,
              rwkv_r_k, rwkv_ln_w, rwkv_ln_b, rwkv_v0, rwkv_v1, rwkv_v2,
              ffn_w_gate, ffn_w_up, ffn_w_down,
              moe_router_w, moe_router_b, moe_w_gate, moe_w_up, moe_w_down):
    B, S, D = x.shape
    c_act = jax.nn.silu(c)
    v_first = None
    for l in range(DEPTH):
        mod = c_act @ w_ada[l] + b_ada[l]
        sh1, sc1, g1, sh2, sc2, g2 = [m[:, None, :] for m in jnp.split(mod, 6, axis=-1)]

        h = rms_norm(x, norm1_g[l]) * (1.0 + sc1) + sh1
        proj = h @ w_in[l]
        q = proj[..., :ATT_WIDTH].reshape(B, S, N_ATT_HEADS, HEAD_DIM)
        k = proj[..., ATT_WIDTH:2 * ATT_WIDTH].reshape(B, S, N_ATT_HEADS, HEAD_DIM)
        v = proj[..., 2 * ATT_WIDTH:3 * ATT_WIDTH].reshape(B, S, N_ATT_HEADS, HEAD_DIM)
        att = dilated_attention(q, k, v, rel_bias)
        v_res = None if l == 0 else (rwkv_v0[l - 1], rwkv_v1[l - 1], rwkv_v2[l - 1])
        rw, v_first = rwkv7_time_mix(proj[..., 3 * ATT_WIDTH:], v_first, v_res,
                                     rwkv_mu[l], rwkv_w0[l], rwkv_w2[l], rwkv_a0[l], rwkv_a2[l],
                                     rwkv_g2[l], rwkv_k_k[l], rwkv_k_a[l], rwkv_r_k[l],
                                     rwkv_ln_w[l], rwkv_ln_b[l])
        mix = jnp.concatenate([att, rw], axis=-1) @ w_out[l]
        x = x + g1 * mix

        h = rms_norm(x, norm2_g[l]) * (1.0 + sc2) + sh2
        i = l // 2
        if l % 2 == 0:
            ff = swiglu(h, ffn_w_gate[i], ffn_w_up[i], ffn_w_down[i])
        else:
            ff = moe_swiglu(h, moe_router_w[i], moe_router_b[i], moe_w_gate[i], moe_w_up[i], moe_w_down[i])
        x = x + g2 * ff
    return rms_norm(x, final_g)
```

```python
import functools
import math

import numpy as np
import jax
import jax.numpy as jnp
from jax import lax
from jax.experimental import pallas as pl
from jax.experimental.pallas import tpu as pltpu

F32 = jnp.float32
BF16 = jnp.bfloat16

D_MODEL = 1024
HEAD_DIM = 64
ATT_WIDTH = 512
RWKV_WIDTH = 512
N_HEADS = 8
HEAD_PAIRS = 4
LANES = 128
DILATED_GROUPS = ((128, 1), (512, 4), (2048, 16))
N_BUCKETS = 32
MAX_DISTANCE = 2048
NEG_INF = -1e30
D_DECAY_LORA = 64
D_AAA_LORA = 64
D_MV_LORA = 32
D_GATE_LORA = 160
N_IN = 3 * ATT_WIDTH + 3 * RWKV_WIDTH + D_DECAY_LORA + D_AAA_LORA + D_GATE_LORA
QKV_COLS = 3 * ATT_WIDTH
RWM_COLS = 3 * RWKV_WIDTH
RWL_COLS = 384
N_IN_PAD = QKV_COLS + RWM_COLS + RWL_COLS
GN_EPS = HEAD_DIM * 1e-5
RMS_EPS = 1e-6
N_EXPERTS = 8
TOP_K = 2
CHUNK = 64
VMEM_LIMIT = 56 * 1024 * 1024


def _cparams(sem):
    return pltpu.CompilerParams(dimension_semantics=sem, vmem_limit_bytes=VMEM_LIMIT)


def _tile(n, pref):
    t = min(n, pref)
    while n % t:
        t //= 2
    return t


def _mm(a, b):
    return jnp.dot(a.astype(BF16), b.astype(BF16), preferred_element_type=F32)


def _mm_nt(a, b):
    return lax.dot_general(a.astype(BF16), b.astype(BF16), (((1,), (1,)), ((), ())),
                           preferred_element_type=F32)


def _sigmoid(x):
    return 1.0 / (1.0 + jnp.exp(-x))


def _silu(x):
    return x * _sigmoid(x)


def _ada_kernel(c_ref, w_ref, b_ref, o_ref):
    ca = _silu(c_ref[...])
    o_ref[0] = _mm(ca, w_ref[0]) + b_ref[0]


def _ada_mod(c, w_ada, b_ada):
    L, D, N = w_ada.shape
    B = c.shape[0]
    tn = _tile(N, 1536)
    return pl.pallas_call(
        _ada_kernel,
        grid=(L, N // tn),
        in_specs=[pl.BlockSpec((B, D), lambda l, j: (0, 0)),
                  pl.BlockSpec((1, D, tn), lambda l, j: (l, 0, j)),
                  pl.BlockSpec((1, 1, tn), lambda l, j: (l, 0, j))],
        out_specs=pl.BlockSpec((1, B, tn), lambda l, j: (l, 0, j)),
        out_shape=jax.ShapeDtypeStruct((L, B, N), F32),
        compiler_params=_cparams(("arbitrary", "arbitrary")),
        name="ada_mod",
    )(c, w_ada, b_ada.reshape(L, 1, N))


def _rms_mod(xf, g, scale, shift):
    ms = jnp.mean(xf * xf, axis=-1, keepdims=True)
    return (xf * lax.rsqrt(ms + RMS_EPS) * g) * (1.0 + scale) + shift


def _inproj_kernel(x_ref, mod_ref, g_ref, w_ref, mu_ref, qkv_ref, rwm_ref, rwl_ref, carry_ref):
    i = pl.program_id(1)
    tm = x_ref.shape[0]
    h = _rms_mod(x_ref[...], g_ref[...], mod_ref[0, 1:2, :], mod_ref[0, 0:1, :])
    acc = jnp.dot(h.astype(BF16), w_ref[0], preferred_element_type=F32)
    qkv_ref[...] = acc[:, :QKV_COLS].astype(BF16)
    p = acc[:, QKV_COLS:]
    first = jnp.where(i == 0, 0.0, carry_ref[...])
    row = lax.broadcasted_iota(jnp.int32, p.shape, 0)
    prev = jnp.where(row == 0, first, pltpu.roll(p, 1, 0))
    carry_ref[...] = p[tm - 1:tm, :]
    pm = p + mu_ref[...] * (prev - p)
    rwm_ref[...] = pm[:, :RWM_COLS]
    rwl_ref[...] = pm[:, RWM_COLS:]


def _inproj(x, mod, g, w_in_l, mu_l, B, S, l):
    T, D = x.shape
    tm = _tile(S, 512)
    nS = S // tm
    return pl.pallas_call(
        _inproj_kernel,
        grid=(B, nS),
        in_specs=[pl.BlockSpec((tm, D), lambda b, i: (b * nS + i, 0)),
                  pl.BlockSpec((1, 6, D), lambda b, i: (l * B + b, 0, 0)),
                  pl.BlockSpec((1, D), lambda b, i: (0, 0)),
                  pl.BlockSpec((1, D, N_IN_PAD), lambda b, i: (l, 0, 0)),
                  pl.BlockSpec((1, RWM_COLS + RWL_COLS), lambda b, i: (0, 0))],
        out_specs=[pl.BlockSpec((tm, QKV_COLS), lambda b, i: (b * nS + i, 0)),
                   pl.BlockSpec((tm, RWM_COLS), lambda b, i: (b * nS + i, 0)),
                   pl.BlockSpec((tm, RWL_COLS), lambda b, i: (b * nS + i, 0))],
        out_shape=[jax.ShapeDtypeStruct((T, QKV_COLS), BF16),
                   jax.ShapeDtypeStruct((T, RWM_COLS), F32),
                   jax.ShapeDtypeStruct((T, RWL_COLS), F32)],
        scratch_shapes=[pltpu.VMEM((1, RWM_COLS + RWL_COLS), F32)],
        compiler_params=_cparams(("arbitrary", "arbitrary")),
        name="inproj",
    )(x, mod, g, w_in_l, mu_l)


def _t5_bucket(n):
    max_exact = N_BUCKETS // 2
    large = max_exact + (np.log(np.maximum(n, 1) / max_exact) / np.log(MAX_DISTANCE / max_exact)
                         * (N_BUCKETS - max_exact)).astype(np.int32)
    large = np.minimum(large, N_BUCKETS - 1)
    return np.where(n < max_exact, n, large).astype(np.int32)


def _att_bias_table(rel_bias, S, tq):
    off = np.arange(S)
    mult = np.zeros(S, np.int64)
    for w, d in DILATED_GROUPS:
        mult += ((off % d == 0) & (off <= w)).astype(np.int64)
    logm = np.where(mult > 0, np.log(np.maximum(mult, 1)), NEG_INF).astype(np.float32)
    per_off = rel_bias[_t5_bucket(off)].T.astype(F32) + jnp.asarray(logm)[None, :]
    nd = S // tq
    rel = (np.arange(nd)[:, None, None] * tq + np.arange(tq)[None, :, None]
           - np.arange(tq)[None, None, :])
    table = jnp.take(per_off, jnp.asarray(np.maximum(rel, 0).reshape(-1)), axis=1)
    table = table.reshape(per_off.shape[0], nd, tq, tq)
    return jnp.where(jnp.asarray(rel >= 0)[None], table, NEG_INF)


def _attn_kernel(q_ref, k_ref, v_ref, bias_ref, o_ref):
    qi = pl.program_id(2)
    tq = q_ref.shape[0]
    q = q_ref[...] * jnp.asarray(1.0 / math.sqrt(HEAD_DIM), BF16)
    lane = lax.broadcasted_iota(jnp.int32, (tq, LANES), 1)
    outs = []
    for h in range(2):
        in_head = (lane < HEAD_DIM) if h == 0 else (lane >= HEAD_DIM)
        qh = jnp.where(in_head, q, jnp.zeros_like(q))

        def body(j, carry):
            m, l, acc = carry
            start = pl.multiple_of((qi - j) * tq, tq)
            kt = k_ref[pl.ds(start, tq), :]
            vt = v_ref[pl.ds(start, tq), :]
            s = lax.dot_general(qh, kt, (((1,), (1,)), ((), ())), preferred_element_type=F32)
            s = s + bias_ref[h, j]
            m_new = jnp.maximum(m, jnp.max(s, axis=-1, keepdims=True))
            alpha = jnp.exp(m - m_new)
            p = jnp.exp(s - m_new)
            l_new = alpha * l + jnp.sum(p, axis=-1, keepdims=True)
            acc_new = alpha * acc + jnp.dot(p.astype(BF16), vt, preferred_element_type=F32)
            return m_new, l_new, acc_new

        init = (jnp.full((tq, 1), NEG_INF, F32), jnp.zeros((tq, 1), F32),
                jnp.zeros((tq, LANES), F32))
        _, l, acc = lax.fori_loop(0, qi + 1, body, init)
        outs.append(acc / l)
    o_ref[...] = jnp.where(lane < HEAD_DIM, outs[0], outs[1]).astype(o_ref.dtype)


def _attention(qkv, bias_table, B, S):
    T = qkv.shape[0]
    tq = bias_table.shape[-1]
    nq = S // tq
    nd = bias_table.shape[1]
    return pl.pallas_call(
        _attn_kernel,
        grid=(HEAD_PAIRS, B, nq),
        in_specs=[pl.BlockSpec((tq, LANES), lambda hp, b, i: (b * nq + i, hp)),
                  pl.BlockSpec((S, LANES), lambda hp, b, i: (b, HEAD_PAIRS + hp)),
                  pl.BlockSpec((S, LANES), lambda hp, b, i: (b, 2 * HEAD_PAIRS + hp)),
                  pl.BlockSpec((2, nd, tq, tq), lambda hp, b, i: (hp, 0, 0, 0))],
        out_specs=pl.BlockSpec((tq, LANES), lambda hp, b, i: (b * nq + i, hp)),
        out_shape=jax.ShapeDtypeStruct((T, ATT_WIDTH), BF16),
        compiler_params=_cparams(("arbitrary", "arbitrary", "arbitrary")),
        name="dilated_attn",
    )(qkv, qkv, qkv, bias_table)


def _split2(x):
    hi = x.astype(BF16)
    lo = (x - hi.astype(F32)).astype(BF16)
    return hi, lo


def _head_sum(x, m2):
    hi, lo = _split2(x)
    return jnp.dot(jnp.concatenate([hi, lo], axis=1), m2, preferred_element_type=F32)


def _rwkv_kernel(*refs, has_vres, n_chunks):
    if has_vres:
        (rwm_ref, rwl_ref, vf_ref, vec_ref, w2_ref, a2_ref, g2_ref, v1_ref, v2_ref,
         o_ref, state_ref, r_s, k_s, v_s, lw_s, kk_s, kb_s, y_s) = refs
    else:
        (rwm_ref, rwl_ref, vec_ref, w2_ref, a2_ref, g2_ref,
         o_ref, vf_out_ref, state_ref, r_s, k_s, v_s, lw_s, kk_s, kb_s, y_s) = refs
    C = CHUNK
    W = RWKV_WIDTH

    @pl.when(pl.program_id(1) == 0)
    def _():
        state_ref[...] = jnp.zeros_like(state_ref)

    ri = lax.broadcasted_iota(jnp.int32, (2 * LANES, LANES), 0)
    ci = lax.broadcasted_iota(jnp.int32, (2 * LANES, LANES), 1)
    m2 = ((ri % LANES) // HEAD_DIM == ci // HEAD_DIM).astype(BF16)

    vec = vec_ref[...]
    w0, a0, k_k, k_a, r_k, ln_w, ln_b, v0 = [vec[i:i + 1, :] for i in range(8)]

    r = rwm_ref[:, 0:W]
    k = rwm_ref[:, W:2 * W]
    v = rwm_ref[:, 2 * W:3 * W]
    lora = rwl_ref[:, 0:LANES]
    w_raw = w0 + _mm(jnp.tanh(lora), w2_ref[...])
    z = -w_raw
    w_log = -(jnp.maximum(z, 0.0) + jnp.log1p(jnp.exp(-jnp.abs(z)))) - 0.5
    lw_s[...] = -jnp.exp(w_log)
    a = _sigmoid(a0 + _mm(lora, a2_ref[...]))
    g = _mm(_sigmoid(rwl_ref[:, LANES:RWL_COLS]), g2_ref[...])
    if has_vres:
        mix = _sigmoid(v0 + _mm(_mm(v, v1_ref[...]), v2_ref[...]))
        v = v + (vf_ref[...] - v) * mix
    else:
        vf_out_ref[...] = v
    kk = k * k_k
    for p in range(HEAD_PAIRS):
        ls = slice(p * LANES, (p + 1) * LANES)
        kkp = kk[:, ls]
        nrm = jnp.sqrt(_head_sum(kkp * kkp, m2))
        kkp = kkp / jnp.maximum(nrm, 1e-12)
        kk_s[:, ls] = kkp
        kb_s[:, ls] = kkp * a[:, ls]
    kmod = k * (1.0 + (a - 1.0) * k_a)
    r_s[...] = r
    k_s[...] = kmod
    v_s[...] = v

    ri = lax.broadcasted_iota(jnp.int32, (LANES, LANES), 0)
    ci = lax.broadcasted_iota(jnp.int32, (LANES, LANES), 1)
    same_head = (ri // C) == (ci // C)
    strict = same_head & ((ri % C) > (ci % C))
    incl = same_head & ((ri % C) >= (ci % C))
    eye = (ri == ci).astype(F32)
    lane = lax.broadcasted_iota(jnp.int32, (C, LANES), 1)
    head0 = lane < HEAD_DIM
    tri_r = lax.broadcasted_iota(jnp.int32, (C, 3 * C), 0)
    tri_c = lax.broadcasted_iota(jnp.int32, (C, 3 * C), 1)
    tri3 = ((tri_c % C) <= tri_r).astype(BF16)

    def chunk_body(c, _):
        rows = pl.ds(pl.multiple_of(c * C, C), C)
        for p in range(HEAD_PAIRS):
            ls = slice(p * LANES, (p + 1) * LANES)
            rc = r_s[rows, ls]
            kc = k_s[rows, ls]
            vc = v_s[rows, ls]
            lw = lw_s[rows, ls]
            kkc = kk_s[rows, ls]
            kbc = kb_s[rows, ls]
            hi = lw.astype(BF16)
            rem = lw - hi.astype(F32)
            mid = rem.astype(BF16)
            lo = (rem - mid.astype(F32)).astype(BF16)
            cum = jnp.dot(tri3, jnp.concatenate([hi, mid, lo], axis=0), preferred_element_type=F32)
            total = cum[C - 1:C, :]
            g_inv = jnp.exp(-cum)
            a_t = -kkc * jnp.exp(cum - lw)
            r_t = rc * jnp.exp(cum)
            b_t = kbc * g_inv
            k_t = kc * g_inv
            to_end = jnp.exp(total - cum)
            zero = jnp.zeros_like(a_t)
            a_stack = jnp.concatenate([jnp.where(head0, a_t, zero), jnp.where(head0, zero, a_t),
                                       jnp.where(head0, r_t, zero), jnp.where(head0, zero, r_t)], axis=0)
            b_stack = jnp.concatenate([jnp.where(head0, b_t, zero), jnp.where(head0, zero, b_t),
                                       jnp.where(head0, k_t, zero), jnp.where(head0, zero, k_t)], axis=0)
            gram = _mm_nt(a_stack, b_stack)
            n_ab = jnp.where(strict, gram[:LANES, :LANES], 0.0)
            n_ak = jnp.where(strict, gram[:LANES, LANES:], 0.0)
            n_rb = jnp.where(incl, gram[LANES:, :LANES], 0.0)
            n_rk = jnp.where(incl, gram[LANES:, LANES:], 0.0)
            pw = n_ab
            inv = eye + n_ab
            for _ in range(int(math.log2(C)) - 1):
                pw = _mm(pw, pw)
                inv = inv + _mm(inv, pw)
            v_stack = jnp.concatenate([vc, vc], axis=0)
            st = state_ref[p]
            ah = _mm_nt(jnp.concatenate([a_t, r_t], axis=0), st)
            a_h = ah[:C]
            r_h = ah[C:]
            u_stack = _mm(inv, jnp.concatenate([a_h, a_h], axis=0) + _mm(n_ak, v_stack))
            y_stack = (jnp.concatenate([r_h, r_h], axis=0)
                       + _mm(jnp.concatenate([n_rb, n_rk], axis=1),
                             jnp.concatenate([u_stack, v_stack], axis=0)))
            u = jnp.where(head0, u_stack[:C], u_stack[C:])
            y_s[rows, ls] = jnp.where(head0, y_stack[:C], y_stack[C:])
            uv_t = jnp.concatenate([u, vc], axis=0).T
            bk = jnp.concatenate([kbc * to_end, kc * to_end], axis=0)
            st_new = st * jnp.exp(total) + _mm(uv_t, bk)
            state_ref[p] = jnp.where(same_head, st_new, 0.0)
        return 0

    lax.fori_loop(0, n_chunks, chunk_body, 0)

    for p in range(HEAD_PAIRS):
        ls = slice(p * LANES, (p + 1) * LANES)
        y = y_s[:, ls]
        mean = _head_sum(y, m2) * (1.0 / HEAD_DIM)
        d = y - mean
        var = _head_sum(d * d, m2) * (1.0 / HEAD_DIM)
        yn = d * lax.rsqrt(var + GN_EPS) * ln_w[:, ls] + ln_b[:, ls]
        bonus = _head_sum(r_s[:, ls] * k_s[:, ls] * r_k[:, ls], m2) * v_s[:, ls]
        o_ref[:, ls] = ((yn + bonus) * g[:, ls]).astype(o_ref.dtype)


def _rwkv(rwm, rwl, v_first, vec, w2p, a2p, g2p, v1p, v2p, B, S):
    T = rwm.shape[0]
    W = RWKV_WIDTH
    tb = _tile(S, 512)
    nS = S // tb
    has_vres = v_first is not None
    row = lambda b, i: (b * nS + i, 0)
    full = lambda b, i: (0, 0)
    in_specs = [pl.BlockSpec((tb, RWM_COLS), row), pl.BlockSpec((tb, RWL_COLS), row)]
    args = [rwm, rwl]
    if has_vres:
        in_specs.append(pl.BlockSpec((tb, W), row))
        args.append(v_first)
    in_specs += [pl.BlockSpec(vec.shape, full), pl.BlockSpec(w2p.shape, full),
                 pl.BlockSpec(a2p.shape, full), pl.BlockSpec(g2p.shape, full)]
    args += [vec, w2p, a2p, g2p]
    if has_vres:
        in_specs += [pl.BlockSpec(v1p.shape, full), pl.BlockSpec(v2p.shape, full)]
        args += [v1p, v2p]
    out_specs = [pl.BlockSpec((tb, W), row)]
    out_shape = [jax.ShapeDtypeStruct((T, W), BF16)]
    if not has_vres:
        out_specs.append(pl.BlockSpec((tb, W), row))
        out_shape.append(jax.ShapeDtypeStruct((T, W), F32))
    outs = pl.pallas_call(
        functools.partial(_rwkv_kernel, has_vres=has_vres, n_chunks=tb // CHUNK),
        grid=(B, nS),
        in_specs=in_specs,
        out_specs=out_specs,
        out_shape=out_shape,
        scratch_shapes=[pltpu.VMEM((HEAD_PAIRS, LANES, LANES), F32)] + [pltpu.VMEM((tb, W), F32)] * 7,
        compiler_params=_cparams(("arbitrary", "arbitrary")),
        name="rwkv7",
    )(*args)
    return (outs[0], v_first) if has_vres else (outs[0], outs[1])


def _outproj_kernel(*refs, moe):
    if moe:
        (x_ref, att_ref, rw_ref, mod_ref, g_ref, w_ref, rw_w_ref, rw_b_ref,
         x1_ref, h_ref, info_ref) = refs
    else:
        x_ref, att_ref, rw_ref, mod_ref, g_ref, w_ref, x1_ref, h_ref = refs
    mix = (jnp.dot(att_ref[...], w_ref[0, :ATT_WIDTH, :], preferred_element_type=F32)
           + jnp.dot(rw_ref[...], w_ref[0, ATT_WIDTH:, :], preferred_element_type=F32))
    x1 = x_ref[...] + mod_ref[0, 2:3, :] * mix
    x1_ref[...] = x1
    h = _rms_mod(x1, g_ref[...], mod_ref[0, 4:5, :], mod_ref[0, 3:4, :])
    h_ref[...] = h.astype(h_ref.dtype)
    if moe:
        logits = jnp.dot(h, rw_w_ref[...], precision=lax.Precision.HIGHEST,
                         preferred_element_type=F32) + rw_b_ref[...]
        lane = lax.broadcasted_iota(jnp.int32, logits.shape, 1)
        logits = jnp.where(lane < N_EXPERTS, logits, -jnp.inf)
        m1 = jnp.max(logits, axis=-1, keepdims=True)
        i1 = jnp.min(jnp.where(logits == m1, lane, LANES), axis=-1, keepdims=True)
        rest = jnp.where(lane == i1, -jnp.inf, logits)
        m2 = jnp.max(rest, axis=-1, keepdims=True)
        i2 = jnp.min(jnp.where(rest == m2, lane, LANES), axis=-1, keepdims=True)
        e = jnp.exp(m2 - m1)
        g1 = 1.0 / (1.0 + e)
        g2 = e / (1.0 + e)
        info = jnp.where(lane == 0, i1.astype(F32),
                         jnp.where(lane == 1, i2.astype(F32),
                                   jnp.where(lane == 2, g1, jnp.where(lane == 3, g2, 0.0))))
        info_ref[...] = info


def _outproj(x, att, rw, mod, g, w_out_b, B, S, l, router=None):
    T, D = x.shape
    tm = _tile(S, 512)
    nS = S // tm
    moe = router is not None
    row = lambda b, i: (b * nS + i, 0)
    full = lambda b, i: (0, 0)
    in_specs = [pl.BlockSpec((tm, D), row), pl.BlockSpec((tm, ATT_WIDTH), row),
                pl.BlockSpec((tm, RWKV_WIDTH), row),
                pl.BlockSpec((1, 6, D), lambda b, i: (l * B + b, 0, 0)),
                pl.BlockSpec((1, D), full),
                pl.BlockSpec((1, D, D), lambda b, i: (l, 0, 0))]
    args = [x, att, rw, mod, g, w_out_b]
    out_specs = [pl.BlockSpec((tm, D), row), pl.BlockSpec((tm, D), row)]
    out_shape = [jax.ShapeDtypeStruct((T, D), F32), jax.ShapeDtypeStruct((T, D), F32 if moe else BF16)]
    if moe:
        in_specs += [pl.BlockSpec((D, LANES), full), pl.BlockSpec((1, LANES), full)]
        args += list(router)
        out_specs.append(pl.BlockSpec((tm, LANES), row))
        out_shape.append(jax.ShapeDtypeStruct((T, LANES), F32))
    return pl.pallas_call(
        functools.partial(_outproj_kernel, moe=moe),
        grid=(B, nS),
        in_specs=in_specs,
        out_specs=out_specs,
        out_shape=out_shape,
        compiler_params=_cparams(("arbitrary", "arbitrary")),
        name="outproj",
    )(*args)


def _ffn_kernel(x_ref, h_ref, mod_ref, wg_ref, wu_ref, wd_ref, o_ref, acc_ref):
    f = pl.program_id(1)

    @pl.when(f == 0)
    def _():
        acc_ref[...] = jnp.zeros_like(acc_ref)

    hb = h_ref[...]
    gate = jnp.dot(hb, wg_ref[0].astype(BF16), preferred_element_type=F32)
    up = jnp.dot(hb, wu_ref[0].astype(BF16), preferred_element_type=F32)
    mid = (_silu(gate) * up).astype(BF16)
    acc_ref[...] += jnp.dot(mid, wd_ref[0].astype(BF16), preferred_element_type=F32)

    @pl.when(f == pl.num_programs(1) - 1)
    def _():
        o_ref[...] = x_ref[...] + mod_ref[0, 5:6, :] * acc_ref[...]


def _ffn_dense(x1, h, mod, wg, wu, wd, B, S, l, li):
    T, D = x1.shape
    F = wg.shape[-1]
    tm = _tile(S, 1024)
    tf = _tile(F, 256)
    per_seq = S // tm
    return pl.pallas_call(
        _ffn_kernel,
        grid=(T // tm, F // tf),
        in_specs=[pl.BlockSpec((tm, D), lambda i, f: (i, 0)),
                  pl.BlockSpec((tm, D), lambda i, f: (i, 0)),
                  pl.BlockSpec((1, 6, D), lambda i, f: (l * B + i // per_seq, 0, 0)),
                  pl.BlockSpec((1, D, tf), lambda i, f: (li, 0, f)),
                  pl.BlockSpec((1, D, tf), lambda i, f: (li, 0, f)),
                  pl.BlockSpec((1, tf, D), lambda i, f: (li, f, 0))],
        out_specs=pl.BlockSpec((tm, D), lambda i, f: (i, 0)),
        out_shape=jax.ShapeDtypeStruct((T, D), F32),
        scratch_shapes=[pltpu.VMEM((tm, D), F32)],
        compiler_params=_cparams(("arbitrary", "arbitrary")),
        name="ffn_dense",
    )(x1, h, mod, wg, wu, wd)


def _gather_rows_kernel(tok_ref, nused_ref, src_ref, dst_ref, sem, *, rows_per_step):
    base = pl.program_id(0) * rows_per_step

    @pl.when(base < nused_ref[0])
    def _():
        def copy(j):
            return pltpu.make_async_copy(src_ref.at[pl.ds(tok_ref[base + j], 1)],
                                         dst_ref.at[pl.ds(base + j, 1)], sem)

        def start(j, _):
            copy(j).start()
            return 0

        def wait(j, _):
            copy(j).wait()
            return 0

        lax.fori_loop(0, rows_per_step, start, 0)
        lax.fori_loop(0, rows_per_step, wait, 0)

    @pl.when(base >= nused_ref[0])
    def _():
        fill = pltpu.make_async_copy(src_ref.at[pl.ds(0, rows_per_step)],
                                     dst_ref.at[pl.ds(base, rows_per_step)], sem)
        fill.start()
        fill.wait()


def _gather_rows(src, row_tok, n_used_rows, rows_per_step):
    n_rows = row_tok.shape[0]
    D = src.shape[1]
    assert src.shape[0] >= rows_per_step
    return pl.pallas_call(
        functools.partial(_gather_rows_kernel, rows_per_step=rows_per_step),
        grid_spec=pltpu.PrefetchScalarGridSpec(
            num_scalar_prefetch=2,
            grid=(n_rows // rows_per_step,),
            in_specs=[pl.BlockSpec(memory_space=pl.ANY)],
            out_specs=pl.BlockSpec(memory_space=pl.ANY),
            scratch_shapes=[pltpu.SemaphoreType.DMA(())]),
        out_shape=jax.ShapeDtypeStruct((n_rows, D), src.dtype),
        compiler_params=_cparams(("arbitrary",)),
        name="moe_gather",
    )(row_tok, n_used_rows, src)


def _experts_kernel(te_ref, nu_ref, x_ref, wg_ref, wu_ref, wd_ref, y_ref, xb_ref, acc_ref):
    i = pl.program_id(0)
    f = pl.program_id(1)

    @pl.when(i < nu_ref[0])
    def _():
        @pl.when(f == 0)
        def _():
            xb_ref[...] = x_ref[...].astype(BF16)
            acc_ref[...] = jnp.zeros_like(acc_ref)

        xb = xb_ref[...]
        gate = jnp.dot(xb, wg_ref[0, 0].astype(BF16), preferred_element_type=F32)
        up = jnp.dot(xb, wu_ref[0, 0].astype(BF16), preferred_element_type=F32)
        mid = (_silu(gate) * up).astype(BF16)
        acc_ref[...] += jnp.dot(mid, wd_ref[0, 0].astype(BF16), preferred_element_type=F32)

        @pl.when(f == pl.num_programs(1) - 1)
        def _():
            y_ref[...] = acc_ref[...]

    @pl.when((i >= nu_ref[0]) & (f == 0))
    def _():
        y_ref[...] = jnp.zeros_like(y_ref)


def _experts(x_rows, tile_e, n_used, wg, wu, wd, li, tm):
    n_rows, D = x_rows.shape
    F = wg.shape[-1]
    tf = _tile(F, 512)
    nf = F // tf

    def rows(i, f, te, nu):
        return (i, 0)

    def wcol(i, f, te, nu):
        return (li, te[jnp.minimum(i, nu[0] - 1)], 0, jnp.where(i < nu[0], f, nf - 1))

    def wrow(i, f, te, nu):
        return (li, te[jnp.minimum(i, nu[0] - 1)], jnp.where(i < nu[0], f, nf - 1), 0)

    return pl.pallas_call(
        _experts_kernel,
        grid_spec=pltpu.PrefetchScalarGridSpec(
            num_scalar_prefetch=2,
            grid=(n_rows // tm, nf),
            in_specs=[pl.BlockSpec((tm, D), rows),
                      pl.BlockSpec((1, 1, D, tf), wcol),
                      pl.BlockSpec((1, 1, D, tf), wcol),
                      pl.BlockSpec((1, 1, tf, D), wrow)],
            out_specs=pl.BlockSpec((tm, D), rows),
            scratch_shapes=[pltpu.VMEM((tm, D), BF16), pltpu.VMEM((tm, D), F32)]),
        out_shape=jax.ShapeDtypeStruct((n_rows, D), F32),
        compiler_params=_cparams(("arbitrary", "arbitrary")),
        name="moe_experts",
    )(tile_e, n_used, x_rows, wg, wu, wd)


def _combine_kernel(pos_ref, y_ref, x_ref, info_ref, mod_ref, fg_ref, o_ref, buf, sem, *, tc, final):
    base = pl.program_id(0) * (2 * tc)

    def copy(j):
        return pltpu.make_async_copy(y_ref.at[pl.ds(pos_ref[base + j], 1)], buf.at[pl.ds(j, 1)], sem)

    def start(j, _):
        copy(j).start()
        return 0

    def wait(j, _):
        copy(j).wait()
        return 0

    lax.fori_loop(0, 2 * tc, start, 0)
    lax.fori_loop(0, 2 * tc, wait, 0)
    info = info_ref[...]
    ff = info[:, 2:3] * buf[0:tc, :] + info[:, 3:4] * buf[tc:2 * tc, :]
    out = x_ref[...] + mod_ref[0, 5:6, :] * ff
    if final:
        ms = jnp.mean(out * out, axis=-1, keepdims=True)
        out = out * lax.rsqrt(ms + RMS_EPS) * fg_ref[...]
    o_ref[...] = out


def _combine(pos, y_rows, x1, info, mod, final_g, B, S, l, final):
    T, D = x1.shape
    tc = _tile(S, 256)
    per_seq = S // tc
    return pl.pallas_call(
        functools.partial(_combine_kernel, tc=tc, final=final),
        grid_spec=pltpu.PrefetchScalarGridSpec(
            num_scalar_prefetch=1,
            grid=(T // tc,),
            in_specs=[pl.BlockSpec(memory_space=pl.ANY),
                      pl.BlockSpec((tc, D), lambda i, pos: (i, 0)),
                      pl.BlockSpec((tc, LANES), lambda i, pos: (i, 0)),
                      pl.BlockSpec((1, 6, D), lambda i, pos: (l * B + i // per_seq, 0, 0)),
                      pl.BlockSpec((1, D), lambda i, pos: (0, 0))],
            out_specs=pl.BlockSpec((tc, D), lambda i, pos: (i, 0)),
            scratch_shapes=[pltpu.VMEM((2 * tc, D), F32), pltpu.SemaphoreType.DMA(())]),
        out_shape=jax.ShapeDtypeStruct((T, D), F32),
        compiler_params=_cparams(("arbitrary",)),
        name="moe_combine",
    )(pos, y_rows, x1, info, mod, final_g)


def _moe(x1, h, info, mod, wg, wu, wd, final_g, B, S, l, li, final):
    T, D = x1.shape
    TK = T * TOP_K
    tm = _tile(TK // N_EXPERTS, 1024)
    n_rows = TK + (N_EXPERTS - 1) * tm
    flat_e = info[:, :TOP_K].astype(jnp.int32).reshape(TK)
    onehot = (flat_e[:, None] == jnp.arange(N_EXPERTS, dtype=jnp.int32)[None, :]).astype(jnp.int32)
    csum = jnp.cumsum(onehot, axis=0)
    counts = csum[-1]
    rank = jnp.sum(onehot * csum, axis=1) - 1
    padded = (counts + tm - 1) // tm * tm
    pad_ends = jnp.cumsum(padded)
    pad_starts = pad_ends - padded
    dest = pad_starts[flat_e] + rank
    row_tok = jnp.zeros((n_rows,), jnp.int32).at[dest].set(jnp.arange(TK, dtype=jnp.int32) // TOP_K)
    tile_start = jnp.arange(n_rows // tm, dtype=jnp.int32) * tm
    tile_e = jnp.minimum(jnp.searchsorted(pad_ends, tile_start, side='right'), N_EXPERTS - 1).astype(jnp.int32)
    n_used_rows = pad_ends[-1:].astype(jnp.int32)
    n_used_tiles = n_used_rows // tm
    tc = _tile(S, 256)
    pos = dest.reshape(T // tc, tc, TOP_K).transpose(0, 2, 1).reshape(TK).astype(jnp.int32)

    x_rows = _gather_rows(h, row_tok, n_used_rows, tm)
    y_rows = _experts(x_rows, tile_e, n_used_tiles, wg, wu, wd, li, tm)
    return _combine(pos, y_rows, x1, info, mod, final_g, B, S, l, final)


def kernel(x, c, w_ada, b_ada, norm1_g, norm2_g, final_g, w_in, w_out, rel_bias, rwkv_mu, rwkv_w0, rwkv_w2, rwkv_a0, rwkv_a2, rwkv_g2, rwkv_k_k, rwkv_k_a, rwkv_r_k, rwkv_ln_w, rwkv_ln_b, rwkv_v0, rwkv_v1, rwkv_v2, ffn_w_gate, ffn_w_up, ffn_w_down, moe_router_w, moe_router_b, moe_w_gate, moe_w_up, moe_w_down):
    B, S, D = x.shape
    L = w_in.shape[0]
    T = B * S
    W = RWKV_WIDTH

    mod = _ada_mod(c, w_ada, b_ada).reshape(L * B, 6, D)
    w_in_b = jnp.pad(w_in, ((0, 0), (0, 0), (0, N_IN_PAD - N_IN))).astype(BF16)
    w_out_b = w_out.astype(BF16)
    mu_p = jnp.pad(rwkv_mu, ((0, 0), (0, RWM_COLS + RWL_COLS - rwkv_mu.shape[1])))
    tq = _tile(S, 256)
    bias_table = _att_bias_table(rel_bias, S, tq)
    zeros_w = jnp.zeros((1, W), F32)
    final_g2 = final_g.reshape(1, D)

    xf = x.reshape(T, D)
    v_first = None
    for l in range(L):
        qkv, rwm, rwl = _inproj(xf, mod, norm1_g[l].reshape(1, D), w_in_b, mu_p[l].reshape(1, -1), B, S, l)
        att = _attention(qkv, bias_table, B, S)
        v0 = rwkv_v0[l - 1].reshape(1, W) if l > 0 else zeros_w
        vec = jnp.concatenate([rwkv_w0[l].reshape(1, W), rwkv_a0[l].reshape(1, W), rwkv_k_k[l].reshape(1, W),
                               rwkv_k_a[l].reshape(1, W), rwkv_r_k[l].reshape(1, W), rwkv_ln_w[l].reshape(1, W),
                               rwkv_ln_b[l].reshape(1, W), v0], axis=0)
        w2p = jnp.pad(rwkv_w2[l], ((0, LANES - D_DECAY_LORA), (0, 0)))
        a2p = jnp.pad(rwkv_a2[l], ((D_DECAY_LORA, 0), (0, 0)))
        g2p = jnp.pad(rwkv_g2[l], ((0, RWL_COLS - LANES - D_GATE_LORA), (0, 0)))
        if l > 0:
            v1p = jnp.pad(rwkv_v1[l - 1], ((0, 0), (0, LANES - D_MV_LORA)))
            v2p = jnp.pad(rwkv_v2[l - 1], ((0, LANES - D_MV_LORA), (0, 0)))
        else:
            v1p = v2p = None
        rw, v_first = _rwkv(rwm, rwl, v_first, vec, w2p, a2p, g2p, v1p, v2p, B, S)
        li = l // 2
        if l % 2 == 0:
            x1, h = _outproj(xf, att, rw, mod, norm2_g[l].reshape(1, D), w_out_b, B, S, l)
            xf = _ffn_dense(x1, h, mod, ffn_w_gate, ffn_w_up, ffn_w_down, B, S, l, li)
        else:
            router = (jnp.pad(moe_router_w[li], ((0, 0), (0, LANES - N_EXPERTS))),
                      jnp.pad(moe_router_b[li], (0, LANES - N_EXPERTS)).reshape(1, LANES))
            x1, h, info = _outproj(xf, att, rw, mod, norm2_g[l].reshape(1, D), w_out_b, B, S, l, router)
            xf = _moe(x1, h, info, mod, moe_w_gate, moe_w_up, moe_w_down, final_g2, B, S, l, li,
                      final=(l == L - 1))
    if L % 2 == 1:
        raise NotImplementedError("final norm is fused into the last (expert) layer")
    return xf.reshape(B, S, D)
```

```python
import functools
import math

import numpy as np
import jax
import jax.numpy as jnp
from jax import lax
from jax.experimental import pallas as pl
from jax.experimental.pallas import tpu as pltpu

F32 = jnp.float32
BF16 = jnp.bfloat16

D_MODEL = 1024
HEAD_DIM = 64
ATT_WIDTH = 512
RWKV_WIDTH = 512
N_HEADS = 8
HEAD_PAIRS = 4
LANES = 128
DILATED_GROUPS = ((128, 1), (512, 4), (2048, 16))
N_BUCKETS = 32
MAX_DISTANCE = 2048
NEG_INF = -1e30
D_DECAY_LORA = 64
D_AAA_LORA = 64
D_MV_LORA = 32
D_GATE_LORA = 160
N_IN = 3 * ATT_WIDTH + 3 * RWKV_WIDTH + D_DECAY_LORA + D_AAA_LORA + D_GATE_LORA
QKV_COLS = 3 * ATT_WIDTH
RWM_COLS = 3 * RWKV_WIDTH
RWL_COLS = 384
N_IN_PAD = QKV_COLS + RWM_COLS + RWL_COLS
GN_EPS = HEAD_DIM * 1e-5
RMS_EPS = 1e-6
N_EXPERTS = 8
TOP_K = 2
CHUNK = 64
VMEM_LIMIT = 56 * 1024 * 1024


def _cparams(sem):
    return pltpu.CompilerParams(dimension_semantics=sem, vmem_limit_bytes=VMEM_LIMIT)


def _tile(n, pref):
    t = min(n, pref)
    while n % t:
        t //= 2
    return t


def _mm(a, b):
    return jnp.dot(a.astype(BF16), b.astype(BF16), preferred_element_type=F32)


def _mm_nt(a, b):
    return lax.dot_general(a.astype(BF16), b.astype(BF16), (((1,), (1,)), ((), ())),
                           preferred_element_type=F32)


def _sigmoid(x):
    return 1.0 / (1.0 + jnp.exp(-x))


def _silu(x):
    return x * _sigmoid(x)


def _ada_kernel(c_ref, w_ref, b_ref, o_ref):
    ca = _silu(c_ref[...])
    o_ref[0] = _mm(ca, w_ref[0]) + b_ref[0]


def _ada_mod(c, w_ada, b_ada):
    L, D, N = w_ada.shape
    B = c.shape[0]
    tn = _tile(N, 1536)
    return pl.pallas_call(
        _ada_kernel,
        grid=(L, N // tn),
        in_specs=[pl.BlockSpec((B, D), lambda l, j: (0, 0)),
                  pl.BlockSpec((1, D, tn), lambda l, j: (l, 0, j)),
                  pl.BlockSpec((1, 1, tn), lambda l, j: (l, 0, j))],
        out_specs=pl.BlockSpec((1, B, tn), lambda l, j: (l, 0, j)),
        out_shape=jax.ShapeDtypeStruct((L, B, N), F32),
        compiler_params=_cparams(("arbitrary", "arbitrary")),
        name="ada_mod",
    )(c, w_ada, b_ada.reshape(L, 1, N))


def _rms_mod(xf, g, scale, shift):
    ms = jnp.mean(xf * xf, axis=-1, keepdims=True)
    return (xf * lax.rsqrt(ms + RMS_EPS) * g) * (1.0 + scale) + shift


def _inproj_kernel(x_ref, mod_ref, g_ref, w_ref, mu_ref, qkv_ref, rwm_ref, rwl_ref, carry_ref):
    i = pl.program_id(1)
    tm = x_ref.shape[0]
    h = _rms_mod(x_ref[...], g_ref[...], mod_ref[0, 1:2, :], mod_ref[0, 0:1, :])
    acc = jnp.dot(h.astype(BF16), w_ref[0], preferred_element_type=F32)
    qkv_ref[...] = acc[:, :QKV_COLS].astype(BF16)
    p = acc[:, QKV_COLS:]
    first = jnp.where(i == 0, 0.0, carry_ref[...])
    row = lax.broadcasted_iota(jnp.int32, p.shape, 0)
    prev = jnp.where(row == 0, first, pltpu.roll(p, 1, 0))
    carry_ref[...] = p[tm - 1:tm, :]
    pm = p + mu_ref[...] * (prev - p)
    rwm_ref[...] = pm[:, :RWM_COLS]
    rwl_ref[...] = pm[:, RWM_COLS:]


def _inproj(x, mod, g, w_in_l, mu_l, B, S, l):
    T, D = x.shape
    tm = _tile(S, 512)
    nS = S // tm
    return pl.pallas_call(
        _inproj_kernel,
        grid=(B, nS),
        in_specs=[pl.BlockSpec((tm, D), lambda b, i: (b * nS + i, 0)),
                  pl.BlockSpec((1, 6, D), lambda b, i: (l * B + b, 0, 0)),
                  pl.BlockSpec((1, D), lambda b, i: (0, 0)),
                  pl.BlockSpec((1, D, N_IN_PAD), lambda b, i: (l, 0, 0)),
                  pl.BlockSpec((1, RWM_COLS + RWL_COLS), lambda b, i: (0, 0))],
        out_specs=[pl.BlockSpec((tm, QKV_COLS), lambda b, i: (b * nS + i, 0)),
                   pl.BlockSpec((tm, RWM_COLS), lambda b, i: (b * nS + i, 0)),
                   pl.BlockSpec((tm, RWL_COLS), lambda b, i: (b * nS + i, 0))],
        out_shape=[jax.ShapeDtypeStruct((T, QKV_COLS), BF16),
                   jax.ShapeDtypeStruct((T, RWM_COLS), F32),
                   jax.ShapeDtypeStruct((T, RWL_COLS), F32)],
        scratch_shapes=[pltpu.VMEM((1, RWM_COLS + RWL_COLS), F32)],
        compiler_params=_cparams(("arbitrary", "arbitrary")),
        name="inproj",
    )(x, mod, g, w_in_l, mu_l)


def _t5_bucket(n):
    max_exact = N_BUCKETS // 2
    large = max_exact + (np.log(np.maximum(n, 1) / max_exact) / np.log(MAX_DISTANCE / max_exact)
                         * (N_BUCKETS - max_exact)).astype(np.int32)
    large = np.minimum(large, N_BUCKETS - 1)
    return np.where(n < max_exact, n, large).astype(np.int32)


def _att_bias_table(rel_bias, S, tq):
    off = np.arange(S)
    mult = np.zeros(S, np.int64)
    for w, d in DILATED_GROUPS:
        mult += ((off % d == 0) & (off <= w)).astype(np.int64)
    logm = np.where(mult > 0, np.log(np.maximum(mult, 1)), NEG_INF).astype(np.float32)
    per_off = rel_bias[_t5_bucket(off)].T.astype(F32) + jnp.asarray(logm)[None, :]
    H = per_off.shape[0]
    nd = S // tq
    period = 2 * tq - 1
    pad = jnp.full((H, tq), NEG_INF, F32)
    w = jnp.concatenate([pad, per_off, pad], axis=1)
    z = jnp.stack([w[:, d * tq + 1: d * tq + 1 + period] for d in range(nd)], axis=1)
    flat = jnp.tile(z, (1, 1, tq + 1))[:, :, :tq * (period + 1)]
    hankel = flat.reshape(H, nd, tq, period + 1)[..., :tq]
    return hankel[..., ::-1]


def _attn_kernel(q_ref, k_ref, v_ref, bias_ref, o_ref):
    qi = pl.program_id(2)
    tq = q_ref.shape[0]
    q = q_ref[...] * jnp.asarray(1.0 / math.sqrt(HEAD_DIM), BF16)
    lane = lax.broadcasted_iota(jnp.int32, (tq, LANES), 1)
    outs = []
    for h in range(2):
        in_head = (lane < HEAD_DIM) if h == 0 else (lane >= HEAD_DIM)
        qh = jnp.where(in_head, q, jnp.zeros_like(q))

        def body(j, carry):
            m, l, acc = carry
            start = pl.multiple_of((qi - j) * tq, tq)
            kt = k_ref[pl.ds(start, tq), :]
            vt = v_ref[pl.ds(start, tq), :]
            s = lax.dot_general(qh, kt, (((1,), (1,)), ((), ())), preferred_element_type=F32)
            s = s + bias_ref[h, j]
            m_new = jnp.maximum(m, jnp.max(s, axis=-1, keepdims=True))
            alpha = jnp.exp(m - m_new)
            p = jnp.exp(s - m_new)
            l_new = alpha * l + jnp.sum(p, axis=-1, keepdims=True)
            acc_new = alpha * acc + jnp.dot(p.astype(BF16), vt, preferred_element_type=F32)
            return m_new, l_new, acc_new

        init = (jnp.full((tq, 1), NEG_INF, F32), jnp.zeros((tq, 1), F32),
                jnp.zeros((tq, LANES), F32))
        _, l, acc = lax.fori_loop(0, qi + 1, body, init)
        outs.append(acc / l)
    o_ref[...] = jnp.where(lane < HEAD_DIM, outs[0], outs[1]).astype(o_ref.dtype)


def _attention(qkv, bias_table, B, S):
    T = qkv.shape[0]
    tq = bias_table.shape[-1]
    nq = S // tq
    nd = bias_table.shape[1]
    return pl.pallas_call(
        _attn_kernel,
        grid=(HEAD_PAIRS, B, nq),
        in_specs=[pl.BlockSpec((tq, LANES), lambda hp, b, i: (b * nq + i, hp)),
                  pl.BlockSpec((S, LANES), lambda hp, b, i: (b, HEAD_PAIRS + hp)),
                  pl.BlockSpec((S, LANES), lambda hp, b, i: (b, 2 * HEAD_PAIRS + hp)),
                  pl.BlockSpec((2, nd, tq, tq), lambda hp, b, i: (hp, 0, 0, 0))],
        out_specs=pl.BlockSpec((tq, LANES), lambda hp, b, i: (b * nq + i, hp)),
        out_shape=jax.ShapeDtypeStruct((T, ATT_WIDTH), BF16),
        compiler_params=_cparams(("arbitrary", "arbitrary", "arbitrary")),
        name="dilated_attn",
    )(qkv, qkv, qkv, bias_table)


def _split2(x):
    hi = x.astype(BF16)
    lo = (x - hi.astype(F32)).astype(BF16)
    return hi, lo


def _head_sum(x, m2):
    hi, lo = _split2(x)
    return jnp.dot(jnp.concatenate([hi, lo], axis=1), m2, preferred_element_type=F32)


def _rwkv_kernel(*refs, has_vres, n_chunks):
    if has_vres:
        (rwm_ref, rwl_ref, vf_ref, vec_ref, w2_ref, a2_ref, g2_ref, v1_ref, v2_ref,
         o_ref, state_ref, r_s, k_s, v_s, lw_s, kk_s, kb_s, y_s) = refs
    else:
        (rwm_ref, rwl_ref, vec_ref, w2_ref, a2_ref, g2_ref,
         o_ref, vf_out_ref, state_ref, r_s, k_s, v_s, lw_s, kk_s, kb_s, y_s) = refs
    C = CHUNK
    W = RWKV_WIDTH

    @pl.when(pl.program_id(1) == 0)
    def _():
        state_ref[...] = jnp.zeros_like(state_ref)

    ri = lax.broadcasted_iota(jnp.int32, (2 * LANES, LANES), 0)
    ci = lax.broadcasted_iota(jnp.int32, (2 * LANES, LANES), 1)
    m2 = ((ri % LANES) // HEAD_DIM == ci // HEAD_DIM).astype(BF16)

    vec = vec_ref[...]
    w0, a0, k_k, k_a, r_k, ln_w, ln_b, v0 = [vec[i:i + 1, :] for i in range(8)]

    r = rwm_ref[:, 0:W]
    k = rwm_ref[:, W:2 * W]
    v = rwm_ref[:, 2 * W:3 * W]
    lora = rwl_ref[:, 0:LANES]
    w_raw = w0 + _mm(jnp.tanh(lora), w2_ref[...])
    z = -w_raw
    w_log = -(jnp.maximum(z, 0.0) + jnp.log1p(jnp.exp(-jnp.abs(z)))) - 0.5
    lw_s[...] = -jnp.exp(w_log)
    a = _sigmoid(a0 + _mm(lora, a2_ref[...]))
    g = _mm(_sigmoid(rwl_ref[:, LANES:RWL_COLS]), g2_ref[...])
    if has_vres:
        mix = _sigmoid(v0 + _mm(_mm(v, v1_ref[...]), v2_ref[...]))
        v = v + (vf_ref[...] - v) * mix
    else:
        vf_out_ref[...] = v
    kk = k * k_k
    for p in range(HEAD_PAIRS):
        ls = slice(p * LANES, (p + 1) * LANES)
        kkp = kk[:, ls]
        nrm = jnp.sqrt(_head_sum(kkp * kkp, m2))
        kkp = kkp / jnp.maximum(nrm, 1e-12)
        kk_s[:, ls] = kkp
        kb_s[:, ls] = kkp * a[:, ls]
    kmod = k * (1.0 + (a - 1.0) * k_a)
    r_s[...] = r
    k_s[...] = kmod
    v_s[...] = v

    ri = lax.broadcasted_iota(jnp.int32, (LANES, LANES), 0)
    ci = lax.broadcasted_iota(jnp.int32, (LANES, LANES), 1)
    same_head = (ri // C) == (ci // C)
    strict = same_head & ((ri % C) > (ci % C))
    incl = same_head & ((ri % C) >= (ci % C))
    eye = (ri == ci).astype(F32)
    lane = lax.broadcasted_iota(jnp.int32, (C, LANES), 1)
    head0 = lane < HEAD_DIM
    tri_r = lax.broadcasted_iota(jnp.int32, (C, 3 * C), 0)
    tri_c = lax.broadcasted_iota(jnp.int32, (C, 3 * C), 1)
    tri3 = ((tri_c % C) <= tri_r).astype(BF16)

    def chunk_body(c, _):
        rows = pl.ds(pl.multiple_of(c * C, C), C)
        for p in range(HEAD_PAIRS):
            ls = slice(p * LANES, (p + 1) * LANES)
            rc = r_s[rows, ls]
            kc = k_s[rows, ls]
            vc = v_s[rows, ls]
            lw = lw_s[rows, ls]
            kkc = kk_s[rows, ls]
            kbc = kb_s[rows, ls]
            hi = lw.astype(BF16)
            rem = lw - hi.astype(F32)
            mid = rem.astype(BF16)
            lo = (rem - mid.astype(F32)).astype(BF16)
            cum = jnp.dot(tri3, jnp.concatenate([hi, mid, lo], axis=0), preferred_element_type=F32)
            total = cum[C - 1:C, :]
            g_inv = jnp.exp(-cum)
            a_t = -kkc * jnp.exp(cum - lw)
            r_t = rc * jnp.exp(cum)
            b_t = kbc * g_inv
            k_t = kc * g_inv
            to_end = jnp.exp(total - cum)
            zero = jnp.zeros_like(a_t)
            a_stack = jnp.concatenate([jnp.where(head0, a_t, zero), jnp.where(head0, zero, a_t),
                                       jnp.where(head0, r_t, zero), jnp.where(head0, zero, r_t)], axis=0)
            b_stack = jnp.concatenate([jnp.where(head0, b_t, zero), jnp.where(head0, zero, b_t),
                                       jnp.where(head0, k_t, zero), jnp.where(head0, zero, k_t)], axis=0)
            gram = _mm_nt(a_stack, b_stack)
            n_ab = jnp.where(strict, gram[:LANES, :LANES], 0.0)
            n_ak = jnp.where(strict, gram[:LANES, LANES:], 0.0)
            n_rb = jnp.where(incl, gram[LANES:, :LANES], 0.0)
            n_rk = jnp.where(incl, gram[LANES:, LANES:], 0.0)
            pw = n_ab
            inv = eye + n_ab
            for _ in range(int(math.log2(C)) - 1):
                pw = _mm(pw, pw)
                inv = inv + _mm(inv, pw)
            v_stack = jnp.concatenate([vc, vc], axis=0)
            st = state_ref[p]
            ah = _mm_nt(jnp.concatenate([a_t, r_t], axis=0), st)
            a_h = ah[:C]
            r_h = ah[C:]
            u_stack = _mm(inv, jnp.concatenate([a_h, a_h], axis=0) + _mm(n_ak, v_stack))
            y_stack = (jnp.concatenate([r_h, r_h], axis=0)
                       + _mm(jnp.concatenate([n_rb, n_rk], axis=1),
                             jnp.concatenate([u_stack, v_stack], axis=0)))
            u = jnp.where(head0, u_stack[:C], u_stack[C:])
            y_s[rows, ls] = jnp.where(head0, y_stack[:C], y_stack[C:])
            uv_t = jnp.concatenate([u, vc], axis=0).T
            bk = jnp.concatenate([kbc * to_end, kc * to_end], axis=0)
            st_new = st * jnp.exp(total) + _mm(uv_t, bk)
            state_ref[p] = jnp.where(same_head, st_new, 0.0)
        return 0

    lax.fori_loop(0, n_chunks, chunk_body, 0)

    for p in range(HEAD_PAIRS):
        ls = slice(p * LANES, (p + 1) * LANES)
        y = y_s[:, ls]
        mean = _head_sum(y, m2) * (1.0 / HEAD_DIM)
        d = y - mean
        var = _head_sum(d * d, m2) * (1.0 / HEAD_DIM)
        yn = d * lax.rsqrt(var + GN_EPS) * ln_w[:, ls] + ln_b[:, ls]
        bonus = _head_sum(r_s[:, ls] * k_s[:, ls] * r_k[:, ls], m2) * v_s[:, ls]
        o_ref[:, ls] = ((yn + bonus) * g[:, ls]).astype(o_ref.dtype)


def _rwkv(rwm, rwl, v_first, vec, w2p, a2p, g2p, v1p, v2p, B, S):
    T = rwm.shape[0]
    W = RWKV_WIDTH
    tb = _tile(S, 512)
    nS = S // tb
    has_vres = v_first is not None
    row = lambda b, i: (b * nS + i, 0)
    full = lambda b, i: (0, 0)
    in_specs = [pl.BlockSpec((tb, RWM_COLS), row), pl.BlockSpec((tb, RWL_COLS), row)]
    args = [rwm, rwl]
    if has_vres:
        in_specs.append(pl.BlockSpec((tb, W), row))
        args.append(v_first)
    in_specs += [pl.BlockSpec(vec.shape, full), pl.BlockSpec(w2p.shape, full),
                 pl.BlockSpec(a2p.shape, full), pl.BlockSpec(g2p.shape, full)]
    args += [vec, w2p, a2p, g2p]
    if has_vres:
        in_specs += [pl.BlockSpec(v1p.shape, full), pl.BlockSpec(v2p.shape, full)]
        args += [v1p, v2p]
    out_specs = [pl.BlockSpec((tb, W), row)]
    out_shape = [jax.ShapeDtypeStruct((T, W), BF16)]
    if not has_vres:
        out_specs.append(pl.BlockSpec((tb, W), row))
        out_shape.append(jax.ShapeDtypeStruct((T, W), F32))
    outs = pl.pallas_call(
        functools.partial(_rwkv_kernel, has_vres=has_vres, n_chunks=tb // CHUNK),
        grid=(B, nS),
        in_specs=in_specs,
        out_specs=out_specs,
        out_shape=out_shape,
        scratch_shapes=[pltpu.VMEM((HEAD_PAIRS, LANES, LANES), F32)] + [pltpu.VMEM((tb, W), F32)] * 7,
        compiler_params=_cparams(("arbitrary", "arbitrary")),
        name="rwkv7",
    )(*args)
    return (outs[0], v_first) if has_vres else (outs[0], outs[1])


def _outproj_kernel(*refs, moe):
    if moe:
        (x_ref, att_ref, rw_ref, mod_ref, g_ref, w_ref, rw_w_ref, rw_b_ref,
         x1_ref, h_ref, info_ref) = refs
    else:
        x_ref, att_ref, rw_ref, mod_ref, g_ref, w_ref, x1_ref, h_ref = refs
    mix = (jnp.dot(att_ref[...], w_ref[0, :ATT_WIDTH, :], preferred_element_type=F32)
           + jnp.dot(rw_ref[...], w_ref[0, ATT_WIDTH:, :], preferred_element_type=F32))
    x1 = x_ref[...] + mod_ref[0, 2:3, :] * mix
    x1_ref[...] = x1
    h = _rms_mod(x1, g_ref[...], mod_ref[0, 4:5, :], mod_ref[0, 3:4, :])
    h_ref[...] = h.astype(h_ref.dtype)
    if moe:
        logits = jnp.dot(h, rw_w_ref[...], precision=lax.Precision.HIGHEST,
                         preferred_element_type=F32) + rw_b_ref[...]
        lane = lax.broadcasted_iota(jnp.int32, logits.shape, 1)
        logits = jnp.where(lane < N_EXPERTS, logits, -jnp.inf)
        m1 = jnp.max(logits, axis=-1, keepdims=True)
        i1 = jnp.min(jnp.where(logits == m1, lane, LANES), axis=-1, keepdims=True)
        rest = jnp.where(lane == i1, -jnp.inf, logits)
        m2 = jnp.max(rest, axis=-1, keepdims=True)
        i2 = jnp.min(jnp.where(rest == m2, lane, LANES), axis=-1, keepdims=True)
        e = jnp.exp(m2 - m1)
        g1 = 1.0 / (1.0 + e)
        g2 = e / (1.0 + e)
        info = jnp.where(lane == 0, i1.astype(F32),
                         jnp.where(lane == 1, i2.astype(F32),
                                   jnp.where(lane == 2, g1, jnp.where(lane == 3, g2, 0.0))))
        info_ref[...] = info


def _outproj(x, att, rw, mod, g, w_out_b, B, S, l, router=None):
    T, D = x.shape
    tm = _tile(S, 512)
    nS = S // tm
    moe = router is not None
    row = lambda b, i: (b * nS + i, 0)
    full = lambda b, i: (0, 0)
    in_specs = [pl.BlockSpec((tm, D), row), pl.BlockSpec((tm, ATT_WIDTH), row),
                pl.BlockSpec((tm, RWKV_WIDTH), row),
                pl.BlockSpec((1, 6, D), lambda b, i: (l * B + b, 0, 0)),
                pl.BlockSpec((1, D), full),
                pl.BlockSpec((1, D, D), lambda b, i: (l, 0, 0))]
    args = [x, att, rw, mod, g, w_out_b]
    out_specs = [pl.BlockSpec((tm, D), row), pl.BlockSpec((tm, D), row)]
    out_shape = [jax.ShapeDtypeStruct((T, D), F32), jax.ShapeDtypeStruct((T, D), F32 if moe else BF16)]
    if moe:
        in_specs += [pl.BlockSpec((D, LANES), full), pl.BlockSpec((1, LANES), full)]
        args += list(router)
        out_specs.append(pl.BlockSpec((tm, LANES), row))
        out_shape.append(jax.ShapeDtypeStruct((T, LANES), F32))
    return pl.pallas_call(
        functools.partial(_outproj_kernel, moe=moe),
        grid=(B, nS),
        in_specs=in_specs,
        out_specs=out_specs,
        out_shape=out_shape,
        compiler_params=_cparams(("arbitrary", "arbitrary")),
        name="outproj",
    )(*args)


def _ffn_kernel(x_ref, h_ref, mod_ref, wg_ref, wu_ref, wd_ref, o_ref, acc_ref):
    f = pl.program_id(1)

    @pl.when(f == 0)
    def _():
        acc_ref[...] = jnp.zeros_like(acc_ref)

    hb = h_ref[...]
    gate = jnp.dot(hb, wg_ref[0].astype(BF16), preferred_element_type=F32)
    up = jnp.dot(hb, wu_ref[0].astype(BF16), preferred_element_type=F32)
    mid = (_silu(gate) * up).astype(BF16)
    acc_ref[...] += jnp.dot(mid, wd_ref[0].astype(BF16), preferred_element_type=F32)

    @pl.when(f == pl.num_programs(1) - 1)
    def _():
        o_ref[...] = x_ref[...] + mod_ref[0, 5:6, :] * acc_ref[...]


def _ffn_dense(x1, h, mod, wg, wu, wd, B, S, l, li):
    T, D = x1.shape
    F = wg.shape[-1]
    tm = _tile(S, 1024)
    tf = _tile(F, 256)
    per_seq = S // tm
    return pl.pallas_call(
        _ffn_kernel,
        grid=(T // tm, F // tf),
        in_specs=[pl.BlockSpec((tm, D), lambda i, f: (i, 0)),
                  pl.BlockSpec((tm, D), lambda i, f: (i, 0)),
                  pl.BlockSpec((1, 6, D), lambda i, f: (l * B + i // per_seq, 0, 0)),
                  pl.BlockSpec((1, D, tf), lambda i, f: (li, 0, f)),
                  pl.BlockSpec((1, D, tf), lambda i, f: (li, 0, f)),
                  pl.BlockSpec((1, tf, D), lambda i, f: (li, f, 0))],
        out_specs=pl.BlockSpec((tm, D), lambda i, f: (i, 0)),
        out_shape=jax.ShapeDtypeStruct((T, D), F32),
        scratch_shapes=[pltpu.VMEM((tm, D), F32)],
        compiler_params=_cparams(("arbitrary", "arbitrary")),
        name="ffn_dense",
    )(x1, h, mod, wg, wu, wd)


def _experts_kernel(te_ref, nu_ref, tok_ref, h_ref, wg_ref, wu_ref, wd_ref, y_ref,
                    land_ref, xb_ref, acc_ref, sem):
    i = pl.program_id(0)
    f = pl.program_id(1)
    tm = land_ref.shape[0]
    n_used = nu_ref[0]

    def start_gather(tile):
        base = tile * tm

        def body(j, _):
            pltpu.make_async_copy(h_ref.at[pl.ds(tok_ref[base + j], 1)],
                                  land_ref.at[pl.ds(j, 1)], sem).start()
            return 0

        lax.fori_loop(0, tm, body, 0)

    @pl.when(i < n_used)
    def _():
        @pl.when(f == 0)
        def _():
            @pl.when(i == 0)
            def _():
                start_gather(0)

            pltpu.make_async_copy(h_ref.at[pl.ds(0, tm)], land_ref, sem).wait()
            xb_ref[...] = land_ref[...].astype(BF16)
            acc_ref[...] = jnp.zeros_like(acc_ref)

            @pl.when(i + 1 < n_used)
            def _():
                start_gather(i + 1)

        xb = xb_ref[...]
        gate = jnp.dot(xb, wg_ref[0, 0].astype(BF16), preferred_element_type=F32)
        up = jnp.dot(xb, wu_ref[0, 0].astype(BF16), preferred_element_type=F32)
        mid = (_silu(gate) * up).astype(BF16)
        acc_ref[...] += jnp.dot(mid, wd_ref[0, 0].astype(BF16), preferred_element_type=F32)

        @pl.when(f == pl.num_programs(1) - 1)
        def _():
            y_ref[...] = acc_ref[...]

    @pl.when((i >= n_used) & (f == 0))
    def _():
        y_ref[...] = jnp.zeros_like(y_ref)


def _experts(h, row_tok, tile_e, n_used, wg, wu, wd, li, tm):
    n_rows = row_tok.shape[0]
    D = h.shape[1]
    assert h.shape[0] >= tm
    F = wg.shape[-1]
    tf = _tile(F, 512)
    nf = F // tf

    def rows(i, f, te, nu, tok):
        return (i, 0)

    def wcol(i, f, te, nu, tok):
        return (li, te[jnp.minimum(i, nu[0] - 1)], 0, jnp.where(i < nu[0], f, nf - 1))

    def wrow(i, f, te, nu, tok):
        return (li, te[jnp.minimum(i, nu[0] - 1)], jnp.where(i < nu[0], f, nf - 1), 0)

    return pl.pallas_call(
        _experts_kernel,
        grid_spec=pltpu.PrefetchScalarGridSpec(
            num_scalar_prefetch=3,
            grid=(n_rows // tm, nf),
            in_specs=[pl.BlockSpec(memory_space=pl.ANY),
                      pl.BlockSpec((1, 1, D, tf), wcol),
                      pl.BlockSpec((1, 1, D, tf), wcol),
                      pl.BlockSpec((1, 1, tf, D), wrow)],
            out_specs=pl.BlockSpec((tm, D), rows),
            scratch_shapes=[pltpu.VMEM((tm, D), F32), pltpu.VMEM((tm, D), BF16),
                            pltpu.VMEM((tm, D), F32), pltpu.SemaphoreType.DMA(())]),
        out_shape=jax.ShapeDtypeStruct((n_rows, D), F32),
        compiler_params=_cparams(("arbitrary", "arbitrary")),
        name="moe_experts",
    )(tile_e, n_used, row_tok, h, wg, wu, wd)


def _combine_kernel(pos_ref, y_ref, x_ref, info_ref, mod_ref, fg_ref, o_ref, buf, sem, *, tc, final):
    base = pl.program_id(0) * (2 * tc)

    def copy(j):
        return pltpu.make_async_copy(y_ref.at[pl.ds(pos_ref[base + j], 1)], buf.at[pl.ds(j, 1)], sem)

    def start(j, _):
        copy(j).start()
        return 0

    def wait(j, _):
        copy(j).wait()
        return 0

    lax.fori_loop(0, 2 * tc, start, 0)
    lax.fori_loop(0, 2 * tc, wait, 0)
    info = info_ref[...]
    ff = info[:, 2:3] * buf[0:tc, :] + info[:, 3:4] * buf[tc:2 * tc, :]
    out = x_ref[...] + mod_ref[0, 5:6, :] * ff
    if final:
        ms = jnp.mean(out * out, axis=-1, keepdims=True)
        out = out * lax.rsqrt(ms + RMS_EPS) * fg_ref[...]
    o_ref[...] = out


def _combine(pos, y_rows, x1, info, mod, final_g, B, S, l, final):
    T, D = x1.shape
    tc = _tile(S, 256)
    per_seq = S // tc
    return pl.pallas_call(
        functools.partial(_combine_kernel, tc=tc, final=final),
        grid_spec=pltpu.PrefetchScalarGridSpec(
            num_scalar_prefetch=1,
            grid=(T // tc,),
            in_specs=[pl.BlockSpec(memory_space=pl.ANY),
                      pl.BlockSpec((tc, D), lambda i, pos: (i, 0)),
                      pl.BlockSpec((tc, LANES), lambda i, pos: (i, 0)),
                      pl.BlockSpec((1, 6, D), lambda i, pos: (l * B + i // per_seq, 0, 0)),
                      pl.BlockSpec((1, D), lambda i, pos: (0, 0))],
            out_specs=pl.BlockSpec((tc, D), lambda i, pos: (i, 0)),
            scratch_shapes=[pltpu.VMEM((2 * tc, D), F32), pltpu.SemaphoreType.DMA(())]),
        out_shape=jax.ShapeDtypeStruct((T, D), F32),
        compiler_params=_cparams(("arbitrary",)),
        name="moe_combine",
    )(pos, y_rows, x1, info, mod, final_g)


def _moe(x1, h, info, mod, wg, wu, wd, final_g, B, S, l, li, final):
    T, D = x1.shape
    TK = T * TOP_K
    tm = _tile(TK // N_EXPERTS, 1024)
    n_rows = TK + (N_EXPERTS - 1) * tm
    flat_e = info[:, :TOP_K].astype(jnp.int32).reshape(TK)
    onehot = (flat_e[:, None] == jnp.arange(N_EXPERTS, dtype=jnp.int32)[None, :]).astype(jnp.int32)
    csum = jnp.cumsum(onehot, axis=0)
    counts = csum[-1]
    rank = jnp.sum(onehot * csum, axis=1) - 1
    padded = (counts + tm - 1) // tm * tm
    pad_ends = jnp.cumsum(padded)
    pad_starts = pad_ends - padded
    dest = pad_starts[flat_e] + rank
    row_tok = jnp.zeros((n_rows,), jnp.int32).at[dest].set(jnp.arange(TK, dtype=jnp.int32) // TOP_K)
    tile_start = jnp.arange(n_rows // tm, dtype=jnp.int32) * tm
    tile_e = jnp.minimum(jnp.searchsorted(pad_ends, tile_start, side='right'), N_EXPERTS - 1).astype(jnp.int32)
    n_used_rows = pad_ends[-1:].astype(jnp.int32)
    n_used_tiles = n_used_rows // tm
    tc = _tile(S, 256)
    pos = dest.reshape(T // tc, tc, TOP_K).transpose(0, 2, 1).reshape(TK).astype(jnp.int32)

    y_rows = _experts(h, row_tok, tile_e, n_used_tiles, wg, wu, wd, li, tm)
    return _combine(pos, y_rows, x1, info, mod, final_g, B, S, l, final)


def kernel(x, c, w_ada, b_ada, norm1_g, norm2_g, final_g, w_in, w_out, rel_bias, rwkv_mu, rwkv_w0, rwkv_w2, rwkv_a0, rwkv_a2, rwkv_g2, rwkv_k_k, rwkv_k_a, rwkv_r_k, rwkv_ln_w, rwkv_ln_b, rwkv_v0, rwkv_v1, rwkv_v2, ffn_w_gate, ffn_w_up, ffn_w_down, moe_router_w, moe_router_b, moe_w_gate, moe_w_up, moe_w_down):
    B, S, D = x.shape
    L = w_in.shape[0]
    T = B * S
    W = RWKV_WIDTH

    mod = _ada_mod(c, w_ada, b_ada).reshape(L * B, 6, D)
    w_in_b = jnp.pad(w_in, ((0, 0), (0, 0), (0, N_IN_PAD - N_IN))).astype(BF16)
    w_out_b = w_out.astype(BF16)
    mu_p = jnp.pad(rwkv_mu, ((0, 0), (0, RWM_COLS + RWL_COLS - rwkv_mu.shape[1])))
    tq = _tile(S, 256)
    bias_table = _att_bias_table(rel_bias, S, tq)
    zeros_w = jnp.zeros((1, W), F32)
    final_g2 = final_g.reshape(1, D)

    xf = x.reshape(T, D)
    v_first = None
    for l in range(L):
        qkv, rwm, rwl = _inproj(xf, mod, norm1_g[l].reshape(1, D), w_in_b, mu_p[l].reshape(1, -1), B, S, l)
        att = _attention(qkv, bias_table, B, S)
        v0 = rwkv_v0[l - 1].reshape(1, W) if l > 0 else zeros_w
        vec = jnp.concatenate([rwkv_w0[l].reshape(1, W), rwkv_a0[l].reshape(1, W), rwkv_k_k[l].reshape(1, W),
                               rwkv_k_a[l].reshape(1, W), rwkv_r_k[l].reshape(1, W), rwkv_ln_w[l].reshape(1, W),
                               rwkv_ln_b[l].reshape(1, W), v0], axis=0)
        w2p = jnp.pad(rwkv_w2[l], ((0, LANES - D_DECAY_LORA), (0, 0)))
        a2p = jnp.pad(rwkv_a2[l], ((D_DECAY_LORA, 0), (0, 0)))
        g2p = jnp.pad(rwkv_g2[l], ((0, RWL_COLS - LANES - D_GATE_LORA), (0, 0)))
        if l > 0:
            v1p = jnp.pad(rwkv_v1[l - 1], ((0, 0), (0, LANES - D_MV_LORA)))
            v2p = jnp.pad(rwkv_v2[l - 1], ((0, LANES - D_MV_LORA), (0, 0)))
        else:
            v1p = v2p = None
        rw, v_first = _rwkv(rwm, rwl, v_first, vec, w2p, a2p, g2p, v1p, v2p, B, S)
        li = l // 2
        if l % 2 == 0:
            x1, h = _outproj(xf, att, rw, mod, norm2_g[l].reshape(1, D), w_out_b, B, S, l)
            xf = _ffn_dense(x1, h, mod, ffn_w_gate, ffn_w_up, ffn_w_down, B, S, l, li)
        else:
            router = (jnp.pad(moe_router_w[li], ((0, 0), (0, LANES - N_EXPERTS))),
                      jnp.pad(moe_router_b[li], (0, LANES - N_EXPERTS)).reshape(1, LANES))
            x1, h, info = _outproj(xf, att, rw, mod, norm2_g[l].reshape(1, D), w_out_b, B, S, l, router)
            xf = _moe(x1, h, info, mod, moe_w_gate, moe_w_up, moe_w_down, final_g2, B, S, l, li,
                      final=(l == L - 1))
    if L % 2 == 1:
        raise NotImplementedError("final norm is fused into the last (expert) layer")
    return xf.reshape(B, S, D)
```

```python
import functools
import math

import numpy as np
import jax
import jax.numpy as jnp
from jax import lax
from jax.experimental import pallas as pl
from jax.experimental.pallas import tpu as pltpu

F32 = jnp.float32
BF16 = jnp.bfloat16

D_MODEL = 1024
HEAD_DIM = 64
ATT_WIDTH = 512
RWKV_WIDTH = 512
N_HEADS = 8
HEAD_PAIRS = 4
LANES = 128
DILATED_GROUPS = ((128, 1), (512, 4), (2048, 16))
N_BUCKETS = 32
MAX_DISTANCE = 2048
NEG_INF = -1e30
D_DECAY_LORA = 64
D_AAA_LORA = 64
D_MV_LORA = 32
D_GATE_LORA = 160
N_IN = 3 * ATT_WIDTH + 3 * RWKV_WIDTH + D_DECAY_LORA + D_AAA_LORA + D_GATE_LORA
QKV_COLS = 3 * ATT_WIDTH
RWM_COLS = 3 * RWKV_WIDTH
RWL_COLS = 384
N_IN_PAD = QKV_COLS + RWM_COLS + RWL_COLS
GN_EPS = HEAD_DIM * 1e-5
RMS_EPS = 1e-6
N_EXPERTS = 8
TOP_K = 2
CHUNK = 64
VMEM_LIMIT = 56 * 1024 * 1024


def _cparams(sem):
    return pltpu.CompilerParams(dimension_semantics=sem, vmem_limit_bytes=VMEM_LIMIT)


def _tile(n, pref):
    t = min(n, pref)
    while n % t:
        t //= 2
    return t


def _mm(a, b):
    return jnp.dot(a.astype(BF16), b.astype(BF16), preferred_element_type=F32)


def _mm_nt(a, b):
    return lax.dot_general(a.astype(BF16), b.astype(BF16), (((1,), (1,)), ((), ())),
                           preferred_element_type=F32)


def _sigmoid(x):
    return 1.0 / (1.0 + jnp.exp(-x))


def _silu(x):
    return x * _sigmoid(x)


def _ada_kernel(c_ref, w_ref, b_ref, o_ref):
    ca = _silu(c_ref[...])
    o_ref[0] = _mm(ca, w_ref[0]) + b_ref[0]


def _ada_mod(c, w_ada, b_ada):
    L, D, N = w_ada.shape
    B = c.shape[0]
    tn = _tile(N, 1536)
    return pl.pallas_call(
        _ada_kernel,
        grid=(L, N // tn),
        in_specs=[pl.BlockSpec((B, D), lambda l, j: (0, 0)),
                  pl.BlockSpec((1, D, tn), lambda l, j: (l, 0, j)),
                  pl.BlockSpec((1, 1, tn), lambda l, j: (l, 0, j))],
        out_specs=pl.BlockSpec((1, B, tn), lambda l, j: (l, 0, j)),
        out_shape=jax.ShapeDtypeStruct((L, B, N), F32),
        compiler_params=_cparams(("arbitrary", "arbitrary")),
        name="ada_mod",
    )(c, w_ada, b_ada.reshape(L, 1, N))


def _rms_mod(xf, g, scale, shift):
    ms = jnp.mean(xf * xf, axis=-1, keepdims=True)
    return (xf * lax.rsqrt(ms + RMS_EPS) * g) * (1.0 + scale) + shift


def _inproj_kernel(x_ref, mod_ref, g_ref, w_ref, mu_ref, qkv_ref, rwm_ref, rwl_ref, carry_ref):
    i = pl.program_id(1)
    tm = x_ref.shape[0]
    h = _rms_mod(x_ref[...], g_ref[...], mod_ref[0, 1:2, :], mod_ref[0, 0:1, :])
    acc = jnp.dot(h.astype(BF16), w_ref[0], preferred_element_type=F32)
    qkv_ref[...] = acc[:, :QKV_COLS].astype(BF16)
    p = acc[:, QKV_COLS:]
    first = jnp.where(i == 0, 0.0, carry_ref[...])
    row = lax.broadcasted_iota(jnp.int32, p.shape, 0)
    prev = jnp.where(row == 0, first, pltpu.roll(p, 1, 0))
    carry_ref[...] = p[tm - 1:tm, :]
    pm = p + mu_ref[...] * (prev - p)
    rwm_ref[...] = pm[:, :RWM_COLS]
    rwl_ref[...] = pm[:, RWM_COLS:]


def _inproj(x, mod, g, w_in_l, mu_l, B, S, l):
    T, D = x.shape
    tm = _tile(S, 512)
    nS = S // tm
    return pl.pallas_call(
        _inproj_kernel,
        grid=(B, nS),
        in_specs=[pl.BlockSpec((tm, D), lambda b, i: (b * nS + i, 0)),
                  pl.BlockSpec((1, 6, D), lambda b, i: (l * B + b, 0, 0)),
                  pl.BlockSpec((1, D), lambda b, i: (0, 0)),
                  pl.BlockSpec((1, D, N_IN_PAD), lambda b, i: (l, 0, 0)),
                  pl.BlockSpec((1, RWM_COLS + RWL_COLS), lambda b, i: (0, 0))],
        out_specs=[pl.BlockSpec((tm, QKV_COLS), lambda b, i: (b * nS + i, 0)),
                   pl.BlockSpec((tm, RWM_COLS), lambda b, i: (b * nS + i, 0)),
                   pl.BlockSpec((tm, RWL_COLS), lambda b, i: (b * nS + i, 0))],
        out_shape=[jax.ShapeDtypeStruct((T, QKV_COLS), BF16),
                   jax.ShapeDtypeStruct((T, RWM_COLS), F32),
                   jax.ShapeDtypeStruct((T, RWL_COLS), F32)],
        scratch_shapes=[pltpu.VMEM((1, RWM_COLS + RWL_COLS), F32)],
        compiler_params=_cparams(("arbitrary", "arbitrary")),
        name="inproj",
    )(x, mod, g, w_in_l, mu_l)


def _t5_bucket(n):
    max_exact = N_BUCKETS // 2
    large = max_exact + (np.log(np.maximum(n, 1) / max_exact) / np.log(MAX_DISTANCE / max_exact)
                         * (N_BUCKETS - max_exact)).astype(np.int32)
    large = np.minimum(large, N_BUCKETS - 1)
    return np.where(n < max_exact, n, large).astype(np.int32)


def _att_bias_table(rel_bias, S, tq, tk):
    off = np.arange(S)
    mult = np.zeros(S, np.int64)
    for w, d in DILATED_GROUPS:
        mult += ((off % d == 0) & (off <= w)).astype(np.int64)
    logm = np.where(mult > 0, np.log(np.maximum(mult, 1)), NEG_INF).astype(np.float32)
    per_off = rel_bias[_t5_bucket(off)].T.astype(F32) + jnp.asarray(logm)[None, :]
    H = per_off.shape[0]
    nd = S // tq
    period = tq + tk - 1
    w = jnp.concatenate([jnp.full((H, tk - 1), NEG_INF, F32), per_off,
                         jnp.full((H, tq), NEG_INF, F32)], axis=1)
    z = jnp.stack([w[:, d * tq: d * tq + period] for d in range(nd)], axis=1)
    flat = jnp.tile(z, (1, 1, tq + 1))[:, :, :tq * (period + 1)]
    hankel = flat.reshape(H, nd, tq, period + 1)[..., :tk]
    return hankel[..., ::-1]


def _attn_kernel(q_ref, k_ref, v_ref, bias_ref, o_ref):
    qi = pl.program_id(2)
    tq = q_ref.shape[0]
    tk = bias_ref.shape[-1]
    q = q_ref[...] * jnp.asarray(1.0 / math.sqrt(HEAD_DIM), BF16)
    head0 = lax.broadcasted_iota(jnp.int32, (tq, LANES), 1) < HEAD_DIM
    zero = jnp.zeros_like(q)
    qh = (jnp.where(head0, q, zero), jnp.where(head0, zero, q))
    last_kt = (qi * tq) // tk

    def body(j, carry):
        kt_idx = last_kt - j
        start = pl.multiple_of(kt_idx * tk, tk)
        kt = k_ref[pl.ds(start, tk), :]
        vt = v_ref[pl.ds(start, tk), :]
        delta = qi - kt_idx * (tk // tq)
        new = []
        for h in range(2):
            m, l, acc = carry[h]
            s = lax.dot_general(qh[h], kt, (((1,), (1,)), ((), ())), preferred_element_type=F32)
            s = s + bias_ref[h, delta]
            m_new = jnp.maximum(m, jnp.max(s, axis=-1, keepdims=True))
            alpha = jnp.exp(m - m_new)
            p = jnp.exp(s - m_new)
            l_new = alpha * l + jnp.sum(p, axis=-1, keepdims=True)
            acc_new = alpha * acc + jnp.dot(p.astype(BF16), vt, preferred_element_type=F32)
            new.append((m_new, l_new, acc_new))
        return tuple(new)

    init = (jnp.full((tq, 1), NEG_INF, F32), jnp.zeros((tq, 1), F32), jnp.zeros((tq, LANES), F32))
    res = lax.fori_loop(0, last_kt + 1, body, (init, init))
    outs = [acc / l for (_, l, acc) in res]
    o_ref[...] = jnp.where(head0, outs[0], outs[1]).astype(o_ref.dtype)


def _attention(qkv, bias_table, B, S):
    T = qkv.shape[0]
    _, nd, tq, tk = bias_table.shape
    nq = S // tq
    return pl.pallas_call(
        _attn_kernel,
        grid=(HEAD_PAIRS, B, nq),
        in_specs=[pl.BlockSpec((tq, LANES), lambda hp, b, i: (b * nq + i, hp)),
                  pl.BlockSpec((S, LANES), lambda hp, b, i: (b, HEAD_PAIRS + hp)),
                  pl.BlockSpec((S, LANES), lambda hp, b, i: (b, 2 * HEAD_PAIRS + hp)),
                  pl.BlockSpec((2, nd, tq, tk), lambda hp, b, i: (hp, 0, 0, 0))],
        out_specs=pl.BlockSpec((tq, LANES), lambda hp, b, i: (b * nq + i, hp)),
        out_shape=jax.ShapeDtypeStruct((T, ATT_WIDTH), BF16),
        compiler_params=_cparams(("arbitrary", "arbitrary", "arbitrary")),
        name="dilated_attn",
    )(qkv, qkv, qkv, bias_table)


def _split2(x):
    hi = x.astype(BF16)
    lo = (x - hi.astype(F32)).astype(BF16)
    return hi, lo


def _head_sum(x, m2):
    hi, lo = _split2(x)
    return jnp.dot(jnp.concatenate([hi, lo], axis=1), m2, preferred_element_type=F32)


def _rwkv_kernel(*refs, has_vres, n_chunks):
    if has_vres:
        (rwm_ref, rwl_ref, vf_ref, vec_ref, w2_ref, a2_ref, g2_ref, v1_ref, v2_ref,
         o_ref, state_ref, r_s, k_s, v_s, lw_s, kk_s, kb_s, y_s) = refs
    else:
        (rwm_ref, rwl_ref, vec_ref, w2_ref, a2_ref, g2_ref,
         o_ref, vf_out_ref, state_ref, r_s, k_s, v_s, lw_s, kk_s, kb_s, y_s) = refs
    C = CHUNK
    W = RWKV_WIDTH

    @pl.when(pl.program_id(1) == 0)
    def _():
        state_ref[...] = jnp.zeros_like(state_ref)

    ri = lax.broadcasted_iota(jnp.int32, (2 * LANES, LANES), 0)
    ci = lax.broadcasted_iota(jnp.int32, (2 * LANES, LANES), 1)
    m2 = ((ri % LANES) // HEAD_DIM == ci // HEAD_DIM).astype(BF16)

    vec = vec_ref[...]
    w0, a0, k_k, k_a, r_k, ln_w, ln_b, v0 = [vec[i:i + 1, :] for i in range(8)]

    r = rwm_ref[:, 0:W]
    k = rwm_ref[:, W:2 * W]
    v = rwm_ref[:, 2 * W:3 * W]
    lora = rwl_ref[:, 0:LANES]
    w_raw = w0 + _mm(jnp.tanh(lora), w2_ref[...])
    z = -w_raw
    w_log = -(jnp.maximum(z, 0.0) + jnp.log1p(jnp.exp(-jnp.abs(z)))) - 0.5
    lw_s[...] = -jnp.exp(w_log)
    a = _sigmoid(a0 + _mm(lora, a2_ref[...]))
    g = _mm(_sigmoid(rwl_ref[:, LANES:RWL_COLS]), g2_ref[...])
    if has_vres:
        mix = _sigmoid(v0 + _mm(_mm(v, v1_ref[...]), v2_ref[...]))
        v = v + (vf_ref[...] - v) * mix
    else:
        vf_out_ref[...] = v
    kk = k * k_k
    for p in range(HEAD_PAIRS):
        ls = slice(p * LANES, (p + 1) * LANES)
        kkp = kk[:, ls]
        nrm = jnp.sqrt(_head_sum(kkp * kkp, m2))
        kkp = kkp / jnp.maximum(nrm, 1e-12)
        kk_s[:, ls] = kkp
        kb_s[:, ls] = kkp * a[:, ls]
    kmod = k * (1.0 + (a - 1.0) * k_a)
    r_s[...] = r
    k_s[...] = kmod
    v_s[...] = v

    ri = lax.broadcasted_iota(jnp.int32, (LANES, LANES), 0)
    ci = lax.broadcasted_iota(jnp.int32, (LANES, LANES), 1)
    same_head = (ri // C) == (ci // C)
    strict = same_head & ((ri % C) > (ci % C))
    incl = same_head & ((ri % C) >= (ci % C))
    eye = (ri == ci).astype(F32)
    lane = lax.broadcasted_iota(jnp.int32, (C, LANES), 1)
    head0 = lane < HEAD_DIM
    tri_r = lax.broadcasted_iota(jnp.int32, (C, 3 * C), 0)
    tri_c = lax.broadcasted_iota(jnp.int32, (C, 3 * C), 1)
    tri3 = ((tri_c % C) <= tri_r).astype(BF16)

    def chunk_body(c, _):
        rows = pl.ds(pl.multiple_of(c * C, C), C)
        pairs = range(HEAD_PAIRS)
        lss = [slice(p * LANES, (p + 1) * LANES) for p in pairs]
        rc = [r_s[rows, ls] for ls in lss]
        kc = [k_s[rows, ls] for ls in lss]
        vc = [v_s[rows, ls] for ls in lss]
        lw = [lw_s[rows, ls] for ls in lss]
        kkc = [kk_s[rows, ls] for ls in lss]
        kbc = [kb_s[rows, ls] for ls in lss]

        def cumsum(x):
            hi = x.astype(BF16)
            rem = x - hi.astype(F32)
            mid = rem.astype(BF16)
            lo = (rem - mid.astype(F32)).astype(BF16)
            return jnp.dot(tri3, jnp.concatenate([hi, mid, lo], axis=0), preferred_element_type=F32)

        def per_head_rows(x, y):
            zero = jnp.zeros_like(x)
            return jnp.concatenate([jnp.where(head0, x, zero), jnp.where(head0, zero, x),
                                    jnp.where(head0, y, zero), jnp.where(head0, zero, y)], axis=0)

        cum = [cumsum(lw[p]) for p in pairs]
        total = [cum[p][C - 1:C, :] for p in pairs]
        g_inv = [jnp.exp(-cum[p]) for p in pairs]
        a_t = [-kkc[p] * jnp.exp(cum[p] - lw[p]) for p in pairs]
        r_t = [rc[p] * jnp.exp(cum[p]) for p in pairs]
        b_t = [kbc[p] * g_inv[p] for p in pairs]
        k_t = [kc[p] * g_inv[p] for p in pairs]
        to_end = [jnp.exp(total[p] - cum[p]) for p in pairs]
        gram = [_mm_nt(per_head_rows(a_t[p], r_t[p]), per_head_rows(b_t[p], k_t[p])) for p in pairs]
        n_ab = [jnp.where(strict, gram[p][:LANES, :LANES], 0.0) for p in pairs]
        n_ak = [jnp.where(strict, gram[p][:LANES, LANES:], 0.0) for p in pairs]
        n_rb = [jnp.where(incl, gram[p][LANES:, :LANES], 0.0) for p in pairs]
        n_rk = [jnp.where(incl, gram[p][LANES:, LANES:], 0.0) for p in pairs]
        pw = n_ab
        inv = [eye + n_ab[p] for p in pairs]
        for _ in range(int(math.log2(C)) - 1):
            pw = [_mm(pw[p], pw[p]) for p in pairs]
            inv = [inv[p] + _mm(inv[p], pw[p]) for p in pairs]
        v_stack = [jnp.concatenate([vc[p], vc[p]], axis=0) for p in pairs]
        akv = [_mm(n_ak[p], v_stack[p]) for p in pairs]
        st = [state_ref[p] for p in pairs]
        ah = [_mm_nt(jnp.concatenate([a_t[p], r_t[p]], axis=0), st[p]) for p in pairs]
        u_stack = [_mm(inv[p], jnp.concatenate([ah[p][:C], ah[p][:C]], axis=0) + akv[p]) for p in pairs]
        y_stack = [jnp.concatenate([ah[p][C:], ah[p][C:]], axis=0)
                   + _mm(jnp.concatenate([n_rb[p], n_rk[p]], axis=1),
                         jnp.concatenate([u_stack[p], v_stack[p]], axis=0)) for p in pairs]
        for p in pairs:
            y_s[rows, lss[p]] = jnp.where(head0, y_stack[p][:C], y_stack[p][C:])
        uv_t = [jnp.concatenate([jnp.where(head0, u_stack[p][:C], u_stack[p][C:]), vc[p]], axis=0).T
                for p in pairs]
        bk = [jnp.concatenate([kbc[p] * to_end[p], kc[p] * to_end[p]], axis=0) for p in pairs]
        st_new = [st[p] * jnp.exp(total[p]) + _mm(uv_t[p], bk[p]) for p in pairs]
        for p in pairs:
            state_ref[p] = jnp.where(same_head, st_new[p], 0.0)
        return 0

    lax.fori_loop(0, n_chunks, chunk_body, 0)

    for p in range(HEAD_PAIRS):
        ls = slice(p * LANES, (p + 1) * LANES)
        y = y_s[:, ls]
        mean = _head_sum(y, m2) * (1.0 / HEAD_DIM)
        d = y - mean
        var = _head_sum(d * d, m2) * (1.0 / HEAD_DIM)
        yn = d * lax.rsqrt(var + GN_EPS) * ln_w[:, ls] + ln_b[:, ls]
        bonus = _head_sum(r_s[:, ls] * k_s[:, ls] * r_k[:, ls], m2) * v_s[:, ls]
        o_ref[:, ls] = ((yn + bonus) * g[:, ls]).astype(o_ref.dtype)


def _rwkv(rwm, rwl, v_first, vec, w2p, a2p, g2p, v1p, v2p, B, S):
    T = rwm.shape[0]
    W = RWKV_WIDTH
    tb = _tile(S, 512)
    nS = S // tb
    has_vres = v_first is not None
    row = lambda b, i: (b * nS + i, 0)
    full = lambda b, i: (0, 0)
    in_specs = [pl.BlockSpec((tb, RWM_COLS), row), pl.BlockSpec((tb, RWL_COLS), row)]
    args = [rwm, rwl]
    if has_vres:
        in_specs.append(pl.BlockSpec((tb, W), row))
        args.append(v_first)
    in_specs += [pl.BlockSpec(vec.shape, full), pl.BlockSpec(w2p.shape, full),
                 pl.BlockSpec(a2p.shape, full), pl.BlockSpec(g2p.shape, full)]
    args += [vec, w2p, a2p, g2p]
    if has_vres:
        in_specs += [pl.BlockSpec(v1p.shape, full), pl.BlockSpec(v2p.shape, full)]
        args += [v1p, v2p]
    out_specs = [pl.BlockSpec((tb, W), row)]
    out_shape = [jax.ShapeDtypeStruct((T, W), BF16)]
    if not has_vres:
        out_specs.append(pl.BlockSpec((tb, W), row))
        out_shape.append(jax.ShapeDtypeStruct((T, W), F32))
    outs = pl.pallas_call(
        functools.partial(_rwkv_kernel, has_vres=has_vres, n_chunks=tb // CHUNK),
        grid=(B, nS),
        in_specs=in_specs,
        out_specs=out_specs,
        out_shape=out_shape,
        scratch_shapes=[pltpu.VMEM((HEAD_PAIRS, LANES, LANES), F32)] + [pltpu.VMEM((tb, W), F32)] * 7,
        compiler_params=_cparams(("arbitrary", "arbitrary")),
        name="rwkv7",
    )(*args)
    return (outs[0], v_first) if has_vres else (outs[0], outs[1])


def _outproj_kernel(*refs, moe):
    if moe:
        (x_ref, att_ref, rw_ref, mod_ref, g_ref, w_ref, rw_w_ref, rw_b_ref,
         x1_ref, h_ref, info_ref) = refs
    else:
        x_ref, att_ref, rw_ref, mod_ref, g_ref, w_ref, x1_ref, h_ref = refs
    mix = (jnp.dot(att_ref[...], w_ref[0, :ATT_WIDTH, :], preferred_element_type=F32)
           + jnp.dot(rw_ref[...], w_ref[0, ATT_WIDTH:, :], preferred_element_type=F32))
    x1 = x_ref[...] + mod_ref[0, 2:3, :] * mix
    x1_ref[...] = x1
    h = _rms_mod(x1, g_ref[...], mod_ref[0, 4:5, :], mod_ref[0, 3:4, :])
    h_ref[...] = h.astype(h_ref.dtype)
    if moe:
        logits = jnp.dot(h, rw_w_ref[...], precision=lax.Precision.HIGHEST,
                         preferred_element_type=F32) + rw_b_ref[...]
        lane = lax.broadcasted_iota(jnp.int32, logits.shape, 1)
        logits = jnp.where(lane < N_EXPERTS, logits, -jnp.inf)
        m1 = jnp.max(logits, axis=-1, keepdims=True)
        i1 = jnp.min(jnp.where(logits == m1, lane, LANES), axis=-1, keepdims=True)
        rest = jnp.where(lane == i1, -jnp.inf, logits)
        m2 = jnp.max(rest, axis=-1, keepdims=True)
        i2 = jnp.min(jnp.where(rest == m2, lane, LANES), axis=-1, keepdims=True)
        e = jnp.exp(m2 - m1)
        g1 = 1.0 / (1.0 + e)
        g2 = e / (1.0 + e)
        info = jnp.where(lane == 0, i1.astype(F32),
                         jnp.where(lane == 1, i2.astype(F32),
                                   jnp.where(lane == 2, g1, jnp.where(lane == 3, g2, 0.0))))
        info_ref[...] = info


def _outproj(x, att, rw, mod, g, w_out_b, B, S, l, router=None):
    T, D = x.shape
    tm = _tile(S, 512)
    nS = S // tm
    moe = router is not None
    row = lambda b, i: (b * nS + i, 0)
    full = lambda b, i: (0, 0)
    in_specs = [pl.BlockSpec((tm, D), row), pl.BlockSpec((tm, ATT_WIDTH), row),
                pl.BlockSpec((tm, RWKV_WIDTH), row),
                pl.BlockSpec((1, 6, D), lambda b, i: (l * B + b, 0, 0)),
                pl.BlockSpec((1, D), full),
                pl.BlockSpec((1, D, D), lambda b, i: (l, 0, 0))]
    args = [x, att, rw, mod, g, w_out_b]
    out_specs = [pl.BlockSpec((tm, D), row), pl.BlockSpec((tm, D), row)]
    out_shape = [jax.ShapeDtypeStruct((T, D), F32), jax.ShapeDtypeStruct((T, D), F32 if moe else BF16)]
    if moe:
        in_specs += [pl.BlockSpec((D, LANES), full), pl.BlockSpec((1, LANES), full)]
        args += list(router)
        out_specs.append(pl.BlockSpec((tm, LANES), row))
        out_shape.append(jax.ShapeDtypeStruct((T, LANES), F32))
    return pl.pallas_call(
        functools.partial(_outproj_kernel, moe=moe),
        grid=(B, nS),
        in_specs=in_specs,
        out_specs=out_specs,
        out_shape=out_shape,
        compiler_params=_cparams(("arbitrary", "arbitrary")),
        name="outproj",
    )(*args)


def _ffn_kernel(x_ref, h_ref, mod_ref, wg_ref, wu_ref, wd_ref, o_ref, acc_ref):
    f = pl.program_id(1)

    @pl.when(f == 0)
    def _():
        acc_ref[...] = jnp.zeros_like(acc_ref)

    hb = h_ref[...]
    gate = jnp.dot(hb, wg_ref[0].astype(BF16), preferred_element_type=F32)
    up = jnp.dot(hb, wu_ref[0].astype(BF16), preferred_element_type=F32)
    mid = (_silu(gate) * up).astype(BF16)
    acc_ref[...] += jnp.dot(mid, wd_ref[0].astype(BF16), preferred_element_type=F32)

    @pl.when(f == pl.num_programs(1) - 1)
    def _():
        o_ref[...] = x_ref[...] + mod_ref[0, 5:6, :] * acc_ref[...]


def _ffn_dense(x1, h, mod, wg, wu, wd, B, S, l, li):
    T, D = x1.shape
    F = wg.shape[-1]
    tm = _tile(S, 1024)
    tf = _tile(F, 256)
    per_seq = S // tm
    return pl.pallas_call(
        _ffn_kernel,
        grid=(T // tm, F // tf),
        in_specs=[pl.BlockSpec((tm, D), lambda i, f: (i, 0)),
                  pl.BlockSpec((tm, D), lambda i, f: (i, 0)),
                  pl.BlockSpec((1, 6, D), lambda i, f: (l * B + i // per_seq, 0, 0)),
                  pl.BlockSpec((1, D, tf), lambda i, f: (li, 0, f)),
                  pl.BlockSpec((1, D, tf), lambda i, f: (li, 0, f)),
                  pl.BlockSpec((1, tf, D), lambda i, f: (li, f, 0))],
        out_specs=pl.BlockSpec((tm, D), lambda i, f: (i, 0)),
        out_shape=jax.ShapeDtypeStruct((T, D), F32),
        scratch_shapes=[pltpu.VMEM((tm, D), F32)],
        compiler_params=_cparams(("arbitrary", "arbitrary")),
        name="ffn_dense",
    )(x1, h, mod, wg, wu, wd)


def _experts_kernel(te_ref, nu_ref, tok_ref, h_ref, wg_ref, wu_ref, wd_ref, y_ref,
                    land_ref, xb_ref, acc_ref, sem):
    i = pl.program_id(0)
    f = pl.program_id(1)
    tm = land_ref.shape[0]
    n_used = nu_ref[0]

    def start_gather(tile):
        base = tile * tm

        def body(j, _):
            pltpu.make_async_copy(h_ref.at[pl.ds(tok_ref[base + j], 1)],
                                  land_ref.at[pl.ds(j, 1)], sem).start()
            return 0

        lax.fori_loop(0, tm, body, 0)

    @pl.when(i < n_used)
    def _():
        @pl.when(f == 0)
        def _():
            @pl.when(i == 0)
            def _():
                start_gather(0)

            pltpu.make_async_copy(h_ref.at[pl.ds(0, tm)], land_ref, sem).wait()
            xb_ref[...] = land_ref[...].astype(BF16)
            acc_ref[...] = jnp.zeros_like(acc_ref)

            @pl.when(i + 1 < n_used)
            def _():
                start_gather(i + 1)

        xb = xb_ref[...]
        gate = jnp.dot(xb, wg_ref[0, 0].astype(BF16), preferred_element_type=F32)
        up = jnp.dot(xb, wu_ref[0, 0].astype(BF16), preferred_element_type=F32)
        mid = (_silu(gate) * up).astype(BF16)
        acc_ref[...] += jnp.dot(mid, wd_ref[0, 0].astype(BF16), preferred_element_type=F32)

        @pl.when(f == pl.num_programs(1) - 1)
        def _():
            y_ref[...] = acc_ref[...]

    @pl.when((i >= n_used) & (f == 0))
    def _():
        y_ref[...] = jnp.zeros_like(y_ref)


def _experts(h, row_tok, tile_e, n_used, wg, wu, wd, li, tm):
    n_rows = row_tok.shape[0]
    D = h.shape[1]
    assert h.shape[0] >= tm
    F = wg.shape[-1]
    tf = _tile(F, 512)
    nf = F // tf

    def rows(i, f, te, nu, tok):
        return (i, 0)

    def wcol(i, f, te, nu, tok):
        return (li, te[jnp.minimum(i, nu[0] - 1)], 0, jnp.where(i < nu[0], f, nf - 1))

    def wrow(i, f, te, nu, tok):
        return (li, te[jnp.minimum(i, nu[0] - 1)], jnp.where(i < nu[0], f, nf - 1), 0)

    return pl.pallas_call(
        _experts_kernel,
        grid_spec=pltpu.PrefetchScalarGridSpec(
            num_scalar_prefetch=3,
            grid=(n_rows // tm, nf),
            in_specs=[pl.BlockSpec(memory_space=pl.ANY),
                      pl.BlockSpec((1, 1, D, tf), wcol),
                      pl.BlockSpec((1, 1, D, tf), wcol),
                      pl.BlockSpec((1, 1, tf, D), wrow)],
            out_specs=pl.BlockSpec((tm, D), rows),
            scratch_shapes=[pltpu.VMEM((tm, D), F32), pltpu.VMEM((tm, D), BF16),
                            pltpu.VMEM((tm, D), F32), pltpu.SemaphoreType.DMA(())]),
        out_shape=jax.ShapeDtypeStruct((n_rows, D), F32),
        compiler_params=_cparams(("arbitrary", "arbitrary")),
        name="moe_experts",
    )(tile_e, n_used, row_tok, h, wg, wu, wd)


def _combine_kernel(pos_ref, y_ref, x_ref, info_ref, mod_ref, fg_ref, o_ref, buf, sem, *, tc, final):
    i = pl.program_id(0)
    slot = i % 2

    def start_gather(tile, dst_slot):
        base = tile * (2 * tc)

        def body(j, _):
            pltpu.make_async_copy(y_ref.at[pl.ds(pos_ref[base + j], 1)],
                                  buf.at[dst_slot, pl.ds(j, 1)], sem.at[dst_slot]).start()
            return 0

        lax.fori_loop(0, 2 * tc, body, 0)

    @pl.when(i == 0)
    def _():
        start_gather(0, 0)

    @pl.when(i + 1 < pl.num_programs(0))
    def _():
        start_gather(i + 1, 1 - slot)

    pltpu.make_async_copy(y_ref.at[pl.ds(0, 2 * tc)], buf.at[slot], sem.at[slot]).wait()
    info = info_ref[...]
    ff = info[:, 2:3] * buf[slot, 0:tc, :] + info[:, 3:4] * buf[slot, tc:2 * tc, :]
    out = x_ref[...] + mod_ref[0, 5:6, :] * ff
    if final:
        ms = jnp.mean(out * out, axis=-1, keepdims=True)
        out = out * lax.rsqrt(ms + RMS_EPS) * fg_ref[...]
    o_ref[...] = out


def _combine(pos, y_rows, x1, info, mod, final_g, B, S, l, final):
    T, D = x1.shape
    tc = _tile(S, 256)
    per_seq = S // tc
    return pl.pallas_call(
        functools.partial(_combine_kernel, tc=tc, final=final),
        grid_spec=pltpu.PrefetchScalarGridSpec(
            num_scalar_prefetch=1,
            grid=(T // tc,),
            in_specs=[pl.BlockSpec(memory_space=pl.ANY),
                      pl.BlockSpec((tc, D), lambda i, pos: (i, 0)),
                      pl.BlockSpec((tc, LANES), lambda i, pos: (i, 0)),
                      pl.BlockSpec((1, 6, D), lambda i, pos: (l * B + i // per_seq, 0, 0)),
                      pl.BlockSpec((1, D), lambda i, pos: (0, 0))],
            out_specs=pl.BlockSpec((tc, D), lambda i, pos: (i, 0)),
            scratch_shapes=[pltpu.VMEM((2, 2 * tc, D), F32), pltpu.SemaphoreType.DMA((2,))]),
        out_shape=jax.ShapeDtypeStruct((T, D), F32),
        compiler_params=_cparams(("arbitrary",)),
        name="moe_combine",
    )(pos, y_rows, x1, info, mod, final_g)


def _moe(x1, h, info, mod, wg, wu, wd, final_g, B, S, l, li, final):
    T, D = x1.shape
    TK = T * TOP_K
    tm = _tile(TK // N_EXPERTS, 1024)
    n_rows = TK + (N_EXPERTS - 1) * tm
    flat_e = info[:, :TOP_K].astype(jnp.int32).reshape(TK)
    onehot = (flat_e[:, None] == jnp.arange(N_EXPERTS, dtype=jnp.int32)[None, :]).astype(jnp.int32)
    csum = jnp.cumsum(onehot, axis=0)
    counts = csum[-1]
    rank = jnp.sum(onehot * csum, axis=1) - 1
    padded = (counts + tm - 1) // tm * tm
    pad_ends = jnp.cumsum(padded)
    pad_starts = pad_ends - padded
    dest = pad_starts[flat_e] + rank
    row_tok = jnp.zeros((n_rows,), jnp.int32).at[dest].set(jnp.arange(TK, dtype=jnp.int32) // TOP_K)
    tile_start = jnp.arange(n_rows // tm, dtype=jnp.int32) * tm
    tile_e = jnp.minimum(jnp.searchsorted(pad_ends, tile_start, side='right'), N_EXPERTS - 1).astype(jnp.int32)
    n_used_rows = pad_ends[-1:].astype(jnp.int32)
    n_used_tiles = n_used_rows // tm
    tc = _tile(S, 256)
    pos = dest.reshape(T // tc, tc, TOP_K).transpose(0, 2, 1).reshape(TK).astype(jnp.int32)

    y_rows = _experts(h, row_tok, tile_e, n_used_tiles, wg, wu, wd, li, tm)
    return _combine(pos, y_rows, x1, info, mod, final_g, B, S, l, final)


def kernel(x, c, w_ada, b_ada, norm1_g, norm2_g, final_g, w_in, w_out, rel_bias, rwkv_mu, rwkv_w0, rwkv_w2, rwkv_a0, rwkv_a2, rwkv_g2, rwkv_k_k, rwkv_k_a, rwkv_r_k, rwkv_ln_w, rwkv_ln_b, rwkv_v0, rwkv_v1, rwkv_v2, ffn_w_gate, ffn_w_up, ffn_w_down, moe_router_w, moe_router_b, moe_w_gate, moe_w_up, moe_w_down):
    B, S, D = x.shape
    L = w_in.shape[0]
    T = B * S
    W = RWKV_WIDTH

    mod = _ada_mod(c, w_ada, b_ada).reshape(L * B, 6, D)
    w_in_b = jnp.pad(w_in, ((0, 0), (0, 0), (0, N_IN_PAD - N_IN))).astype(BF16)
    w_out_b = w_out.astype(BF16)
    mu_p = jnp.pad(rwkv_mu, ((0, 0), (0, RWM_COLS + RWL_COLS - rwkv_mu.shape[1])))
    bias_table = _att_bias_table(rel_bias, S, _tile(S, 256), _tile(S, 512))
    zeros_w = jnp.zeros((1, W), F32)
    final_g2 = final_g.reshape(1, D)

    xf = x.reshape(T, D)
    v_first = None
    for l in range(L):
        qkv, rwm, rwl = _inproj(xf, mod, norm1_g[l].reshape(1, D), w_in_b, mu_p[l].reshape(1, -1), B, S, l)
        att = _attention(qkv, bias_table, B, S)
        v0 = rwkv_v0[l - 1].reshape(1, W) if l > 0 else zeros_w
        vec = jnp.concatenate([rwkv_w0[l].reshape(1, W), rwkv_a0[l].reshape(1, W), rwkv_k_k[l].reshape(1, W),
                               rwkv_k_a[l].reshape(1, W), rwkv_r_k[l].reshape(1, W), rwkv_ln_w[l].reshape(1, W),
                               rwkv_ln_b[l].reshape(1, W), v0], axis=0)
        w2p = jnp.pad(rwkv_w2[l], ((0, LANES - D_DECAY_LORA), (0, 0)))
        a2p = jnp.pad(rwkv_a2[l], ((D_DECAY_LORA, 0), (0, 0)))
        g2p = jnp.pad(rwkv_g2[l], ((0, RWL_COLS - LANES - D_GATE_LORA), (0, 0)))
        if l > 0:
            v1p = jnp.pad(rwkv_v1[l - 1], ((0, 0), (0, LANES - D_MV_LORA)))
            v2p = jnp.pad(rwkv_v2[l - 1], ((0, LANES - D_MV_LORA), (0, 0)))
        else:
            v1p = v2p = None
        rw, v_first = _rwkv(rwm, rwl, v_first, vec, w2p, a2p, g2p, v1p, v2p, B, S)
        li = l // 2
        if l % 2 == 0:
            x1, h = _outproj(xf, att, rw, mod, norm2_g[l].reshape(1, D), w_out_b, B, S, l)
            xf = _ffn_dense(x1, h, mod, ffn_w_gate, ffn_w_up, ffn_w_down, B, S, l, li)
        else:
            router = (jnp.pad(moe_router_w[li], ((0, 0), (0, LANES - N_EXPERTS))),
                      jnp.pad(moe_router_b[li], (0, LANES - N_EXPERTS)).reshape(1, LANES))
            x1, h, info = _outproj(xf, att, rw, mod, norm2_g[l].reshape(1, D), w_out_b, B, S, l, router)
            xf = _moe(x1, h, info, mod, moe_w_gate, moe_w_up, moe_w_down, final_g2, B, S, l, li,
                      final=(l == L - 1))
    if L % 2 == 1:
        raise NotImplementedError("final norm is fused into the last (expert) layer")
    return xf.reshape(B, S, D)
```

```python
import functools
import math

import numpy as np
import jax
import jax.numpy as jnp
from jax import lax
from jax.experimental import pallas as pl
from jax.experimental.pallas import tpu as pltpu

F32 = jnp.float32
BF16 = jnp.bfloat16

D_MODEL = 1024
HEAD_DIM = 64
ATT_WIDTH = 512
RWKV_WIDTH = 512
N_HEADS = 8
HEAD_PAIRS = 4
LANES = 128
DILATED_GROUPS = ((128, 1), (512, 4), (2048, 16))
N_BUCKETS = 32
MAX_DISTANCE = 2048
NEG_INF = -1e30
D_DECAY_LORA = 64
D_AAA_LORA = 64
D_MV_LORA = 32
D_GATE_LORA = 160
N_IN = 3 * ATT_WIDTH + 3 * RWKV_WIDTH + D_DECAY_LORA + D_AAA_LORA + D_GATE_LORA
QKV_COLS = 3 * ATT_WIDTH
RWM_COLS = 3 * RWKV_WIDTH
RWL_COLS = 384
N_IN_PAD = QKV_COLS + RWM_COLS + RWL_COLS
GN_EPS = HEAD_DIM * 1e-5
RMS_EPS = 1e-6
N_EXPERTS = 8
TOP_K = 2
CHUNK = 64
VMEM_LIMIT = 56 * 1024 * 1024
DMA_ISSUE_UNROLL = 8


def _cparams(sem):
    return pltpu.CompilerParams(dimension_semantics=sem, vmem_limit_bytes=VMEM_LIMIT)


def _tile(n, pref):
    t = min(n, pref)
    while n % t:
        t //= 2
    return t


def _mm(a, b):
    return jnp.dot(a.astype(BF16), b.astype(BF16), preferred_element_type=F32)


def _mm_nt(a, b):
    return lax.dot_general(a.astype(BF16), b.astype(BF16), (((1,), (1,)), ((), ())),
                           preferred_element_type=F32)


def _sigmoid(x):
    return 1.0 / (1.0 + jnp.exp(-x))


def _silu(x):
    return x * _sigmoid(x)


def _ada_kernel(c_ref, w_ref, b_ref, o_ref):
    ca = _silu(c_ref[...])
    o_ref[0] = _mm(ca, w_ref[0]) + b_ref[0]


def _ada_mod(c, w_ada, b_ada):
    L, D, N = w_ada.shape
    B = c.shape[0]
    tn = _tile(N, 1536)
    return pl.pallas_call(
        _ada_kernel,
        grid=(L, N // tn),
        in_specs=[pl.BlockSpec((B, D), lambda l, j: (0, 0)),
                  pl.BlockSpec((1, D, tn), lambda l, j: (l, 0, j)),
                  pl.BlockSpec((1, 1, tn), lambda l, j: (l, 0, j))],
        out_specs=pl.BlockSpec((1, B, tn), lambda l, j: (l, 0, j)),
        out_shape=jax.ShapeDtypeStruct((L, B, N), F32),
        compiler_params=_cparams(("arbitrary", "arbitrary")),
        name="ada_mod",
    )(c, w_ada, b_ada.reshape(L, 1, N))


def _rms_mod(xf, g, scale, shift):
    ms = jnp.mean(xf * xf, axis=-1, keepdims=True)
    return (xf * lax.rsqrt(ms + RMS_EPS) * g) * (1.0 + scale) + shift


def _inproj_kernel(x_ref, mod_ref, g_ref, w_ref, mu_ref, qkv_ref, rwm_ref, rwl_ref, carry_ref):
    i = pl.program_id(1)
    tm = x_ref.shape[0]
    h = _rms_mod(x_ref[...], g_ref[...], mod_ref[0, 1:2, :], mod_ref[0, 0:1, :])
    acc = jnp.dot(h.astype(BF16), w_ref[0], preferred_element_type=F32)
    qkv_ref[...] = acc[:, :QKV_COLS].astype(BF16)
    p = acc[:, QKV_COLS:]
    first = jnp.where(i == 0, 0.0, carry_ref[...])
    row = lax.broadcasted_iota(jnp.int32, p.shape, 0)
    prev = jnp.where(row == 0, first, pltpu.roll(p, 1, 0))
    carry_ref[...] = p[tm - 1:tm, :]
    pm = p + mu_ref[...] * (prev - p)
    rwm_ref[...] = pm[:, :RWM_COLS]
    rwl_ref[...] = pm[:, RWM_COLS:]


def _inproj(x, mod, g, w_in_l, mu_l, B, S, l):
    T, D = x.shape
    tm = _tile(S, 512)
    nS = S // tm
    return pl.pallas_call(
        _inproj_kernel,
        grid=(B, nS),
        in_specs=[pl.BlockSpec((tm, D), lambda b, i: (b * nS + i, 0)),
                  pl.BlockSpec((1, 6, D), lambda b, i: (l * B + b, 0, 0)),
                  pl.BlockSpec((1, D), lambda b, i: (0, 0)),
                  pl.BlockSpec((1, D, N_IN_PAD), lambda b, i: (l, 0, 0)),
                  pl.BlockSpec((1, RWM_COLS + RWL_COLS), lambda b, i: (0, 0))],
        out_specs=[pl.BlockSpec((tm, QKV_COLS), lambda b, i: (b * nS + i, 0)),
                   pl.BlockSpec((tm, RWM_COLS), lambda b, i: (b * nS + i, 0)),
                   pl.BlockSpec((tm, RWL_COLS), lambda b, i: (b * nS + i, 0))],
        out_shape=[jax.ShapeDtypeStruct((T, QKV_COLS), BF16),
                   jax.ShapeDtypeStruct((T, RWM_COLS), F32),
                   jax.ShapeDtypeStruct((T, RWL_COLS), F32)],
        scratch_shapes=[pltpu.VMEM((1, RWM_COLS + RWL_COLS), F32)],
        compiler_params=_cparams(("arbitrary", "arbitrary")),
        name="inproj",
    )(x, mod, g, w_in_l, mu_l)


def _t5_bucket(n):
    max_exact = N_BUCKETS // 2
    large = max_exact + (np.log(np.maximum(n, 1) / max_exact) / np.log(MAX_DISTANCE / max_exact)
                         * (N_BUCKETS - max_exact)).astype(np.int32)
    large = np.minimum(large, N_BUCKETS - 1)
    return np.where(n < max_exact, n, large).astype(np.int32)


def _att_bias_table(rel_bias, S, tq, tk):
    off = np.arange(S)
    mult = np.zeros(S, np.int64)
    for w, d in DILATED_GROUPS:
        mult += ((off % d == 0) & (off <= w)).astype(np.int64)
    logm = np.where(mult > 0, np.log(np.maximum(mult, 1)), NEG_INF).astype(np.float32)
    per_off = rel_bias[_t5_bucket(off)].T.astype(F32) + jnp.asarray(logm)[None, :]
    H = per_off.shape[0]
    nd = S // tq
    period = tq + tk - 1
    w = jnp.concatenate([jnp.full((H, tk - 1), NEG_INF, F32), per_off], axis=1)
    rw = w[:, ::-1]
    n = w.shape[1]
    zs = []
    for d in range(nd):
        a = n - 1 - (d * tq + tk - 1)
        zs.append(jnp.concatenate([rw[:, a:a + tk], rw[:, a - (tq - 1):a]], axis=1))
    z = jnp.stack(zs, axis=1)
    flat = jnp.tile(z, (1, 1, tq))[:, :, :tq * (period - 1)]
    return flat.reshape(H, nd, tq, period - 1)[..., :tk]


def _attn_kernel(q_ref, k_ref, v_ref, bias_ref, o_ref):
    qi = pl.program_id(2)
    tq = q_ref.shape[0]
    tk = bias_ref.shape[-1]
    q = q_ref[...] * jnp.asarray(1.0 / math.sqrt(HEAD_DIM), BF16)
    head0 = lax.broadcasted_iota(jnp.int32, (tq, LANES), 1) < HEAD_DIM
    zero = jnp.zeros_like(q)
    qh = (jnp.where(head0, q, zero), jnp.where(head0, zero, q))
    last_kt = (qi * tq) // tk

    def body(j, carry):
        kt_idx = last_kt - j
        start = pl.multiple_of(kt_idx * tk, tk)
        kt = k_ref[pl.ds(start, tk), :]
        vt = v_ref[pl.ds(start, tk), :]
        delta = qi - kt_idx * (tk // tq)
        new = []
        for h in range(2):
            m, l, acc = carry[h]
            s = lax.dot_general(qh[h], kt, (((1,), (1,)), ((), ())), preferred_element_type=F32)
            s = s + bias_ref[h, delta]
            m_new = jnp.maximum(m, jnp.max(s, axis=-1, keepdims=True))
            alpha = jnp.exp(m - m_new)
            p = jnp.exp(s - m_new)
            l_new = alpha * l + jnp.sum(p, axis=-1, keepdims=True)
            acc_new = alpha * acc + jnp.dot(p.astype(BF16), vt, preferred_element_type=F32)
            new.append((m_new, l_new, acc_new))
        return tuple(new)

    init = (jnp.full((tq, 1), NEG_INF, F32), jnp.zeros((tq, 1), F32), jnp.zeros((tq, LANES), F32))
    res = lax.fori_loop(0, last_kt + 1, body, (init, init))
    outs = [acc / l for (_, l, acc) in res]
    o_ref[...] = jnp.where(head0, outs[0], outs[1]).astype(o_ref.dtype)


def _attention(qkv, bias_table, B, S):
    T = qkv.shape[0]
    _, nd, tq, tk = bias_table.shape
    nq = S // tq
    return pl.pallas_call(
        _attn_kernel,
        grid=(HEAD_PAIRS, B, nq),
        in_specs=[pl.BlockSpec((tq, LANES), lambda hp, b, i: (b * nq + i, hp)),
                  pl.BlockSpec((S, LANES), lambda hp, b, i: (b, HEAD_PAIRS + hp)),
                  pl.BlockSpec((S, LANES), lambda hp, b, i: (b, 2 * HEAD_PAIRS + hp)),
                  pl.BlockSpec((2, nd, tq, tk), lambda hp, b, i: (hp, 0, 0, 0))],
        out_specs=pl.BlockSpec((tq, LANES), lambda hp, b, i: (b * nq + i, hp)),
        out_shape=jax.ShapeDtypeStruct((T, ATT_WIDTH), BF16),
        compiler_params=_cparams(("arbitrary", "arbitrary", "arbitrary")),
        name="dilated_attn",
    )(qkv, qkv, qkv, bias_table)


def _split2(x):
    hi = x.astype(BF16)
    lo = (x - hi.astype(F32)).astype(BF16)
    return hi, lo


def _head_sum(x, m2):
    hi, lo = _split2(x)
    return jnp.dot(jnp.concatenate([hi, lo], axis=1), m2, preferred_element_type=F32)


def _rwkv_kernel(*refs, has_vres, n_chunks):
    if has_vres:
        (rwm_ref, rwl_ref, vf_ref, vec_ref, w2_ref, a2_ref, g2_ref, v1_ref, v2_ref,
         o_ref, state_ref, r_s, k_s, v_s, lw_s, kk_s, kb_s, y_s) = refs
    else:
        (rwm_ref, rwl_ref, vec_ref, w2_ref, a2_ref, g2_ref,
         o_ref, vf_out_ref, state_ref, r_s, k_s, v_s, lw_s, kk_s, kb_s, y_s) = refs
    C = CHUNK
    W = RWKV_WIDTH

    @pl.when(pl.program_id(1) == 0)
    def _():
        state_ref[...] = jnp.zeros_like(state_ref)

    ri = lax.broadcasted_iota(jnp.int32, (2 * LANES, LANES), 0)
    ci = lax.broadcasted_iota(jnp.int32, (2 * LANES, LANES), 1)
    m2 = ((ri % LANES) // HEAD_DIM == ci // HEAD_DIM).astype(BF16)

    vec = vec_ref[...]
    w0, a0, k_k, k_a, r_k, ln_w, ln_b, v0 = [vec[i:i + 1, :] for i in range(8)]

    r = rwm_ref[:, 0:W]
    k = rwm_ref[:, W:2 * W]
    v = rwm_ref[:, 2 * W:3 * W]
    lora = rwl_ref[:, 0:LANES]
    w_raw = w0 + _mm(jnp.tanh(lora), w2_ref[...])
    z = -w_raw
    w_log = -(jnp.maximum(z, 0.0) + jnp.log1p(jnp.exp(-jnp.abs(z)))) - 0.5
    lw_s[...] = -jnp.exp(w_log)
    a = _sigmoid(a0 + _mm(lora, a2_ref[...]))
    g = _mm(_sigmoid(rwl_ref[:, LANES:RWL_COLS]), g2_ref[...])
    if has_vres:
        mix = _sigmoid(v0 + _mm(_mm(v, v1_ref[...]), v2_ref[...]))
        v = v + (vf_ref[...] - v) * mix
    else:
        vf_out_ref[...] = v
    kk = k * k_k
    for p in range(HEAD_PAIRS):
        ls = slice(p * LANES, (p + 1) * LANES)
        kkp = kk[:, ls]
        nrm = jnp.sqrt(_head_sum(kkp * kkp, m2))
        kkp = kkp / jnp.maximum(nrm, 1e-12)
        kk_s[:, ls] = kkp
        kb_s[:, ls] = kkp * a[:, ls]
    kmod = k * (1.0 + (a - 1.0) * k_a)
    r_s[...] = r
    k_s[...] = kmod
    v_s[...] = v

    ri = lax.broadcasted_iota(jnp.int32, (LANES, LANES), 0)
    ci = lax.broadcasted_iota(jnp.int32, (LANES, LANES), 1)
    same_head = (ri // C) == (ci // C)
    strict = same_head & ((ri % C) > (ci % C))
    incl = same_head & ((ri % C) >= (ci % C))
    eye = (ri == ci).astype(F32)
    lane = lax.broadcasted_iota(jnp.int32, (C, LANES), 1)
    head0 = lane < HEAD_DIM
    tri_r = lax.broadcasted_iota(jnp.int32, (C, 3 * C), 0)
    tri_c = lax.broadcasted_iota(jnp.int32, (C, 3 * C), 1)
    tri3 = ((tri_c % C) <= tri_r).astype(BF16)

    def chunk_body(c, _):
        rows = pl.ds(pl.multiple_of(c * C, C), C)
        pairs = range(HEAD_PAIRS)
        lss = [slice(p * LANES, (p + 1) * LANES) for p in pairs]
        rc = [r_s[rows, ls] for ls in lss]
        kc = [k_s[rows, ls] for ls in lss]
        vc = [v_s[rows, ls] for ls in lss]
        lw = [lw_s[rows, ls] for ls in lss]
        kkc = [kk_s[rows, ls] for ls in lss]
        kbc = [kb_s[rows, ls] for ls in lss]

        def cumsum(x):
            hi = x.astype(BF16)
            rem = x - hi.astype(F32)
            mid = rem.astype(BF16)
            lo = (rem - mid.astype(F32)).astype(BF16)
            return jnp.dot(tri3, jnp.concatenate([hi, mid, lo], axis=0), preferred_element_type=F32)

        def per_head_rows(x, y):
            zero = jnp.zeros_like(x)
            return jnp.concatenate([jnp.where(head0, x, zero), jnp.where(head0, zero, x),
                                    jnp.where(head0, y, zero), jnp.where(head0, zero, y)], axis=0)

        cum = [cumsum(lw[p]) for p in pairs]
        total = [cum[p][C - 1:C, :] for p in pairs]
        g_inv = [jnp.exp(-cum[p]) for p in pairs]
        a_t = [-kkc[p] * jnp.exp(cum[p] - lw[p]) for p in pairs]
        r_t = [rc[p] * jnp.exp(cum[p]) for p in pairs]
        b_t = [kbc[p] * g_inv[p] for p in pairs]
        k_t = [kc[p] * g_inv[p] for p in pairs]
        to_end = [jnp.exp(total[p] - cum[p]) for p in pairs]
        gram = [_mm_nt(per_head_rows(a_t[p], r_t[p]), per_head_rows(b_t[p], k_t[p])) for p in pairs]
        n_ab = [jnp.where(strict, gram[p][:LANES, :LANES], 0.0) for p in pairs]
        n_ak = [jnp.where(strict, gram[p][:LANES, LANES:], 0.0) for p in pairs]
        n_rb = [jnp.where(incl, gram[p][LANES:, :LANES], 0.0) for p in pairs]
        n_rk = [jnp.where(incl, gram[p][LANES:, LANES:], 0.0) for p in pairs]
        pw = n_ab
        inv = [eye + n_ab[p] for p in pairs]
        for _ in range(int(math.log2(C)) - 1):
            pw = [_mm(pw[p], pw[p]) for p in pairs]
            inv = [inv[p] + _mm(inv[p], pw[p]) for p in pairs]
        v_stack = [jnp.concatenate([vc[p], vc[p]], axis=0) for p in pairs]
        akv = [_mm(n_ak[p], v_stack[p]) for p in pairs]
        st = [state_ref[p] for p in pairs]
        ah = [_mm_nt(jnp.concatenate([a_t[p], r_t[p]], axis=0), st[p]) for p in pairs]
        u_stack = [_mm(inv[p], jnp.concatenate([ah[p][:C], ah[p][:C]], axis=0) + akv[p]) for p in pairs]
        y_stack = [jnp.concatenate([ah[p][C:], ah[p][C:]], axis=0)
                   + _mm(jnp.concatenate([n_rb[p], n_rk[p]], axis=1),
                         jnp.concatenate([u_stack[p], v_stack[p]], axis=0)) for p in pairs]
        for p in pairs:
            y_s[rows, lss[p]] = jnp.where(head0, y_stack[p][:C], y_stack[p][C:])
        uv_t = [jnp.concatenate([jnp.where(head0, u_stack[p][:C], u_stack[p][C:]), vc[p]], axis=0).T
                for p in pairs]
        bk = [jnp.concatenate([kbc[p] * to_end[p], kc[p] * to_end[p]], axis=0) for p in pairs]
        st_new = [st[p] * jnp.exp(total[p]) + _mm(uv_t[p], bk[p]) for p in pairs]
        for p in pairs:
            state_ref[p] = jnp.where(same_head, st_new[p], 0.0)
        return 0

    lax.fori_loop(0, n_chunks, chunk_body, 0)

    for p in range(HEAD_PAIRS):
        ls = slice(p * LANES, (p + 1) * LANES)
        y = y_s[:, ls]
        mean = _head_sum(y, m2) * (1.0 / HEAD_DIM)
        d = y - mean
        var = _head_sum(d * d, m2) * (1.0 / HEAD_DIM)
        yn = d * lax.rsqrt(var + GN_EPS) * ln_w[:, ls] + ln_b[:, ls]
        bonus = _head_sum(r_s[:, ls] * k_s[:, ls] * r_k[:, ls], m2) * v_s[:, ls]
        o_ref[:, ls] = ((yn + bonus) * g[:, ls]).astype(o_ref.dtype)


def _rwkv(rwm, rwl, v_first, vec, w2p, a2p, g2p, v1p, v2p, B, S):
    T = rwm.shape[0]
    W = RWKV_WIDTH
    tb = _tile(S, 512)
    nS = S // tb
    has_vres = v_first is not None
    row = lambda b, i: (b * nS + i, 0)
    full = lambda b, i: (0, 0)
    in_specs = [pl.BlockSpec((tb, RWM_COLS), row), pl.BlockSpec((tb, RWL_COLS), row)]
    args = [rwm, rwl]
    if has_vres:
        in_specs.append(pl.BlockSpec((tb, W), row))
        args.append(v_first)
    in_specs += [pl.BlockSpec(vec.shape, full), pl.BlockSpec(w2p.shape, full),
                 pl.BlockSpec(a2p.shape, full), pl.BlockSpec(g2p.shape, full)]
    args += [vec, w2p, a2p, g2p]
    if has_vres:
        in_specs += [pl.BlockSpec(v1p.shape, full), pl.BlockSpec(v2p.shape, full)]
        args += [v1p, v2p]
    out_specs = [pl.BlockSpec((tb, W), row)]
    out_shape = [jax.ShapeDtypeStruct((T, W), BF16)]
    if not has_vres:
        out_specs.append(pl.BlockSpec((tb, W), row))
        out_shape.append(jax.ShapeDtypeStruct((T, W), F32))
    outs = pl.pallas_call(
        functools.partial(_rwkv_kernel, has_vres=has_vres, n_chunks=tb // CHUNK),
        grid=(B, nS),
        in_specs=in_specs,
        out_specs=out_specs,
        out_shape=out_shape,
        scratch_shapes=[pltpu.VMEM((HEAD_PAIRS, LANES, LANES), F32)] + [pltpu.VMEM((tb, W), F32)] * 7,
        compiler_params=_cparams(("arbitrary", "arbitrary")),
        name="rwkv7",
    )(*args)
    return (outs[0], v_first) if has_vres else (outs[0], outs[1])


def _outproj_kernel(*refs, moe):
    if moe:
        (x_ref, att_ref, rw_ref, mod_ref, g_ref, w_ref, rw_w_ref, rw_b_ref,
         x1_ref, h_ref, info_ref) = refs
    else:
        x_ref, att_ref, rw_ref, mod_ref, g_ref, w_ref, x1_ref, h_ref = refs
    mix = (jnp.dot(att_ref[...], w_ref[0, :ATT_WIDTH, :], preferred_element_type=F32)
           + jnp.dot(rw_ref[...], w_ref[0, ATT_WIDTH:, :], preferred_element_type=F32))
    x1 = x_ref[...] + mod_ref[0, 2:3, :] * mix
    x1_ref[...] = x1
    h = _rms_mod(x1, g_ref[...], mod_ref[0, 4:5, :], mod_ref[0, 3:4, :])
    h_ref[...] = h.astype(h_ref.dtype)
    if moe:
        h_hi, h_lo = _split2(h)
        w_hi, w_lo = _split2(rw_w_ref[...])
        logits = (jnp.dot(h_hi, w_hi, preferred_element_type=F32)
                  + jnp.dot(h_hi, w_lo, preferred_element_type=F32)
                  + jnp.dot(h_lo, w_hi, preferred_element_type=F32)) + rw_b_ref[...]
        lane = lax.broadcasted_iota(jnp.int32, logits.shape, 1)
        logits = jnp.where(lane < N_EXPERTS, logits, -jnp.inf)
        m1 = jnp.max(logits, axis=-1, keepdims=True)
        i1 = jnp.min(jnp.where(logits == m1, lane, LANES), axis=-1, keepdims=True)
        rest = jnp.where(lane == i1, -jnp.inf, logits)
        m2 = jnp.max(rest, axis=-1, keepdims=True)
        i2 = jnp.min(jnp.where(rest == m2, lane, LANES), axis=-1, keepdims=True)
        e = jnp.exp(m2 - m1)
        g1 = 1.0 / (1.0 + e)
        g2 = e / (1.0 + e)
        info = jnp.where(lane == 0, i1.astype(F32),
                         jnp.where(lane == 1, i2.astype(F32),
                                   jnp.where(lane == 2, g1, jnp.where(lane == 3, g2, 0.0))))
        info_ref[...] = info


def _outproj(x, att, rw, mod, g, w_out_b, B, S, l, router=None):
    T, D = x.shape
    tm = _tile(S, 512)
    nS = S // tm
    moe = router is not None
    row = lambda b, i: (b * nS + i, 0)
    full = lambda b, i: (0, 0)
    in_specs = [pl.BlockSpec((tm, D), row), pl.BlockSpec((tm, ATT_WIDTH), row),
                pl.BlockSpec((tm, RWKV_WIDTH), row),
                pl.BlockSpec((1, 6, D), lambda b, i: (l * B + b, 0, 0)),
                pl.BlockSpec((1, D), full),
                pl.BlockSpec((1, D, D), lambda b, i: (l, 0, 0))]
    args = [x, att, rw, mod, g, w_out_b]
    out_specs = [pl.BlockSpec((tm, D), row), pl.BlockSpec((tm, D), row)]
    out_shape = [jax.ShapeDtypeStruct((T, D), F32), jax.ShapeDtypeStruct((T, D), F32 if moe else BF16)]
    if moe:
        in_specs += [pl.BlockSpec((D, LANES), full), pl.BlockSpec((1, LANES), full)]
        args += list(router)
        out_specs.append(pl.BlockSpec((tm, LANES), row))
        out_shape.append(jax.ShapeDtypeStruct((T, LANES), F32))
    return pl.pallas_call(
        functools.partial(_outproj_kernel, moe=moe),
        grid=(B, nS),
        in_specs=in_specs,
        out_specs=out_specs,
        out_shape=out_shape,
        compiler_params=_cparams(("arbitrary", "arbitrary")),
        name="outproj",
    )(*args)


def _ffn_kernel(x_ref, h_ref, mod_ref, wg_ref, wu_ref, wd_ref, o_ref, acc_ref):
    f = pl.program_id(1)

    @pl.when(f == 0)
    def _():
        acc_ref[...] = jnp.zeros_like(acc_ref)

    hb = h_ref[...]
    gate = jnp.dot(hb, wg_ref[0], preferred_element_type=F32)
    up = jnp.dot(hb, wu_ref[0], preferred_element_type=F32)
    mid = (_silu(gate) * up).astype(BF16)
    acc_ref[...] += jnp.dot(mid, wd_ref[0], preferred_element_type=F32)

    @pl.when(f == pl.num_programs(1) - 1)
    def _():
        o_ref[...] = x_ref[...] + mod_ref[0, 5:6, :] * acc_ref[...]


def _ffn_dense(x1, h, mod, wg, wu, wd, B, S, l, li):
    T, D = x1.shape
    F = wg.shape[-1]
    tm = _tile(S, 1024)
    tf = _tile(F, 256)
    per_seq = S // tm
    return pl.pallas_call(
        _ffn_kernel,
        grid=(T // tm, F // tf),
        in_specs=[pl.BlockSpec((tm, D), lambda i, f: (i, 0)),
                  pl.BlockSpec((tm, D), lambda i, f: (i, 0)),
                  pl.BlockSpec((1, 6, D), lambda i, f: (l * B + i // per_seq, 0, 0)),
                  pl.BlockSpec((1, D, tf), lambda i, f: (li, 0, f)),
                  pl.BlockSpec((1, D, tf), lambda i, f: (li, 0, f)),
                  pl.BlockSpec((1, tf, D), lambda i, f: (li, f, 0))],
        out_specs=pl.BlockSpec((tm, D), lambda i, f: (i, 0)),
        out_shape=jax.ShapeDtypeStruct((T, D), F32),
        scratch_shapes=[pltpu.VMEM((tm, D), F32)],
        compiler_params=_cparams(("arbitrary", "arbitrary")),
        name="ffn_dense",
    )(x1, h, mod, wg, wu, wd)


def _experts_kernel(te_ref, nu_ref, tok_ref, h_ref, wg_ref, wu_ref, wd_ref, y_ref,
                    land_ref, xb_ref, acc_ref, sem):
    i = pl.program_id(0)
    f = pl.program_id(1)
    tm = land_ref.shape[0]
    n_used = nu_ref[0]

    def start_gather(tile):
        base = tile * tm

        def body(j, _):
            pltpu.make_async_copy(h_ref.at[pl.ds(tok_ref[base + j], 1)],
                                  land_ref.at[pl.ds(j, 1)], sem).start()
            return 0

        lax.fori_loop(0, tm, body, 0, unroll=DMA_ISSUE_UNROLL)

    @pl.when(i < n_used)
    def _():
        @pl.when(f == 0)
        def _():
            @pl.when(i == 0)
            def _():
                start_gather(0)

            pltpu.make_async_copy(h_ref.at[pl.ds(0, tm)], land_ref, sem).wait()
            xb_ref[...] = land_ref[...].astype(BF16)
            acc_ref[...] = jnp.zeros_like(acc_ref)

            @pl.when(i + 1 < n_used)
            def _():
                start_gather(i + 1)

        xb = xb_ref[...]
        gate = jnp.dot(xb, wg_ref[0, 0], preferred_element_type=F32)
        up = jnp.dot(xb, wu_ref[0, 0], preferred_element_type=F32)
        mid = (_silu(gate) * up).astype(BF16)
        acc_ref[...] += jnp.dot(mid, wd_ref[0, 0], preferred_element_type=F32)

        @pl.when(f == pl.num_programs(1) - 1)
        def _():
            y_ref[...] = acc_ref[...]

    @pl.when((i >= n_used) & (f == 0))
    def _():
        y_ref[...] = jnp.zeros_like(y_ref)


def _experts(h, row_tok, tile_e, n_used, wg, wu, wd, li, tm):
    n_rows = row_tok.shape[0]
    D = h.shape[1]
    assert h.shape[0] >= tm
    F = wg.shape[-1]
    tf = _tile(F, 512)
    nf = F // tf

    def rows(i, f, te, nu, tok):
        return (i, 0)

    def wcol(i, f, te, nu, tok):
        return (li, te[jnp.minimum(i, nu[0] - 1)], 0, jnp.where(i < nu[0], f, nf - 1))

    def wrow(i, f, te, nu, tok):
        return (li, te[jnp.minimum(i, nu[0] - 1)], jnp.where(i < nu[0], f, nf - 1), 0)

    return pl.pallas_call(
        _experts_kernel,
        grid_spec=pltpu.PrefetchScalarGridSpec(
            num_scalar_prefetch=3,
            grid=(n_rows // tm, nf),
            in_specs=[pl.BlockSpec(memory_space=pl.ANY),
                      pl.BlockSpec((1, 1, D, tf), wcol),
                      pl.BlockSpec((1, 1, D, tf), wcol),
                      pl.BlockSpec((1, 1, tf, D), wrow)],
            out_specs=pl.BlockSpec((tm, D), rows),
            scratch_shapes=[pltpu.VMEM((tm, D), F32), pltpu.VMEM((tm, D), BF16),
                            pltpu.VMEM((tm, D), F32), pltpu.SemaphoreType.DMA(())]),
        out_shape=jax.ShapeDtypeStruct((n_rows, D), F32),
        compiler_params=_cparams(("arbitrary", "arbitrary")),
        name="moe_experts",
    )(tile_e, n_used, row_tok, h, wg, wu, wd)


def _combine_kernel(pos_ref, y_ref, x_ref, info_ref, mod_ref, fg_ref, o_ref, buf, sem, *, tc, final):
    i = pl.program_id(0)
    slot = i % 2

    def start_gather(tile, dst_slot):
        base = tile * (2 * tc)

        def body(j, _):
            pltpu.make_async_copy(y_ref.at[pl.ds(pos_ref[base + j], 1)],
                                  buf.at[dst_slot, pl.ds(j, 1)], sem.at[dst_slot]).start()
            return 0

        lax.fori_loop(0, 2 * tc, body, 0, unroll=DMA_ISSUE_UNROLL)

    @pl.when(i == 0)
    def _():
        start_gather(0, 0)

    @pl.when(i + 1 < pl.num_programs(0))
    def _():
        start_gather(i + 1, 1 - slot)

    pltpu.make_async_copy(y_ref.at[pl.ds(0, 2 * tc)], buf.at[slot], sem.at[slot]).wait()
    info = info_ref[...]
    ff = info[:, 2:3] * buf[slot, 0:tc, :] + info[:, 3:4] * buf[slot, tc:2 * tc, :]
    out = x_ref[...] + mod_ref[0, 5:6, :] * ff
    if final:
        ms = jnp.mean(out * out, axis=-1, keepdims=True)
        out = out * lax.rsqrt(ms + RMS_EPS) * fg_ref[...]
    o_ref[...] = out


def _combine(pos, y_rows, x1, info, mod, final_g, B, S, l, final):
    T, D = x1.shape
    tc = _tile(S, 256)
    per_seq = S // tc
    return pl.pallas_call(
        functools.partial(_combine_kernel, tc=tc, final=final),
        grid_spec=pltpu.PrefetchScalarGridSpec(
            num_scalar_prefetch=1,
            grid=(T // tc,),
            in_specs=[pl.BlockSpec(memory_space=pl.ANY),
                      pl.BlockSpec((tc, D), lambda i, pos: (i, 0)),
                      pl.BlockSpec((tc, LANES), lambda i, pos: (i, 0)),
                      pl.BlockSpec((1, 6, D), lambda i, pos: (l * B + i // per_seq, 0, 0)),
                      pl.BlockSpec((1, D), lambda i, pos: (0, 0))],
            out_specs=pl.BlockSpec((tc, D), lambda i, pos: (i, 0)),
            scratch_shapes=[pltpu.VMEM((2, 2 * tc, D), F32), pltpu.SemaphoreType.DMA((2,))]),
        out_shape=jax.ShapeDtypeStruct((T, D), F32),
        compiler_params=_cparams(("arbitrary",)),
        name="moe_combine",
    )(pos, y_rows, x1, info, mod, final_g)


def _moe(x1, h, info, mod, wg, wu, wd, final_g, B, S, l, li, final):
    T, D = x1.shape
    TK = T * TOP_K
    tm = _tile(TK // N_EXPERTS, 512)
    n_rows = TK + (N_EXPERTS - 1) * tm
    flat_e = info[:, :TOP_K].astype(jnp.int32).reshape(TK)
    onehot = (flat_e[:, None] == jnp.arange(N_EXPERTS, dtype=jnp.int32)[None, :]).astype(jnp.int32)
    csum = jnp.cumsum(onehot, axis=0)
    counts = csum[-1]
    rank = jnp.sum(onehot * csum, axis=1) - 1
    padded = (counts + tm - 1) // tm * tm
    pad_ends = jnp.cumsum(padded)
    pad_starts = pad_ends - padded
    dest = pad_starts[flat_e] + rank
    row_tok = jnp.zeros((n_rows,), jnp.int32).at[dest].set(jnp.arange(TK, dtype=jnp.int32) // TOP_K)
    tile_start = jnp.arange(n_rows // tm, dtype=jnp.int32) * tm
    tile_e = jnp.minimum(jnp.searchsorted(pad_ends, tile_start, side='right'), N_EXPERTS - 1).astype(jnp.int32)
    n_used_rows = pad_ends[-1:].astype(jnp.int32)
    n_used_tiles = n_used_rows // tm
    tc = _tile(S, 256)
    pos = dest.reshape(T // tc, tc, TOP_K).transpose(0, 2, 1).reshape(TK).astype(jnp.int32)

    y_rows = _experts(h, row_tok, tile_e, n_used_tiles, wg, wu, wd, li, tm)
    return _combine(pos, y_rows, x1, info, mod, final_g, B, S, l, final)


def kernel(x, c, w_ada, b_ada, norm1_g, norm2_g, final_g, w_in, w_out, rel_bias, rwkv_mu, rwkv_w0, rwkv_w2, rwkv_a0, rwkv_a2, rwkv_g2, rwkv_k_k, rwkv_k_a, rwkv_r_k, rwkv_ln_w, rwkv_ln_b, rwkv_v0, rwkv_v1, rwkv_v2, ffn_w_gate, ffn_w_up, ffn_w_down, moe_router_w, moe_router_b, moe_w_gate, moe_w_up, moe_w_down):
    B, S, D = x.shape
    L = w_in.shape[0]
    T = B * S
    W = RWKV_WIDTH

    mod = _ada_mod(c, w_ada, b_ada).reshape(L * B, 6, D)
    w_in_b = jnp.pad(w_in, ((0, 0), (0, 0), (0, N_IN_PAD - N_IN))).astype(BF16)
    w_out_b = w_out.astype(BF16)
    ffn_w = [w.astype(BF16) for w in (ffn_w_gate, ffn_w_up, ffn_w_down)]
    moe_w = [w.astype(BF16) for w in (moe_w_gate, moe_w_up, moe_w_down)]
    mu_p = jnp.pad(rwkv_mu, ((0, 0), (0, RWM_COLS + RWL_COLS - rwkv_mu.shape[1])))
    bias_table = _att_bias_table(rel_bias, S, _tile(S, 256), _tile(S, 512))
    zeros_w = jnp.zeros((1, W), F32)
    final_g2 = final_g.reshape(1, D)

    xf = x.reshape(T, D)
    v_first = None
    for l in range(L):
        qkv, rwm, rwl = _inproj(xf, mod, norm1_g[l].reshape(1, D), w_in_b, mu_p[l].reshape(1, -1), B, S, l)
        att = _attention(qkv, bias_table, B, S)
        v0 = rwkv_v0[l - 1].reshape(1, W) if l > 0 else zeros_w
        vec = jnp.concatenate([rwkv_w0[l].reshape(1, W), rwkv_a0[l].reshape(1, W), rwkv_k_k[l].reshape(1, W),
                               rwkv_k_a[l].reshape(1, W), rwkv_r_k[l].reshape(1, W), rwkv_ln_w[l].reshape(1, W),
                               rwkv_ln_b[l].reshape(1, W), v0], axis=0)
        w2p = jnp.pad(rwkv_w2[l], ((0, LANES - D_DECAY_LORA), (0, 0)))
        a2p = jnp.pad(rwkv_a2[l], ((D_DECAY_LORA, 0), (0, 0)))
        g2p = jnp.pad(rwkv_g2[l], ((0, RWL_COLS - LANES - D_GATE_LORA), (0, 0)))
        if l > 0:
            v1p = jnp.pad(rwkv_v1[l - 1], ((0, 0), (0, LANES - D_MV_LORA)))
            v2p = jnp.pad(rwkv_v2[l - 1], ((0, LANES - D_MV_LORA), (0, 0)))
        else:
            v1p = v2p = None
        rw, v_first = _rwkv(rwm, rwl, v_first, vec, w2p, a2p, g2p, v1p, v2p, B, S)
        li = l // 2
        if l % 2 == 0:
            x1, h = _outproj(xf, att, rw, mod, norm2_g[l].reshape(1, D), w_out_b, B, S, l)
            xf = _ffn_dense(x1, h, mod, *ffn_w, B, S, l, li)
        else:
            router = (jnp.pad(moe_router_w[li], ((0, 0), (0, LANES - N_EXPERTS))),
                      jnp.pad(moe_router_b[li], (0, LANES - N_EXPERTS)).reshape(1, LANES))
            x1, h, info = _outproj(xf, att, rw, mod, norm2_g[l].reshape(1, D), w_out_b, B, S, l, router)
            xf = _moe(x1, h, info, mod, *moe_w, final_g2, B, S, l, li,
                      final=(l == L - 1))
    if L % 2 == 1:
        raise NotImplementedError("final norm is fused into the last (expert) layer")
    return xf.reshape(B, S, D)
```

```python
import functools
import math

import numpy as np
import jax
import jax.numpy as jnp
from jax import lax
from jax.experimental import pallas as pl
from jax.experimental.pallas import tpu as pltpu

F32 = jnp.float32
BF16 = jnp.bfloat16

D_MODEL = 1024
HEAD_DIM = 64
ATT_WIDTH = 512
RWKV_WIDTH = 512
N_HEADS = 8
HEAD_PAIRS = 4
LANES = 128
DILATED_GROUPS = ((128, 1), (512, 4), (2048, 16))
N_BUCKETS = 32
MAX_DISTANCE = 2048
NEG_INF = -1e30
D_DECAY_LORA = 64
D_AAA_LORA = 64
D_MV_LORA = 32
D_GATE_LORA = 160
N_IN = 3 * ATT_WIDTH + 3 * RWKV_WIDTH + D_DECAY_LORA + D_AAA_LORA + D_GATE_LORA
QKV_COLS = 3 * ATT_WIDTH
RWM_COLS = 3 * RWKV_WIDTH
RWL_COLS = 384
N_IN_PAD = QKV_COLS + RWM_COLS + RWL_COLS
GN_EPS = HEAD_DIM * 1e-5
RMS_EPS = 1e-6
N_EXPERTS = 8
TOP_K = 2
CHUNK = 64
VMEM_LIMIT = 56 * 1024 * 1024
LOG2_E = math.log2(math.e)
Q_SCALE = LOG2_E / math.sqrt(HEAD_DIM)
DMA_ISSUE_UNROLL = 8


def _cparams(sem):
    return pltpu.CompilerParams(dimension_semantics=sem, vmem_limit_bytes=VMEM_LIMIT)


def _tile(n, pref):
    t = min(n, pref)
    while n % t:
        t //= 2
    return t


def _mm(a, b):
    return jnp.dot(a.astype(BF16), b.astype(BF16), preferred_element_type=F32)


def _mm_nt(a, b):
    return lax.dot_general(a.astype(BF16), b.astype(BF16), (((1,), (1,)), ((), ())),
                           preferred_element_type=F32)


def _sigmoid(x):
    return 1.0 / (1.0 + jnp.exp(-x))


def _silu(x):
    return x * _sigmoid(x)


def _ada_kernel(c_ref, w_ref, b_ref, o_ref):
    ca = _silu(c_ref[...])
    o_ref[0] = _mm(ca, w_ref[0]) + b_ref[0]


def _ada_mod(c, w_ada, b_ada):
    L, D, N = w_ada.shape
    B = c.shape[0]
    tn = _tile(N, 1536)
    return pl.pallas_call(
        _ada_kernel,
        grid=(L, N // tn),
        in_specs=[pl.BlockSpec((B, D), lambda l, j: (0, 0)),
                  pl.BlockSpec((1, D, tn), lambda l, j: (l, 0, j)),
                  pl.BlockSpec((1, 1, tn), lambda l, j: (l, 0, j))],
        out_specs=pl.BlockSpec((1, B, tn), lambda l, j: (l, 0, j)),
        out_shape=jax.ShapeDtypeStruct((L, B, N), F32),
        compiler_params=_cparams(("arbitrary", "arbitrary")),
        name="ada_mod",
    )(c, w_ada, b_ada.reshape(L, 1, N))


def _rms_mod(xf, g, scale, shift):
    ms = jnp.mean(xf * xf, axis=-1, keepdims=True)
    return (xf * lax.rsqrt(ms + RMS_EPS) * g) * (1.0 + scale) + shift


def _inproj_kernel(x_ref, mod_ref, g_ref, w_ref, mu_ref, qkv_ref, rwm_ref, rwl_ref, carry_ref):
    i = pl.program_id(1)
    tm = x_ref.shape[0]
    h = _rms_mod(x_ref[...], g_ref[...], mod_ref[0, 1:2, :], mod_ref[0, 0:1, :])
    acc = jnp.dot(h.astype(BF16), w_ref[0], preferred_element_type=F32)
    qkv_ref[:, :ATT_WIDTH] = (acc[:, :ATT_WIDTH] * Q_SCALE).astype(BF16)
    qkv_ref[:, ATT_WIDTH:] = acc[:, ATT_WIDTH:QKV_COLS].astype(BF16)
    p = acc[:, QKV_COLS:]
    first = jnp.where(i == 0, 0.0, carry_ref[...])
    row = lax.broadcasted_iota(jnp.int32, p.shape, 0)
    prev = jnp.where(row == 0, first, pltpu.roll(p, 1, 0))
    carry_ref[...] = p[tm - 1:tm, :]
    pm = p + mu_ref[...] * (prev - p)
    rwm_ref[...] = pm[:, :RWM_COLS]
    rwl_ref[...] = pm[:, RWM_COLS:]


def _inproj(x, mod, g, w_in_l, mu_l, B, S, l):
    T, D = x.shape
    tm = _tile(S, 512)
    nS = S // tm
    return pl.pallas_call(
        _inproj_kernel,
        grid=(B, nS),
        in_specs=[pl.BlockSpec((tm, D), lambda b, i: (b * nS + i, 0)),
                  pl.BlockSpec((1, 6, D), lambda b, i: (l * B + b, 0, 0)),
                  pl.BlockSpec((1, D), lambda b, i: (0, 0)),
                  pl.BlockSpec((1, D, N_IN_PAD), lambda b, i: (l, 0, 0)),
                  pl.BlockSpec((1, RWM_COLS + RWL_COLS), lambda b, i: (0, 0))],
        out_specs=[pl.BlockSpec((tm, QKV_COLS), lambda b, i: (b * nS + i, 0)),
                   pl.BlockSpec((tm, RWM_COLS), lambda b, i: (b * nS + i, 0)),
                   pl.BlockSpec((tm, RWL_COLS), lambda b, i: (b * nS + i, 0))],
        out_shape=[jax.ShapeDtypeStruct((T, QKV_COLS), BF16),
                   jax.ShapeDtypeStruct((T, RWM_COLS), F32),
                   jax.ShapeDtypeStruct((T, RWL_COLS), F32)],
        scratch_shapes=[pltpu.VMEM((1, RWM_COLS + RWL_COLS), F32)],
        compiler_params=_cparams(("arbitrary", "arbitrary")),
        name="inproj",
    )(x, mod, g, w_in_l, mu_l)


def _t5_bucket(n):
    max_exact = N_BUCKETS // 2
    large = max_exact + (np.log(np.maximum(n, 1) / max_exact) / np.log(MAX_DISTANCE / max_exact)
                         * (N_BUCKETS - max_exact)).astype(np.int32)
    large = np.minimum(large, N_BUCKETS - 1)
    return np.where(n < max_exact, n, large).astype(np.int32)


def _att_bias_table(rel_bias, S, tq, tk):
    off = np.arange(S)
    mult = np.zeros(S, np.int64)
    for w, d in DILATED_GROUPS:
        mult += ((off % d == 0) & (off <= w)).astype(np.int64)
    logm = np.where(mult > 0, np.log(np.maximum(mult, 1)), NEG_INF).astype(np.float32)
    per_off = (rel_bias[_t5_bucket(off)].T.astype(F32) + jnp.asarray(logm)[None, :]) * LOG2_E
    H = per_off.shape[0]
    nd = S // tq
    period = tq + tk - 1
    w = jnp.concatenate([jnp.full((H, tk - 1), NEG_INF, F32), per_off], axis=1)
    rw = w[:, ::-1]
    n = w.shape[1]
    zs = []
    for d in range(nd):
        a = n - 1 - (d * tq + tk - 1)
        zs.append(jnp.concatenate([rw[:, a:a + tk], rw[:, a - (tq - 1):a]], axis=1))
    z = jnp.stack(zs, axis=1)
    flat = jnp.tile(z, (1, 1, tq))[:, :, :tq * (period - 1)]
    return flat.reshape(H, nd, tq, period - 1)[..., :tk]


def _attn_kernel(q_ref, k_ref, v_ref, bias_ref, o_ref):
    qi = pl.program_id(2)
    tq = q_ref.shape[0]
    tk = bias_ref.shape[-1]
    q = q_ref[...]
    head0 = lax.broadcasted_iota(jnp.int32, (tq, LANES), 1) < HEAD_DIM
    zero = jnp.zeros_like(q)
    qh = (jnp.where(head0, q, zero), jnp.where(head0, zero, q))
    last_kt = (qi * tq) // tk

    def body(j, carry):
        kt_idx = last_kt - j
        start = pl.multiple_of(kt_idx * tk, tk)
        kt = k_ref[pl.ds(start, tk), :]
        vt = v_ref[pl.ds(start, tk), :]
        delta = qi - kt_idx * (tk // tq)
        new = []
        for h in range(2):
            m, l, acc = carry[h]
            s = lax.dot_general(qh[h], kt, (((1,), (1,)), ((), ())), preferred_element_type=F32)
            s = s + bias_ref[h, delta]
            m_new = jnp.maximum(m, jnp.max(s, axis=-1, keepdims=True))
            alpha = jnp.exp2(m - m_new)
            p = jnp.exp2(s - m_new)
            l_new = alpha * l + jnp.sum(p, axis=-1, keepdims=True)
            acc_new = alpha * acc + jnp.dot(p.astype(BF16), vt, preferred_element_type=F32)
            new.append((m_new, l_new, acc_new))
        return tuple(new)

    init = (jnp.full((tq, 1), NEG_INF, F32), jnp.zeros((tq, 1), F32), jnp.zeros((tq, LANES), F32))
    res = lax.fori_loop(0, last_kt + 1, body, (init, init))
    outs = [acc / l for (_, l, acc) in res]
    o_ref[...] = jnp.where(head0, outs[0], outs[1]).astype(o_ref.dtype)


def _attention(qkv, bias_table, B, S):
    T = qkv.shape[0]
    _, nd, tq, tk = bias_table.shape
    nq = S // tq
    return pl.pallas_call(
        _attn_kernel,
        grid=(HEAD_PAIRS, B, nq),
        in_specs=[pl.BlockSpec((tq, LANES), lambda hp, b, i: (b * nq + i, hp)),
                  pl.BlockSpec((S, LANES), lambda hp, b, i: (b, HEAD_PAIRS + hp)),
                  pl.BlockSpec((S, LANES), lambda hp, b, i: (b, 2 * HEAD_PAIRS + hp)),
                  pl.BlockSpec((2, nd, tq, tk), lambda hp, b, i: (hp, 0, 0, 0))],
        out_specs=pl.BlockSpec((tq, LANES), lambda hp, b, i: (b * nq + i, hp)),
        out_shape=jax.ShapeDtypeStruct((T, ATT_WIDTH), BF16),
        compiler_params=_cparams(("arbitrary", "arbitrary", "arbitrary")),
        name="dilated_attn",
    )(qkv, qkv, qkv, bias_table)


def _split2(x):
    hi = x.astype(BF16)
    lo = (x - hi.astype(F32)).astype(BF16)
    return hi, lo


def _head_sum(x, m2):
    hi, lo = _split2(x)
    return jnp.dot(jnp.concatenate([hi, lo], axis=1), m2, preferred_element_type=F32)


def _rwkv_kernel(*refs, has_vres, n_chunks):
    if has_vres:
        (rwm_ref, rwl_ref, vf_ref, vec_ref, w2_ref, a2_ref, g2_ref, v1_ref, v2_ref,
         o_ref, state_ref, r_s, k_s, v_s, lw_s, kk_s, kb_s, y_s,
         ar_s, inv_s, bk_s, nrbk_s, akv_s, dec_s) = refs
    else:
        (rwm_ref, rwl_ref, vec_ref, w2_ref, a2_ref, g2_ref,
         o_ref, vf_out_ref, state_ref, r_s, k_s, v_s, lw_s, kk_s, kb_s, y_s,
         ar_s, inv_s, bk_s, nrbk_s, akv_s, dec_s) = refs
    C = CHUNK
    W = RWKV_WIDTH

    @pl.when(pl.program_id(1) == 0)
    def _():
        state_ref[...] = jnp.zeros_like(state_ref)

    ri = lax.broadcasted_iota(jnp.int32, (2 * LANES, LANES), 0)
    ci = lax.broadcasted_iota(jnp.int32, (2 * LANES, LANES), 1)
    m2 = ((ri % LANES) // HEAD_DIM == ci // HEAD_DIM).astype(BF16)

    vec = vec_ref[...]
    w0, a0, k_k, k_a, r_k, ln_w, ln_b, v0 = [vec[i:i + 1, :] for i in range(8)]

    r = rwm_ref[:, 0:W]
    k = rwm_ref[:, W:2 * W]
    v = rwm_ref[:, 2 * W:3 * W]
    lora = rwl_ref[:, 0:LANES]
    w_raw = w0 + _mm(jnp.tanh(lora), w2_ref[...])
    z = -w_raw
    w_log = -(jnp.maximum(z, 0.0) + jnp.log1p(jnp.exp(-jnp.abs(z)))) - 0.5
    lw_s[...] = -jnp.exp(w_log)
    a = _sigmoid(a0 + _mm(lora, a2_ref[...]))
    g = _mm(_sigmoid(rwl_ref[:, LANES:RWL_COLS]), g2_ref[...])
    if has_vres:
        mix = _sigmoid(v0 + _mm(_mm(v, v1_ref[...]), v2_ref[...]))
        v = v + (vf_ref[...] - v) * mix
    else:
        vf_out_ref[...] = v
    kk = k * k_k
    for p in range(HEAD_PAIRS):
        ls = slice(p * LANES, (p + 1) * LANES)
        kkp = kk[:, ls]
        nrm = jnp.sqrt(_head_sum(kkp * kkp, m2))
        kkp = kkp / jnp.maximum(nrm, 1e-12)
        kk_s[:, ls] = kkp
        kb_s[:, ls] = kkp * a[:, ls]
    kmod = k * (1.0 + (a - 1.0) * k_a)
    r_s[...] = r
    k_s[...] = kmod
    v_s[...] = v

    ri = lax.broadcasted_iota(jnp.int32, (LANES, LANES), 0)
    ci = lax.broadcasted_iota(jnp.int32, (LANES, LANES), 1)
    same_head = (ri // C) == (ci // C)
    strict = same_head & ((ri % C) > (ci % C))
    incl = same_head & ((ri % C) >= (ci % C))
    eye = (ri == ci).astype(F32)
    lane = lax.broadcasted_iota(jnp.int32, (C, LANES), 1)
    head0 = lane < HEAD_DIM
    tri_r = lax.broadcasted_iota(jnp.int32, (C, 3 * C), 0)
    tri_c = lax.broadcasted_iota(jnp.int32, (C, 3 * C), 1)
    tri3 = ((tri_c % C) <= tri_r).astype(BF16)

    lane_slices = [slice(p * LANES, (p + 1) * LANES) for p in range(HEAD_PAIRS)]

    def cumsum(x):
        hi = x.astype(BF16)
        rem = x - hi.astype(F32)
        mid = rem.astype(BF16)
        lo = (rem - mid.astype(F32)).astype(BF16)
        return jnp.dot(tri3, jnp.concatenate([hi, mid, lo], axis=0), preferred_element_type=F32)

    def per_head_rows(x, y):
        zero = jnp.zeros_like(x)
        return jnp.concatenate([jnp.where(head0, x, zero), jnp.where(head0, zero, x),
                                jnp.where(head0, y, zero), jnp.where(head0, zero, y)], axis=0)

    def intra_body(it, _):
        units = [(it * chunk_group + g, p) for g in range(chunk_group) for p in range(HEAD_PAIRS)]
        idx = range(len(units))
        rows = [pl.ds(pl.multiple_of(c * C, C), C) for c, _ in units]
        ls = [lane_slices[p] for _, p in units]
        rc = [r_s[rows[u], ls[u]] for u in idx]
        kc = [k_s[rows[u], ls[u]] for u in idx]
        vc = [v_s[rows[u], ls[u]] for u in idx]
        lw = [lw_s[rows[u], ls[u]] for u in idx]
        kkc = [kk_s[rows[u], ls[u]] for u in idx]
        kbc = [kb_s[rows[u], ls[u]] for u in idx]
        cum = [cumsum(lw[u]) for u in idx]
        total = [cum[u][C - 1:C, :] for u in idx]
        g_inv = [jnp.exp(-cum[u]) for u in idx]
        a_t = [-kkc[u] * jnp.exp(cum[u] - lw[u]) for u in idx]
        r_t = [rc[u] * jnp.exp(cum[u]) for u in idx]
        b_t = [kbc[u] * g_inv[u] for u in idx]
        k_t = [kc[u] * g_inv[u] for u in idx]
        to_end = [jnp.exp(total[u] - cum[u]) for u in idx]
        gram = [_mm_nt(per_head_rows(a_t[u], r_t[u]), per_head_rows(b_t[u], k_t[u])) for u in idx]
        n_ab = [jnp.where(strict, gram[u][:LANES, :LANES], 0.0) for u in idx]
        n_ak = [jnp.where(strict, gram[u][:LANES, LANES:], 0.0) for u in idx]
        n_rb = [jnp.where(incl, gram[u][LANES:, :LANES], 0.0) for u in idx]
        n_rk = [jnp.where(incl, gram[u][LANES:, LANES:], 0.0) for u in idx]
        inv = [eye + n_ab[u] for u in idx]
        pw = [_mm(n_ab[u], n_ab[u]) for u in idx]
        n_rounds = int(math.log2(C)) - 1
        for i in range(n_rounds):
            if i + 1 < n_rounds:
                both = [_mm(pw[u], jnp.concatenate([inv[u], pw[u]], axis=1)) for u in idx]
                inv = [inv[u] + both[u][:, :LANES] for u in idx]
                pw = [both[u][:, LANES:] for u in idx]
            else:
                inv = [inv[u] + _mm(pw[u], inv[u]) for u in idx]
        for u, (c, p) in enumerate(units):
            ar_s[c, p] = jnp.concatenate([a_t[u], r_t[u]], axis=0).astype(BF16)
            inv_s[c, p] = inv[u].astype(BF16)
            akv_s[c, p] = _mm(n_ak[u], jnp.concatenate([vc[u], vc[u]], axis=0))
            nrbk_s[c, p] = jnp.concatenate([n_rb[u], n_rk[u]], axis=1).astype(BF16)
            bk_s[c, p] = jnp.concatenate([kbc[u] * to_end[u], kc[u] * to_end[u]], axis=0).astype(BF16)
            dec_s[c, p] = jnp.broadcast_to(jnp.exp(total[u]), (8, LANES))
        return 0

    def state_body(c, _):
        rows = pl.ds(pl.multiple_of(c * C, C), C)
        pairs = range(HEAD_PAIRS)
        vc = [v_s[rows, lane_slices[p]] for p in pairs]
        st = [state_ref[p] for p in pairs]
        ah = [_mm_nt(ar_s[c, p], st[p]) for p in pairs]
        u_stack = [_mm(inv_s[c, p], jnp.concatenate([ah[p][:C], ah[p][:C]], axis=0) + akv_s[c, p])
                   for p in pairs]
        y_stack = [jnp.concatenate([ah[p][C:], ah[p][C:]], axis=0)
                   + _mm(nrbk_s[c, p], jnp.concatenate([u_stack[p], vc[p], vc[p]], axis=0)) for p in pairs]
        for p in pairs:
            y_s[rows, lane_slices[p]] = jnp.where(head0, y_stack[p][:C], y_stack[p][C:])
        uv_t = [jnp.concatenate([jnp.where(head0, u_stack[p][:C], u_stack[p][C:]), vc[p]], axis=0).T
                for p in pairs]
        st_new = [st[p] * dec_s[c, p][0:1, :] + _mm(uv_t[p], bk_s[c, p]) for p in pairs]
        for p in pairs:
            state_ref[p] = jnp.where(same_head, st_new[p], 0.0)
        return 0

    chunk_group = 2 if n_chunks % 2 == 0 else 1
    lax.fori_loop(0, n_chunks // chunk_group, intra_body, 0)
    lax.fori_loop(0, n_chunks, state_body, 0)

    for p in range(HEAD_PAIRS):
        ls = slice(p * LANES, (p + 1) * LANES)
        y = y_s[:, ls]
        mean = _head_sum(y, m2) * (1.0 / HEAD_DIM)
        d = y - mean
        var = _head_sum(d * d, m2) * (1.0 / HEAD_DIM)
        yn = d * lax.rsqrt(var + GN_EPS) * ln_w[:, ls] + ln_b[:, ls]
        bonus = _head_sum(r_s[:, ls] * k_s[:, ls] * r_k[:, ls], m2) * v_s[:, ls]
        o_ref[:, ls] = ((yn + bonus) * g[:, ls]).astype(o_ref.dtype)


def _rwkv(rwm, rwl, v_first, vec, w2p, a2p, g2p, v1p, v2p, B, S):
    T = rwm.shape[0]
    W = RWKV_WIDTH
    tb = _tile(S, 512)
    nS = S // tb
    n_chunks = tb // CHUNK
    has_vres = v_first is not None
    row = lambda b, i: (b * nS + i, 0)
    full = lambda b, i: (0, 0)
    in_specs = [pl.BlockSpec((tb, RWM_COLS), row), pl.BlockSpec((tb, RWL_COLS), row)]
    args = [rwm, rwl]
    if has_vres:
        in_specs.append(pl.BlockSpec((tb, W), row))
        args.append(v_first)
    in_specs += [pl.BlockSpec(vec.shape, full), pl.BlockSpec(w2p.shape, full),
                 pl.BlockSpec(a2p.shape, full), pl.BlockSpec(g2p.shape, full)]
    args += [vec, w2p, a2p, g2p]
    if has_vres:
        in_specs += [pl.BlockSpec(v1p.shape, full), pl.BlockSpec(v2p.shape, full)]
        args += [v1p, v2p]
    out_specs = [pl.BlockSpec((tb, W), row)]
    out_shape = [jax.ShapeDtypeStruct((T, W), BF16)]
    if not has_vres:
        out_specs.append(pl.BlockSpec((tb, W), row))
        out_shape.append(jax.ShapeDtypeStruct((T, W), F32))
    outs = pl.pallas_call(
        functools.partial(_rwkv_kernel, has_vres=has_vres, n_chunks=n_chunks),
        grid=(B, nS),
        in_specs=in_specs,
        out_specs=out_specs,
        out_shape=out_shape,
        scratch_shapes=([pltpu.VMEM((HEAD_PAIRS, LANES, LANES), F32)] + [pltpu.VMEM((tb, W), F32)] * 7
                        + [pltpu.VMEM((n_chunks, HEAD_PAIRS, 2 * CHUNK, LANES), BF16)] * 3
                        + [pltpu.VMEM((n_chunks, HEAD_PAIRS, 2 * CHUNK, 2 * LANES), BF16),
                           pltpu.VMEM((n_chunks, HEAD_PAIRS, 2 * CHUNK, LANES), F32),
                           pltpu.VMEM((n_chunks, HEAD_PAIRS, 8, LANES), F32)]),
        compiler_params=_cparams(("arbitrary", "arbitrary")),
        name="rwkv7",
    )(*args)
    return (outs[0], v_first) if has_vres else (outs[0], outs[1])


def _outproj_kernel(*refs, moe):
    if moe:
        (x_ref, att_ref, rw_ref, mod_ref, g_ref, w_ref, rw_w_ref, rw_b_ref,
         x1_ref, h_ref, info_ref) = refs
    else:
        x_ref, att_ref, rw_ref, mod_ref, g_ref, w_ref, x1_ref, h_ref = refs
    mix = (jnp.dot(att_ref[...], w_ref[0, :ATT_WIDTH, :], preferred_element_type=F32)
           + jnp.dot(rw_ref[...], w_ref[0, ATT_WIDTH:, :], preferred_element_type=F32))
    x1 = x_ref[...] + mod_ref[0, 2:3, :] * mix
    x1_ref[...] = x1
    h = _rms_mod(x1, g_ref[...], mod_ref[0, 4:5, :], mod_ref[0, 3:4, :])
    h_ref[...] = h.astype(h_ref.dtype)
    if moe:
        h_hi, h_lo = _split2(h)
        w_hi, w_lo = _split2(rw_w_ref[...])
        logits = (jnp.dot(h_hi, w_hi, preferred_element_type=F32)
                  + jnp.dot(h_hi, w_lo, preferred_element_type=F32)
                  + jnp.dot(h_lo, w_hi, preferred_element_type=F32)) + rw_b_ref[...]
        lane = lax.broadcasted_iota(jnp.int32, logits.shape, 1)
        logits = jnp.where(lane < N_EXPERTS, logits, -jnp.inf)
        m1 = jnp.max(logits, axis=-1, keepdims=True)
        i1 = jnp.min(jnp.where(logits == m1, lane, LANES), axis=-1, keepdims=True)
        rest = jnp.where(lane == i1, -jnp.inf, logits)
        m2 = jnp.max(rest, axis=-1, keepdims=True)
        i2 = jnp.min(jnp.where(rest == m2, lane, LANES), axis=-1, keepdims=True)
        e = jnp.exp(m2 - m1)
        g1 = 1.0 / (1.0 + e)
        g2 = e / (1.0 + e)
        info = jnp.where(lane == 0, i1.astype(F32),
                         jnp.where(lane == 1, i2.astype(F32),
                                   jnp.where(lane == 2, g1, jnp.where(lane == 3, g2, 0.0))))
        info_ref[...] = info


def _outproj(x, att, rw, mod, g, w_out_b, B, S, l, router=None):
    T, D = x.shape
    tm = _tile(S, 512)
    nS = S // tm
    moe = router is not None
    row = lambda b, i: (b * nS + i, 0)
    full = lambda b, i: (0, 0)
    in_specs = [pl.BlockSpec((tm, D), row), pl.BlockSpec((tm, ATT_WIDTH), row),
                pl.BlockSpec((tm, RWKV_WIDTH), row),
                pl.BlockSpec((1, 6, D), lambda b, i: (l * B + b, 0, 0)),
                pl.BlockSpec((1, D), full),
                pl.BlockSpec((1, D, D), lambda b, i: (l, 0, 0))]
    args = [x, att, rw, mod, g, w_out_b]
    out_specs = [pl.BlockSpec((tm, D), row), pl.BlockSpec((tm, D), row)]
    out_shape = [jax.ShapeDtypeStruct((T, D), F32), jax.ShapeDtypeStruct((T, D), F32 if moe else BF16)]
    if moe:
        in_specs += [pl.BlockSpec((D, LANES), full), pl.BlockSpec((1, LANES), full)]
        args += list(router)
        out_specs.append(pl.BlockSpec((tm, LANES), row))
        out_shape.append(jax.ShapeDtypeStruct((T, LANES), F32))
    return pl.pallas_call(
        functools.partial(_outproj_kernel, moe=moe),
        grid=(B, nS),
        in_specs=in_specs,
        out_specs=out_specs,
        out_shape=out_shape,
        compiler_params=_cparams(("arbitrary", "arbitrary")),
        name="outproj",
    )(*args)


def _ffn_kernel(x_ref, h_ref, mod_ref, wg_ref, wu_ref, wd_ref, o_ref, acc_ref):
    f = pl.program_id(1)

    @pl.when(f == 0)
    def _():
        acc_ref[...] = jnp.zeros_like(acc_ref)

    hb = h_ref[...]
    gate = jnp.dot(hb, wg_ref[0], preferred_element_type=F32)
    up = jnp.dot(hb, wu_ref[0], preferred_element_type=F32)
    mid = (_silu(gate) * up).astype(BF16)
    acc_ref[...] += jnp.dot(mid, wd_ref[0], preferred_element_type=F32)

    @pl.when(f == pl.num_programs(1) - 1)
    def _():
        o_ref[...] = x_ref[...] + mod_ref[0, 5:6, :] * acc_ref[...]


def _ffn_dense(x1, h, mod, wg, wu, wd, B, S, l, li):
    T, D = x1.shape
    F = wg.shape[-1]
    tm = _tile(S, 512)
    tf = _tile(F, 1408)
    per_seq = S // tm
    return pl.pallas_call(
        _ffn_kernel,
        grid=(T // tm, F // tf),
        in_specs=[pl.BlockSpec((tm, D), lambda i, f: (i, 0)),
                  pl.BlockSpec((tm, D), lambda i, f: (i, 0)),
                  pl.BlockSpec((1, 6, D), lambda i, f: (l * B + i // per_seq, 0, 0)),
                  pl.BlockSpec((1, D, tf), lambda i, f: (li, 0, f)),
                  pl.BlockSpec((1, D, tf), lambda i, f: (li, 0, f)),
                  pl.BlockSpec((1, tf, D), lambda i, f: (li, f, 0))],
        out_specs=pl.BlockSpec((tm, D), lambda i, f: (i, 0)),
        out_shape=jax.ShapeDtypeStruct((T, D), F32),
        scratch_shapes=[pltpu.VMEM((tm, D), F32)],
        compiler_params=_cparams(("arbitrary", "arbitrary")),
        name="ffn_dense",
    )(x1, h, mod, wg, wu, wd)


def _experts_kernel(te_ref, nu_ref, tok_ref, h_ref, wg_ref, wu_ref, wd_ref, y_ref,
                    land_ref, xb_ref, acc_ref, sem):
    i = pl.program_id(0)
    f = pl.program_id(1)
    tm = land_ref.shape[0]
    n_used = nu_ref[0]

    def start_gather(tile):
        base = tile * tm

        def body(j, _):
            pltpu.make_async_copy(h_ref.at[pl.ds(tok_ref[base + j], 1)],
                                  land_ref.at[pl.ds(j, 1)], sem).start()
            return 0

        lax.fori_loop(0, tm, body, 0, unroll=DMA_ISSUE_UNROLL)

    @pl.when(i < n_used)
    def _():
        @pl.when(f == 0)
        def _():
            @pl.when(i == 0)
            def _():
                start_gather(0)

            pltpu.make_async_copy(h_ref.at[pl.ds(0, tm)], land_ref, sem).wait()
            xb_ref[...] = land_ref[...].astype(BF16)
            acc_ref[...] = jnp.zeros_like(acc_ref)

            @pl.when(i + 1 < n_used)
            def _():
                start_gather(i + 1)

        xb = xb_ref[...]
        gate = jnp.dot(xb, wg_ref[0, 0], preferred_element_type=F32)
        up = jnp.dot(xb, wu_ref[0, 0], preferred_element_type=F32)
        mid = (_silu(gate) * up).astype(BF16)
        acc_ref[...] += jnp.dot(mid, wd_ref[0, 0], preferred_element_type=F32)

        @pl.when(f == pl.num_programs(1) - 1)
        def _():
            y_ref[...] = acc_ref[...]

    @pl.when((i >= n_used) & (f == 0))
    def _():
        y_ref[...] = jnp.zeros_like(y_ref)


def _experts(h, row_tok, tile_e, n_used, wg, wu, wd, li, tm):
    n_rows = row_tok.shape[0]
    D = h.shape[1]
    assert h.shape[0] >= tm
    F = wg.shape[-1]
    tf = _tile(F, 1792)
    nf = F // tf

    def rows(i, f, te, nu, tok):
        return (i, 0)

    def wcol(i, f, te, nu, tok):
        return (li, te[jnp.minimum(i, nu[0] - 1)], 0, jnp.where(i < nu[0], f, nf - 1))

    def wrow(i, f, te, nu, tok):
        return (li, te[jnp.minimum(i, nu[0] - 1)], jnp.where(i < nu[0], f, nf - 1), 0)

    return pl.pallas_call(
        _experts_kernel,
        grid_spec=pltpu.PrefetchScalarGridSpec(
            num_scalar_prefetch=3,
            grid=(n_rows // tm, nf),
            in_specs=[pl.BlockSpec(memory_space=pl.ANY),
                      pl.BlockSpec((1, 1, D, tf), wcol),
                      pl.BlockSpec((1, 1, D, tf), wcol),
                      pl.BlockSpec((1, 1, tf, D), wrow)],
            out_specs=pl.BlockSpec((tm, D), rows),
            scratch_shapes=[pltpu.VMEM((tm, D), F32), pltpu.VMEM((tm, D), BF16),
                            pltpu.VMEM((tm, D), F32), pltpu.SemaphoreType.DMA(())]),
        out_shape=jax.ShapeDtypeStruct((n_rows, D), F32),
        compiler_params=_cparams(("arbitrary", "arbitrary")),
        name="moe_experts",
    )(tile_e, n_used, row_tok, h, wg, wu, wd)


def _combine_kernel(pos_ref, y_ref, x_ref, info_ref, mod_ref, fg_ref, o_ref, buf, sem, *, tc, final):
    i = pl.program_id(0)
    slot = i % 2

    def start_gather(tile, dst_slot):
        base = tile * (2 * tc)

        def body(j, _):
            pltpu.make_async_copy(y_ref.at[pl.ds(pos_ref[base + j], 1)],
                                  buf.at[dst_slot, pl.ds(j, 1)], sem.at[dst_slot]).start()
            return 0

        lax.fori_loop(0, 2 * tc, body, 0, unroll=DMA_ISSUE_UNROLL)

    @pl.when(i == 0)
    def _():
        start_gather(0, 0)

    @pl.when(i + 1 < pl.num_programs(0))
    def _():
        start_gather(i + 1, 1 - slot)

    pltpu.make_async_copy(y_ref.at[pl.ds(0, 2 * tc)], buf.at[slot], sem.at[slot]).wait()
    info = info_ref[...]
    ff = info[:, 2:3] * buf[slot, 0:tc, :] + info[:, 3:4] * buf[slot, tc:2 * tc, :]
    out = x_ref[...] + mod_ref[0, 5:6, :] * ff
    if final:
        ms = jnp.mean(out * out, axis=-1, keepdims=True)
        out = out * lax.rsqrt(ms + RMS_EPS) * fg_ref[...]
    o_ref[...] = out


def _combine(pos, y_rows, x1, info, mod, final_g, B, S, l, final):
    T, D = x1.shape
    tc = _tile(S, 256)
    per_seq = S // tc
    return pl.pallas_call(
        functools.partial(_combine_kernel, tc=tc, final=final),
        grid_spec=pltpu.PrefetchScalarGridSpec(
            num_scalar_prefetch=1,
            grid=(T // tc,),
            in_specs=[pl.BlockSpec(memory_space=pl.ANY),
                      pl.BlockSpec((tc, D), lambda i, pos: (i, 0)),
                      pl.BlockSpec((tc, LANES), lambda i, pos: (i, 0)),
                      pl.BlockSpec((1, 6, D), lambda i, pos: (l * B + i // per_seq, 0, 0)),
                      pl.BlockSpec((1, D), lambda i, pos: (0, 0))],
            out_specs=pl.BlockSpec((tc, D), lambda i, pos: (i, 0)),
            scratch_shapes=[pltpu.VMEM((2, 2 * tc, D), F32), pltpu.SemaphoreType.DMA((2,))]),
        out_shape=jax.ShapeDtypeStruct((T, D), F32),
        compiler_params=_cparams(("arbitrary",)),
        name="moe_combine",
    )(pos, y_rows, x1, info, mod, final_g)


def _moe(x1, h, info, mod, wg, wu, wd, final_g, B, S, l, li, final):
    T, D = x1.shape
    TK = T * TOP_K
    tm = _tile(TK // N_EXPERTS, 512)
    n_rows = TK + (N_EXPERTS - 1) * tm
    flat_e = info[:, :TOP_K].astype(jnp.int32).reshape(TK)
    onehot = (flat_e[:, None] == jnp.arange(N_EXPERTS, dtype=jnp.int32)[None, :]).astype(jnp.int32)
    csum = jnp.cumsum(onehot, axis=0)
    counts = csum[-1]
    rank = jnp.sum(onehot * csum, axis=1) - 1
    padded = (counts + tm - 1) // tm * tm
    pad_ends = jnp.cumsum(padded)
    pad_starts = pad_ends - padded
    dest = pad_starts[flat_e] + rank
    row_tok = jnp.zeros((n_rows,), jnp.int32).at[dest].set(jnp.arange(TK, dtype=jnp.int32) // TOP_K)
    tile_start = jnp.arange(n_rows // tm, dtype=jnp.int32) * tm
    tile_e = jnp.minimum(jnp.searchsorted(pad_ends, tile_start, side='right'), N_EXPERTS - 1).astype(jnp.int32)
    n_used_rows = pad_ends[-1:].astype(jnp.int32)
    n_used_tiles = n_used_rows // tm
    tc = _tile(S, 256)
    pos = dest.reshape(T // tc, tc, TOP_K).transpose(0, 2, 1).reshape(TK).astype(jnp.int32)

    y_rows = _experts(h, row_tok, tile_e, n_used_tiles, wg, wu, wd, li, tm)
    return _combine(pos, y_rows, x1, info, mod, final_g, B, S, l, final)


def kernel(x, c, w_ada, b_ada, norm1_g, norm2_g, final_g, w_in, w_out, rel_bias, rwkv_mu, rwkv_w0, rwkv_w2, rwkv_a0, rwkv_a2, rwkv_g2, rwkv_k_k, rwkv_k_a, rwkv_r_k, rwkv_ln_w, rwkv_ln_b, rwkv_v0, rwkv_v1, rwkv_v2, ffn_w_gate, ffn_w_up, ffn_w_down, moe_router_w, moe_router_b, moe_w_gate, moe_w_up, moe_w_down):
    B, S, D = x.shape
    L = w_in.shape[0]
    T = B * S
    W = RWKV_WIDTH

    mod = _ada_mod(c, w_ada, b_ada).reshape(L * B, 6, D)
    w_in_b = jnp.pad(w_in, ((0, 0), (0, 0), (0, N_IN_PAD - N_IN))).astype(BF16)
    w_out_b = w_out.astype(BF16)
    ffn_w = [w.astype(BF16) for w in (ffn_w_gate, ffn_w_up, ffn_w_down)]
    moe_w = [w.astype(BF16) for w in (moe_w_gate, moe_w_up, moe_w_down)]
    mu_p = jnp.pad(rwkv_mu, ((0, 0), (0, RWM_COLS + RWL_COLS - rwkv_mu.shape[1])))
    bias_table = _att_bias_table(rel_bias, S, _tile(S, 256), _tile(S, 512))
    zeros_w = jnp.zeros((1, W), F32)
    final_g2 = final_g.reshape(1, D)

    xf = x.reshape(T, D)
    v_first = None
    for l in range(L):
        qkv, rwm, rwl = _inproj(xf, mod, norm1_g[l].reshape(1, D), w_in_b, mu_p[l].reshape(1, -1), B, S, l)
        att = _attention(qkv, bias_table, B, S)
        v0 = rwkv_v0[l - 1].reshape(1, W) if l > 0 else zeros_w
        vec = jnp.concatenate([rwkv_w0[l].reshape(1, W), rwkv_a0[l].reshape(1, W), rwkv_k_k[l].reshape(1, W),
                               rwkv_k_a[l].reshape(1, W), rwkv_r_k[l].reshape(1, W), rwkv_ln_w[l].reshape(1, W),
                               rwkv_ln_b[l].reshape(1, W), v0], axis=0)
        w2p = jnp.pad(rwkv_w2[l], ((0, LANES - D_DECAY_LORA), (0, 0)))
        a2p = jnp.pad(rwkv_a2[l], ((D_DECAY_LORA, 0), (0, 0)))
        g2p = jnp.pad(rwkv_g2[l], ((0, RWL_COLS - LANES - D_GATE_LORA), (0, 0)))
        if l > 0:
            v1p = jnp.pad(rwkv_v1[l - 1], ((0, 0), (0, LANES - D_MV_LORA)))
            v2p = jnp.pad(rwkv_v2[l - 1], ((0, LANES - D_MV_LORA), (0, 0)))
        else:
            v1p = v2p = None
        rw, v_first = _rwkv(rwm, rwl, v_first, vec, w2p, a2p, g2p, v1p, v2p, B, S)
        li = l // 2
        if l % 2 == 0:
            x1, h = _outproj(xf, att, rw, mod, norm2_g[l].reshape(1, D), w_out_b, B, S, l)
            xf = _ffn_dense(x1, h, mod, *ffn_w, B, S, l, li)
        else:
            router = (jnp.pad(moe_router_w[li], ((0, 0), (0, LANES - N_EXPERTS))),
                      jnp.pad(moe_router_b[li], (0, LANES - N_EXPERTS)).reshape(1, LANES))
            x1, h, info = _outproj(xf, att, rw, mod, norm2_g[l].reshape(1, D), w_out_b, B, S, l, router)
            xf = _moe(x1, h, info, mod, *moe_w, final_g2, B, S, l, li,
                      final=(l == L - 1))
    if L % 2 == 1:
        raise NotImplementedError("final norm is fused into the last (expert) layer")
    return xf.reshape(B, S, D)
```

```python
import functools
import math

import numpy as np
import jax
import jax.numpy as jnp
from jax import lax
from jax.experimental import pallas as pl
from jax.experimental.pallas import tpu as pltpu

F32 = jnp.float32
BF16 = jnp.bfloat16

D_MODEL = 1024
HEAD_DIM = 64
ATT_WIDTH = 512
RWKV_WIDTH = 512
N_HEADS = 8
HEAD_PAIRS = 4
LANES = 128
DILATED_GROUPS = ((128, 1), (512, 4), (2048, 16))
N_BUCKETS = 32
MAX_DISTANCE = 2048
NEG_INF = -1e30
D_DECAY_LORA = 64
D_AAA_LORA = 64
D_MV_LORA = 32
D_GATE_LORA = 160
N_IN = 3 * ATT_WIDTH + 3 * RWKV_WIDTH + D_DECAY_LORA + D_AAA_LORA + D_GATE_LORA
QKV_COLS = 3 * ATT_WIDTH
RWM_COLS = 3 * RWKV_WIDTH
RWL_COLS = 384
N_IN_PAD = QKV_COLS + RWM_COLS + RWL_COLS
GN_EPS = HEAD_DIM * 1e-5
RMS_EPS = 1e-6
N_EXPERTS = 8
TOP_K = 2
CHUNK = 64
VMEM_LIMIT = 56 * 1024 * 1024
LOG2_E = math.log2(math.e)
Q_SCALE = LOG2_E / math.sqrt(HEAD_DIM)
DMA_ISSUE_UNROLL = 8


def _cparams(sem):
    return pltpu.CompilerParams(dimension_semantics=sem, vmem_limit_bytes=VMEM_LIMIT)


def _tile(n, pref):
    t = min(n, pref)
    while n % t:
        t //= 2
    return t


def _mm(a, b):
    return jnp.dot(a.astype(BF16), b.astype(BF16), preferred_element_type=F32)


def _mm_nt(a, b):
    return lax.dot_general(a.astype(BF16), b.astype(BF16), (((1,), (1,)), ((), ())),
                           preferred_element_type=F32)


def _sigmoid(x):
    return 1.0 / (1.0 + jnp.exp(-x))


def _silu(x):
    return x * _sigmoid(x)


def _ada_kernel(c_ref, w_ref, b_ref, o_ref):
    ca = _silu(c_ref[...])
    o_ref[0] = _mm(ca, w_ref[0]) + b_ref[0]


def _ada_mod(c, w_ada, b_ada):
    L, D, N = w_ada.shape
    B = c.shape[0]
    tn = _tile(N, 1536)
    return pl.pallas_call(
        _ada_kernel,
        grid=(L, N // tn),
        in_specs=[pl.BlockSpec((B, D), lambda l, j: (0, 0)),
                  pl.BlockSpec((1, D, tn), lambda l, j: (l, 0, j)),
                  pl.BlockSpec((1, 1, tn), lambda l, j: (l, 0, j))],
        out_specs=pl.BlockSpec((1, B, tn), lambda l, j: (l, 0, j)),
        out_shape=jax.ShapeDtypeStruct((L, B, N), F32),
        compiler_params=_cparams(("arbitrary", "arbitrary")),
        name="ada_mod",
    )(c, w_ada, b_ada.reshape(L, 1, N))


def _rms_mod(xf, g, scale, shift):
    ms = jnp.mean(xf * xf, axis=-1, keepdims=True)
    return (xf * lax.rsqrt(ms + RMS_EPS) * g) * (1.0 + scale) + shift


def _inproj_kernel(x_ref, mod_ref, g_ref, w_ref, mu_ref, qkv_ref, rwm_ref, rwl_ref, carry_ref):
    i = pl.program_id(1)
    tm = x_ref.shape[0]
    h = _rms_mod(x_ref[...], g_ref[...], mod_ref[0, 1:2, :], mod_ref[0, 0:1, :])
    acc = jnp.dot(h.astype(BF16), w_ref[0], preferred_element_type=F32)
    qkv_ref[:, :ATT_WIDTH] = (acc[:, :ATT_WIDTH] * Q_SCALE).astype(BF16)
    qkv_ref[:, ATT_WIDTH:] = acc[:, ATT_WIDTH:QKV_COLS].astype(BF16)
    p = acc[:, QKV_COLS:]
    first = jnp.where(i == 0, 0.0, carry_ref[...])
    row = lax.broadcasted_iota(jnp.int32, p.shape, 0)
    prev = jnp.where(row == 0, first, pltpu.roll(p, 1, 0))
    carry_ref[...] = p[tm - 1:tm, :]
    pm = p + mu_ref[...] * (prev - p)
    rwm_ref[...] = pm[:, :RWM_COLS]
    rwl_ref[...] = pm[:, RWM_COLS:]


def _inproj(x, mod, g, w_in_l, mu_l, B, S, l):
    T, D = x.shape
    tm = _tile(S, 512)
    nS = S // tm
    return pl.pallas_call(
        _inproj_kernel,
        grid=(B, nS),
        in_specs=[pl.BlockSpec((tm, D), lambda b, i: (b * nS + i, 0)),
                  pl.BlockSpec((1, 6, D), lambda b, i: (l * B + b, 0, 0)),
                  pl.BlockSpec((1, D), lambda b, i: (0, 0)),
                  pl.BlockSpec((1, D, N_IN_PAD), lambda b, i: (l, 0, 0)),
                  pl.BlockSpec((1, RWM_COLS + RWL_COLS), lambda b, i: (0, 0))],
        out_specs=[pl.BlockSpec((tm, QKV_COLS), lambda b, i: (b * nS + i, 0)),
                   pl.BlockSpec((tm, RWM_COLS), lambda b, i: (b * nS + i, 0)),
                   pl.BlockSpec((tm, RWL_COLS), lambda b, i: (b * nS + i, 0))],
        out_shape=[jax.ShapeDtypeStruct((T, QKV_COLS), BF16),
                   jax.ShapeDtypeStruct((T, RWM_COLS), F32),
                   jax.ShapeDtypeStruct((T, RWL_COLS), F32)],
        scratch_shapes=[pltpu.VMEM((1, RWM_COLS + RWL_COLS), F32)],
        compiler_params=_cparams(("arbitrary", "arbitrary")),
        name="inproj",
    )(x, mod, g, w_in_l, mu_l)


def _t5_bucket(n):
    max_exact = N_BUCKETS // 2
    large = max_exact + (np.log(np.maximum(n, 1) / max_exact) / np.log(MAX_DISTANCE / max_exact)
                         * (N_BUCKETS - max_exact)).astype(np.int32)
    large = np.minimum(large, N_BUCKETS - 1)
    return np.where(n < max_exact, n, large).astype(np.int32)


ATT_BLOCK = 128
ATT_KEYS = 256


def _toeplitz(vec, c0, rows, cols):
    H, n = vec.shape
    period = rows + cols - 1
    w = jnp.concatenate([jnp.full((H, cols - 1), NEG_INF, F32), vec, jnp.full((H, rows), NEG_INF, F32)], axis=1)
    rw = w[:, ::-1]
    a = w.shape[1] - 1 - (c0 + cols - 1)
    z = jnp.concatenate([rw[:, a:a + cols], rw[:, a - (rows - 1):a]], axis=1)
    flat = jnp.tile(z, (1, rows))[:, :rows * (period - 1)]
    return flat.reshape(H, rows, period - 1)[..., :cols]


def _att_tables(rel_bias, S):
    (w1, d1), (w2, d2), (w3, d3) = DILATED_GROUPS
    assert d1 == 1 and w1 == ATT_KEYS - ATT_BLOCK and w2 // d2 == w1 and S % (d3 * 8) == 0
    off = np.arange(S)
    mult = np.zeros(S, np.int64)
    for w, d in DILATED_GROUPS:
        mult += ((off % d == 0) & (off <= w)).astype(np.int64)
    logm = np.where(mult > 0, np.log(np.maximum(mult, 1)), NEG_INF).astype(np.float32)
    per_off = (rel_bias[_t5_bucket(off)].T.astype(F32) + jnp.asarray(logm)[None, :]) * LOG2_E
    u2 = np.arange(S // d2)
    u3 = np.arange(S // d3)
    band = jnp.where(jnp.asarray(off <= w1)[None], per_off, NEG_INF)
    mid = jnp.where(jnp.asarray((u2 > w1 // d2) & (u2 <= w2 // d2))[None], per_off[:, ::d2], NEG_INF)
    far = jnp.where(jnp.asarray((u3 > w2 // d3) & (u3 <= w3 // d3))[None], per_off[:, ::d3], NEG_INF)
    shift = ATT_KEYS - ATT_BLOCK
    t_band = jnp.stack([_toeplitz(band, 0, ATT_BLOCK, ATT_KEYS), _toeplitz(band, shift, ATT_BLOCK, ATT_KEYS)], axis=1)
    t_mid = jnp.stack([_toeplitz(mid, 0, ATT_BLOCK, ATT_KEYS), _toeplitz(mid, shift, ATT_BLOCK, ATT_KEYS)], axis=1)
    t_far = _toeplitz(far, 0, S // d3, S // d3)
    return t_band, t_mid, t_far


def _attend(units):
    idx = range(len(units))
    heads = range(2)
    head0 = [lax.broadcasted_iota(jnp.int32, u[0].shape, 1) < HEAD_DIM for u in units]
    qh = [[jnp.where(head0[i], units[i][0], jnp.zeros_like(units[i][0])),
           jnp.where(head0[i], jnp.zeros_like(units[i][0]), units[i][0])] for i in idx]
    s = [[lax.dot_general(qh[i][h], units[i][1], (((1,), (1,)), ((), ())), preferred_element_type=F32)
          + units[i][3 + h] for h in heads] for i in idx]
    mx = [[jnp.max(s[i][h], axis=-1, keepdims=True) for h in heads] for i in idx]
    p = [[jnp.exp2(s[i][h] - mx[i][h]) for h in heads] for i in idx]
    ps = [[jnp.sum(p[i][h], axis=-1, keepdims=True) for h in heads] for i in idx]
    pv = [[jnp.dot(p[i][h].astype(BF16), units[i][2], preferred_element_type=F32) for h in heads] for i in idx]
    out = []
    for i in idx:
        shape = units[i][0].shape
        out.append((jnp.where(head0[i], jnp.broadcast_to(mx[i][0], shape), jnp.broadcast_to(mx[i][1], shape)),
                    jnp.where(head0[i], jnp.broadcast_to(ps[i][0], shape), jnp.broadcast_to(ps[i][1], shape)),
                    jnp.where(head0[i], pv[i][0], pv[i][1])))
    return out


def _attn_kernel(qa_ref, ka_ref, va_ref, qm_ref, km_ref, vm_ref, qf_ref, kf_ref, vf_ref,
                 tband_ref, tmid_ref, tfar_ref, o_ref, part_ref):
    R, K = ATT_BLOCK, ATT_KEYS
    S = qa_ref.shape[0]
    n_mid_res, n_mid = qm_ref.shape[0], qm_ref.shape[1]
    n_far_res, n_far = qf_ref.shape[0], qf_ref.shape[1]
    group = 4

    def window(blk):
        start = pl.multiple_of(jnp.maximum(blk * R - (K - R), 0), R)
        return pl.ds(pl.multiple_of(blk * R, R), R), pl.ds(start, K), jnp.minimum(blk, 1)

    def park(cls, rows, results):
        for nat, parts in zip(rows, results):
            for j in range(3):
                part_ref[cls, j, nat, :] = parts[j]

    def band_body(it, _):
        units, rows = [], []
        for g in range(group):
            qrows, krows, var = window(it * group + g)
            rows.append(qrows)
            units.append((qa_ref[qrows, :], ka_ref[krows, :], va_ref[krows, :],
                          tband_ref[0, var], tband_ref[1, var]))
        park(0, rows, _attend(units))
        return 0

    lax.fori_loop(0, S // (R * group), band_body, 0)

    def mid_body(blk, _):
        qrows, krows, var = window(blk)
        units, rows = [], []
        for r in range(n_mid_res):
            rows.append(pl.ds(blk * (R * n_mid_res) + r, R, stride=n_mid_res))
            units.append((qm_ref[r, qrows, :], km_ref[r, krows, :], vm_ref[r, krows, :],
                          tmid_ref[0, var], tmid_ref[1, var]))
        park(1, rows, _attend(units))
        return 0

    lax.fori_loop(0, n_mid // R, mid_body, 0)

    def far_body(it, _):
        units, rows = [], []
        for g in range(group):
            r = it * group + g
            rows.append(pl.ds(r, n_far, stride=n_far_res))
            units.append((qf_ref[r], kf_ref[r], vf_ref[r], tfar_ref[0], tfar_ref[1]))
        park(2, rows, _attend(units))
        return 0

    lax.fori_loop(0, n_far_res // group, far_body, 0)

    def merge_body(i, _):
        rows = pl.ds(pl.multiple_of(i * R, R), R)
        m = [part_ref[c, 0, rows, :] for c in range(3)]
        top = jnp.maximum(jnp.maximum(m[0], m[1]), m[2])
        w = [jnp.exp2(m[c] - top) for c in range(3)]
        total = sum(w[c] * part_ref[c, 1, rows, :] for c in range(3))
        acc = sum(w[c] * part_ref[c, 2, rows, :] for c in range(3))
        o_ref[rows, :] = (acc / total).astype(o_ref.dtype)
        return 0

    lax.fori_loop(0, S // R, merge_body, 0)


def _attention(qkv, tables, B, S):
    T, C = qkv.shape
    t_band, t_mid, t_far = tables
    d2, d3 = DILATED_GROUPS[1][1], DILATED_GROUPS[2][1]
    assert S % (ATT_BLOCK * 4) == 0 and S // d2 >= ATT_KEYS
    qkv_mid = qkv.reshape(B, S // d2, d2, C).transpose(0, 2, 1, 3)
    qkv_far = qkv.reshape(B, S // d3, d3, C).transpose(0, 2, 1, 3)
    nat = lambda c: pl.BlockSpec((S, LANES), lambda hp, b: (b, c * HEAD_PAIRS + hp))
    mid = lambda c: pl.BlockSpec((None, d2, S // d2, LANES), lambda hp, b: (b, 0, 0, c * HEAD_PAIRS + hp))
    far = lambda c: pl.BlockSpec((None, d3, S // d3, LANES), lambda hp, b: (b, 0, 0, c * HEAD_PAIRS + hp))
    return pl.pallas_call(
        _attn_kernel,
        grid=(HEAD_PAIRS, B),
        in_specs=[nat(0), nat(1), nat(2), mid(0), mid(1), mid(2), far(0), far(1), far(2),
                  pl.BlockSpec((2,) + t_band.shape[1:], lambda hp, b: (hp, 0, 0, 0)),
                  pl.BlockSpec((2,) + t_mid.shape[1:], lambda hp, b: (hp, 0, 0, 0)),
                  pl.BlockSpec((2,) + t_far.shape[1:], lambda hp, b: (hp, 0, 0))],
        out_specs=pl.BlockSpec((S, LANES), lambda hp, b: (b, hp)),
        out_shape=jax.ShapeDtypeStruct((T, ATT_WIDTH), BF16),
        scratch_shapes=[pltpu.VMEM((3, 3, S, LANES), F32)],
        compiler_params=_cparams(("arbitrary", "arbitrary")),
        name="dilated_attn",
    )(qkv, qkv, qkv, qkv_mid, qkv_mid, qkv_mid, qkv_far, qkv_far, qkv_far, t_band, t_mid, t_far)


def _split2(x):
    hi = x.astype(BF16)
    lo = (x - hi.astype(F32)).astype(BF16)
    return hi, lo


def _head_sum(x, m2):
    hi, lo = _split2(x)
    return jnp.dot(jnp.concatenate([hi, lo], axis=1), m2, preferred_element_type=F32)


def _rwkv_kernel(*refs, has_vres, n_chunks):
    if has_vres:
        (rwm_ref, rwl_ref, vf_ref, vec_ref, w2_ref, a2_ref, g2_ref, v1_ref, v2_ref,
         o_ref, state_ref, r_s, k_s, v_s, lw_s, kk_s, kb_s, y_s,
         ar_s, inv_s, bk_s, nrbk_s, akv_s, dec_s) = refs
    else:
        (rwm_ref, rwl_ref, vec_ref, w2_ref, a2_ref, g2_ref,
         o_ref, vf_out_ref, state_ref, r_s, k_s, v_s, lw_s, kk_s, kb_s, y_s,
         ar_s, inv_s, bk_s, nrbk_s, akv_s, dec_s) = refs
    C = CHUNK
    W = RWKV_WIDTH

    @pl.when(pl.program_id(1) == 0)
    def _():
        state_ref[...] = jnp.zeros_like(state_ref)

    ri = lax.broadcasted_iota(jnp.int32, (2 * LANES, LANES), 0)
    ci = lax.broadcasted_iota(jnp.int32, (2 * LANES, LANES), 1)
    m2 = ((ri % LANES) // HEAD_DIM == ci // HEAD_DIM).astype(BF16)

    vec = vec_ref[...]
    w0, a0, k_k, k_a, r_k, ln_w, ln_b, v0 = [vec[i:i + 1, :] for i in range(8)]

    r = rwm_ref[:, 0:W]
    k = rwm_ref[:, W:2 * W]
    v = rwm_ref[:, 2 * W:3 * W]
    lora = rwl_ref[:, 0:LANES]
    w_raw = w0 + _mm(jnp.tanh(lora), w2_ref[...])
    z = -w_raw
    w_log = -(jnp.maximum(z, 0.0) + jnp.log1p(jnp.exp(-jnp.abs(z)))) - 0.5
    lw_s[...] = -jnp.exp(w_log)
    a = _sigmoid(a0 + _mm(lora, a2_ref[...]))
    g = _mm(_sigmoid(rwl_ref[:, LANES:RWL_COLS]), g2_ref[...])
    if has_vres:
        mix = _sigmoid(v0 + _mm(_mm(v, v1_ref[...]), v2_ref[...]))
        v = v + (vf_ref[...] - v) * mix
    else:
        vf_out_ref[...] = v
    kk = k * k_k
    for p in range(HEAD_PAIRS):
        ls = slice(p * LANES, (p + 1) * LANES)
        kkp = kk[:, ls]
        nrm = jnp.sqrt(_head_sum(kkp * kkp, m2))
        kkp = kkp / jnp.maximum(nrm, 1e-12)
        kk_s[:, ls] = kkp
        kb_s[:, ls] = kkp * a[:, ls]
    kmod = k * (1.0 + (a - 1.0) * k_a)
    r_s[...] = r
    k_s[...] = kmod
    v_s[...] = v

    ri = lax.broadcasted_iota(jnp.int32, (LANES, LANES), 0)
    ci = lax.broadcasted_iota(jnp.int32, (LANES, LANES), 1)
    same_head = (ri // C) == (ci // C)
    strict = same_head & ((ri % C) > (ci % C))
    incl = same_head & ((ri % C) >= (ci % C))
    eye = (ri == ci).astype(F32)
    lane = lax.broadcasted_iota(jnp.int32, (C, LANES), 1)
    head0 = lane < HEAD_DIM
    tri_r = lax.broadcasted_iota(jnp.int32, (C, 3 * C), 0)
    tri_c = lax.broadcasted_iota(jnp.int32, (C, 3 * C), 1)
    tri3 = ((tri_c % C) <= tri_r).astype(BF16)

    lane_slices = [slice(p * LANES, (p + 1) * LANES) for p in range(HEAD_PAIRS)]

    def cumsum(x):
        hi = x.astype(BF16)
        rem = x - hi.astype(F32)
        mid = rem.astype(BF16)
        lo = (rem - mid.astype(F32)).astype(BF16)
        return jnp.dot(tri3, jnp.concatenate([hi, mid, lo], axis=0), preferred_element_type=F32)

    def per_head_rows(x, y):
        zero = jnp.zeros_like(x)
        return jnp.concatenate([jnp.where(head0, x, zero), jnp.where(head0, zero, x),
                                jnp.where(head0, y, zero), jnp.where(head0, zero, y)], axis=0)

    def intra_body(it, _):
        units = [(it * chunk_group + g, p) for g in range(chunk_group) for p in range(HEAD_PAIRS)]
        idx = range(len(units))
        rows = [pl.ds(pl.multiple_of(c * C, C), C) for c, _ in units]
        ls = [lane_slices[p] for _, p in units]
        rc = [r_s[rows[u], ls[u]] for u in idx]
        kc = [k_s[rows[u], ls[u]] for u in idx]
        vc = [v_s[rows[u], ls[u]] for u in idx]
        lw = [lw_s[rows[u], ls[u]] for u in idx]
        kkc = [kk_s[rows[u], ls[u]] for u in idx]
        kbc = [kb_s[rows[u], ls[u]] for u in idx]
        cum = [cumsum(lw[u]) for u in idx]
        total = [cum[u][C - 1:C, :] for u in idx]
        g_inv = [jnp.exp(-cum[u]) for u in idx]
        a_t = [-kkc[u] * jnp.exp(cum[u] - lw[u]) for u in idx]
        r_t = [rc[u] * jnp.exp(cum[u]) for u in idx]
        b_t = [kbc[u] * g_inv[u] for u in idx]
        k_t = [kc[u] * g_inv[u] for u in idx]
        to_end = [jnp.exp(total[u] - cum[u]) for u in idx]
        gram = [_mm_nt(per_head_rows(a_t[u], r_t[u]), per_head_rows(b_t[u], k_t[u])) for u in idx]
        n_ab = [jnp.where(strict, gram[u][:LANES, :LANES], 0.0) for u in idx]
        n_ak = [jnp.where(strict, gram[u][:LANES, LANES:], 0.0) for u in idx]
        n_rb = [jnp.where(incl, gram[u][LANES:, :LANES], 0.0) for u in idx]
        n_rk = [jnp.where(incl, gram[u][LANES:, LANES:], 0.0) for u in idx]
        inv = [eye + n_ab[u] for u in idx]
        pw = [_mm(n_ab[u], n_ab[u]) for u in idx]
        n_rounds = int(math.log2(C)) - 1
        for i in range(n_rounds):
            if i + 1 < n_rounds:
                both = [_mm(pw[u], jnp.concatenate([inv[u], pw[u]], axis=1)) for u in idx]
                inv = [inv[u] + both[u][:, :LANES] for u in idx]
                pw = [both[u][:, LANES:] for u in idx]
            else:
                inv = [inv[u] + _mm(pw[u], inv[u]) for u in idx]
        for u, (c, p) in enumerate(units):
            ar_s[c, p] = jnp.concatenate([a_t[u], r_t[u]], axis=0).astype(BF16)
            inv_s[c, p] = inv[u].astype(BF16)
            akv_s[c, p] = _mm(n_ak[u], jnp.concatenate([vc[u], vc[u]], axis=0))
            nrbk_s[c, p] = jnp.concatenate([n_rb[u], n_rk[u]], axis=1).astype(BF16)
            bk_s[c, p] = jnp.concatenate([kbc[u] * to_end[u], kc[u] * to_end[u]], axis=0).astype(BF16)
            dec_s[c, p] = jnp.broadcast_to(jnp.exp(total[u]), (8, LANES))
        return 0

    def state_body(c, _):
        rows = pl.ds(pl.multiple_of(c * C, C), C)
        pairs = range(HEAD_PAIRS)
        vc = [v_s[rows, lane_slices[p]] for p in pairs]
        st = [state_ref[p] for p in pairs]
        ah = [_mm_nt(ar_s[c, p], st[p]) for p in pairs]
        u_stack = [_mm(inv_s[c, p], jnp.concatenate([ah[p][:C], ah[p][:C]], axis=0) + akv_s[c, p])
                   for p in pairs]
        y_stack = [jnp.concatenate([ah[p][C:], ah[p][C:]], axis=0)
                   + _mm(nrbk_s[c, p], jnp.concatenate([u_stack[p], vc[p], vc[p]], axis=0)) for p in pairs]
        for p in pairs:
            y_s[rows, lane_slices[p]] = jnp.where(head0, y_stack[p][:C], y_stack[p][C:])
        uv_t = [jnp.concatenate([jnp.where(head0, u_stack[p][:C], u_stack[p][C:]), vc[p]], axis=0).T
                for p in pairs]
        st_new = [st[p] * dec_s[c, p][0:1, :] + _mm(uv_t[p], bk_s[c, p]) for p in pairs]
        for p in pairs:
            state_ref[p] = jnp.where(same_head, st_new[p], 0.0)
        return 0

    chunk_group = 2 if n_chunks % 2 == 0 else 1
    lax.fori_loop(0, n_chunks // chunk_group, intra_body, 0)
    lax.fori_loop(0, n_chunks, state_body, 0)

    for p in range(HEAD_PAIRS):
        ls = slice(p * LANES, (p + 1) * LANES)
        y = y_s[:, ls]
        mean = _head_sum(y, m2) * (1.0 / HEAD_DIM)
        d = y - mean
        var = _head_sum(d * d, m2) * (1.0 / HEAD_DIM)
        yn = d * lax.rsqrt(var + GN_EPS) * ln_w[:, ls] + ln_b[:, ls]
        bonus = _head_sum(r_s[:, ls] * k_s[:, ls] * r_k[:, ls], m2) * v_s[:, ls]
        o_ref[:, ls] = ((yn + bonus) * g[:, ls]).astype(o_ref.dtype)


def _rwkv(rwm, rwl, v_first, vec, w2p, a2p, g2p, v1p, v2p, B, S):
    T = rwm.shape[0]
    W = RWKV_WIDTH
    tb = _tile(S, 512)
    nS = S // tb
    n_chunks = tb // CHUNK
    has_vres = v_first is not None
    row = lambda b, i: (b * nS + i, 0)
    full = lambda b, i: (0, 0)
    in_specs = [pl.BlockSpec((tb, RWM_COLS), row), pl.BlockSpec((tb, RWL_COLS), row)]
    args = [rwm, rwl]
    if has_vres:
        in_specs.append(pl.BlockSpec((tb, W), row))
        args.append(v_first)
    in_specs += [pl.BlockSpec(vec.shape, full), pl.BlockSpec(w2p.shape, full),
                 pl.BlockSpec(a2p.shape, full), pl.BlockSpec(g2p.shape, full)]
    args += [vec, w2p, a2p, g2p]
    if has_vres:
        in_specs += [pl.BlockSpec(v1p.shape, full), pl.BlockSpec(v2p.shape, full)]
        args += [v1p, v2p]
    out_specs = [pl.BlockSpec((tb, W), row)]
    out_shape = [jax.ShapeDtypeStruct((T, W), BF16)]
    if not has_vres:
        out_specs.append(pl.BlockSpec((tb, W), row))
        out_shape.append(jax.ShapeDtypeStruct((T, W), F32))
    outs = pl.pallas_call(
        functools.partial(_rwkv_kernel, has_vres=has_vres, n_chunks=n_chunks),
        grid=(B, nS),
        in_specs=in_specs,
        out_specs=out_specs,
        out_shape=out_shape,
        scratch_shapes=([pltpu.VMEM((HEAD_PAIRS, LANES, LANES), F32)] + [pltpu.VMEM((tb, W), F32)] * 7
                        + [pltpu.VMEM((n_chunks, HEAD_PAIRS, 2 * CHUNK, LANES), BF16)] * 3
                        + [pltpu.VMEM((n_chunks, HEAD_PAIRS, 2 * CHUNK, 2 * LANES), BF16),
                           pltpu.VMEM((n_chunks, HEAD_PAIRS, 2 * CHUNK, LANES), F32),
                           pltpu.VMEM((n_chunks, HEAD_PAIRS, 8, LANES), F32)]),
        compiler_params=_cparams(("arbitrary", "arbitrary")),
        name="rwkv7",
    )(*args)
    return (outs[0], v_first) if has_vres else (outs[0], outs[1])


def _outproj_kernel(*refs, moe):
    if moe:
        (x_ref, att_ref, rw_ref, mod_ref, g_ref, w_ref, rw_w_ref, rw_b_ref,
         x1_ref, h_ref, info_ref) = refs
    else:
        x_ref, att_ref, rw_ref, mod_ref, g_ref, w_ref, x1_ref, h_ref = refs
    mix = (jnp.dot(att_ref[...], w_ref[0, :ATT_WIDTH, :], preferred_element_type=F32)
           + jnp.dot(rw_ref[...], w_ref[0, ATT_WIDTH:, :], preferred_element_type=F32))
    x1 = x_ref[...] + mod_ref[0, 2:3, :] * mix
    x1_ref[...] = x1
    h = _rms_mod(x1, g_ref[...], mod_ref[0, 4:5, :], mod_ref[0, 3:4, :])
    h_ref[...] = h.astype(h_ref.dtype)
    if moe:
        h_hi, h_lo = _split2(h)
        w_hi, w_lo = _split2(rw_w_ref[...])
        logits = (jnp.dot(h_hi, w_hi, preferred_element_type=F32)
                  + jnp.dot(h_hi, w_lo, preferred_element_type=F32)
                  + jnp.dot(h_lo, w_hi, preferred_element_type=F32)) + rw_b_ref[...]
        lane = lax.broadcasted_iota(jnp.int32, logits.shape, 1)
        logits = jnp.where(lane < N_EXPERTS, logits, -jnp.inf)
        m1 = jnp.max(logits, axis=-1, keepdims=True)
        i1 = jnp.min(jnp.where(logits == m1, lane, LANES), axis=-1, keepdims=True)
        rest = jnp.where(lane == i1, -jnp.inf, logits)
        m2 = jnp.max(rest, axis=-1, keepdims=True)
        i2 = jnp.min(jnp.where(rest == m2, lane, LANES), axis=-1, keepdims=True)
        e = jnp.exp(m2 - m1)
        g1 = 1.0 / (1.0 + e)
        g2 = e / (1.0 + e)
        info = jnp.where(lane == 0, i1.astype(F32),
                         jnp.where(lane == 1, i2.astype(F32),
                                   jnp.where(lane == 2, g1, jnp.where(lane == 3, g2, 0.0))))
        info_ref[...] = info


def _outproj(x, att, rw, mod, g, w_out_b, B, S, l, router=None):
    T, D = x.shape
    tm = _tile(S, 512)
    nS = S // tm
    moe = router is not None
    row = lambda b, i: (b * nS + i, 0)
    full = lambda b, i: (0, 0)
    in_specs = [pl.BlockSpec((tm, D), row), pl.BlockSpec((tm, ATT_WIDTH), row),
                pl.BlockSpec((tm, RWKV_WIDTH), row),
                pl.BlockSpec((1, 6, D), lambda b, i: (l * B + b, 0, 0)),
                pl.BlockSpec((1, D), full),
                pl.BlockSpec((1, D, D), lambda b, i: (l, 0, 0))]
    args = [x, att, rw, mod, g, w_out_b]
    out_specs = [pl.BlockSpec((tm, D), row), pl.BlockSpec((tm, D), row)]
    out_shape = [jax.ShapeDtypeStruct((T, D), F32), jax.ShapeDtypeStruct((T, D), F32 if moe else BF16)]
    if moe:
        in_specs += [pl.BlockSpec((D, LANES), full), pl.BlockSpec((1, LANES), full)]
        args += list(router)
        out_specs.append(pl.BlockSpec((tm, LANES), row))
        out_shape.append(jax.ShapeDtypeStruct((T, LANES), F32))
    return pl.pallas_call(
        functools.partial(_outproj_kernel, moe=moe),
        grid=(B, nS),
        in_specs=in_specs,
        out_specs=out_specs,
        out_shape=out_shape,
        compiler_params=_cparams(("arbitrary", "arbitrary")),
        name="outproj",
    )(*args)


def _ffn_kernel(x_ref, h_ref, mod_ref, wg_ref, wu_ref, wd_ref, o_ref, acc_ref):
    f = pl.program_id(1)

    @pl.when(f == 0)
    def _():
        acc_ref[...] = jnp.zeros_like(acc_ref)

    hb = h_ref[...]
    gate = jnp.dot(hb, wg_ref[0], preferred_element_type=F32)
    up = jnp.dot(hb, wu_ref[0], preferred_element_type=F32)
    mid = (_silu(gate) * up).astype(BF16)
    acc_ref[...] += jnp.dot(mid, wd_ref[0], preferred_element_type=F32)

    @pl.when(f == pl.num_programs(1) - 1)
    def _():
        o_ref[...] = x_ref[...] + mod_ref[0, 5:6, :] * acc_ref[...]


def _ffn_dense(x1, h, mod, wg, wu, wd, B, S, l, li):
    T, D = x1.shape
    F = wg.shape[-1]
    tm = _tile(S, 512)
    tf = _tile(F, 1408)
    per_seq = S // tm
    return pl.pallas_call(
        _ffn_kernel,
        grid=(T // tm, F // tf),
        in_specs=[pl.BlockSpec((tm, D), lambda i, f: (i, 0)),
                  pl.BlockSpec((tm, D), lambda i, f: (i, 0)),
                  pl.BlockSpec((1, 6, D), lambda i, f: (l * B + i // per_seq, 0, 0)),
                  pl.BlockSpec((1, D, tf), lambda i, f: (li, 0, f)),
                  pl.BlockSpec((1, D, tf), lambda i, f: (li, 0, f)),
                  pl.BlockSpec((1, tf, D), lambda i, f: (li, f, 0))],
        out_specs=pl.BlockSpec((tm, D), lambda i, f: (i, 0)),
        out_shape=jax.ShapeDtypeStruct((T, D), F32),
        scratch_shapes=[pltpu.VMEM((tm, D), F32)],
        compiler_params=_cparams(("arbitrary", "arbitrary")),
        name="ffn_dense",
    )(x1, h, mod, wg, wu, wd)


def _experts_kernel(te_ref, nu_ref, tok_ref, h_ref, wg_ref, wu_ref, wd_ref, y_ref,
                    land_ref, xb_ref, acc_ref, sem):
    i = pl.program_id(0)
    f = pl.program_id(1)
    tm = land_ref.shape[0]
    n_used = nu_ref[0]

    def start_gather(tile):
        base = tile * tm

        def body(j, _):
            pltpu.make_async_copy(h_ref.at[pl.ds(tok_ref[base + j], 1)],
                                  land_ref.at[pl.ds(j, 1)], sem).start()
            return 0

        lax.fori_loop(0, tm, body, 0, unroll=DMA_ISSUE_UNROLL)

    @pl.when(i < n_used)
    def _():
        @pl.when(f == 0)
        def _():
            @pl.when(i == 0)
            def _():
                start_gather(0)

            pltpu.make_async_copy(h_ref.at[pl.ds(0, tm)], land_ref, sem).wait()
            xb_ref[...] = land_ref[...].astype(BF16)
            acc_ref[...] = jnp.zeros_like(acc_ref)

            @pl.when(i + 1 < n_used)
            def _():
                start_gather(i + 1)

        xb = xb_ref[...]
        gate = jnp.dot(xb, wg_ref[0, 0], preferred_element_type=F32)
        up = jnp.dot(xb, wu_ref[0, 0], preferred_element_type=F32)
        mid = (_silu(gate) * up).astype(BF16)
        acc_ref[...] += jnp.dot(mid, wd_ref[0, 0], preferred_element_type=F32)

        @pl.when(f == pl.num_programs(1) - 1)
        def _():
            y_ref[...] = acc_ref[...]

    @pl.when((i >= n_used) & (f == 0))
    def _():
        y_ref[...] = jnp.zeros_like(y_ref)


def _experts(h, row_tok, tile_e, n_used, wg, wu, wd, li, tm):
    n_rows = row_tok.shape[0]
    D = h.shape[1]
    assert h.shape[0] >= tm
    F = wg.shape[-1]
    tf = _tile(F, 1792)
    nf = F // tf

    def rows(i, f, te, nu, tok):
        return (i, 0)

    def wcol(i, f, te, nu, tok):
        return (li, te[jnp.minimum(i, nu[0] - 1)], 0, jnp.where(i < nu[0], f, nf - 1))

    def wrow(i, f, te, nu, tok):
        return (li, te[jnp.minimum(i, nu[0] - 1)], jnp.where(i < nu[0], f, nf - 1), 0)

    return pl.pallas_call(
        _experts_kernel,
        grid_spec=pltpu.PrefetchScalarGridSpec(
            num_scalar_prefetch=3,
            grid=(n_rows // tm, nf),
            in_specs=[pl.BlockSpec(memory_space=pl.ANY),
                      pl.BlockSpec((1, 1, D, tf), wcol),
                      pl.BlockSpec((1, 1, D, tf), wcol),
                      pl.BlockSpec((1, 1, tf, D), wrow)],
            out_specs=pl.BlockSpec((tm, D), rows),
            scratch_shapes=[pltpu.VMEM((tm, D), F32), pltpu.VMEM((tm, D), BF16),
                            pltpu.VMEM((tm, D), F32), pltpu.SemaphoreType.DMA(())]),
        out_shape=jax.ShapeDtypeStruct((n_rows, D), F32),
        compiler_params=_cparams(("arbitrary", "arbitrary")),
        name="moe_experts",
    )(tile_e, n_used, row_tok, h, wg, wu, wd)


def _combine_kernel(pos_ref, y_ref, x_ref, info_ref, mod_ref, fg_ref, o_ref, buf, sem, *, tc, final):
    i = pl.program_id(0)
    slot = i % 2

    def start_gather(tile, dst_slot):
        base = tile * (2 * tc)

        def body(j, _):
            pltpu.make_async_copy(y_ref.at[pl.ds(pos_ref[base + j], 1)],
                                  buf.at[dst_slot, pl.ds(j, 1)], sem.at[dst_slot]).start()
            return 0

        lax.fori_loop(0, 2 * tc, body, 0, unroll=DMA_ISSUE_UNROLL)

    @pl.when(i == 0)
    def _():
        start_gather(0, 0)

    @pl.when(i + 1 < pl.num_programs(0))
    def _():
        start_gather(i + 1, 1 - slot)

    pltpu.make_async_copy(y_ref.at[pl.ds(0, 2 * tc)], buf.at[slot], sem.at[slot]).wait()
    info = info_ref[...]
    ff = info[:, 2:3] * buf[slot, 0:tc, :] + info[:, 3:4] * buf[slot, tc:2 * tc, :]
    out = x_ref[...] + mod_ref[0, 5:6, :] * ff
    if final:
        ms = jnp.mean(out * out, axis=-1, keepdims=True)
        out = out * lax.rsqrt(ms + RMS_EPS) * fg_ref[...]
    o_ref[...] = out


def _combine(pos, y_rows, x1, info, mod, final_g, B, S, l, final):
    T, D = x1.shape
    tc = _tile(S, 256)
    per_seq = S // tc
    return pl.pallas_call(
        functools.partial(_combine_kernel, tc=tc, final=final),
        grid_spec=pltpu.PrefetchScalarGridSpec(
            num_scalar_prefetch=1,
            grid=(T // tc,),
            in_specs=[pl.BlockSpec(memory_space=pl.ANY),
                      pl.BlockSpec((tc, D), lambda i, pos: (i, 0)),
                      pl.BlockSpec((tc, LANES), lambda i, pos: (i, 0)),
                      pl.BlockSpec((1, 6, D), lambda i, pos: (l * B + i // per_seq, 0, 0)),
                      pl.BlockSpec((1, D), lambda i, pos: (0, 0))],
            out_specs=pl.BlockSpec((tc, D), lambda i, pos: (i, 0)),
            scratch_shapes=[pltpu.VMEM((2, 2 * tc, D), F32), pltpu.SemaphoreType.DMA((2,))]),
        out_shape=jax.ShapeDtypeStruct((T, D), F32),
        compiler_params=_cparams(("arbitrary",)),
        name="moe_combine",
    )(pos, y_rows, x1, info, mod, final_g)


def _moe(x1, h, info, mod, wg, wu, wd, final_g, B, S, l, li, final):
    T, D = x1.shape
    TK = T * TOP_K
    tm = _tile(TK // N_EXPERTS, 512)
    n_rows = TK + (N_EXPERTS - 1) * tm
    flat_e = info[:, :TOP_K].astype(jnp.int32).reshape(TK)
    onehot = (flat_e[:, None] == jnp.arange(N_EXPERTS, dtype=jnp.int32)[None, :]).astype(jnp.int32)
    csum = jnp.cumsum(onehot, axis=0)
    counts = csum[-1]
    rank = jnp.sum(onehot * csum, axis=1) - 1
    padded = (counts + tm - 1) // tm * tm
    pad_ends = jnp.cumsum(padded)
    pad_starts = pad_ends - padded
    dest = pad_starts[flat_e] + rank
    row_tok = jnp.zeros((n_rows,), jnp.int32).at[dest].set(jnp.arange(TK, dtype=jnp.int32) // TOP_K)
    tile_start = jnp.arange(n_rows // tm, dtype=jnp.int32) * tm
    tile_e = jnp.minimum(jnp.searchsorted(pad_ends, tile_start, side='right'), N_EXPERTS - 1).astype(jnp.int32)
    n_used_rows = pad_ends[-1:].astype(jnp.int32)
    n_used_tiles = n_used_rows // tm
    tc = _tile(S, 256)
    pos = dest.reshape(T // tc, tc, TOP_K).transpose(0, 2, 1).reshape(TK).astype(jnp.int32)

    y_rows = _experts(h, row_tok, tile_e, n_used_tiles, wg, wu, wd, li, tm)
    return _combine(pos, y_rows, x1, info, mod, final_g, B, S, l, final)


def kernel(x, c, w_ada, b_ada, norm1_g, norm2_g, final_g, w_in, w_out, rel_bias, rwkv_mu, rwkv_w0, rwkv_w2, rwkv_a0, rwkv_a2, rwkv_g2, rwkv_k_k, rwkv_k_a, rwkv_r_k, rwkv_ln_w, rwkv_ln_b, rwkv_v0, rwkv_v1, rwkv_v2, ffn_w_gate, ffn_w_up, ffn_w_down, moe_router_w, moe_router_b, moe_w_gate, moe_w_up, moe_w_down):
    B, S, D = x.shape
    L = w_in.shape[0]
    T = B * S
    W = RWKV_WIDTH

    mod = _ada_mod(c, w_ada, b_ada).reshape(L * B, 6, D)
    w_in_b = jnp.pad(w_in, ((0, 0), (0, 0), (0, N_IN_PAD - N_IN))).astype(BF16)
    w_out_b = w_out.astype(BF16)
    ffn_w = [w.astype(BF16) for w in (ffn_w_gate, ffn_w_up, ffn_w_down)]
    moe_w = [w.astype(BF16) for w in (moe_w_gate, moe_w_up, moe_w_down)]
    mu_p = jnp.pad(rwkv_mu, ((0, 0), (0, RWM_COLS + RWL_COLS - rwkv_mu.shape[1])))
    att_tables = _att_tables(rel_bias, S)
    zeros_w = jnp.zeros((1, W), F32)
    final_g2 = final_g.reshape(1, D)

    xf = x.reshape(T, D)
    v_first = None
    for l in range(L):
        qkv, rwm, rwl = _inproj(xf, mod, norm1_g[l].reshape(1, D), w_in_b, mu_p[l].reshape(1, -1), B, S, l)
        att = _attention(qkv, att_tables, B, S)
        v0 = rwkv_v0[l - 1].reshape(1, W) if l > 0 else zeros_w
        vec = jnp.concatenate([rwkv_w0[l].reshape(1, W), rwkv_a0[l].reshape(1, W), rwkv_k_k[l].reshape(1, W),
                               rwkv_k_a[l].reshape(1, W), rwkv_r_k[l].reshape(1, W), rwkv_ln_w[l].reshape(1, W),
                               rwkv_ln_b[l].reshape(1, W), v0], axis=0)
        w2p = jnp.pad(rwkv_w2[l], ((0, LANES - D_DECAY_LORA), (0, 0)))
        a2p = jnp.pad(rwkv_a2[l], ((D_DECAY_LORA, 0), (0, 0)))
        g2p = jnp.pad(rwkv_g2[l], ((0, RWL_COLS - LANES - D_GATE_LORA), (0, 0)))
        if l > 0:
            v1p = jnp.pad(rwkv_v1[l - 1], ((0, 0), (0, LANES - D_MV_LORA)))
            v2p = jnp.pad(rwkv_v2[l - 1], ((0, LANES - D_MV_LORA), (0, 0)))
        else:
            v1p = v2p = None
        rw, v_first = _rwkv(rwm, rwl, v_first, vec, w2p, a2p, g2p, v1p, v2p, B, S)
        li = l // 2
        if l % 2 == 0:
            x1, h = _outproj(xf, att, rw, mod, norm2_g[l].reshape(1, D), w_out_b, B, S, l)
            xf = _ffn_dense(x1, h, mod, *ffn_w, B, S, l, li)
        else:
            router = (jnp.pad(moe_router_w[li], ((0, 0), (0, LANES - N_EXPERTS))),
                      jnp.pad(moe_router_b[li], (0, LANES - N_EXPERTS)).reshape(1, LANES))
            x1, h, info = _outproj(xf, att, rw, mod, norm2_g[l].reshape(1, D), w_out_b, B, S, l, router)
            xf = _moe(x1, h, info, mod, *moe_w, final_g2, B, S, l, li,
                      final=(l == L - 1))
    if L % 2 == 1:
        raise NotImplementedError("final norm is fused into the last (expert) layer")
    return xf.reshape(B, S, D)
```

```python
import functools
import math

import numpy as np
import jax
import jax.numpy as jnp
from jax import lax
from jax.experimental import pallas as pl
from jax.experimental.pallas import tpu as pltpu

F32 = jnp.float32
BF16 = jnp.bfloat16

D_MODEL = 1024
HEAD_DIM = 64
ATT_WIDTH = 512
RWKV_WIDTH = 512
N_HEADS = 8
HEAD_PAIRS = 4
LANES = 128
DILATED_GROUPS = ((128, 1), (512, 4), (2048, 16))
N_BUCKETS = 32
MAX_DISTANCE = 2048
NEG_INF = -1e30
D_DECAY_LORA = 64
D_AAA_LORA = 64
D_MV_LORA = 32
D_GATE_LORA = 160
N_IN = 3 * ATT_WIDTH + 3 * RWKV_WIDTH + D_DECAY_LORA + D_AAA_LORA + D_GATE_LORA
QKV_COLS = 3 * ATT_WIDTH
RWM_COLS = 3 * RWKV_WIDTH
RWL_COLS = 384
N_IN_PAD = QKV_COLS + RWM_COLS + RWL_COLS
GN_EPS = HEAD_DIM * 1e-5
RMS_EPS = 1e-6
N_EXPERTS = 8
TOP_K = 2
CHUNK = 64
VMEM_LIMIT = 56 * 1024 * 1024
LOG2_E = math.log2(math.e)
Q_SCALE = LOG2_E / math.sqrt(HEAD_DIM)
DMA_ISSUE_UNROLL = 8


def _cparams(sem):
    return pltpu.CompilerParams(dimension_semantics=sem, vmem_limit_bytes=VMEM_LIMIT)


def _tile(n, pref):
    t = min(n, pref)
    while n % t:
        t //= 2
    return t


def _mm(a, b):
    return jnp.dot(a.astype(BF16), b.astype(BF16), preferred_element_type=F32)


def _mm_nt(a, b):
    return lax.dot_general(a.astype(BF16), b.astype(BF16), (((1,), (1,)), ((), ())),
                           preferred_element_type=F32)


def _sigmoid(x):
    return 1.0 / (1.0 + jnp.exp(-x))


def _silu(x):
    return x * _sigmoid(x)


def _ada_kernel(c_ref, w_ref, b_ref, o_ref):
    ca = _silu(c_ref[...])
    o_ref[0] = _mm(ca, w_ref[0]) + b_ref[0]


def _ada_mod(c, w_ada, b_ada):
    L, D, N = w_ada.shape
    B = c.shape[0]
    tn = _tile(N, 1536)
    return pl.pallas_call(
        _ada_kernel,
        grid=(L, N // tn),
        in_specs=[pl.BlockSpec((B, D), lambda l, j: (0, 0)),
                  pl.BlockSpec((1, D, tn), lambda l, j: (l, 0, j)),
                  pl.BlockSpec((1, 1, tn), lambda l, j: (l, 0, j))],
        out_specs=pl.BlockSpec((1, B, tn), lambda l, j: (l, 0, j)),
        out_shape=jax.ShapeDtypeStruct((L, B, N), F32),
        compiler_params=_cparams(("arbitrary", "arbitrary")),
        name="ada_mod",
    )(c, w_ada, b_ada.reshape(L, 1, N))


def _rms_mod(xf, g, scale, shift):
    ms = jnp.mean(xf * xf, axis=-1, keepdims=True)
    return (xf * lax.rsqrt(ms + RMS_EPS) * g) * (1.0 + scale) + shift


def _inproj_kernel(x_ref, mod_ref, g_ref, w_ref, mu_ref, qkv_ref, qkv_mid_ref, qkv_far_ref, rwm_ref, rwl_ref,
                   carry_ref, qkv_scr):
    i = pl.program_id(1)
    tm = x_ref.shape[0]
    h = _rms_mod(x_ref[...], g_ref[...], mod_ref[0, 1:2, :], mod_ref[0, 0:1, :])
    acc = jnp.dot(h.astype(BF16), w_ref[0], preferred_element_type=F32)
    qkv = jnp.concatenate([acc[:, :ATT_WIDTH] * Q_SCALE, acc[:, ATT_WIDTH:QKV_COLS]], axis=1)
    qkv_ref[...] = qkv.astype(BF16)
    for c in range(QKV_COLS // LANES):
        cols = slice(c * LANES, (c + 1) * LANES)
        qkv_scr[c] = qkv[:, cols]
        for out_ref in (qkv_mid_ref, qkv_far_ref):
            d, rows = out_ref.shape[0], out_ref.shape[1]
            for r in range(d):
                out_ref[r, :, cols] = qkv_scr[c, pl.ds(r, rows, stride=d), :].astype(BF16)
    p = acc[:, QKV_COLS:]
    first = jnp.where(i == 0, 0.0, carry_ref[...])
    row = lax.broadcasted_iota(jnp.int32, p.shape, 0)
    prev = jnp.where(row == 0, first, pltpu.roll(p, 1, 0))
    carry_ref[...] = p[tm - 1:tm, :]
    pm = p + mu_ref[...] * (prev - p)
    rwm_ref[...] = pm[:, :RWM_COLS]
    rwl_ref[...] = pm[:, RWM_COLS:]


def _inproj(x, mod, g, w_in_l, mu_l, B, S, l):
    T, D = x.shape
    tm = _tile(S, 512)
    nS = S // tm
    d2, d3 = DILATED_GROUPS[1][1], DILATED_GROUPS[2][1]
    return pl.pallas_call(
        _inproj_kernel,
        grid=(B, nS),
        in_specs=[pl.BlockSpec((tm, D), lambda b, i: (b * nS + i, 0)),
                  pl.BlockSpec((1, 6, D), lambda b, i: (l * B + b, 0, 0)),
                  pl.BlockSpec((1, D), lambda b, i: (0, 0)),
                  pl.BlockSpec((1, D, N_IN_PAD), lambda b, i: (l, 0, 0)),
                  pl.BlockSpec((1, RWM_COLS + RWL_COLS), lambda b, i: (0, 0))],
        out_specs=[pl.BlockSpec((tm, QKV_COLS), lambda b, i: (b * nS + i, 0)),
                   pl.BlockSpec((None, d2, tm // d2, QKV_COLS), lambda b, i: (b, 0, i, 0)),
                   pl.BlockSpec((None, d3, tm // d3, QKV_COLS), lambda b, i: (b, 0, i, 0)),
                   pl.BlockSpec((tm, RWM_COLS), lambda b, i: (b * nS + i, 0)),
                   pl.BlockSpec((tm, RWL_COLS), lambda b, i: (b * nS + i, 0))],
        out_shape=[jax.ShapeDtypeStruct((T, QKV_COLS), BF16),
                   jax.ShapeDtypeStruct((B, d2, S // d2, QKV_COLS), BF16),
                   jax.ShapeDtypeStruct((B, d3, S // d3, QKV_COLS), BF16),
                   jax.ShapeDtypeStruct((T, RWM_COLS), F32),
                   jax.ShapeDtypeStruct((T, RWL_COLS), F32)],
        scratch_shapes=[pltpu.VMEM((1, RWM_COLS + RWL_COLS), F32), pltpu.VMEM((QKV_COLS // LANES, tm, LANES), F32)],
        compiler_params=_cparams(("arbitrary", "arbitrary")),
        name="inproj",
    )(x, mod, g, w_in_l, mu_l)


def _t5_bucket(n):
    max_exact = N_BUCKETS // 2
    large = max_exact + (np.log(np.maximum(n, 1) / max_exact) / np.log(MAX_DISTANCE / max_exact)
                         * (N_BUCKETS - max_exact)).astype(np.int32)
    large = np.minimum(large, N_BUCKETS - 1)
    return np.where(n < max_exact, n, large).astype(np.int32)


ATT_BLOCK = 128
ATT_KEYS = 256


def _toeplitz(vec, c0, rows, cols):
    H, n = vec.shape
    period = rows + cols - 1
    w = jnp.concatenate([jnp.full((H, cols - 1), NEG_INF, F32), vec, jnp.full((H, rows), NEG_INF, F32)], axis=1)
    rw = w[:, ::-1]
    a = w.shape[1] - 1 - (c0 + cols - 1)
    z = jnp.concatenate([rw[:, a:a + cols], rw[:, a - (rows - 1):a]], axis=1)
    flat = jnp.tile(z, (1, rows))[:, :rows * (period - 1)]
    return flat.reshape(H, rows, period - 1)[..., :cols]


def _att_tables(rel_bias, S):
    (w1, d1), (w2, d2), (w3, d3) = DILATED_GROUPS
    assert d1 == 1 and w1 == ATT_KEYS - ATT_BLOCK and w2 // d2 == w1 and S % (d3 * 8) == 0
    off = np.arange(S)
    mult = np.zeros(S, np.int64)
    for w, d in DILATED_GROUPS:
        mult += ((off % d == 0) & (off <= w)).astype(np.int64)
    logm = np.where(mult > 0, np.log(np.maximum(mult, 1)), NEG_INF).astype(np.float32)
    per_off = (rel_bias[_t5_bucket(off)].T.astype(F32) + jnp.asarray(logm)[None, :]) * LOG2_E
    u2 = np.arange(S // d2)
    u3 = np.arange(S // d3)
    band = jnp.where(jnp.asarray(off <= w1)[None], per_off, NEG_INF)
    mid = jnp.where(jnp.asarray((u2 > w1 // d2) & (u2 <= w2 // d2))[None], per_off[:, ::d2], NEG_INF)
    far = jnp.where(jnp.asarray((u3 > w2 // d3) & (u3 <= w3 // d3))[None], per_off[:, ::d3], NEG_INF)
    shift = ATT_KEYS - ATT_BLOCK
    t_band = jnp.stack([_toeplitz(band, 0, ATT_BLOCK, ATT_KEYS), _toeplitz(band, shift, ATT_BLOCK, ATT_KEYS)], axis=1)
    t_mid = jnp.stack([_toeplitz(mid, 0, ATT_BLOCK, ATT_KEYS), _toeplitz(mid, shift, ATT_BLOCK, ATT_KEYS)], axis=1)
    t_far = _toeplitz(far, 0, S // d3, S // d3)
    return t_band, t_mid, t_far


def _attend(units):
    idx = range(len(units))
    heads = range(2)
    head0 = [lax.broadcasted_iota(jnp.int32, u[0].shape, 1) < HEAD_DIM for u in units]
    qh = [[jnp.where(head0[i], units[i][0], jnp.zeros_like(units[i][0])),
           jnp.where(head0[i], jnp.zeros_like(units[i][0]), units[i][0])] for i in idx]
    s = [[lax.dot_general(qh[i][h], units[i][1], (((1,), (1,)), ((), ())), preferred_element_type=F32)
          + units[i][3 + h] for h in heads] for i in idx]
    mx = [[jnp.max(s[i][h], axis=-1, keepdims=True) for h in heads] for i in idx]
    p = [[jnp.exp2(s[i][h] - mx[i][h]) for h in heads] for i in idx]
    ps = [[jnp.sum(p[i][h], axis=-1, keepdims=True) for h in heads] for i in idx]
    pv = [[jnp.dot(p[i][h].astype(BF16), units[i][2], preferred_element_type=F32) for h in heads] for i in idx]
    out = []
    for i in idx:
        shape = units[i][0].shape
        out.append((jnp.where(head0[i], jnp.broadcast_to(mx[i][0], shape), jnp.broadcast_to(mx[i][1], shape)),
                    jnp.where(head0[i], jnp.broadcast_to(ps[i][0], shape), jnp.broadcast_to(ps[i][1], shape)),
                    jnp.where(head0[i], pv[i][0], pv[i][1])))
    return out


def _attn_kernel(qa_ref, ka_ref, va_ref, qm_ref, km_ref, vm_ref, qf_ref, kf_ref, vf_ref,
                 tband_ref, tmid_ref, tfar_ref, o_ref, part_ref):
    R, K = ATT_BLOCK, ATT_KEYS
    S = qa_ref.shape[0]
    n_mid_res, n_mid = qm_ref.shape[0], qm_ref.shape[1]
    n_far_res, n_far = qf_ref.shape[0], qf_ref.shape[1]
    group = 4

    def window(blk):
        start = pl.multiple_of(jnp.maximum(blk * R - (K - R), 0), R)
        return pl.ds(pl.multiple_of(blk * R, R), R), pl.ds(start, K), jnp.minimum(blk, 1)

    def park(cls, rows, results):
        for nat, parts in zip(rows, results):
            for j in range(3):
                part_ref[cls, j, nat, :] = parts[j]

    def band_body(it, _):
        units, rows = [], []
        for g in range(group):
            qrows, krows, var = window(it * group + g)
            rows.append(qrows)
            units.append((qa_ref[qrows, :], ka_ref[krows, :], va_ref[krows, :],
                          tband_ref[0, var], tband_ref[1, var]))
        park(0, rows, _attend(units))
        return 0

    lax.fori_loop(0, S // (R * group), band_body, 0)

    def mid_body(blk, _):
        qrows, krows, var = window(blk)
        units, rows = [], []
        for r in range(n_mid_res):
            rows.append(pl.ds(blk * (R * n_mid_res) + r, R, stride=n_mid_res))
            units.append((qm_ref[r, qrows, :], km_ref[r, krows, :], vm_ref[r, krows, :],
                          tmid_ref[0, var], tmid_ref[1, var]))
        park(1, rows, _attend(units))
        return 0

    lax.fori_loop(0, n_mid // R, mid_body, 0)

    def far_body(it, _):
        units, rows = [], []
        for g in range(group):
            r = it * group + g
            rows.append(pl.ds(r, n_far, stride=n_far_res))
            units.append((qf_ref[r], kf_ref[r], vf_ref[r], tfar_ref[0], tfar_ref[1]))
        park(2, rows, _attend(units))
        return 0

    lax.fori_loop(0, n_far_res // group, far_body, 0)

    def merge_body(i, _):
        rows = pl.ds(pl.multiple_of(i * R, R), R)
        m = [part_ref[c, 0, rows, :] for c in range(3)]
        top = jnp.maximum(jnp.maximum(m[0], m[1]), m[2])
        w = [jnp.exp2(m[c] - top) for c in range(3)]
        total = sum(w[c] * part_ref[c, 1, rows, :] for c in range(3))
        acc = sum(w[c] * part_ref[c, 2, rows, :] for c in range(3))
        o_ref[rows, :] = (acc / total).astype(o_ref.dtype)
        return 0

    lax.fori_loop(0, S // R, merge_body, 0)


def _attention(qkv, qkv_mid, qkv_far, tables, B, S):
    T = qkv.shape[0]
    t_band, t_mid, t_far = tables
    d2, d3 = DILATED_GROUPS[1][1], DILATED_GROUPS[2][1]
    assert S % (ATT_BLOCK * 4) == 0 and S // d2 >= ATT_KEYS
    nat = lambda c: pl.BlockSpec((S, LANES), lambda hp, b: (b, c * HEAD_PAIRS + hp))
    mid = lambda c: pl.BlockSpec((None, d2, S // d2, LANES), lambda hp, b: (b, 0, 0, c * HEAD_PAIRS + hp))
    far = lambda c: pl.BlockSpec((None, d3, S // d3, LANES), lambda hp, b: (b, 0, 0, c * HEAD_PAIRS + hp))
    return pl.pallas_call(
        _attn_kernel,
        grid=(HEAD_PAIRS, B),
        in_specs=[nat(0), nat(1), nat(2), mid(0), mid(1), mid(2), far(0), far(1), far(2),
                  pl.BlockSpec((2,) + t_band.shape[1:], lambda hp, b: (hp, 0, 0, 0)),
                  pl.BlockSpec((2,) + t_mid.shape[1:], lambda hp, b: (hp, 0, 0, 0)),
                  pl.BlockSpec((2,) + t_far.shape[1:], lambda hp, b: (hp, 0, 0))],
        out_specs=pl.BlockSpec((S, LANES), lambda hp, b: (b, hp)),
        out_shape=jax.ShapeDtypeStruct((T, ATT_WIDTH), BF16),
        scratch_shapes=[pltpu.VMEM((3, 3, S, LANES), F32)],
        compiler_params=_cparams(("arbitrary", "arbitrary")),
        name="dilated_attn",
    )(qkv, qkv, qkv, qkv_mid, qkv_mid, qkv_mid, qkv_far, qkv_far, qkv_far, t_band, t_mid, t_far)


def _split2(x):
    hi = x.astype(BF16)
    lo = (x - hi.astype(F32)).astype(BF16)
    return hi, lo


def _head_sum(x, m2):
    hi, lo = _split2(x)
    return jnp.dot(jnp.concatenate([hi, lo], axis=1), m2, preferred_element_type=F32)


def _rwkv_kernel(*refs, has_vres, n_chunks):
    if has_vres:
        (rwm_ref, rwl_ref, vf_ref, vec_ref, w2_ref, a2_ref, g2_ref, v1_ref, v2_ref,
         o_ref, state_ref, r_s, k_s, v_s, lw_s, kk_s, kb_s, y_s,
         ar_s, inv_s, bk_s, nrbk_s, akv_s, dec_s) = refs
    else:
        (rwm_ref, rwl_ref, vec_ref, w2_ref, a2_ref, g2_ref,
         o_ref, vf_out_ref, state_ref, r_s, k_s, v_s, lw_s, kk_s, kb_s, y_s,
         ar_s, inv_s, bk_s, nrbk_s, akv_s, dec_s) = refs
    C = CHUNK
    W = RWKV_WIDTH

    @pl.when(pl.program_id(1) == 0)
    def _():
        state_ref[...] = jnp.zeros_like(state_ref)

    ri = lax.broadcasted_iota(jnp.int32, (2 * LANES, LANES), 0)
    ci = lax.broadcasted_iota(jnp.int32, (2 * LANES, LANES), 1)
    m2 = ((ri % LANES) // HEAD_DIM == ci // HEAD_DIM).astype(BF16)

    vec = vec_ref[...]
    w0, a0, k_k, k_a, r_k, ln_w, ln_b, v0 = [vec[i:i + 1, :] for i in range(8)]

    r = rwm_ref[:, 0:W]
    k = rwm_ref[:, W:2 * W]
    v = rwm_ref[:, 2 * W:3 * W]
    lora = rwl_ref[:, 0:LANES]
    w_raw = w0 + _mm(jnp.tanh(lora), w2_ref[...])
    z = -w_raw
    w_log = -(jnp.maximum(z, 0.0) + jnp.log1p(jnp.exp(-jnp.abs(z)))) - 0.5
    lw_s[...] = -jnp.exp(w_log)
    a = _sigmoid(a0 + _mm(lora, a2_ref[...]))
    g = _mm(_sigmoid(rwl_ref[:, LANES:RWL_COLS]), g2_ref[...])
    if has_vres:
        mix = _sigmoid(v0 + _mm(_mm(v, v1_ref[...]), v2_ref[...]))
        v = v + (vf_ref[...] - v) * mix
    else:
        vf_out_ref[...] = v
    kk = k * k_k
    for p in range(HEAD_PAIRS):
        ls = slice(p * LANES, (p + 1) * LANES)
        kkp = kk[:, ls]
        nrm = jnp.sqrt(_head_sum(kkp * kkp, m2))
        kkp = kkp / jnp.maximum(nrm, 1e-12)
        kk_s[:, ls] = kkp
        kb_s[:, ls] = kkp * a[:, ls]
    kmod = k * (1.0 + (a - 1.0) * k_a)
    r_s[...] = r
    k_s[...] = kmod
    v_s[...] = v

    ri = lax.broadcasted_iota(jnp.int32, (LANES, LANES), 0)
    ci = lax.broadcasted_iota(jnp.int32, (LANES, LANES), 1)
    same_head = (ri // C) == (ci // C)
    strict = same_head & ((ri % C) > (ci % C))
    incl = same_head & ((ri % C) >= (ci % C))
    eye = (ri == ci).astype(F32)
    lane = lax.broadcasted_iota(jnp.int32, (C, LANES), 1)
    head0 = lane < HEAD_DIM
    tri_r = lax.broadcasted_iota(jnp.int32, (C, 3 * C), 0)
    tri_c = lax.broadcasted_iota(jnp.int32, (C, 3 * C), 1)
    tri3 = ((tri_c % C) <= tri_r).astype(BF16)

    lane_slices = [slice(p * LANES, (p + 1) * LANES) for p in range(HEAD_PAIRS)]

    def cumsum(x):
        hi = x.astype(BF16)
        rem = x - hi.astype(F32)
        mid = rem.astype(BF16)
        lo = (rem - mid.astype(F32)).astype(BF16)
        return jnp.dot(tri3, jnp.concatenate([hi, mid, lo], axis=0), preferred_element_type=F32)

    def per_head_rows(x, y):
        zero = jnp.zeros_like(x)
        return jnp.concatenate([jnp.where(head0, x, zero), jnp.where(head0, zero, x),
                                jnp.where(head0, y, zero), jnp.where(head0, zero, y)], axis=0)

    def intra_body(it, _):
        units = [(it * chunk_group + g, p) for g in range(chunk_group) for p in range(HEAD_PAIRS)]
        idx = range(len(units))
        rows = [pl.ds(pl.multiple_of(c * C, C), C) for c, _ in units]
        ls = [lane_slices[p] for _, p in units]
        rc = [r_s[rows[u], ls[u]] for u in idx]
        kc = [k_s[rows[u], ls[u]] for u in idx]
        vc = [v_s[rows[u], ls[u]] for u in idx]
        lw = [lw_s[rows[u], ls[u]] for u in idx]
        kkc = [kk_s[rows[u], ls[u]] for u in idx]
        kbc = [kb_s[rows[u], ls[u]] for u in idx]
        cum = [cumsum(lw[u]) for u in idx]
        total = [cum[u][C - 1:C, :] for u in idx]
        g_inv = [jnp.exp(-cum[u]) for u in idx]
        a_t = [-kkc[u] * jnp.exp(cum[u] - lw[u]) for u in idx]
        r_t = [rc[u] * jnp.exp(cum[u]) for u in idx]
        b_t = [kbc[u] * g_inv[u] for u in idx]
        k_t = [kc[u] * g_inv[u] for u in idx]
        to_end = [jnp.exp(total[u] - cum[u]) for u in idx]
        gram = [_mm_nt(per_head_rows(a_t[u], r_t[u]), per_head_rows(b_t[u], k_t[u])) for u in idx]
        n_ab = [jnp.where(strict, gram[u][:LANES, :LANES], 0.0) for u in idx]
        n_ak = [jnp.where(strict, gram[u][:LANES, LANES:], 0.0) for u in idx]
        n_rb = [jnp.where(incl, gram[u][LANES:, :LANES], 0.0) for u in idx]
        n_rk = [jnp.where(incl, gram[u][LANES:, LANES:], 0.0) for u in idx]
        inv = [eye + n_ab[u] for u in idx]
        pw = [_mm(n_ab[u], n_ab[u]) for u in idx]
        n_rounds = int(math.log2(C)) - 1
        for i in range(n_rounds):
            if i + 1 < n_rounds:
                both = [_mm(pw[u], jnp.concatenate([inv[u], pw[u]], axis=1)) for u in idx]
                inv = [inv[u] + both[u][:, :LANES] for u in idx]
                pw = [both[u][:, LANES:] for u in idx]
            else:
                inv = [inv[u] + _mm(pw[u], inv[u]) for u in idx]
        for u, (c, p) in enumerate(units):
            ar_s[c, p] = jnp.concatenate([a_t[u], r_t[u]], axis=0).astype(BF16)
            inv_s[c, p] = inv[u].astype(BF16)
            akv_s[c, p] = _mm(n_ak[u], jnp.concatenate([vc[u], vc[u]], axis=0))
            nrbk_s[c, p] = jnp.concatenate([n_rb[u], n_rk[u]], axis=1).astype(BF16)
            bk_s[c, p] = jnp.concatenate([kbc[u] * to_end[u], kc[u] * to_end[u]], axis=0).astype(BF16)
            dec_s[c, p] = jnp.broadcast_to(jnp.exp(total[u]), (8, LANES))
        return 0

    def state_body(c, _):
        rows = pl.ds(pl.multiple_of(c * C, C), C)
        pairs = range(HEAD_PAIRS)
        vc = [v_s[rows, lane_slices[p]] for p in pairs]
        st = [state_ref[p] for p in pairs]
        ah = [_mm_nt(ar_s[c, p], st[p]) for p in pairs]
        u_stack = [_mm(inv_s[c, p], jnp.concatenate([ah[p][:C], ah[p][:C]], axis=0) + akv_s[c, p])
                   for p in pairs]
        y_stack = [jnp.concatenate([ah[p][C:], ah[p][C:]], axis=0)
                   + _mm(nrbk_s[c, p], jnp.concatenate([u_stack[p], vc[p], vc[p]], axis=0)) for p in pairs]
        for p in pairs:
            y_s[rows, lane_slices[p]] = jnp.where(head0, y_stack[p][:C], y_stack[p][C:])
        uv_t = [jnp.concatenate([jnp.where(head0, u_stack[p][:C], u_stack[p][C:]), vc[p]], axis=0).T
                for p in pairs]
        st_new = [st[p] * dec_s[c, p][0:1, :] + _mm(uv_t[p], bk_s[c, p]) for p in pairs]
        for p in pairs:
            state_ref[p] = jnp.where(same_head, st_new[p], 0.0)
        return 0

    chunk_group = 2 if n_chunks % 2 == 0 else 1
    lax.fori_loop(0, n_chunks // chunk_group, intra_body, 0)
    lax.fori_loop(0, n_chunks, state_body, 0)

    for p in range(HEAD_PAIRS):
        ls = slice(p * LANES, (p + 1) * LANES)
        y = y_s[:, ls]
        mean = _head_sum(y, m2) * (1.0 / HEAD_DIM)
        d = y - mean
        var = _head_sum(d * d, m2) * (1.0 / HEAD_DIM)
        yn = d * lax.rsqrt(var + GN_EPS) * ln_w[:, ls] + ln_b[:, ls]
        bonus = _head_sum(r_s[:, ls] * k_s[:, ls] * r_k[:, ls], m2) * v_s[:, ls]
        o_ref[:, ls] = ((yn + bonus) * g[:, ls]).astype(o_ref.dtype)


def _rwkv(rwm, rwl, v_first, vec, w2p, a2p, g2p, v1p, v2p, B, S):
    T = rwm.shape[0]
    W = RWKV_WIDTH
    tb = _tile(S, 512)
    nS = S // tb
    n_chunks = tb // CHUNK
    has_vres = v_first is not None
    row = lambda b, i: (b * nS + i, 0)
    full = lambda b, i: (0, 0)
    in_specs = [pl.BlockSpec((tb, RWM_COLS), row), pl.BlockSpec((tb, RWL_COLS), row)]
    args = [rwm, rwl]
    if has_vres:
        in_specs.append(pl.BlockSpec((tb, W), row))
        args.append(v_first)
    in_specs += [pl.BlockSpec(vec.shape, full), pl.BlockSpec(w2p.shape, full),
                 pl.BlockSpec(a2p.shape, full), pl.BlockSpec(g2p.shape, full)]
    args += [vec, w2p, a2p, g2p]
    if has_vres:
        in_specs += [pl.BlockSpec(v1p.shape, full), pl.BlockSpec(v2p.shape, full)]
        args += [v1p, v2p]
    out_specs = [pl.BlockSpec((tb, W), row)]
    out_shape = [jax.ShapeDtypeStruct((T, W), BF16)]
    if not has_vres:
        out_specs.append(pl.BlockSpec((tb, W), row))
        out_shape.append(jax.ShapeDtypeStruct((T, W), F32))
    outs = pl.pallas_call(
        functools.partial(_rwkv_kernel, has_vres=has_vres, n_chunks=n_chunks),
        grid=(B, nS),
        in_specs=in_specs,
        out_specs=out_specs,
        out_shape=out_shape,
        scratch_shapes=([pltpu.VMEM((HEAD_PAIRS, LANES, LANES), F32)] + [pltpu.VMEM((tb, W), F32)] * 7
                        + [pltpu.VMEM((n_chunks, HEAD_PAIRS, 2 * CHUNK, LANES), BF16)] * 3
                        + [pltpu.VMEM((n_chunks, HEAD_PAIRS, 2 * CHUNK, 2 * LANES), BF16),
                           pltpu.VMEM((n_chunks, HEAD_PAIRS, 2 * CHUNK, LANES), F32),
                           pltpu.VMEM((n_chunks, HEAD_PAIRS, 8, LANES), F32)]),
        compiler_params=_cparams(("arbitrary", "arbitrary")),
        name="rwkv7",
    )(*args)
    return (outs[0], v_first) if has_vres else (outs[0], outs[1])


def _outproj_kernel(*refs, moe):
    if moe:
        (x_ref, att_ref, rw_ref, mod_ref, g_ref, w_ref, rw_w_ref, rw_b_ref,
         x1_ref, h_ref, info_ref) = refs
    else:
        x_ref, att_ref, rw_ref, mod_ref, g_ref, w_ref, x1_ref, h_ref = refs
    mix = (jnp.dot(att_ref[...], w_ref[0, :ATT_WIDTH, :], preferred_element_type=F32)
           + jnp.dot(rw_ref[...], w_ref[0, ATT_WIDTH:, :], preferred_element_type=F32))
    x1 = x_ref[...] + mod_ref[0, 2:3, :] * mix
    x1_ref[...] = x1
    h = _rms_mod(x1, g_ref[...], mod_ref[0, 4:5, :], mod_ref[0, 3:4, :])
    h_ref[...] = h.astype(h_ref.dtype)
    if moe:
        h_hi, h_lo = _split2(h)
        w_hi, w_lo = _split2(rw_w_ref[...])
        logits = (jnp.dot(h_hi, w_hi, preferred_element_type=F32)
                  + jnp.dot(h_hi, w_lo, preferred_element_type=F32)
                  + jnp.dot(h_lo, w_hi, preferred_element_type=F32)) + rw_b_ref[...]
        lane = lax.broadcasted_iota(jnp.int32, logits.shape, 1)
        logits = jnp.where(lane < N_EXPERTS, logits, -jnp.inf)
        m1 = jnp.max(logits, axis=-1, keepdims=True)
        i1 = jnp.min(jnp.where(logits == m1, lane, LANES), axis=-1, keepdims=True)
        rest = jnp.where(lane == i1, -jnp.inf, logits)
        m2 = jnp.max(rest, axis=-1, keepdims=True)
        i2 = jnp.min(jnp.where(rest == m2, lane, LANES), axis=-1, keepdims=True)
        e = jnp.exp(m2 - m1)
        g1 = 1.0 / (1.0 + e)
        g2 = e / (1.0 + e)
        info = jnp.where(lane == 0, i1.astype(F32),
                         jnp.where(lane == 1, i2.astype(F32),
                                   jnp.where(lane == 2, g1, jnp.where(lane == 3, g2, 0.0))))
        info_ref[...] = info


def _outproj(x, att, rw, mod, g, w_out_b, B, S, l, router=None):
    T, D = x.shape
    tm = _tile(S, 512)
    nS = S // tm
    moe = router is not None
    row = lambda b, i: (b * nS + i, 0)
    full = lambda b, i: (0, 0)
    in_specs = [pl.BlockSpec((tm, D), row), pl.BlockSpec((tm, ATT_WIDTH), row),
                pl.BlockSpec((tm, RWKV_WIDTH), row),
                pl.BlockSpec((1, 6, D), lambda b, i: (l * B + b, 0, 0)),
                pl.BlockSpec((1, D), full),
                pl.BlockSpec((1, D, D), lambda b, i: (l, 0, 0))]
    args = [x, att, rw, mod, g, w_out_b]
    out_specs = [pl.BlockSpec((tm, D), row), pl.BlockSpec((tm, D), row)]
    out_shape = [jax.ShapeDtypeStruct((T, D), F32), jax.ShapeDtypeStruct((T, D), F32 if moe else BF16)]
    if moe:
        in_specs += [pl.BlockSpec((D, LANES), full), pl.BlockSpec((1, LANES), full)]
        args += list(router)
        out_specs.append(pl.BlockSpec((tm, LANES), row))
        out_shape.append(jax.ShapeDtypeStruct((T, LANES), F32))
    return pl.pallas_call(
        functools.partial(_outproj_kernel, moe=moe),
        grid=(B, nS),
        in_specs=in_specs,
        out_specs=out_specs,
        out_shape=out_shape,
        compiler_params=_cparams(("arbitrary", "arbitrary")),
        name="outproj",
    )(*args)


def _ffn_kernel(x_ref, h_ref, mod_ref, wg_ref, wu_ref, wd_ref, o_ref, acc_ref):
    f = pl.program_id(1)

    @pl.when(f == 0)
    def _():
        acc_ref[...] = jnp.zeros_like(acc_ref)

    hb = h_ref[...]
    gate = jnp.dot(hb, wg_ref[0], preferred_element_type=F32)
    up = jnp.dot(hb, wu_ref[0], preferred_element_type=F32)
    mid = (_silu(gate) * up).astype(BF16)
    acc_ref[...] += jnp.dot(mid, wd_ref[0], preferred_element_type=F32)

    @pl.when(f == pl.num_programs(1) - 1)
    def _():
        o_ref[...] = x_ref[...] + mod_ref[0, 5:6, :] * acc_ref[...]


def _ffn_dense(x1, h, mod, wg, wu, wd, B, S, l, li):
    T, D = x1.shape
    F = wg.shape[-1]
    tm = _tile(S, 512)
    tf = _tile(F, 1408)
    per_seq = S // tm
    return pl.pallas_call(
        _ffn_kernel,
        grid=(T // tm, F // tf),
        in_specs=[pl.BlockSpec((tm, D), lambda i, f: (i, 0)),
                  pl.BlockSpec((tm, D), lambda i, f: (i, 0)),
                  pl.BlockSpec((1, 6, D), lambda i, f: (l * B + i // per_seq, 0, 0)),
                  pl.BlockSpec((1, D, tf), lambda i, f: (li, 0, f)),
                  pl.BlockSpec((1, D, tf), lambda i, f: (li, 0, f)),
                  pl.BlockSpec((1, tf, D), lambda i, f: (li, f, 0))],
        out_specs=pl.BlockSpec((tm, D), lambda i, f: (i, 0)),
        out_shape=jax.ShapeDtypeStruct((T, D), F32),
        scratch_shapes=[pltpu.VMEM((tm, D), F32)],
        compiler_params=_cparams(("arbitrary", "arbitrary")),
        name="ffn_dense",
    )(x1, h, mod, wg, wu, wd)


def _experts_kernel(te_ref, nu_ref, tok_ref, h_ref, wg_ref, wu_ref, wd_ref, y_ref,
                    land_ref, xb_ref, acc_ref, sem):
    i = pl.program_id(0)
    f = pl.program_id(1)
    tm = land_ref.shape[0]
    n_used = nu_ref[0]

    def start_gather(tile):
        base = tile * tm

        def body(j, _):
            pltpu.make_async_copy(h_ref.at[pl.ds(tok_ref[base + j], 1)],
                                  land_ref.at[pl.ds(j, 1)], sem).start()
            return 0

        lax.fori_loop(0, tm, body, 0, unroll=DMA_ISSUE_UNROLL)

    @pl.when(i < n_used)
    def _():
        @pl.when(f == 0)
        def _():
            @pl.when(i == 0)
            def _():
                start_gather(0)

            pltpu.make_async_copy(h_ref.at[pl.ds(0, tm)], land_ref, sem).wait()
            xb_ref[...] = land_ref[...].astype(BF16)
            acc_ref[...] = jnp.zeros_like(acc_ref)

            @pl.when(i + 1 < n_used)
            def _():
                start_gather(i + 1)

        xb = xb_ref[...]
        gate = jnp.dot(xb, wg_ref[0, 0], preferred_element_type=F32)
        up = jnp.dot(xb, wu_ref[0, 0], preferred_element_type=F32)
        mid = (_silu(gate) * up).astype(BF16)
        acc_ref[...] += jnp.dot(mid, wd_ref[0, 0], preferred_element_type=F32)

        @pl.when(f == pl.num_programs(1) - 1)
        def _():
            y_ref[...] = acc_ref[...]

    @pl.when((i >= n_used) & (f == 0))
    def _():
        y_ref[...] = jnp.zeros_like(y_ref)


def _experts(h, row_tok, tile_e, n_used, wg, wu, wd, li, tm):
    n_rows = row_tok.shape[0]
    D = h.shape[1]
    assert h.shape[0] >= tm
    F = wg.shape[-1]
    tf = _tile(F, 1792)
    nf = F // tf

    def rows(i, f, te, nu, tok):
        return (i, 0)

    def wcol(i, f, te, nu, tok):
        return (li, te[jnp.minimum(i, nu[0] - 1)], 0, jnp.where(i < nu[0], f, nf - 1))

    def wrow(i, f, te, nu, tok):
        return (li, te[jnp.minimum(i, nu[0] - 1)], jnp.where(i < nu[0], f, nf - 1), 0)

    return pl.pallas_call(
        _experts_kernel,
        grid_spec=pltpu.PrefetchScalarGridSpec(
            num_scalar_prefetch=3,
            grid=(n_rows // tm, nf),
            in_specs=[pl.BlockSpec(memory_space=pl.ANY),
                      pl.BlockSpec((1, 1, D, tf), wcol),
                      pl.BlockSpec((1, 1, D, tf), wcol),
                      pl.BlockSpec((1, 1, tf, D), wrow)],
            out_specs=pl.BlockSpec((tm, D), rows),
            scratch_shapes=[pltpu.VMEM((tm, D), F32), pltpu.VMEM((tm, D), BF16),
                            pltpu.VMEM((tm, D), F32), pltpu.SemaphoreType.DMA(())]),
        out_shape=jax.ShapeDtypeStruct((n_rows, D), F32),
        compiler_params=_cparams(("arbitrary", "arbitrary")),
        name="moe_experts",
    )(tile_e, n_used, row_tok, h, wg, wu, wd)


def _combine_kernel(pos_ref, y_ref, x_ref, info_ref, mod_ref, fg_ref, o_ref, buf, sem, *, tc, final):
    i = pl.program_id(0)
    slot = i % 2

    def start_gather(tile, dst_slot):
        base = tile * (2 * tc)

        def body(j, _):
            pltpu.make_async_copy(y_ref.at[pl.ds(pos_ref[base + j], 1)],
                                  buf.at[dst_slot, pl.ds(j, 1)], sem.at[dst_slot]).start()
            return 0

        lax.fori_loop(0, 2 * tc, body, 0, unroll=DMA_ISSUE_UNROLL)

    @pl.when(i == 0)
    def _():
        start_gather(0, 0)

    @pl.when(i + 1 < pl.num_programs(0))
    def _():
        start_gather(i + 1, 1 - slot)

    pltpu.make_async_copy(y_ref.at[pl.ds(0, 2 * tc)], buf.at[slot], sem.at[slot]).wait()
    info = info_ref[...]
    ff = info[:, 2:3] * buf[slot, 0:tc, :] + info[:, 3:4] * buf[slot, tc:2 * tc, :]
    out = x_ref[...] + mod_ref[0, 5:6, :] * ff
    if final:
        ms = jnp.mean(out * out, axis=-1, keepdims=True)
        out = out * lax.rsqrt(ms + RMS_EPS) * fg_ref[...]
    o_ref[...] = out


def _combine(pos, y_rows, x1, info, mod, final_g, B, S, l, final):
    T, D = x1.shape
    tc = _tile(S, 256)
    per_seq = S // tc
    return pl.pallas_call(
        functools.partial(_combine_kernel, tc=tc, final=final),
        grid_spec=pltpu.PrefetchScalarGridSpec(
            num_scalar_prefetch=1,
            grid=(T // tc,),
            in_specs=[pl.BlockSpec(memory_space=pl.ANY),
                      pl.BlockSpec((tc, D), lambda i, pos: (i, 0)),
                      pl.BlockSpec((tc, LANES), lambda i, pos: (i, 0)),
                      pl.BlockSpec((1, 6, D), lambda i, pos: (l * B + i // per_seq, 0, 0)),
                      pl.BlockSpec((1, D), lambda i, pos: (0, 0))],
            out_specs=pl.BlockSpec((tc, D), lambda i, pos: (i, 0)),
            scratch_shapes=[pltpu.VMEM((2, 2 * tc, D), F32), pltpu.SemaphoreType.DMA((2,))]),
        out_shape=jax.ShapeDtypeStruct((T, D), F32),
        compiler_params=_cparams(("arbitrary",)),
        name="moe_combine",
    )(pos, y_rows, x1, info, mod, final_g)


def _moe(x1, h, info, mod, wg, wu, wd, final_g, B, S, l, li, final):
    T, D = x1.shape
    TK = T * TOP_K
    tm = _tile(TK // N_EXPERTS, 512)
    n_rows = TK + (N_EXPERTS - 1) * tm
    flat_e = info[:, :TOP_K].astype(jnp.int32).reshape(TK)
    onehot = (flat_e[:, None] == jnp.arange(N_EXPERTS, dtype=jnp.int32)[None, :]).astype(jnp.int32)
    csum = jnp.cumsum(onehot, axis=0)
    counts = csum[-1]
    rank = jnp.sum(onehot * csum, axis=1) - 1
    padded = (counts + tm - 1) // tm * tm
    pad_ends = jnp.cumsum(padded)
    pad_starts = pad_ends - padded
    dest = pad_starts[flat_e] + rank
    row_tok = jnp.zeros((n_rows,), jnp.int32).at[dest].set(jnp.arange(TK, dtype=jnp.int32) // TOP_K)
    tile_start = jnp.arange(n_rows // tm, dtype=jnp.int32) * tm
    tile_e = jnp.minimum(jnp.searchsorted(pad_ends, tile_start, side='right'), N_EXPERTS - 1).astype(jnp.int32)
    n_used_rows = pad_ends[-1:].astype(jnp.int32)
    n_used_tiles = n_used_rows // tm
    tc = _tile(S, 256)
    pos = dest.reshape(T // tc, tc, TOP_K).transpose(0, 2, 1).reshape(TK).astype(jnp.int32)

    y_rows = _experts(h, row_tok, tile_e, n_used_tiles, wg, wu, wd, li, tm)
    return _combine(pos, y_rows, x1, info, mod, final_g, B, S, l, final)


def kernel(x, c, w_ada, b_ada, norm1_g, norm2_g, final_g, w_in, w_out, rel_bias, rwkv_mu, rwkv_w0, rwkv_w2, rwkv_a0, rwkv_a2, rwkv_g2, rwkv_k_k, rwkv_k_a, rwkv_r_k, rwkv_ln_w, rwkv_ln_b, rwkv_v0, rwkv_v1, rwkv_v2, ffn_w_gate, ffn_w_up, ffn_w_down, moe_router_w, moe_router_b, moe_w_gate, moe_w_up, moe_w_down):
    B, S, D = x.shape
    L = w_in.shape[0]
    T = B * S
    W = RWKV_WIDTH

    mod = _ada_mod(c, w_ada, b_ada).reshape(L * B, 6, D)
    w_in_b = jnp.pad(w_in, ((0, 0), (0, 0), (0, N_IN_PAD - N_IN))).astype(BF16)
    w_out_b = w_out.astype(BF16)
    ffn_w = [w.astype(BF16) for w in (ffn_w_gate, ffn_w_up, ffn_w_down)]
    moe_w = [w.astype(BF16) for w in (moe_w_gate, moe_w_up, moe_w_down)]
    mu_p = jnp.pad(rwkv_mu, ((0, 0), (0, RWM_COLS + RWL_COLS - rwkv_mu.shape[1])))
    att_tables = _att_tables(rel_bias, S)
    zeros_w = jnp.zeros((1, W), F32)
    final_g2 = final_g.reshape(1, D)

    xf = x.reshape(T, D)
    v_first = None
    for l in range(L):
        qkv, qkv_mid, qkv_far, rwm, rwl = _inproj(xf, mod, norm1_g[l].reshape(1, D), w_in_b,
                                                   mu_p[l].reshape(1, -1), B, S, l)
        att = _attention(qkv, qkv_mid, qkv_far, att_tables, B, S)
        v0 = rwkv_v0[l - 1].reshape(1, W) if l > 0 else zeros_w
        vec = jnp.concatenate([rwkv_w0[l].reshape(1, W), rwkv_a0[l].reshape(1, W), rwkv_k_k[l].reshape(1, W),
                               rwkv_k_a[l].reshape(1, W), rwkv_r_k[l].reshape(1, W), rwkv_ln_w[l].reshape(1, W),
                               rwkv_ln_b[l].reshape(1, W), v0], axis=0)
        w2p = jnp.pad(rwkv_w2[l], ((0, LANES - D_DECAY_LORA), (0, 0)))
        a2p = jnp.pad(rwkv_a2[l], ((D_DECAY_LORA, 0), (0, 0)))
        g2p = jnp.pad(rwkv_g2[l], ((0, RWL_COLS - LANES - D_GATE_LORA), (0, 0)))
        if l > 0:
            v1p = jnp.pad(rwkv_v1[l - 1], ((0, 0), (0, LANES - D_MV_LORA)))
            v2p = jnp.pad(rwkv_v2[l - 1], ((0, LANES - D_MV_LORA), (0, 0)))
        else:
            v1p = v2p = None
        rw, v_first = _rwkv(rwm, rwl, v_first, vec, w2p, a2p, g2p, v1p, v2p, B, S)
        li = l // 2
        if l % 2 == 0:
            x1, h = _outproj(xf, att, rw, mod, norm2_g[l].reshape(1, D), w_out_b, B, S, l)
            xf = _ffn_dense(x1, h, mod, *ffn_w, B, S, l, li)
        else:
            router = (jnp.pad(moe_router_w[li], ((0, 0), (0, LANES - N_EXPERTS))),
                      jnp.pad(moe_router_b[li], (0, LANES - N_EXPERTS)).reshape(1, LANES))
            x1, h, info = _outproj(xf, att, rw, mod, norm2_g[l].reshape(1, D), w_out_b, B, S, l, router)
            xf = _moe(x1, h, info, mod, *moe_w, final_g2, B, S, l, li,
                      final=(l == L - 1))
    if L % 2 == 1:
        raise NotImplementedError("final norm is fused into the last (expert) layer")
    return xf.reshape(B, S, D)
```

```python
import functools
import math

import numpy as np
import jax
import jax.numpy as jnp
from jax import lax
from jax.experimental import pallas as pl
from jax.experimental.pallas import tpu as pltpu

F32 = jnp.float32
BF16 = jnp.bfloat16

D_MODEL = 1024
HEAD_DIM = 64
ATT_WIDTH = 512
RWKV_WIDTH = 512
N_HEADS = 8
HEAD_PAIRS = 4
LANES = 128
DILATED_GROUPS = ((128, 1), (512, 4), (2048, 16))
N_BUCKETS = 32
MAX_DISTANCE = 2048
NEG_INF = -1e30
D_DECAY_LORA = 64
D_AAA_LORA = 64
D_MV_LORA = 32
D_GATE_LORA = 160
N_IN = 3 * ATT_WIDTH + 3 * RWKV_WIDTH + D_DECAY_LORA + D_AAA_LORA + D_GATE_LORA
QKV_COLS = 3 * ATT_WIDTH
RWM_COLS = 3 * RWKV_WIDTH
RWL_COLS = 384
N_IN_PAD = QKV_COLS + RWM_COLS + RWL_COLS
GN_EPS = HEAD_DIM * 1e-5
RMS_EPS = 1e-6
N_EXPERTS = 8
TOP_K = 2
CHUNK = 64
VMEM_LIMIT = 56 * 1024 * 1024
LOG2_E = math.log2(math.e)
Q_SCALE = LOG2_E / math.sqrt(HEAD_DIM)
DMA_ISSUE_UNROLL = 8


def _cparams(sem):
    return pltpu.CompilerParams(dimension_semantics=sem, vmem_limit_bytes=VMEM_LIMIT)


def _tile(n, pref):
    t = min(n, pref)
    while n % t:
        t //= 2
    return t


def _mm(a, b):
    return jnp.dot(a.astype(BF16), b.astype(BF16), preferred_element_type=F32)


def _mm_nt(a, b):
    return lax.dot_general(a.astype(BF16), b.astype(BF16), (((1,), (1,)), ((), ())),
                           preferred_element_type=F32)


def _sigmoid(x):
    return 1.0 / (1.0 + jnp.exp(-x))


def _silu(x):
    return x * _sigmoid(x)


def _ada_kernel(c_ref, w_ref, b_ref, o_ref):
    ca = _silu(c_ref[...])
    o_ref[0] = _mm(ca, w_ref[0]) + b_ref[0]


def _ada_mod(c, w_ada, b_ada):
    L, D, N = w_ada.shape
    B = c.shape[0]
    tn = _tile(N, 1536)
    return pl.pallas_call(
        _ada_kernel,
        grid=(L, N // tn),
        in_specs=[pl.BlockSpec((B, D), lambda l, j: (0, 0)),
                  pl.BlockSpec((1, D, tn), lambda l, j: (l, 0, j)),
                  pl.BlockSpec((1, 1, tn), lambda l, j: (l, 0, j))],
        out_specs=pl.BlockSpec((1, B, tn), lambda l, j: (l, 0, j)),
        out_shape=jax.ShapeDtypeStruct((L, B, N), F32),
        compiler_params=_cparams(("arbitrary", "arbitrary")),
        name="ada_mod",
    )(c, w_ada, b_ada.reshape(L, 1, N))


def _rms_mod(xf, g, scale, shift):
    ms = jnp.mean(xf * xf, axis=-1, keepdims=True)
    return (xf * lax.rsqrt(ms + RMS_EPS) * g) * (1.0 + scale) + shift


def _inproj_kernel(x_ref, mod_ref, g_ref, w_ref, mu_ref, qkv_ref, qkv_mid_ref, qkv_far_ref, rwm_ref, rwl_ref,
                   carry_ref, qkv_scr):
    i = pl.program_id(1)
    tm = x_ref.shape[0]
    h = _rms_mod(x_ref[...], g_ref[...], mod_ref[0, 1:2, :], mod_ref[0, 0:1, :])
    acc = jnp.dot(h.astype(BF16), w_ref[0], preferred_element_type=F32)
    qkv = jnp.concatenate([acc[:, :ATT_WIDTH] * Q_SCALE, acc[:, ATT_WIDTH:QKV_COLS]], axis=1)
    qkv_ref[...] = qkv.astype(BF16)
    for c in range(QKV_COLS // LANES):
        cols = slice(c * LANES, (c + 1) * LANES)
        qkv_scr[c] = qkv[:, cols]
        for out_ref in (qkv_mid_ref, qkv_far_ref):
            d, rows = out_ref.shape[0], out_ref.shape[1]
            for r in range(d):
                out_ref[r, :, cols] = qkv_scr[c, pl.ds(r, rows, stride=d), :].astype(BF16)
    p = acc[:, QKV_COLS:]
    first = jnp.where(i == 0, 0.0, carry_ref[...])
    row = lax.broadcasted_iota(jnp.int32, p.shape, 0)
    prev = jnp.where(row == 0, first, pltpu.roll(p, 1, 0))
    carry_ref[...] = p[tm - 1:tm, :]
    pm = p + mu_ref[...] * (prev - p)
    rwm_ref[...] = pm[:, :RWM_COLS]
    rwl_ref[...] = pm[:, RWM_COLS:]


def _inproj(x, mod, g, w_in_l, mu_l, B, S, l):
    T, D = x.shape
    tm = _tile(S, 512)
    nS = S // tm
    d2, d3 = DILATED_GROUPS[1][1], DILATED_GROUPS[2][1]
    return pl.pallas_call(
        _inproj_kernel,
        grid=(B, nS),
        in_specs=[pl.BlockSpec((tm, D), lambda b, i: (b * nS + i, 0)),
                  pl.BlockSpec((1, 6, D), lambda b, i: (l * B + b, 0, 0)),
                  pl.BlockSpec((1, D), lambda b, i: (0, 0)),
                  pl.BlockSpec((1, D, N_IN_PAD), lambda b, i: (l, 0, 0)),
                  pl.BlockSpec((1, RWM_COLS + RWL_COLS), lambda b, i: (0, 0))],
        out_specs=[pl.BlockSpec((tm, QKV_COLS), lambda b, i: (b * nS + i, 0)),
                   pl.BlockSpec((None, d2, tm // d2, QKV_COLS), lambda b, i: (b, 0, i, 0)),
                   pl.BlockSpec((None, d3, tm // d3, QKV_COLS), lambda b, i: (b, 0, i, 0)),
                   pl.BlockSpec((tm, RWM_COLS), lambda b, i: (b * nS + i, 0)),
                   pl.BlockSpec((tm, RWL_COLS), lambda b, i: (b * nS + i, 0))],
        out_shape=[jax.ShapeDtypeStruct((T, QKV_COLS), BF16),
                   jax.ShapeDtypeStruct((B, d2, S // d2, QKV_COLS), BF16),
                   jax.ShapeDtypeStruct((B, d3, S // d3, QKV_COLS), BF16),
                   jax.ShapeDtypeStruct((T, RWM_COLS), F32),
                   jax.ShapeDtypeStruct((T, RWL_COLS), F32)],
        scratch_shapes=[pltpu.VMEM((1, RWM_COLS + RWL_COLS), F32), pltpu.VMEM((QKV_COLS // LANES, tm, LANES), F32)],
        compiler_params=_cparams(("arbitrary", "arbitrary")),
        name="inproj",
    )(x, mod, g, w_in_l, mu_l)


def _t5_bucket(n):
    max_exact = N_BUCKETS // 2
    large = max_exact + (np.log(np.maximum(n, 1) / max_exact) / np.log(MAX_DISTANCE / max_exact)
                         * (N_BUCKETS - max_exact)).astype(np.int32)
    large = np.minimum(large, N_BUCKETS - 1)
    return np.where(n < max_exact, n, large).astype(np.int32)


ATT_BLOCK = 128
ATT_KEYS = 256


def _toeplitz(vec, c0, rows, cols):
    H, n = vec.shape
    period = rows + cols - 1
    w = jnp.concatenate([jnp.full((H, cols - 1), NEG_INF, F32), vec, jnp.full((H, rows), NEG_INF, F32)], axis=1)
    rw = w[:, ::-1]
    a = w.shape[1] - 1 - (c0 + cols - 1)
    z = jnp.concatenate([rw[:, a:a + cols], rw[:, a - (rows - 1):a]], axis=1)
    flat = jnp.tile(z, (1, rows))[:, :rows * (period - 1)]
    return flat.reshape(H, rows, period - 1)[..., :cols]


def _att_tables(rel_bias, S):
    (w1, d1), (w2, d2), (w3, d3) = DILATED_GROUPS
    assert d1 == 1 and w1 == ATT_KEYS - ATT_BLOCK and w2 // d2 == w1 and S % (d3 * 8) == 0
    off = np.arange(S)
    mult = np.zeros(S, np.int64)
    for w, d in DILATED_GROUPS:
        mult += ((off % d == 0) & (off <= w)).astype(np.int64)
    logm = np.where(mult > 0, np.log(np.maximum(mult, 1)), NEG_INF).astype(np.float32)
    per_off = (rel_bias[_t5_bucket(off)].T.astype(F32) + jnp.asarray(logm)[None, :]) * LOG2_E
    u2 = np.arange(S // d2)
    u3 = np.arange(S // d3)
    band = jnp.where(jnp.asarray(off <= w1)[None], per_off, NEG_INF)
    mid = jnp.where(jnp.asarray((u2 > w1 // d2) & (u2 <= w2 // d2))[None], per_off[:, ::d2], NEG_INF)
    far = jnp.where(jnp.asarray((u3 > w2 // d3) & (u3 <= w3 // d3))[None], per_off[:, ::d3], NEG_INF)
    shift = ATT_KEYS - ATT_BLOCK
    t_band = jnp.stack([_toeplitz(band, 0, ATT_BLOCK, ATT_KEYS), _toeplitz(band, shift, ATT_BLOCK, ATT_KEYS)], axis=1)
    t_mid = jnp.stack([_toeplitz(mid, 0, ATT_BLOCK, ATT_KEYS), _toeplitz(mid, shift, ATT_BLOCK, ATT_KEYS)], axis=1)
    t_far = _toeplitz(far, 0, S // d3, S // d3)
    return t_band, t_mid, t_far


def _attend(units):
    idx = range(len(units))
    heads = range(2)
    head0 = [lax.broadcasted_iota(jnp.int32, u[0].shape, 1) < HEAD_DIM for u in units]
    qh = [[jnp.where(head0[i], units[i][0], jnp.zeros_like(units[i][0])),
           jnp.where(head0[i], jnp.zeros_like(units[i][0]), units[i][0])] for i in idx]
    s = [[lax.dot_general(qh[i][h], units[i][1], (((1,), (1,)), ((), ())), preferred_element_type=F32)
          + units[i][3 + h] for h in heads] for i in idx]
    mx = [[jnp.max(s[i][h], axis=-1, keepdims=True) for h in heads] for i in idx]
    p = [[jnp.exp2(s[i][h] - mx[i][h]) for h in heads] for i in idx]
    ps = [[jnp.sum(p[i][h], axis=-1, keepdims=True) for h in heads] for i in idx]
    pv = [[jnp.dot(p[i][h].astype(BF16), units[i][2], preferred_element_type=F32) for h in heads] for i in idx]
    out = []
    for i in idx:
        shape = units[i][0].shape
        out.append((jnp.where(head0[i], jnp.broadcast_to(mx[i][0], shape), jnp.broadcast_to(mx[i][1], shape)),
                    jnp.where(head0[i], jnp.broadcast_to(ps[i][0], shape), jnp.broadcast_to(ps[i][1], shape)),
                    jnp.where(head0[i], pv[i][0], pv[i][1])))
    return out


def _attn_kernel(qa_ref, ka_ref, va_ref, qm_ref, km_ref, vm_ref, qf_ref, kf_ref, vf_ref,
                 tband_ref, tmid_ref, tfar_ref, o_ref, part_ref):
    R, K = ATT_BLOCK, ATT_KEYS
    S = qa_ref.shape[0]
    n_mid_res, n_mid = qm_ref.shape[0], qm_ref.shape[1]
    n_far_res, n_far = qf_ref.shape[0], qf_ref.shape[1]
    group = 4

    def window(blk):
        start = pl.multiple_of(jnp.maximum(blk * R - (K - R), 0), R)
        return pl.ds(pl.multiple_of(blk * R, R), R), pl.ds(start, K), jnp.minimum(blk, 1)

    def park(cls, rows, results):
        for nat, parts in zip(rows, results):
            for j in range(3):
                part_ref[cls, j, nat, :] = parts[j]

    def band_body(it, _):
        units, rows = [], []
        for g in range(group):
            qrows, krows, var = window(it * group + g)
            rows.append(qrows)
            units.append((qa_ref[qrows, :], ka_ref[krows, :], va_ref[krows, :],
                          tband_ref[0, var], tband_ref[1, var]))
        park(0, rows, _attend(units))
        return 0

    lax.fori_loop(0, S // (R * group), band_body, 0)

    def mid_body(blk, _):
        qrows, krows, var = window(blk)
        units, rows = [], []
        for r in range(n_mid_res):
            rows.append(pl.ds(blk * (R * n_mid_res) + r, R, stride=n_mid_res))
            units.append((qm_ref[r, qrows, :], km_ref[r, krows, :], vm_ref[r, krows, :],
                          tmid_ref[0, var], tmid_ref[1, var]))
        park(1, rows, _attend(units))
        return 0

    lax.fori_loop(0, n_mid // R, mid_body, 0)

    def far_body(it, _):
        units, rows = [], []
        for g in range(group):
            r = it * group + g
            rows.append(pl.ds(r, n_far, stride=n_far_res))
            units.append((qf_ref[r], kf_ref[r], vf_ref[r], tfar_ref[0], tfar_ref[1]))
        park(2, rows, _attend(units))
        return 0

    lax.fori_loop(0, n_far_res // group, far_body, 0)

    def merge_body(i, _):
        rows = pl.ds(pl.multiple_of(i * R, R), R)
        m = [part_ref[c, 0, rows, :] for c in range(3)]
        top = jnp.maximum(jnp.maximum(m[0], m[1]), m[2])
        w = [jnp.exp2(m[c] - top) for c in range(3)]
        total = sum(w[c] * part_ref[c, 1, rows, :] for c in range(3))
        acc = sum(w[c] * part_ref[c, 2, rows, :] for c in range(3))
        o_ref[rows, :] = (acc / total).astype(o_ref.dtype)
        return 0

    lax.fori_loop(0, S // R, merge_body, 0)


def _attention(qkv, qkv_mid, qkv_far, tables, B, S):
    T = qkv.shape[0]
    t_band, t_mid, t_far = tables
    d2, d3 = DILATED_GROUPS[1][1], DILATED_GROUPS[2][1]
    assert S % (ATT_BLOCK * 4) == 0 and S // d2 >= ATT_KEYS
    nat = lambda c: pl.BlockSpec((S, LANES), lambda hp, b: (b, c * HEAD_PAIRS + hp))
    mid = lambda c: pl.BlockSpec((None, d2, S // d2, LANES), lambda hp, b: (b, 0, 0, c * HEAD_PAIRS + hp))
    far = lambda c: pl.BlockSpec((None, d3, S // d3, LANES), lambda hp, b: (b, 0, 0, c * HEAD_PAIRS + hp))
    return pl.pallas_call(
        _attn_kernel,
        grid=(HEAD_PAIRS, B),
        in_specs=[nat(0), nat(1), nat(2), mid(0), mid(1), mid(2), far(0), far(1), far(2),
                  pl.BlockSpec((2,) + t_band.shape[1:], lambda hp, b: (hp, 0, 0, 0)),
                  pl.BlockSpec((2,) + t_mid.shape[1:], lambda hp, b: (hp, 0, 0, 0)),
                  pl.BlockSpec((2,) + t_far.shape[1:], lambda hp, b: (hp, 0, 0))],
        out_specs=pl.BlockSpec((S, LANES), lambda hp, b: (b, hp)),
        out_shape=jax.ShapeDtypeStruct((T, ATT_WIDTH), BF16),
        scratch_shapes=[pltpu.VMEM((3, 3, S, LANES), F32)],
        compiler_params=_cparams(("arbitrary", "arbitrary")),
        name="dilated_attn",
    )(qkv, qkv, qkv, qkv_mid, qkv_mid, qkv_mid, qkv_far, qkv_far, qkv_far, t_band, t_mid, t_far)


def _split2(x):
    hi = x.astype(BF16)
    lo = (x - hi.astype(F32)).astype(BF16)
    return hi, lo


def _head_sum(x, m2):
    hi, lo = _split2(x)
    return jnp.dot(jnp.concatenate([hi, lo], axis=1), m2, preferred_element_type=F32)


def _rwkv_kernel(*refs, has_vres, n_chunks):
    if has_vres:
        (rwm_ref, rwl_ref, vf_ref, vec_ref, w2_ref, a2_ref, g2_ref, v1_ref, v2_ref,
         o_ref, state_ref, r_s, k_s, v_s, lw_s, kk_s, kb_s, y_s,
         ar_s, inv_s, bk_s, nrbk_s, akv_s, dec_s) = refs
    else:
        (rwm_ref, rwl_ref, vec_ref, w2_ref, a2_ref, g2_ref,
         o_ref, vf_out_ref, state_ref, r_s, k_s, v_s, lw_s, kk_s, kb_s, y_s,
         ar_s, inv_s, bk_s, nrbk_s, akv_s, dec_s) = refs
    C = CHUNK
    W = RWKV_WIDTH

    @pl.when(pl.program_id(1) == 0)
    def _():
        state_ref[...] = jnp.zeros_like(state_ref)

    ri = lax.broadcasted_iota(jnp.int32, (2 * LANES, LANES), 0)
    ci = lax.broadcasted_iota(jnp.int32, (2 * LANES, LANES), 1)
    m2 = ((ri % LANES) // HEAD_DIM == ci // HEAD_DIM).astype(BF16)

    vec = vec_ref[...]
    w0, a0, k_k, k_a, r_k, ln_w, ln_b, v0 = [vec[i:i + 1, :] for i in range(8)]

    r = rwm_ref[:, 0:W]
    k = rwm_ref[:, W:2 * W]
    v = rwm_ref[:, 2 * W:3 * W]
    lora = rwl_ref[:, 0:LANES]
    w_raw = w0 + _mm(jnp.tanh(lora), w2_ref[...])
    lw_s[...] = -math.exp(-0.5) * _sigmoid(w_raw)
    a = _sigmoid(a0 + _mm(lora, a2_ref[...]))
    g = _mm(_sigmoid(rwl_ref[:, LANES:RWL_COLS]), g2_ref[...])
    if has_vres:
        mix = _sigmoid(v0 + _mm(_mm(v, v1_ref[...]), v2_ref[...]))
        v = v + (vf_ref[...] - v) * mix
    else:
        vf_out_ref[...] = v
    kk = k * k_k
    for p in range(HEAD_PAIRS):
        ls = slice(p * LANES, (p + 1) * LANES)
        kkp = kk[:, ls]
        kkp = kkp * jnp.minimum(lax.rsqrt(_head_sum(kkp * kkp, m2)), 1e12)
        kk_s[:, ls] = kkp
        kb_s[:, ls] = kkp * a[:, ls]
    kmod = k * (1.0 + (a - 1.0) * k_a)
    r_s[...] = r
    k_s[...] = kmod
    v_s[...] = v

    ri = lax.broadcasted_iota(jnp.int32, (LANES, LANES), 0)
    ci = lax.broadcasted_iota(jnp.int32, (LANES, LANES), 1)
    same_head = (ri // C) == (ci // C)
    strict = same_head & ((ri % C) > (ci % C))
    incl = same_head & ((ri % C) >= (ci % C))
    eye = (ri == ci).astype(F32)
    lane = lax.broadcasted_iota(jnp.int32, (C, LANES), 1)
    head0 = lane < HEAD_DIM
    tri_r = lax.broadcasted_iota(jnp.int32, (C, 3 * C), 0)
    tri_c = lax.broadcasted_iota(jnp.int32, (C, 3 * C), 1)
    tri3 = ((tri_c % C) <= tri_r).astype(BF16)

    lane_slices = [slice(p * LANES, (p + 1) * LANES) for p in range(HEAD_PAIRS)]

    def cumsum(x):
        hi = x.astype(BF16)
        rem = x - hi.astype(F32)
        mid = rem.astype(BF16)
        lo = (rem - mid.astype(F32)).astype(BF16)
        return jnp.dot(tri3, jnp.concatenate([hi, mid, lo], axis=0), preferred_element_type=F32)

    def per_head_rows(x, y):
        zero = jnp.zeros_like(x)
        return jnp.concatenate([jnp.where(head0, x, zero), jnp.where(head0, zero, x),
                                jnp.where(head0, y, zero), jnp.where(head0, zero, y)], axis=0)

    def intra_body(it, _):
        units = [(it * chunk_group + g, p) for g in range(chunk_group) for p in range(HEAD_PAIRS)]
        idx = range(len(units))
        rows = [pl.ds(pl.multiple_of(c * C, C), C) for c, _ in units]
        ls = [lane_slices[p] for _, p in units]
        rc = [r_s[rows[u], ls[u]] for u in idx]
        kc = [k_s[rows[u], ls[u]] for u in idx]
        vc = [v_s[rows[u], ls[u]] for u in idx]
        lw = [lw_s[rows[u], ls[u]] for u in idx]
        kkc = [kk_s[rows[u], ls[u]] for u in idx]
        kbc = [kb_s[rows[u], ls[u]] for u in idx]
        cum = [cumsum(lw[u]) for u in idx]
        total = [cum[u][C - 1:C, :] for u in idx]
        g_inv = [jnp.exp(-cum[u]) for u in idx]
        a_t = [-kkc[u] * jnp.exp(cum[u] - lw[u]) for u in idx]
        r_t = [rc[u] * jnp.exp(cum[u]) for u in idx]
        b_t = [kbc[u] * g_inv[u] for u in idx]
        k_t = [kc[u] * g_inv[u] for u in idx]
        to_end = [jnp.exp(total[u] - cum[u]) for u in idx]
        gram = [_mm_nt(per_head_rows(a_t[u], r_t[u]), per_head_rows(b_t[u], k_t[u])) for u in idx]
        n_ab = [jnp.where(strict, gram[u][:LANES, :LANES], 0.0) for u in idx]
        n_ak = [jnp.where(strict, gram[u][:LANES, LANES:], 0.0) for u in idx]
        n_rb = [jnp.where(incl, gram[u][LANES:, :LANES], 0.0) for u in idx]
        n_rk = [jnp.where(incl, gram[u][LANES:, LANES:], 0.0) for u in idx]
        inv = [eye + n_ab[u] for u in idx]
        pw = [_mm(n_ab[u], n_ab[u]) for u in idx]
        n_rounds = int(math.log2(C)) - 1
        for i in range(n_rounds):
            if i + 1 < n_rounds:
                both = [_mm(pw[u], jnp.concatenate([inv[u], pw[u]], axis=1)) for u in idx]
                inv = [inv[u] + both[u][:, :LANES] for u in idx]
                pw = [both[u][:, LANES:] for u in idx]
            else:
                inv = [inv[u] + _mm(pw[u], inv[u]) for u in idx]
        for u, (c, p) in enumerate(units):
            ar_s[c, p] = jnp.concatenate([a_t[u], r_t[u]], axis=0).astype(BF16)
            inv_s[c, p] = inv[u].astype(BF16)
            akv_s[c, p] = _mm(n_ak[u], jnp.concatenate([vc[u], vc[u]], axis=0))
            nrbk_s[c, p] = jnp.concatenate([n_rb[u], n_rk[u]], axis=1).astype(BF16)
            bk_s[c, p] = jnp.concatenate([kbc[u] * to_end[u], kc[u] * to_end[u]], axis=0).astype(BF16)
            dec_s[c, p] = jnp.broadcast_to(jnp.exp(total[u]), (8, LANES))
        return 0

    def state_body(c, _):
        rows = pl.ds(pl.multiple_of(c * C, C), C)
        pairs = range(HEAD_PAIRS)
        vc = [v_s[rows, lane_slices[p]] for p in pairs]
        st = [state_ref[p] for p in pairs]
        ah = [_mm_nt(ar_s[c, p], st[p]) for p in pairs]
        u_stack = [_mm(inv_s[c, p], jnp.concatenate([ah[p][:C], ah[p][:C]], axis=0) + akv_s[c, p])
                   for p in pairs]
        y_stack = [jnp.concatenate([ah[p][C:], ah[p][C:]], axis=0)
                   + _mm(nrbk_s[c, p], jnp.concatenate([u_stack[p], vc[p], vc[p]], axis=0)) for p in pairs]
        for p in pairs:
            y_s[rows, lane_slices[p]] = jnp.where(head0, y_stack[p][:C], y_stack[p][C:])
        uv_t = [jnp.concatenate([jnp.where(head0, u_stack[p][:C], u_stack[p][C:]), vc[p]], axis=0).T
                for p in pairs]
        st_new = [st[p] * dec_s[c, p][0:1, :] + _mm(uv_t[p], bk_s[c, p]) for p in pairs]
        for p in pairs:
            state_ref[p] = jnp.where(same_head, st_new[p], 0.0)
        return 0

    chunk_group = 2 if n_chunks % 2 == 0 else 1
    lax.fori_loop(0, n_chunks // chunk_group, intra_body, 0)
    lax.fori_loop(0, n_chunks, state_body, 0)

    for p in range(HEAD_PAIRS):
        ls = slice(p * LANES, (p + 1) * LANES)
        y = y_s[:, ls]
        mean = _head_sum(y, m2) * (1.0 / HEAD_DIM)
        d = y - mean
        var = _head_sum(d * d, m2) * (1.0 / HEAD_DIM)
        yn = d * lax.rsqrt(var + GN_EPS) * ln_w[:, ls] + ln_b[:, ls]
        bonus = _head_sum(r_s[:, ls] * k_s[:, ls] * r_k[:, ls], m2) * v_s[:, ls]
        o_ref[:, ls] = ((yn + bonus) * g[:, ls]).astype(o_ref.dtype)


def _rwkv(rwm, rwl, v_first, vec, w2p, a2p, g2p, v1p, v2p, B, S):
    T = rwm.shape[0]
    W = RWKV_WIDTH
    tb = _tile(S, 512)
    nS = S // tb
    n_chunks = tb // CHUNK
    has_vres = v_first is not None
    row = lambda b, i: (b * nS + i, 0)
    full = lambda b, i: (0, 0)
    in_specs = [pl.BlockSpec((tb, RWM_COLS), row), pl.BlockSpec((tb, RWL_COLS), row)]
    args = [rwm, rwl]
    if has_vres:
        in_specs.append(pl.BlockSpec((tb, W), row))
        args.append(v_first)
    in_specs += [pl.BlockSpec(vec.shape, full), pl.BlockSpec(w2p.shape, full),
                 pl.BlockSpec(a2p.shape, full), pl.BlockSpec(g2p.shape, full)]
    args += [vec, w2p, a2p, g2p]
    if has_vres:
        in_specs += [pl.BlockSpec(v1p.shape, full), pl.BlockSpec(v2p.shape, full)]
        args += [v1p, v2p]
    out_specs = [pl.BlockSpec((tb, W), row)]
    out_shape = [jax.ShapeDtypeStruct((T, W), BF16)]
    if not has_vres:
        out_specs.append(pl.BlockSpec((tb, W), row))
        out_shape.append(jax.ShapeDtypeStruct((T, W), F32))
    outs = pl.pallas_call(
        functools.partial(_rwkv_kernel, has_vres=has_vres, n_chunks=n_chunks),
        grid=(B, nS),
        in_specs=in_specs,
        out_specs=out_specs,
        out_shape=out_shape,
        scratch_shapes=([pltpu.VMEM((HEAD_PAIRS, LANES, LANES), F32)] + [pltpu.VMEM((tb, W), F32)] * 7
                        + [pltpu.VMEM((n_chunks, HEAD_PAIRS, 2 * CHUNK, LANES), BF16)] * 3
                        + [pltpu.VMEM((n_chunks, HEAD_PAIRS, 2 * CHUNK, 2 * LANES), BF16),
                           pltpu.VMEM((n_chunks, HEAD_PAIRS, 2 * CHUNK, LANES), F32),
                           pltpu.VMEM((n_chunks, HEAD_PAIRS, 8, LANES), F32)]),
        compiler_params=_cparams(("arbitrary", "arbitrary")),
        name="rwkv7",
    )(*args)
    return (outs[0], v_first) if has_vres else (outs[0], outs[1])


def _outproj_kernel(*refs, moe):
    if moe:
        (x_ref, att_ref, rw_ref, mod_ref, g_ref, w_ref, rw_w_ref, rw_b_ref,
         x1_ref, h_ref, info_ref) = refs
    else:
        x_ref, att_ref, rw_ref, mod_ref, g_ref, w_ref, x1_ref, h_ref = refs
    mix = (jnp.dot(att_ref[...], w_ref[0, :ATT_WIDTH, :], preferred_element_type=F32)
           + jnp.dot(rw_ref[...], w_ref[0, ATT_WIDTH:, :], preferred_element_type=F32))
    x1 = x_ref[...] + mod_ref[0, 2:3, :] * mix
    x1_ref[...] = x1
    h = _rms_mod(x1, g_ref[...], mod_ref[0, 4:5, :], mod_ref[0, 3:4, :])
    h_ref[...] = h.astype(h_ref.dtype)
    if moe:
        h_hi, h_lo = _split2(h)
        w_hi, w_lo = _split2(rw_w_ref[...])
        logits = (jnp.dot(h_hi, w_hi, preferred_element_type=F32)
                  + jnp.dot(h_hi, w_lo, preferred_element_type=F32)
                  + jnp.dot(h_lo, w_hi, preferred_element_type=F32)) + rw_b_ref[...]
        lane = lax.broadcasted_iota(jnp.int32, logits.shape, 1)
        logits = jnp.where(lane < N_EXPERTS, logits, -jnp.inf)
        m1 = jnp.max(logits, axis=-1, keepdims=True)
        i1 = jnp.min(jnp.where(logits == m1, lane, LANES), axis=-1, keepdims=True)
        rest = jnp.where(lane == i1, -jnp.inf, logits)
        m2 = jnp.max(rest, axis=-1, keepdims=True)
        i2 = jnp.min(jnp.where(rest == m2, lane, LANES), axis=-1, keepdims=True)
        e = jnp.exp(m2 - m1)
        g1 = 1.0 / (1.0 + e)
        g2 = e / (1.0 + e)
        info = jnp.where(lane == 0, i1.astype(F32),
                         jnp.where(lane == 1, i2.astype(F32),
                                   jnp.where(lane == 2, g1, jnp.where(lane == 3, g2, 0.0))))
        info_ref[...] = info


def _outproj(x, att, rw, mod, g, w_out_b, B, S, l, router=None):
    T, D = x.shape
    tm = _tile(S, 512)
    nS = S // tm
    moe = router is not None
    row = lambda b, i: (b * nS + i, 0)
    full = lambda b, i: (0, 0)
    in_specs = [pl.BlockSpec((tm, D), row), pl.BlockSpec((tm, ATT_WIDTH), row),
                pl.BlockSpec((tm, RWKV_WIDTH), row),
                pl.BlockSpec((1, 6, D), lambda b, i: (l * B + b, 0, 0)),
                pl.BlockSpec((1, D), full),
                pl.BlockSpec((1, D, D), lambda b, i: (l, 0, 0))]
    args = [x, att, rw, mod, g, w_out_b]
    out_specs = [pl.BlockSpec((tm, D), row), pl.BlockSpec((tm, D), row)]
    out_shape = [jax.ShapeDtypeStruct((T, D), F32), jax.ShapeDtypeStruct((T, D), F32 if moe else BF16)]
    if moe:
        in_specs += [pl.BlockSpec((D, LANES), full), pl.BlockSpec((1, LANES), full)]
        args += list(router)
        out_specs.append(pl.BlockSpec((tm, LANES), row))
        out_shape.append(jax.ShapeDtypeStruct((T, LANES), F32))
    return pl.pallas_call(
        functools.partial(_outproj_kernel, moe=moe),
        grid=(B, nS),
        in_specs=in_specs,
        out_specs=out_specs,
        out_shape=out_shape,
        compiler_params=_cparams(("arbitrary", "arbitrary")),
        name="outproj",
    )(*args)


def _ffn_kernel(x_ref, h_ref, mod_ref, wg_ref, wu_ref, wd_ref, o_ref, acc_ref):
    f = pl.program_id(1)

    @pl.when(f == 0)
    def _():
        acc_ref[...] = jnp.zeros_like(acc_ref)

    hb = h_ref[...]
    gate = jnp.dot(hb, wg_ref[0], preferred_element_type=F32)
    up = jnp.dot(hb, wu_ref[0], preferred_element_type=F32)
    mid = (_silu(gate) * up).astype(BF16)
    acc_ref[...] += jnp.dot(mid, wd_ref[0], preferred_element_type=F32)

    @pl.when(f == pl.num_programs(1) - 1)
    def _():
        o_ref[...] = x_ref[...] + mod_ref[0, 5:6, :] * acc_ref[...]


def _ffn_dense(x1, h, mod, wg, wu, wd, B, S, l, li):
    T, D = x1.shape
    F = wg.shape[-1]
    tm = _tile(S, 512)
    tf = _tile(F, 1408)
    per_seq = S // tm
    return pl.pallas_call(
        _ffn_kernel,
        grid=(T // tm, F // tf),
        in_specs=[pl.BlockSpec((tm, D), lambda i, f: (i, 0)),
                  pl.BlockSpec((tm, D), lambda i, f: (i, 0)),
                  pl.BlockSpec((1, 6, D), lambda i, f: (l * B + i // per_seq, 0, 0)),
                  pl.BlockSpec((1, D, tf), lambda i, f: (li, 0, f)),
                  pl.BlockSpec((1, D, tf), lambda i, f: (li, 0, f)),
                  pl.BlockSpec((1, tf, D), lambda i, f: (li, f, 0))],
        out_specs=pl.BlockSpec((tm, D), lambda i, f: (i, 0)),
        out_shape=jax.ShapeDtypeStruct((T, D), F32),
        scratch_shapes=[pltpu.VMEM((tm, D), F32)],
        compiler_params=_cparams(("arbitrary", "arbitrary")),
        name="ffn_dense",
    )(x1, h, mod, wg, wu, wd)


def _experts_kernel(te_ref, nu_ref, tok_ref, h_ref, wg_ref, wu_ref, wd_ref, y_ref,
                    land_ref, xb_ref, acc_ref, sem, *, n_f):
    i = pl.program_id(0)
    f = pl.program_id(1)
    tm = land_ref.shape[0]
    n_used = nu_ref[0]

    def start_gather(tile):
        base = tile * tm

        def body(j, _):
            pltpu.make_async_copy(h_ref.at[pl.ds(tok_ref[base + j], 1)],
                                  land_ref.at[pl.ds(j, 1)], sem).start()
            return 0

        lax.fori_loop(0, tm, body, 0, unroll=DMA_ISSUE_UNROLL)

    @pl.when(i < n_used)
    def _():
        @pl.when(f == 0)
        def _():
            @pl.when(i == 0)
            def _():
                start_gather(0)

            pltpu.make_async_copy(h_ref.at[pl.ds(0, tm)], land_ref, sem).wait()
            xb_ref[...] = land_ref[...].astype(BF16)
            acc_ref[...] = jnp.zeros_like(acc_ref)

        rows_per_step = tm // n_f
        nxt = jnp.minimum(i + 1, n_used - 1)
        src_base = nxt * tm + f * rows_per_step
        dst_base = f * rows_per_step
        for j in range(rows_per_step):
            pltpu.make_async_copy(h_ref.at[pl.ds(tok_ref[src_base + j], 1)],
                                  land_ref.at[pl.ds(dst_base + j, 1)], sem).start()

        xb = xb_ref[...]
        gate = jnp.dot(xb, wg_ref[0, 0], preferred_element_type=F32)
        up = jnp.dot(xb, wu_ref[0, 0], preferred_element_type=F32)
        mid = (_silu(gate) * up).astype(BF16)
        acc_ref[...] += jnp.dot(mid, wd_ref[0, 0], preferred_element_type=F32)

        @pl.when(f == n_f - 1)
        def _():
            y_ref[...] = acc_ref[...]

            @pl.when(i == n_used - 1)
            def _():
                pltpu.make_async_copy(h_ref.at[pl.ds(0, tm)], land_ref, sem).wait()

    @pl.when((i >= n_used) & (f == 0))
    def _():
        y_ref[...] = jnp.zeros_like(y_ref)


def _experts(h, row_tok, tile_e, n_used, wg, wu, wd, li, tm):
    n_rows = row_tok.shape[0]
    D = h.shape[1]
    assert h.shape[0] >= tm
    F = wg.shape[-1]
    tf = _tile(F, 1792)
    nf = F // tf

    def rows(i, f, te, nu, tok):
        return (i, 0)

    def wcol(i, f, te, nu, tok):
        return (li, te[jnp.minimum(i, nu[0] - 1)], 0, jnp.where(i < nu[0], f, nf - 1))

    def wrow(i, f, te, nu, tok):
        return (li, te[jnp.minimum(i, nu[0] - 1)], jnp.where(i < nu[0], f, nf - 1), 0)

    return pl.pallas_call(
        functools.partial(_experts_kernel, n_f=nf),
        grid_spec=pltpu.PrefetchScalarGridSpec(
            num_scalar_prefetch=3,
            grid=(n_rows // tm, nf),
            in_specs=[pl.BlockSpec(memory_space=pl.ANY),
                      pl.BlockSpec((1, 1, D, tf), wcol),
                      pl.BlockSpec((1, 1, D, tf), wcol),
                      pl.BlockSpec((1, 1, tf, D), wrow)],
            out_specs=pl.BlockSpec((tm, D), rows),
            scratch_shapes=[pltpu.VMEM((tm, D), F32), pltpu.VMEM((tm, D), BF16),
                            pltpu.VMEM((tm, D), F32), pltpu.SemaphoreType.DMA(())]),
        out_shape=jax.ShapeDtypeStruct((n_rows, D), F32),
        compiler_params=_cparams(("arbitrary", "arbitrary")),
        name="moe_experts",
    )(tile_e, n_used, row_tok, h, wg, wu, wd)


def _combine_kernel(pos_ref, y_ref, x_ref, info_ref, mod_ref, fg_ref, o_ref, buf, sem, *, tc, final):
    i = pl.program_id(0)
    slot = i % 2

    def start_gather(tile, dst_slot):
        base = tile * (2 * tc)

        def body(j, _):
            pltpu.make_async_copy(y_ref.at[pl.ds(pos_ref[base + j], 1)],
                                  buf.at[dst_slot, pl.ds(j, 1)], sem.at[dst_slot]).start()
            return 0

        lax.fori_loop(0, 2 * tc, body, 0, unroll=DMA_ISSUE_UNROLL)

    @pl.when(i == 0)
    def _():
        start_gather(0, 0)

    @pl.when(i + 1 < pl.num_programs(0))
    def _():
        start_gather(i + 1, 1 - slot)

    pltpu.make_async_copy(y_ref.at[pl.ds(0, 2 * tc)], buf.at[slot], sem.at[slot]).wait()
    info = info_ref[...]
    ff = info[:, 2:3] * buf[slot, 0:tc, :] + info[:, 3:4] * buf[slot, tc:2 * tc, :]
    out = x_ref[...] + mod_ref[0, 5:6, :] * ff
    if final:
        ms = jnp.mean(out * out, axis=-1, keepdims=True)
        out = out * lax.rsqrt(ms + RMS_EPS) * fg_ref[...]
    o_ref[...] = out


def _combine(pos, y_rows, x1, info, mod, final_g, B, S, l, final):
    T, D = x1.shape
    tc = _tile(S, 256)
    per_seq = S // tc
    return pl.pallas_call(
        functools.partial(_combine_kernel, tc=tc, final=final),
        grid_spec=pltpu.PrefetchScalarGridSpec(
            num_scalar_prefetch=1,
            grid=(T // tc,),
            in_specs=[pl.BlockSpec(memory_space=pl.ANY),
                      pl.BlockSpec((tc, D), lambda i, pos: (i, 0)),
                      pl.BlockSpec((tc, LANES), lambda i, pos: (i, 0)),
                      pl.BlockSpec((1, 6, D), lambda i, pos: (l * B + i // per_seq, 0, 0)),
                      pl.BlockSpec((1, D), lambda i, pos: (0, 0))],
            out_specs=pl.BlockSpec((tc, D), lambda i, pos: (i, 0)),
            scratch_shapes=[pltpu.VMEM((2, 2 * tc, D), F32), pltpu.SemaphoreType.DMA((2,))]),
        out_shape=jax.ShapeDtypeStruct((T, D), F32),
        compiler_params=_cparams(("arbitrary",)),
        name="moe_combine",
    )(pos, y_rows, x1, info, mod, final_g)


def _moe(x1, h, info, mod, wg, wu, wd, final_g, B, S, l, li, final):
    T, D = x1.shape
    TK = T * TOP_K
    tm = _tile(TK // N_EXPERTS, 512)
    n_rows = TK + (N_EXPERTS - 1) * tm
    flat_e = info[:, :TOP_K].astype(jnp.int32).reshape(TK)
    onehot = (flat_e[:, None] == jnp.arange(N_EXPERTS, dtype=jnp.int32)[None, :]).astype(jnp.int32)
    csum = jnp.cumsum(onehot, axis=0)
    counts = csum[-1]
    rank = jnp.sum(onehot * csum, axis=1) - 1
    padded = (counts + tm - 1) // tm * tm
    pad_ends = jnp.cumsum(padded)
    pad_starts = pad_ends - padded
    dest = pad_starts[flat_e] + rank
    row_tok = jnp.zeros((n_rows,), jnp.int32).at[dest].set(jnp.arange(TK, dtype=jnp.int32) // TOP_K)
    tile_start = jnp.arange(n_rows // tm, dtype=jnp.int32) * tm
    tile_e = jnp.minimum(jnp.searchsorted(pad_ends, tile_start, side='right'), N_EXPERTS - 1).astype(jnp.int32)
    n_used_rows = pad_ends[-1:].astype(jnp.int32)
    n_used_tiles = n_used_rows // tm
    tc = _tile(S, 256)
    pos = dest.reshape(T // tc, tc, TOP_K).transpose(0, 2, 1).reshape(TK).astype(jnp.int32)

    y_rows = _experts(h, row_tok, tile_e, n_used_tiles, wg, wu, wd, li, tm)
    return _combine(pos, y_rows, x1, info, mod, final_g, B, S, l, final)


def kernel(x, c, w_ada, b_ada, norm1_g, norm2_g, final_g, w_in, w_out, rel_bias, rwkv_mu, rwkv_w0, rwkv_w2, rwkv_a0, rwkv_a2, rwkv_g2, rwkv_k_k, rwkv_k_a, rwkv_r_k, rwkv_ln_w, rwkv_ln_b, rwkv_v0, rwkv_v1, rwkv_v2, ffn_w_gate, ffn_w_up, ffn_w_down, moe_router_w, moe_router_b, moe_w_gate, moe_w_up, moe_w_down):
    B, S, D = x.shape
    L = w_in.shape[0]
    T = B * S
    W = RWKV_WIDTH

    mod = _ada_mod(c, w_ada, b_ada).reshape(L * B, 6, D)
    w_in_b = jnp.pad(w_in, ((0, 0), (0, 0), (0, N_IN_PAD - N_IN))).astype(BF16)
    w_out_b = w_out.astype(BF16)
    ffn_w = [w.astype(BF16) for w in (ffn_w_gate, ffn_w_up, ffn_w_down)]
    moe_w = [w.astype(BF16) for w in (moe_w_gate, moe_w_up, moe_w_down)]
    mu_p = jnp.pad(rwkv_mu, ((0, 0), (0, RWM_COLS + RWL_COLS - rwkv_mu.shape[1])))
    att_tables = _att_tables(rel_bias, S)
    zeros_w = jnp.zeros((1, W), F32)
    final_g2 = final_g.reshape(1, D)

    xf = x.reshape(T, D)
    v_first = None
    for l in range(L):
        qkv, qkv_mid, qkv_far, rwm, rwl = _inproj(xf, mod, norm1_g[l].reshape(1, D), w_in_b,
                                                   mu_p[l].reshape(1, -1), B, S, l)
        att = _attention(qkv, qkv_mid, qkv_far, att_tables, B, S)
        v0 = rwkv_v0[l - 1].reshape(1, W) if l > 0 else zeros_w
        vec = jnp.concatenate([rwkv_w0[l].reshape(1, W), rwkv_a0[l].reshape(1, W), rwkv_k_k[l].reshape(1, W),
                               rwkv_k_a[l].reshape(1, W), rwkv_r_k[l].reshape(1, W), rwkv_ln_w[l].reshape(1, W),
                               rwkv_ln_b[l].reshape(1, W), v0], axis=0)
        w2p = jnp.pad(rwkv_w2[l], ((0, LANES - D_DECAY_LORA), (0, 0)))
        a2p = jnp.pad(rwkv_a2[l], ((D_DECAY_LORA, 0), (0, 0)))
        g2p = jnp.pad(rwkv_g2[l], ((0, RWL_COLS - LANES - D_GATE_LORA), (0, 0)))
        if l > 0:
            v1p = jnp.pad(rwkv_v1[l - 1], ((0, 0), (0, LANES - D_MV_LORA)))
            v2p = jnp.pad(rwkv_v2[l - 1], ((0, LANES - D_MV_LORA), (0, 0)))
        else:
            v1p = v2p = None
        rw, v_first = _rwkv(rwm, rwl, v_first, vec, w2p, a2p, g2p, v1p, v2p, B, S)
        li = l // 2
        if l % 2 == 0:
            x1, h = _outproj(xf, att, rw, mod, norm2_g[l].reshape(1, D), w_out_b, B, S, l)
            xf = _ffn_dense(x1, h, mod, *ffn_w, B, S, l, li)
        else:
            router = (jnp.pad(moe_router_w[li], ((0, 0), (0, LANES - N_EXPERTS))),
                      jnp.pad(moe_router_b[li], (0, LANES - N_EXPERTS)).reshape(1, LANES))
            x1, h, info = _outproj(xf, att, rw, mod, norm2_g[l].reshape(1, D), w_out_b, B, S, l, router)
            xf = _moe(x1, h, info, mod, *moe_w, final_g2, B, S, l, li,
                      final=(l == L - 1))
    if L % 2 == 1:
        raise NotImplementedError("final norm is fused into the last (expert) layer")
    return xf.reshape(B, S, D)
```

```python
import functools
import math

import numpy as np
import jax
import jax.numpy as jnp
from jax import lax
from jax.experimental import pallas as pl
from jax.experimental.pallas import tpu as pltpu

F32 = jnp.float32
BF16 = jnp.bfloat16

D_MODEL = 1024
HEAD_DIM = 64
ATT_WIDTH = 512
RWKV_WIDTH = 512
N_HEADS = 8
HEAD_PAIRS = 4
LANES = 128
DILATED_GROUPS = ((128, 1), (512, 4), (2048, 16))
N_BUCKETS = 32
MAX_DISTANCE = 2048
NEG_INF = -1e30
D_DECAY_LORA = 64
D_AAA_LORA = 64
D_MV_LORA = 32
D_GATE_LORA = 160
N_IN = 3 * ATT_WIDTH + 3 * RWKV_WIDTH + D_DECAY_LORA + D_AAA_LORA + D_GATE_LORA
QKV_COLS = 3 * ATT_WIDTH
RWM_COLS = 3 * RWKV_WIDTH
RWL_COLS = 384
N_IN_PAD = QKV_COLS + RWM_COLS + RWL_COLS
GN_EPS = HEAD_DIM * 1e-5
RMS_EPS = 1e-6
N_EXPERTS = 8
TOP_K = 2
CHUNK = 64
VMEM_LIMIT = 56 * 1024 * 1024
LOG2_E = math.log2(math.e)
Q_SCALE = LOG2_E / math.sqrt(HEAD_DIM)
DMA_ISSUE_UNROLL = 8


def _cparams(sem):
    return pltpu.CompilerParams(dimension_semantics=sem, vmem_limit_bytes=VMEM_LIMIT)


def _tile(n, pref):
    t = min(n, pref)
    while n % t:
        t //= 2
    return t


def _mm(a, b):
    return jnp.dot(a.astype(BF16), b.astype(BF16), preferred_element_type=F32)


def _mm_nt(a, b):
    return lax.dot_general(a.astype(BF16), b.astype(BF16), (((1,), (1,)), ((), ())),
                           preferred_element_type=F32)


def _sigmoid(x):
    return 1.0 / (1.0 + jnp.exp(-x))


def _silu(x):
    return x * _sigmoid(x)


def _ada_kernel(c_ref, w_ref, b_ref, o_ref):
    ca = _silu(c_ref[...])
    o_ref[0] = _mm(ca, w_ref[0]) + b_ref[0]


def _ada_mod(c, w_ada, b_ada):
    L, D, N = w_ada.shape
    B = c.shape[0]
    tn = _tile(N, 1536)
    return pl.pallas_call(
        _ada_kernel,
        grid=(L, N // tn),
        in_specs=[pl.BlockSpec((B, D), lambda l, j: (0, 0)),
                  pl.BlockSpec((1, D, tn), lambda l, j: (l, 0, j)),
                  pl.BlockSpec((1, 1, tn), lambda l, j: (l, 0, j))],
        out_specs=pl.BlockSpec((1, B, tn), lambda l, j: (l, 0, j)),
        out_shape=jax.ShapeDtypeStruct((L, B, N), F32),
        compiler_params=_cparams(("arbitrary", "arbitrary")),
        name="ada_mod",
    )(c, w_ada, b_ada.reshape(L, 1, N))


def _rms_mod(xf, g, scale, shift):
    ms = jnp.mean(xf * xf, axis=-1, keepdims=True)
    return (xf * lax.rsqrt(ms + RMS_EPS) * g) * (1.0 + scale) + shift


def _inproj_kernel(x_ref, mod_ref, g_ref, w_ref, mu_ref, qkv_ref, qkv_mid_ref, qkv_far_ref, rwm_ref, rwl_ref,
                   carry_ref, qkv_scr):
    i = pl.program_id(1)
    tm = x_ref.shape[0]
    h = _rms_mod(x_ref[...], g_ref[...], mod_ref[0, 1:2, :], mod_ref[0, 0:1, :])
    acc = jnp.dot(h.astype(BF16), w_ref[0], preferred_element_type=F32)
    for c in range(QKV_COLS // LANES):
        cols = slice(c * LANES, (c + 1) * LANES)
        blk = acc[:, cols] * Q_SCALE if (c + 1) * LANES <= ATT_WIDTH else acc[:, cols]
        qkv_ref[:, cols] = blk.astype(BF16)
        qkv_scr[c] = blk
        for out_ref in (qkv_mid_ref, qkv_far_ref):
            d, rows = out_ref.shape[0], out_ref.shape[1]
            for r in range(d):
                out_ref[r, :, cols] = qkv_scr[c, pl.ds(r, rows, stride=d), :].astype(BF16)
    p = acc[:, QKV_COLS:]
    first = jnp.where(i == 0, 0.0, carry_ref[...])
    row = lax.broadcasted_iota(jnp.int32, p.shape, 0)
    prev = jnp.where(row == 0, first, pltpu.roll(p, 1, 0))
    carry_ref[...] = p[tm - 1:tm, :]
    pm = p + mu_ref[...] * (prev - p)
    rwm_ref[...] = pm[:, :RWM_COLS]
    rwl_ref[...] = pm[:, RWM_COLS:]


def _inproj(x, mod, g, w_in_l, mu_l, B, S, l):
    T, D = x.shape
    tm = _tile(S, 512)
    nS = S // tm
    d2, d3 = DILATED_GROUPS[1][1], DILATED_GROUPS[2][1]
    return pl.pallas_call(
        _inproj_kernel,
        grid=(B, nS),
        in_specs=[pl.BlockSpec((tm, D), lambda b, i: (b * nS + i, 0)),
                  pl.BlockSpec((1, 6, D), lambda b, i: (l * B + b, 0, 0)),
                  pl.BlockSpec((1, D), lambda b, i: (0, 0)),
                  pl.BlockSpec((1, D, N_IN_PAD), lambda b, i: (l, 0, 0)),
                  pl.BlockSpec((1, RWM_COLS + RWL_COLS), lambda b, i: (0, 0))],
        out_specs=[pl.BlockSpec((tm, QKV_COLS), lambda b, i: (b * nS + i, 0)),
                   pl.BlockSpec((None, d2, tm // d2, QKV_COLS), lambda b, i: (b, 0, i, 0)),
                   pl.BlockSpec((None, d3, tm // d3, QKV_COLS), lambda b, i: (b, 0, i, 0)),
                   pl.BlockSpec((tm, RWM_COLS), lambda b, i: (b * nS + i, 0)),
                   pl.BlockSpec((tm, RWL_COLS), lambda b, i: (b * nS + i, 0))],
        out_shape=[jax.ShapeDtypeStruct((T, QKV_COLS), BF16),
                   jax.ShapeDtypeStruct((B, d2, S // d2, QKV_COLS), BF16),
                   jax.ShapeDtypeStruct((B, d3, S // d3, QKV_COLS), BF16),
                   jax.ShapeDtypeStruct((T, RWM_COLS), F32),
                   jax.ShapeDtypeStruct((T, RWL_COLS), F32)],
        scratch_shapes=[pltpu.VMEM((1, RWM_COLS + RWL_COLS), F32), pltpu.VMEM((QKV_COLS // LANES, tm, LANES), F32)],
        compiler_params=_cparams(("arbitrary", "arbitrary")),
        name="inproj",
    )(x, mod, g, w_in_l, mu_l)


def _t5_bucket(n):
    max_exact = N_BUCKETS // 2
    large = max_exact + (np.log(np.maximum(n, 1) / max_exact) / np.log(MAX_DISTANCE / max_exact)
                         * (N_BUCKETS - max_exact)).astype(np.int32)
    large = np.minimum(large, N_BUCKETS - 1)
    return np.where(n < max_exact, n, large).astype(np.int32)


ATT_BLOCK = 128
ATT_KEYS = 256


def _toeplitz(vec, c0, rows, cols):
    H, n = vec.shape
    period = rows + cols - 1
    w = jnp.concatenate([jnp.full((H, cols - 1), NEG_INF, F32), vec, jnp.full((H, rows), NEG_INF, F32)], axis=1)
    rw = w[:, ::-1]
    a = w.shape[1] - 1 - (c0 + cols - 1)
    z = jnp.concatenate([rw[:, a:a + cols], rw[:, a - (rows - 1):a]], axis=1)
    flat = jnp.tile(z, (1, rows))[:, :rows * (period - 1)]
    return flat.reshape(H, rows, period - 1)[..., :cols]


def _att_tables(rel_bias, S):
    (w1, d1), (w2, d2), (w3, d3) = DILATED_GROUPS
    assert d1 == 1 and w1 == ATT_KEYS - ATT_BLOCK and w2 // d2 == w1 and S % (d3 * 8) == 0
    off = np.arange(S)
    mult = np.zeros(S, np.int64)
    for w, d in DILATED_GROUPS:
        mult += ((off % d == 0) & (off <= w)).astype(np.int64)
    logm = np.where(mult > 0, np.log(np.maximum(mult, 1)), NEG_INF).astype(np.float32)
    per_off = (rel_bias[_t5_bucket(off)].T.astype(F32) + jnp.asarray(logm)[None, :]) * LOG2_E
    u2 = np.arange(S // d2)
    u3 = np.arange(S // d3)
    band = jnp.where(jnp.asarray(off <= w1)[None], per_off, NEG_INF)
    mid = jnp.where(jnp.asarray((u2 > w1 // d2) & (u2 <= w2 // d2))[None], per_off[:, ::d2], NEG_INF)
    far = jnp.where(jnp.asarray((u3 > w2 // d3) & (u3 <= w3 // d3))[None], per_off[:, ::d3], NEG_INF)
    shift = ATT_KEYS - ATT_BLOCK
    t_band = jnp.stack([_toeplitz(band, 0, ATT_BLOCK, ATT_KEYS), _toeplitz(band, shift, ATT_BLOCK, ATT_KEYS)], axis=1)
    t_mid = jnp.stack([_toeplitz(mid, 0, ATT_BLOCK, ATT_KEYS), _toeplitz(mid, shift, ATT_BLOCK, ATT_KEYS)], axis=1)
    t_far = _toeplitz(far, 0, S // d3, S // d3)
    return t_band, t_mid, t_far


def _attend(units):
    idx = range(len(units))
    heads = range(2)
    head0 = [lax.broadcasted_iota(jnp.int32, u[0].shape, 1) < HEAD_DIM for u in units]
    qh = [[jnp.where(head0[i], units[i][0], jnp.zeros_like(units[i][0])),
           jnp.where(head0[i], jnp.zeros_like(units[i][0]), units[i][0])] for i in idx]
    s = [[lax.dot_general(qh[i][h], units[i][1], (((1,), (1,)), ((), ())), preferred_element_type=F32)
          + units[i][3 + h] for h in heads] for i in idx]
    mx = [[jnp.max(s[i][h], axis=-1, keepdims=True) for h in heads] for i in idx]
    p = [[jnp.exp2(s[i][h] - mx[i][h]) for h in heads] for i in idx]
    ps = [[jnp.sum(p[i][h], axis=-1, keepdims=True) for h in heads] for i in idx]
    pv = [[jnp.dot(p[i][h].astype(BF16), units[i][2], preferred_element_type=F32) for h in heads] for i in idx]
    out = []
    for i in idx:
        shape = units[i][0].shape
        out.append((jnp.where(head0[i], jnp.broadcast_to(mx[i][0], shape), jnp.broadcast_to(mx[i][1], shape)),
                    jnp.where(head0[i], jnp.broadcast_to(ps[i][0], shape), jnp.broadcast_to(ps[i][1], shape)),
                    jnp.where(head0[i], pv[i][0], pv[i][1])))
    return out


def _attn_kernel(qa_ref, ka_ref, va_ref, qm_ref, km_ref, vm_ref, qf_ref, kf_ref, vf_ref,
                 tband_ref, tmid_ref, tfar_ref, o_ref, part_ref):
    R, K = ATT_BLOCK, ATT_KEYS
    S = qa_ref.shape[0]
    n_mid_res, n_mid = qm_ref.shape[0], qm_ref.shape[1]
    n_far_res, n_far = qf_ref.shape[0], qf_ref.shape[1]
    group = 4

    def window(blk):
        start = pl.multiple_of(jnp.maximum(blk * R - (K - R), 0), R)
        return pl.ds(pl.multiple_of(blk * R, R), R), pl.ds(start, K), jnp.minimum(blk, 1)

    def park(cls, rows, results):
        for nat, parts in zip(rows, results):
            for j in range(3):
                part_ref[cls, j, nat, :] = parts[j]

    def band_body(it, _):
        units, rows = [], []
        for g in range(group):
            qrows, krows, var = window(it * group + g)
            rows.append(qrows)
            units.append((qa_ref[qrows, :], ka_ref[krows, :], va_ref[krows, :],
                          tband_ref[0, var], tband_ref[1, var]))
        park(0, rows, _attend(units))
        return 0

    lax.fori_loop(0, S // (R * group), band_body, 0)

    def mid_body(blk, _):
        qrows, krows, var = window(blk)
        units, rows = [], []
        for r in range(n_mid_res):
            rows.append(pl.ds(blk * (R * n_mid_res) + r, R, stride=n_mid_res))
            units.append((qm_ref[r, qrows, :], km_ref[r, krows, :], vm_ref[r, krows, :],
                          tmid_ref[0, var], tmid_ref[1, var]))
        park(1, rows, _attend(units))
        return 0

    lax.fori_loop(0, n_mid // R, mid_body, 0)

    def far_body(it, _):
        units, rows = [], []
        for g in range(group):
            r = it * group + g
            rows.append(pl.ds(r, n_far, stride=n_far_res))
            units.append((qf_ref[r], kf_ref[r], vf_ref[r], tfar_ref[0], tfar_ref[1]))
        park(2, rows, _attend(units))
        return 0

    lax.fori_loop(0, n_far_res // group, far_body, 0)

    def merge_body(i, _):
        rows = pl.ds(pl.multiple_of(i * R, R), R)
        m = [part_ref[c, 0, rows, :] for c in range(3)]
        top = jnp.maximum(jnp.maximum(m[0], m[1]), m[2])
        w = [jnp.exp2(m[c] - top) for c in range(3)]
        total = sum(w[c] * part_ref[c, 1, rows, :] for c in range(3))
        acc = sum(w[c] * part_ref[c, 2, rows, :] for c in range(3))
        o_ref[rows, :] = (acc / total).astype(o_ref.dtype)
        return 0

    lax.fori_loop(0, S // R, merge_body, 0)


def _attention(qkv, qkv_mid, qkv_far, tables, B, S):
    T = qkv.shape[0]
    t_band, t_mid, t_far = tables
    d2, d3 = DILATED_GROUPS[1][1], DILATED_GROUPS[2][1]
    assert S % (ATT_BLOCK * 4) == 0 and S // d2 >= ATT_KEYS
    nat = lambda c: pl.BlockSpec((S, LANES), lambda hp, b: (b, c * HEAD_PAIRS + hp))
    mid = lambda c: pl.BlockSpec((None, d2, S // d2, LANES), lambda hp, b: (b, 0, 0, c * HEAD_PAIRS + hp))
    far = lambda c: pl.BlockSpec((None, d3, S // d3, LANES), lambda hp, b: (b, 0, 0, c * HEAD_PAIRS + hp))
    return pl.pallas_call(
        _attn_kernel,
        grid=(HEAD_PAIRS, B),
        in_specs=[nat(0), nat(1), nat(2), mid(0), mid(1), mid(2), far(0), far(1), far(2),
                  pl.BlockSpec((2,) + t_band.shape[1:], lambda hp, b: (hp, 0, 0, 0)),
                  pl.BlockSpec((2,) + t_mid.shape[1:], lambda hp, b: (hp, 0, 0, 0)),
                  pl.BlockSpec((2,) + t_far.shape[1:], lambda hp, b: (hp, 0, 0))],
        out_specs=pl.BlockSpec((S, LANES), lambda hp, b: (b, hp)),
        out_shape=jax.ShapeDtypeStruct((T, ATT_WIDTH), BF16),
        scratch_shapes=[pltpu.VMEM((3, 3, S, LANES), F32)],
        compiler_params=_cparams(("arbitrary", "arbitrary")),
        name="dilated_attn",
    )(qkv, qkv, qkv, qkv_mid, qkv_mid, qkv_mid, qkv_far, qkv_far, qkv_far, t_band, t_mid, t_far)


def _split2(x):
    hi = x.astype(BF16)
    lo = (x - hi.astype(F32)).astype(BF16)
    return hi, lo


def _head_sum(x, m2):
    hi, lo = _split2(x)
    return jnp.dot(jnp.concatenate([hi, lo], axis=1), m2, preferred_element_type=F32)


def _rwkv_kernel(*refs, has_vres, n_chunks):
    if has_vres:
        (rwm_ref, rwl_ref, vf_ref, vec_ref, w2_ref, a2_ref, g2_ref, v1_ref, v2_ref,
         o_ref, state_ref, r_s, k_s, v_s, lw_s, kk_s, kb_s, y_s,
         ar_s, inv_s, bk_s, nrbk_s, akv_s, dec_s) = refs
    else:
        (rwm_ref, rwl_ref, vec_ref, w2_ref, a2_ref, g2_ref,
         o_ref, vf_out_ref, state_ref, r_s, k_s, v_s, lw_s, kk_s, kb_s, y_s,
         ar_s, inv_s, bk_s, nrbk_s, akv_s, dec_s) = refs
    C = CHUNK
    W = RWKV_WIDTH

    @pl.when(pl.program_id(1) == 0)
    def _():
        state_ref[...] = jnp.zeros_like(state_ref)

    ri = lax.broadcasted_iota(jnp.int32, (2 * LANES, LANES), 0)
    ci = lax.broadcasted_iota(jnp.int32, (2 * LANES, LANES), 1)
    m2 = ((ri % LANES) // HEAD_DIM == ci // HEAD_DIM).astype(BF16)

    vec = vec_ref[...]
    w0, a0, k_k, k_a, r_k, ln_w, ln_b, v0 = [vec[i:i + 1, :] for i in range(8)]

    r = rwm_ref[:, 0:W]
    k = rwm_ref[:, W:2 * W]
    v = rwm_ref[:, 2 * W:3 * W]
    lora = rwl_ref[:, 0:LANES]
    w_raw = w0 + _mm(jnp.tanh(lora), w2_ref[...])
    lw_s[...] = -math.exp(-0.5) * _sigmoid(w_raw)
    a = _sigmoid(a0 + _mm(lora, a2_ref[...]))
    g = _mm(_sigmoid(rwl_ref[:, LANES:RWL_COLS]), g2_ref[...])
    if has_vres:
        mix = _sigmoid(v0 + _mm(_mm(v, v1_ref[...]), v2_ref[...]))
        v = v + (vf_ref[...] - v) * mix
    else:
        vf_out_ref[...] = v
    kk = k * k_k
    for p in range(HEAD_PAIRS):
        ls = slice(p * LANES, (p + 1) * LANES)
        kkp = kk[:, ls]
        kkp = kkp * jnp.minimum(lax.rsqrt(_head_sum(kkp * kkp, m2)), 1e12)
        kk_s[:, ls] = kkp
        kb_s[:, ls] = kkp * a[:, ls]
    kmod = k * (1.0 + (a - 1.0) * k_a)
    r_s[...] = r
    k_s[...] = kmod
    v_s[...] = v

    ri = lax.broadcasted_iota(jnp.int32, (LANES, LANES), 0)
    ci = lax.broadcasted_iota(jnp.int32, (LANES, LANES), 1)
    same_head = (ri // C) == (ci // C)
    strict = same_head & ((ri % C) > (ci % C))
    incl = same_head & ((ri % C) >= (ci % C))
    eye = (ri == ci).astype(F32)
    lane = lax.broadcasted_iota(jnp.int32, (C, LANES), 1)
    head0 = lane < HEAD_DIM
    tri_r = lax.broadcasted_iota(jnp.int32, (C, 3 * C), 0)
    tri_c = lax.broadcasted_iota(jnp.int32, (C, 3 * C), 1)
    tri3 = ((tri_c % C) <= tri_r).astype(BF16)

    lane_slices = [slice(p * LANES, (p + 1) * LANES) for p in range(HEAD_PAIRS)]

    def cumsum(x):
        hi = x.astype(BF16)
        rem = x - hi.astype(F32)
        mid = rem.astype(BF16)
        lo = (rem - mid.astype(F32)).astype(BF16)
        return jnp.dot(tri3, jnp.concatenate([hi, mid, lo], axis=0), preferred_element_type=F32)

    def per_head_rows(x, y):
        zero = jnp.zeros_like(x)
        return jnp.concatenate([jnp.where(head0, x, zero), jnp.where(head0, zero, x),
                                jnp.where(head0, y, zero), jnp.where(head0, zero, y)], axis=0)

    def intra_body(it, _):
        units = [(it * chunk_group + g, p) for g in range(chunk_group) for p in range(HEAD_PAIRS)]
        idx = range(len(units))
        rows = [pl.ds(pl.multiple_of(c * C, C), C) for c, _ in units]
        ls = [lane_slices[p] for _, p in units]
        rc = [r_s[rows[u], ls[u]] for u in idx]
        kc = [k_s[rows[u], ls[u]] for u in idx]
        vc = [v_s[rows[u], ls[u]] for u in idx]
        lw = [lw_s[rows[u], ls[u]] for u in idx]
        kkc = [kk_s[rows[u], ls[u]] for u in idx]
        kbc = [kb_s[rows[u], ls[u]] for u in idx]
        cum = [cumsum(lw[u]) for u in idx]
        total = [cum[u][C - 1:C, :] for u in idx]
        g_inv = [jnp.exp(-cum[u]) for u in idx]
        a_t = [-kkc[u] * jnp.exp(cum[u] - lw[u]) for u in idx]
        r_t = [rc[u] * jnp.exp(cum[u]) for u in idx]
        b_t = [kbc[u] * g_inv[u] for u in idx]
        k_t = [kc[u] * g_inv[u] for u in idx]
        to_end = [jnp.exp(total[u] - cum[u]) for u in idx]
        gram = [_mm_nt(per_head_rows(a_t[u], r_t[u]), per_head_rows(b_t[u], k_t[u])) for u in idx]
        n_ab = [jnp.where(strict, gram[u][:LANES, :LANES], 0.0) for u in idx]
        n_ak = [jnp.where(strict, gram[u][:LANES, LANES:], 0.0) for u in idx]
        n_rb = [jnp.where(incl, gram[u][LANES:, :LANES], 0.0) for u in idx]
        n_rk = [jnp.where(incl, gram[u][LANES:, LANES:], 0.0) for u in idx]
        inv = [eye + n_ab[u] for u in idx]
        pw = [_mm(n_ab[u], n_ab[u]) for u in idx]
        n_rounds = int(math.log2(C)) - 1
        for i in range(n_rounds):
            if i + 1 < n_rounds:
                both = [_mm(pw[u], jnp.concatenate([inv[u], pw[u]], axis=1)) for u in idx]
                inv = [inv[u] + both[u][:, :LANES] for u in idx]
                pw = [both[u][:, LANES:] for u in idx]
            else:
                inv = [inv[u] + _mm(pw[u], inv[u]) for u in idx]
        for u, (c, p) in enumerate(units):
            ar_s[c, p] = jnp.concatenate([a_t[u], r_t[u]], axis=0).astype(BF16)
            inv_s[c, p] = inv[u].astype(BF16)
            akv_s[c, p] = _mm(n_ak[u], jnp.concatenate([vc[u], vc[u]], axis=0))
            nrbk_s[c, p] = jnp.concatenate([n_rb[u], n_rk[u]], axis=1).astype(BF16)
            bk_s[c, p] = jnp.concatenate([kbc[u] * to_end[u], kc[u] * to_end[u]], axis=0).astype(BF16)
            dec_s[c, p] = jnp.broadcast_to(jnp.exp(total[u]), (8, LANES))
        return 0

    def state_body(c, _):
        rows = pl.ds(pl.multiple_of(c * C, C), C)
        pairs = range(HEAD_PAIRS)
        vc = [v_s[rows, lane_slices[p]] for p in pairs]
        st = [state_ref[p] for p in pairs]
        ah = [_mm_nt(ar_s[c, p], st[p]) for p in pairs]
        u_stack = [_mm(inv_s[c, p], jnp.concatenate([ah[p][:C], ah[p][:C]], axis=0) + akv_s[c, p])
                   for p in pairs]
        y_stack = [jnp.concatenate([ah[p][C:], ah[p][C:]], axis=0)
                   + _mm(nrbk_s[c, p], jnp.concatenate([u_stack[p], vc[p], vc[p]], axis=0)) for p in pairs]
        for p in pairs:
            y_s[rows, lane_slices[p]] = jnp.where(head0, y_stack[p][:C], y_stack[p][C:])
        uv_t = [jnp.concatenate([jnp.where(head0, u_stack[p][:C], u_stack[p][C:]), vc[p]], axis=0).T
                for p in pairs]
        st_new = [st[p] * dec_s[c, p][0:1, :] + _mm(uv_t[p], bk_s[c, p]) for p in pairs]
        for p in pairs:
            state_ref[p] = jnp.where(same_head, st_new[p], 0.0)
        return 0

    chunk_group = 2 if n_chunks % 2 == 0 else 1
    lax.fori_loop(0, n_chunks // chunk_group, intra_body, 0)
    lax.fori_loop(0, n_chunks, state_body, 0)

    for p in range(HEAD_PAIRS):
        ls = slice(p * LANES, (p + 1) * LANES)
        y = y_s[:, ls]
        mean = _head_sum(y, m2) * (1.0 / HEAD_DIM)
        d = y - mean
        var = _head_sum(d * d, m2) * (1.0 / HEAD_DIM)
        yn = d * lax.rsqrt(var + GN_EPS) * ln_w[:, ls] + ln_b[:, ls]
        bonus = _head_sum(r_s[:, ls] * k_s[:, ls] * r_k[:, ls], m2) * v_s[:, ls]
        o_ref[:, ls] = ((yn + bonus) * g[:, ls]).astype(o_ref.dtype)


def _rwkv(rwm, rwl, v_first, vec, w2p, a2p, g2p, v1p, v2p, B, S):
    T = rwm.shape[0]
    W = RWKV_WIDTH
    tb = _tile(S, 512)
    nS = S // tb
    n_chunks = tb // CHUNK
    has_vres = v_first is not None
    row = lambda b, i: (b * nS + i, 0)
    full = lambda b, i: (0, 0)
    in_specs = [pl.BlockSpec((tb, RWM_COLS), row), pl.BlockSpec((tb, RWL_COLS), row)]
    args = [rwm, rwl]
    if has_vres:
        in_specs.append(pl.BlockSpec((tb, W), row))
        args.append(v_first)
    in_specs += [pl.BlockSpec(vec.shape, full), pl.BlockSpec(w2p.shape, full),
                 pl.BlockSpec(a2p.shape, full), pl.BlockSpec(g2p.shape, full)]
    args += [vec, w2p, a2p, g2p]
    if has_vres:
        in_specs += [pl.BlockSpec(v1p.shape, full), pl.BlockSpec(v2p.shape, full)]
        args += [v1p, v2p]
    out_specs = [pl.BlockSpec((tb, W), row)]
    out_shape = [jax.ShapeDtypeStruct((T, W), BF16)]
    if not has_vres:
        out_specs.append(pl.BlockSpec((tb, W), row))
        out_shape.append(jax.ShapeDtypeStruct((T, W), F32))
    outs = pl.pallas_call(
        functools.partial(_rwkv_kernel, has_vres=has_vres, n_chunks=n_chunks),
        grid=(B, nS),
        in_specs=in_specs,
        out_specs=out_specs,
        out_shape=out_shape,
        scratch_shapes=([pltpu.VMEM((HEAD_PAIRS, LANES, LANES), F32)] + [pltpu.VMEM((tb, W), F32)] * 7
                        + [pltpu.VMEM((n_chunks, HEAD_PAIRS, 2 * CHUNK, LANES), BF16)] * 3
                        + [pltpu.VMEM((n_chunks, HEAD_PAIRS, 2 * CHUNK, 2 * LANES), BF16),
                           pltpu.VMEM((n_chunks, HEAD_PAIRS, 2 * CHUNK, LANES), F32),
                           pltpu.VMEM((n_chunks, HEAD_PAIRS, 8, LANES), F32)]),
        compiler_params=_cparams(("arbitrary", "arbitrary")),
        name="rwkv7",
    )(*args)
    return (outs[0], v_first) if has_vres else (outs[0], outs[1])


TOKEN_TILE = (8, LANES)


def _store_token_tiles(ref, x):
    n = x.shape[0]
    for s in range(TOKEN_TILE[0]):
        ref[pl.ds(s, n, stride=TOKEN_TILE[0]), :] = x[:, s * LANES:(s + 1) * LANES]


def _load_token_tiles(ref, n, start=0):
    return jnp.concatenate([ref[pl.ds(start + s, n, stride=TOKEN_TILE[0]), :] for s in range(TOKEN_TILE[0])],
                           axis=1)


def _outproj_kernel(*refs, moe):
    if moe:
        (x_ref, att_ref, rw_ref, mod_ref, g_ref, w_ref, rw_w_ref, rw_b_ref,
         x1_ref, h_ref, info_ref) = refs
    else:
        x_ref, att_ref, rw_ref, mod_ref, g_ref, w_ref, x1_ref, h_ref = refs
    mix = (jnp.dot(att_ref[...], w_ref[0, :ATT_WIDTH, :], preferred_element_type=F32)
           + jnp.dot(rw_ref[...], w_ref[0, ATT_WIDTH:, :], preferred_element_type=F32))
    x1 = x_ref[...] + mod_ref[0, 2:3, :] * mix
    x1_ref[...] = x1
    h = _rms_mod(x1, g_ref[...], mod_ref[0, 4:5, :], mod_ref[0, 3:4, :])
    if moe:
        _store_token_tiles(h_ref, h)
    else:
        h_ref[...] = h.astype(h_ref.dtype)
    if moe:
        h_hi, h_lo = _split2(h)
        w_hi, w_lo = _split2(rw_w_ref[...])
        logits = (jnp.dot(h_hi, w_hi, preferred_element_type=F32)
                  + jnp.dot(h_hi, w_lo, preferred_element_type=F32)
                  + jnp.dot(h_lo, w_hi, preferred_element_type=F32)) + rw_b_ref[...]
        lane = lax.broadcasted_iota(jnp.int32, logits.shape, 1)
        logits = jnp.where(lane < N_EXPERTS, logits, -jnp.inf)
        m1 = jnp.max(logits, axis=-1, keepdims=True)
        i1 = jnp.min(jnp.where(logits == m1, lane, LANES), axis=-1, keepdims=True)
        rest = jnp.where(lane == i1, -jnp.inf, logits)
        m2 = jnp.max(rest, axis=-1, keepdims=True)
        i2 = jnp.min(jnp.where(rest == m2, lane, LANES), axis=-1, keepdims=True)
        e = jnp.exp(m2 - m1)
        g1 = 1.0 / (1.0 + e)
        g2 = e / (1.0 + e)
        info = jnp.where(lane == 0, i1.astype(F32),
                         jnp.where(lane == 1, i2.astype(F32),
                                   jnp.where(lane == 2, g1, jnp.where(lane == 3, g2, 0.0))))
        info_ref[...] = info


def _outproj(x, att, rw, mod, g, w_out_b, B, S, l, router=None):
    T, D = x.shape
    tm = _tile(S, 512)
    nS = S // tm
    moe = router is not None
    row = lambda b, i: (b * nS + i, 0)
    full = lambda b, i: (0, 0)
    in_specs = [pl.BlockSpec((tm, D), row), pl.BlockSpec((tm, ATT_WIDTH), row),
                pl.BlockSpec((tm, RWKV_WIDTH), row),
                pl.BlockSpec((1, 6, D), lambda b, i: (l * B + b, 0, 0)),
                pl.BlockSpec((1, D), full),
                pl.BlockSpec((1, D, D), lambda b, i: (l, 0, 0))]
    args = [x, att, rw, mod, g, w_out_b]
    if moe:
        out_specs = [pl.BlockSpec((tm, D), row), pl.BlockSpec((tm * TOKEN_TILE[0], LANES), row)]
        out_shape = [jax.ShapeDtypeStruct((T, D), F32), jax.ShapeDtypeStruct((T * TOKEN_TILE[0], LANES), F32)]
    else:
        out_specs = [pl.BlockSpec((tm, D), row), pl.BlockSpec((tm, D), row)]
        out_shape = [jax.ShapeDtypeStruct((T, D), F32), jax.ShapeDtypeStruct((T, D), BF16)]
    if moe:
        in_specs += [pl.BlockSpec((D, LANES), full), pl.BlockSpec((1, LANES), full)]
        args += list(router)
        out_specs.append(pl.BlockSpec((tm, LANES), row))
        out_shape.append(jax.ShapeDtypeStruct((T, LANES), F32))
    return pl.pallas_call(
        functools.partial(_outproj_kernel, moe=moe),
        grid=(B, nS),
        in_specs=in_specs,
        out_specs=out_specs,
        out_shape=out_shape,
        compiler_params=_cparams(("arbitrary", "arbitrary")),
        name="outproj",
    )(*args)


def _ffn_kernel(x_ref, h_ref, mod_ref, wg_ref, wu_ref, wd_ref, o_ref, acc_ref):
    f = pl.program_id(1)

    @pl.when(f == 0)
    def _():
        acc_ref[...] = jnp.zeros_like(acc_ref)

    hb = h_ref[...]
    gate = jnp.dot(hb, wg_ref[0], preferred_element_type=F32)
    up = jnp.dot(hb, wu_ref[0], preferred_element_type=F32)
    mid = (_silu(gate) * up).astype(BF16)
    acc_ref[...] += jnp.dot(mid, wd_ref[0], preferred_element_type=F32)

    @pl.when(f == pl.num_programs(1) - 1)
    def _():
        o_ref[...] = x_ref[...] + mod_ref[0, 5:6, :] * acc_ref[...]


def _ffn_dense(x1, h, mod, wg, wu, wd, B, S, l, li):
    T, D = x1.shape
    F = wg.shape[-1]
    tm = _tile(S, 512)
    tf = _tile(F, 1408)
    per_seq = S // tm
    return pl.pallas_call(
        _ffn_kernel,
        grid=(T // tm, F // tf),
        in_specs=[pl.BlockSpec((tm, D), lambda i, f: (i, 0)),
                  pl.BlockSpec((tm, D), lambda i, f: (i, 0)),
                  pl.BlockSpec((1, 6, D), lambda i, f: (l * B + i // per_seq, 0, 0)),
                  pl.BlockSpec((1, D, tf), lambda i, f: (li, 0, f)),
                  pl.BlockSpec((1, D, tf), lambda i, f: (li, 0, f)),
                  pl.BlockSpec((1, tf, D), lambda i, f: (li, f, 0))],
        out_specs=pl.BlockSpec((tm, D), lambda i, f: (i, 0)),
        out_shape=jax.ShapeDtypeStruct((T, D), F32),
        scratch_shapes=[pltpu.VMEM((tm, D), F32)],
        compiler_params=_cparams(("arbitrary", "arbitrary")),
        name="ffn_dense",
    )(x1, h, mod, wg, wu, wd)


def _experts_kernel(te_ref, nu_ref, tok_ref, h_ref, wg_ref, wu_ref, wd_ref, y_ref,
                    land_ref, xb_ref, acc_ref, sem):
    i = pl.program_id(0)
    f = pl.program_id(1)
    tm = xb_ref.shape[0]
    rows_per_token = TOKEN_TILE[0]
    n_used = nu_ref[0]

    def start_gather(tile):
        base = tile * tm

        def body(j, _):
            src = pl.multiple_of(tok_ref[base + j] * rows_per_token, rows_per_token)
            dst = pl.multiple_of(j * rows_per_token, rows_per_token)
            pltpu.make_async_copy(h_ref.at[pl.ds(src, rows_per_token)],
                                  land_ref.at[pl.ds(dst, rows_per_token)], sem).start()
            return 0

        lax.fori_loop(0, tm, body, 0, unroll=DMA_ISSUE_UNROLL)

    @pl.when(i < n_used)
    def _():
        @pl.when(f == 0)
        def _():
            @pl.when(i == 0)
            def _():
                start_gather(0)

            pltpu.make_async_copy(h_ref.at[pl.ds(0, tm * rows_per_token)], land_ref, sem).wait()
            xb_ref[...] = _load_token_tiles(land_ref, tm).astype(BF16)
            acc_ref[...] = jnp.zeros_like(acc_ref)

            @pl.when(i + 1 < n_used)
            def _():
                start_gather(i + 1)

        xb = xb_ref[...]
        gate = jnp.dot(xb, wg_ref[0, 0], preferred_element_type=F32)
        up = jnp.dot(xb, wu_ref[0, 0], preferred_element_type=F32)
        mid = (_silu(gate) * up).astype(BF16)
        acc_ref[...] += jnp.dot(mid, wd_ref[0, 0], preferred_element_type=F32)

        @pl.when(f == pl.num_programs(1) - 1)
        def _():
            _store_token_tiles(y_ref, acc_ref[...])

    @pl.when((i >= n_used) & (f == 0))
    def _():
        y_ref[...] = jnp.zeros_like(y_ref)


def _experts(h, row_tok, tile_e, n_used, wg, wu, wd, li, tm):
    n_rows = row_tok.shape[0]
    D = wg.shape[-2]
    rpt = TOKEN_TILE[0]
    assert h.shape[0] >= tm * rpt
    F = wg.shape[-1]
    tf = _tile(F, 1792)
    nf = F // tf

    def rows(i, f, te, nu, tok):
        return (i, 0)

    def wcol(i, f, te, nu, tok):
        return (li, te[jnp.minimum(i, nu[0] - 1)], 0, jnp.where(i < nu[0], f, nf - 1))

    def wrow(i, f, te, nu, tok):
        return (li, te[jnp.minimum(i, nu[0] - 1)], jnp.where(i < nu[0], f, nf - 1), 0)

    return pl.pallas_call(
        _experts_kernel,
        grid_spec=pltpu.PrefetchScalarGridSpec(
            num_scalar_prefetch=3,
            grid=(n_rows // tm, nf),
            in_specs=[pl.BlockSpec(memory_space=pl.ANY),
                      pl.BlockSpec((1, 1, D, tf), wcol),
                      pl.BlockSpec((1, 1, D, tf), wcol),
                      pl.BlockSpec((1, 1, tf, D), wrow)],
            out_specs=pl.BlockSpec((tm * rpt, LANES), rows),
            scratch_shapes=[pltpu.VMEM((tm * rpt, LANES), F32), pltpu.VMEM((tm, D), BF16),
                            pltpu.VMEM((tm, D), F32), pltpu.SemaphoreType.DMA(())]),
        out_shape=jax.ShapeDtypeStruct((n_rows * rpt, LANES), F32),
        compiler_params=_cparams(("arbitrary", "arbitrary")),
        name="moe_experts",
    )(tile_e, n_used, row_tok, h, wg, wu, wd)


def _combine_kernel(pos_ref, y_ref, x_ref, info_ref, mod_ref, fg_ref, o_ref, buf, sem, *, tc, final):
    i = pl.program_id(0)
    slot = i % 2
    rows_per_token = TOKEN_TILE[0]

    def start_gather(tile, dst_slot):
        base = tile * (2 * tc)

        def body(j, _):
            src = pl.multiple_of(pos_ref[base + j] * rows_per_token, rows_per_token)
            dst = pl.multiple_of(j * rows_per_token, rows_per_token)
            pltpu.make_async_copy(y_ref.at[pl.ds(src, rows_per_token)],
                                  buf.at[dst_slot, pl.ds(dst, rows_per_token)], sem.at[dst_slot]).start()
            return 0

        lax.fori_loop(0, 2 * tc, body, 0, unroll=DMA_ISSUE_UNROLL)

    @pl.when(i == 0)
    def _():
        start_gather(0, 0)

    @pl.when(i + 1 < pl.num_programs(0))
    def _():
        start_gather(i + 1, 1 - slot)

    pltpu.make_async_copy(y_ref.at[pl.ds(0, 2 * tc * rows_per_token)], buf.at[slot], sem.at[slot]).wait()
    info = info_ref[...]
    ff = (info[:, 2:3] * _load_token_tiles(buf.at[slot], tc)
          + info[:, 3:4] * _load_token_tiles(buf.at[slot], tc, tc * rows_per_token))
    out = x_ref[...] + mod_ref[0, 5:6, :] * ff
    if final:
        ms = jnp.mean(out * out, axis=-1, keepdims=True)
        out = out * lax.rsqrt(ms + RMS_EPS) * fg_ref[...]
    o_ref[...] = out


def _combine(pos, y_rows, x1, info, mod, final_g, B, S, l, final):
    T, D = x1.shape
    tc = _tile(S, 256)
    per_seq = S // tc
    return pl.pallas_call(
        functools.partial(_combine_kernel, tc=tc, final=final),
        grid_spec=pltpu.PrefetchScalarGridSpec(
            num_scalar_prefetch=1,
            grid=(T // tc,),
            in_specs=[pl.BlockSpec(memory_space=pl.ANY),
                      pl.BlockSpec((tc, D), lambda i, pos: (i, 0)),
                      pl.BlockSpec((tc, LANES), lambda i, pos: (i, 0)),
                      pl.BlockSpec((1, 6, D), lambda i, pos: (l * B + i // per_seq, 0, 0)),
                      pl.BlockSpec((1, D), lambda i, pos: (0, 0))],
            out_specs=pl.BlockSpec((tc, D), lambda i, pos: (i, 0)),
            scratch_shapes=[pltpu.VMEM((2, 2 * tc * TOKEN_TILE[0], LANES), F32), pltpu.SemaphoreType.DMA((2,))]),
        out_shape=jax.ShapeDtypeStruct((T, D), F32),
        compiler_params=_cparams(("arbitrary",)),
        name="moe_combine",
    )(pos, y_rows, x1, info, mod, final_g)


def _moe(x1, h, info, mod, wg, wu, wd, final_g, B, S, l, li, final):
    T, D = x1.shape
    TK = T * TOP_K
    tm = _tile(TK // N_EXPERTS, 512)
    n_rows = TK + (N_EXPERTS - 1) * tm
    flat_e = info[:, :TOP_K].astype(jnp.int32).reshape(TK)
    onehot = (flat_e[:, None] == jnp.arange(N_EXPERTS, dtype=jnp.int32)[None, :]).astype(jnp.int32)
    csum = jnp.cumsum(onehot, axis=0)
    counts = csum[-1]
    rank = jnp.sum(onehot * csum, axis=1) - 1
    padded = (counts + tm - 1) // tm * tm
    pad_ends = jnp.cumsum(padded)
    pad_starts = pad_ends - padded
    dest = pad_starts[flat_e] + rank
    row_tok = jnp.zeros((n_rows,), jnp.int32).at[dest].set(jnp.arange(TK, dtype=jnp.int32) // TOP_K)
    tile_start = jnp.arange(n_rows // tm, dtype=jnp.int32) * tm
    tile_e = jnp.minimum(jnp.searchsorted(pad_ends, tile_start, side='right'), N_EXPERTS - 1).astype(jnp.int32)
    n_used_rows = pad_ends[-1:].astype(jnp.int32)
    n_used_tiles = n_used_rows // tm
    tc = _tile(S, 256)
    pos = dest.reshape(T // tc, tc, TOP_K).transpose(0, 2, 1).reshape(TK).astype(jnp.int32)

    y_rows = _experts(h, row_tok, tile_e, n_used_tiles, wg, wu, wd, li, tm)
    return _combine(pos, y_rows, x1, info, mod, final_g, B, S, l, final)


def kernel(x, c, w_ada, b_ada, norm1_g, norm2_g, final_g, w_in, w_out, rel_bias, rwkv_mu, rwkv_w0, rwkv_w2, rwkv_a0, rwkv_a2, rwkv_g2, rwkv_k_k, rwkv_k_a, rwkv_r_k, rwkv_ln_w, rwkv_ln_b, rwkv_v0, rwkv_v1, rwkv_v2, ffn_w_gate, ffn_w_up, ffn_w_down, moe_router_w, moe_router_b, moe_w_gate, moe_w_up, moe_w_down):
    B, S, D = x.shape
    L = w_in.shape[0]
    T = B * S
    W = RWKV_WIDTH

    mod = _ada_mod(c, w_ada, b_ada).reshape(L * B, 6, D)
    w_in_b = jnp.pad(w_in, ((0, 0), (0, 0), (0, N_IN_PAD - N_IN))).astype(BF16)
    w_out_b = w_out.astype(BF16)
    ffn_w = [w.astype(BF16) for w in (ffn_w_gate, ffn_w_up, ffn_w_down)]
    moe_w = [w.astype(BF16) for w in (moe_w_gate, moe_w_up, moe_w_down)]
    mu_p = jnp.pad(rwkv_mu, ((0, 0), (0, RWM_COLS + RWL_COLS - rwkv_mu.shape[1])))
    att_tables = _att_tables(rel_bias, S)
    zeros_w = jnp.zeros((1, W), F32)
    final_g2 = final_g.reshape(1, D)

    xf = x.reshape(T, D)
    v_first = None
    for l in range(L):
        qkv, qkv_mid, qkv_far, rwm, rwl = _inproj(xf, mod, norm1_g[l].reshape(1, D), w_in_b,
                                                   mu_p[l].reshape(1, -1), B, S, l)
        att = _attention(qkv, qkv_mid, qkv_far, att_tables, B, S)
        v0 = rwkv_v0[l - 1].reshape(1, W) if l > 0 else zeros_w
        vec = jnp.concatenate([rwkv_w0[l].reshape(1, W), rwkv_a0[l].reshape(1, W), rwkv_k_k[l].reshape(1, W),
                               rwkv_k_a[l].reshape(1, W), rwkv_r_k[l].reshape(1, W), rwkv_ln_w[l].reshape(1, W),
                               rwkv_ln_b[l].reshape(1, W), v0], axis=0)
        w2p = jnp.pad(rwkv_w2[l], ((0, LANES - D_DECAY_LORA), (0, 0)))
        a2p = jnp.pad(rwkv_a2[l], ((D_DECAY_LORA, 0), (0, 0)))
        g2p = jnp.pad(rwkv_g2[l], ((0, RWL_COLS - LANES - D_GATE_LORA), (0, 0)))
        if l > 0:
            v1p = jnp.pad(rwkv_v1[l - 1], ((0, 0), (0, LANES - D_MV_LORA)))
            v2p = jnp.pad(rwkv_v2[l - 1], ((0, LANES - D_MV_LORA), (0, 0)))
        else:
            v1p = v2p = None
        rw, v_first = _rwkv(rwm, rwl, v_first, vec, w2p, a2p, g2p, v1p, v2p, B, S)
        li = l // 2
        if l % 2 == 0:
            x1, h = _outproj(xf, att, rw, mod, norm2_g[l].reshape(1, D), w_out_b, B, S, l)
            xf = _ffn_dense(x1, h, mod, *ffn_w, B, S, l, li)
        else:
            router = (jnp.pad(moe_router_w[li], ((0, 0), (0, LANES - N_EXPERTS))),
                      jnp.pad(moe_router_b[li], (0, LANES - N_EXPERTS)).reshape(1, LANES))
            x1, h, info = _outproj(xf, att, rw, mod, norm2_g[l].reshape(1, D), w_out_b, B, S, l, router)
            xf = _moe(x1, h, info, mod, *moe_w, final_g2, B, S, l, li,
                      final=(l == L - 1))
    if L % 2 == 1:
        raise NotImplementedError("final norm is fused into the last (expert) layer")
    return xf.reshape(B, S, D)
```

```python
import functools
import math

import numpy as np
import jax
import jax.numpy as jnp
from jax import lax
from jax.experimental import pallas as pl
from jax.experimental.pallas import tpu as pltpu

F32 = jnp.float32
BF16 = jnp.bfloat16

D_MODEL = 1024
HEAD_DIM = 64
ATT_WIDTH = 512
RWKV_WIDTH = 512
N_HEADS = 8
HEAD_PAIRS = 4
LANES = 128
DILATED_GROUPS = ((128, 1), (512, 4), (2048, 16))
N_BUCKETS = 32
MAX_DISTANCE = 2048
NEG_INF = -1e30
D_DECAY_LORA = 64
D_AAA_LORA = 64
D_MV_LORA = 32
D_GATE_LORA = 160
N_IN = 3 * ATT_WIDTH + 3 * RWKV_WIDTH + D_DECAY_LORA + D_AAA_LORA + D_GATE_LORA
QKV_COLS = 3 * ATT_WIDTH
RWM_COLS = 3 * RWKV_WIDTH
RWL_COLS = 384
N_IN_PAD = QKV_COLS + RWM_COLS + RWL_COLS
GN_EPS = HEAD_DIM * 1e-5
RMS_EPS = 1e-6
N_EXPERTS = 8
TOP_K = 2
CHUNK = 64
VMEM_LIMIT = 56 * 1024 * 1024
LOG2_E = math.log2(math.e)
Q_SCALE = LOG2_E / math.sqrt(HEAD_DIM)
DMA_ISSUE_UNROLL = 8


def _cparams(sem):
    return pltpu.CompilerParams(dimension_semantics=sem, vmem_limit_bytes=VMEM_LIMIT)


def _tile(n, pref):
    t = min(n, pref)
    while n % t:
        t //= 2
    return t


def _mm(a, b):
    return jnp.dot(a.astype(BF16), b.astype(BF16), preferred_element_type=F32)


def _mm_nt(a, b):
    return lax.dot_general(a.astype(BF16), b.astype(BF16), (((1,), (1,)), ((), ())),
                           preferred_element_type=F32)


def _sigmoid(x):
    return 1.0 / (1.0 + jnp.exp(-x))


def _silu(x):
    return x * _sigmoid(x)


def _ada_kernel(c_ref, w_ref, b_ref, o_ref):
    ca = _silu(c_ref[...])
    o_ref[0] = _mm(ca, w_ref[0]) + b_ref[0]


def _ada_mod(c, w_ada, b_ada):
    L, D, N = w_ada.shape
    B = c.shape[0]
    tn = _tile(N, 1536)
    return pl.pallas_call(
        _ada_kernel,
        grid=(L, N // tn),
        in_specs=[pl.BlockSpec((B, D), lambda l, j: (0, 0)),
                  pl.BlockSpec((1, D, tn), lambda l, j: (l, 0, j)),
                  pl.BlockSpec((1, 1, tn), lambda l, j: (l, 0, j))],
        out_specs=pl.BlockSpec((1, B, tn), lambda l, j: (l, 0, j)),
        out_shape=jax.ShapeDtypeStruct((L, B, N), F32),
        compiler_params=_cparams(("arbitrary", "arbitrary")),
        name="ada_mod",
    )(c, w_ada, b_ada.reshape(L, 1, N))


def _rms_mod(xf, g, scale, shift):
    ms = jnp.mean(xf * xf, axis=-1, keepdims=True)
    return (xf * lax.rsqrt(ms + RMS_EPS) * g) * (1.0 + scale) + shift


def _inproj_kernel(x_ref, mod_ref, g_ref, w_ref, mu_ref, qkv_ref, qkv_mid_ref, qkv_far_ref, rwm_ref, rwl_ref,
                   carry_ref, qkv_scr):
    i = pl.program_id(1)
    tm = x_ref.shape[0]
    h = _rms_mod(x_ref[...], g_ref[...], mod_ref[0, 1:2, :], mod_ref[0, 0:1, :])
    acc = jnp.dot(h.astype(BF16), w_ref[0], preferred_element_type=F32)
    for c in range(QKV_COLS // LANES):
        cols = slice(c * LANES, (c + 1) * LANES)
        blk = acc[:, cols] * Q_SCALE if (c + 1) * LANES <= ATT_WIDTH else acc[:, cols]
        qkv_ref[:, cols] = blk.astype(BF16)
        qkv_scr[c] = blk
        for out_ref in (qkv_mid_ref, qkv_far_ref):
            d, rows = out_ref.shape[0], out_ref.shape[1]
            for r in range(d):
                out_ref[r, :, cols] = qkv_scr[c, pl.ds(r, rows, stride=d), :].astype(BF16)
    p = acc[:, QKV_COLS:]
    first = jnp.where(i == 0, 0.0, carry_ref[...])
    row = lax.broadcasted_iota(jnp.int32, p.shape, 0)
    prev = jnp.where(row == 0, first, pltpu.roll(p, 1, 0))
    carry_ref[...] = p[tm - 1:tm, :]
    pm = p + mu_ref[...] * (prev - p)
    rwm_ref[...] = pm[:, :RWM_COLS]
    rwl_ref[...] = pm[:, RWM_COLS:]


def _inproj(x, mod, g, w_in_l, mu_l, B, S, l):
    T, D = x.shape
    tm = _tile(S, 512)
    nS = S // tm
    d2, d3 = DILATED_GROUPS[1][1], DILATED_GROUPS[2][1]
    return pl.pallas_call(
        _inproj_kernel,
        grid=(B, nS),
        in_specs=[pl.BlockSpec((tm, D), lambda b, i: (b * nS + i, 0)),
                  pl.BlockSpec((1, 6, D), lambda b, i: (l * B + b, 0, 0)),
                  pl.BlockSpec((1, D), lambda b, i: (0, 0)),
                  pl.BlockSpec((1, D, N_IN_PAD), lambda b, i: (l, 0, 0)),
                  pl.BlockSpec((1, RWM_COLS + RWL_COLS), lambda b, i: (0, 0))],
        out_specs=[pl.BlockSpec((tm, QKV_COLS), lambda b, i: (b * nS + i, 0)),
                   pl.BlockSpec((None, d2, tm // d2, QKV_COLS), lambda b, i: (b, 0, i, 0)),
                   pl.BlockSpec((None, d3, tm // d3, QKV_COLS), lambda b, i: (b, 0, i, 0)),
                   pl.BlockSpec((tm, RWM_COLS), lambda b, i: (b * nS + i, 0)),
                   pl.BlockSpec((tm, RWL_COLS), lambda b, i: (b * nS + i, 0))],
        out_shape=[jax.ShapeDtypeStruct((T, QKV_COLS), BF16),
                   jax.ShapeDtypeStruct((B, d2, S // d2, QKV_COLS), BF16),
                   jax.ShapeDtypeStruct((B, d3, S // d3, QKV_COLS), BF16),
                   jax.ShapeDtypeStruct((T, RWM_COLS), F32),
                   jax.ShapeDtypeStruct((T, RWL_COLS), F32)],
        scratch_shapes=[pltpu.VMEM((1, RWM_COLS + RWL_COLS), F32), pltpu.VMEM((QKV_COLS // LANES, tm, LANES), F32)],
        compiler_params=_cparams(("arbitrary", "arbitrary")),
        name="inproj",
    )(x, mod, g, w_in_l, mu_l)


def _t5_bucket(n):
    max_exact = N_BUCKETS // 2
    large = max_exact + (np.log(np.maximum(n, 1) / max_exact) / np.log(MAX_DISTANCE / max_exact)
                         * (N_BUCKETS - max_exact)).astype(np.int32)
    large = np.minimum(large, N_BUCKETS - 1)
    return np.where(n < max_exact, n, large).astype(np.int32)


ATT_BLOCK = 128
ATT_KEYS = 256


def _toeplitz(vec, c0, rows, cols):
    H, n = vec.shape
    period = rows + cols - 1
    w = jnp.concatenate([jnp.full((H, cols - 1), NEG_INF, F32), vec, jnp.full((H, rows), NEG_INF, F32)], axis=1)
    rw = w[:, ::-1]
    a = w.shape[1] - 1 - (c0 + cols - 1)
    z = jnp.concatenate([rw[:, a:a + cols], rw[:, a - (rows - 1):a]], axis=1)
    flat = jnp.tile(z, (1, rows))[:, :rows * (period - 1)]
    return flat.reshape(H, rows, period - 1)[..., :cols]


def _att_tables(rel_bias, S):
    (w1, d1), (w2, d2), (w3, d3) = DILATED_GROUPS
    assert d1 == 1 and w1 == ATT_KEYS - ATT_BLOCK and w2 // d2 == w1 and S % (d3 * 8) == 0
    off = np.arange(S)
    mult = np.zeros(S, np.int64)
    for w, d in DILATED_GROUPS:
        mult += ((off % d == 0) & (off <= w)).astype(np.int64)
    logm = np.where(mult > 0, np.log(np.maximum(mult, 1)), NEG_INF).astype(np.float32)
    per_off = (rel_bias[_t5_bucket(off)].T.astype(F32) + jnp.asarray(logm)[None, :]) * LOG2_E
    u2 = np.arange(S // d2)
    u3 = np.arange(S // d3)
    band = jnp.where(jnp.asarray(off <= w1)[None], per_off, NEG_INF)
    mid = jnp.where(jnp.asarray((u2 > w1 // d2) & (u2 <= w2 // d2))[None], per_off[:, ::d2], NEG_INF)
    far = jnp.where(jnp.asarray((u3 > w2 // d3) & (u3 <= w3 // d3))[None], per_off[:, ::d3], NEG_INF)
    shift = ATT_KEYS - ATT_BLOCK
    t_band = jnp.stack([_toeplitz(band, 0, ATT_BLOCK, ATT_KEYS), _toeplitz(band, shift, ATT_BLOCK, ATT_KEYS)], axis=1)
    t_mid = jnp.stack([_toeplitz(mid, 0, ATT_BLOCK, ATT_KEYS), _toeplitz(mid, shift, ATT_BLOCK, ATT_KEYS)], axis=1)
    t_far = _toeplitz(far, 0, S // d3, S // d3)
    return t_band, t_mid, t_far


def _attend(units):
    idx = range(len(units))
    heads = range(2)
    head0 = [lax.broadcasted_iota(jnp.int32, u[0].shape, 1) < HEAD_DIM for u in units]
    qh = [[jnp.where(head0[i], units[i][0], jnp.zeros_like(units[i][0])),
           jnp.where(head0[i], jnp.zeros_like(units[i][0]), units[i][0])] for i in idx]
    s = [[lax.dot_general(qh[i][h], units[i][1], (((1,), (1,)), ((), ())), preferred_element_type=F32)
          + units[i][3 + h] for h in heads] for i in idx]
    mx = [[jnp.max(s[i][h], axis=-1, keepdims=True) for h in heads] for i in idx]
    p = [[jnp.exp2(s[i][h] - mx[i][h]) for h in heads] for i in idx]
    ps = [[jnp.sum(p[i][h], axis=-1, keepdims=True) for h in heads] for i in idx]
    pv = [[jnp.dot(p[i][h].astype(BF16), units[i][2], preferred_element_type=F32) for h in heads] for i in idx]
    out = []
    for i in idx:
        shape = units[i][0].shape
        out.append((jnp.where(head0[i], jnp.broadcast_to(mx[i][0], shape), jnp.broadcast_to(mx[i][1], shape)),
                    jnp.where(head0[i], jnp.broadcast_to(ps[i][0], shape), jnp.broadcast_to(ps[i][1], shape)),
                    jnp.where(head0[i], pv[i][0], pv[i][1])))
    return out


def _attn_kernel(qa_ref, ka_ref, va_ref, qm_ref, km_ref, vm_ref, qf_ref, kf_ref, vf_ref,
                 tband_ref, tmid_ref, tfar_ref, o_ref, part_ref):
    R, K = ATT_BLOCK, ATT_KEYS
    S = qa_ref.shape[0]
    n_mid_res, n_mid = qm_ref.shape[0], qm_ref.shape[1]
    n_far_res, n_far = qf_ref.shape[0], qf_ref.shape[1]
    group = 4

    def window(blk):
        start = pl.multiple_of(jnp.maximum(blk * R - (K - R), 0), R)
        return pl.ds(pl.multiple_of(blk * R, R), R), pl.ds(start, K), jnp.minimum(blk, 1)

    def park(cls, rows, results):
        for nat, parts in zip(rows, results):
            for j in range(3):
                part_ref[cls, j, nat, :] = parts[j]

    def band_body(it, _):
        units, rows = [], []
        for g in range(group):
            qrows, krows, var = window(it * group + g)
            rows.append(qrows)
            units.append((qa_ref[qrows, :], ka_ref[krows, :], va_ref[krows, :],
                          tband_ref[0, var], tband_ref[1, var]))
        park(0, rows, _attend(units))
        return 0

    lax.fori_loop(0, S // (R * group), band_body, 0)

    def mid_body(blk, _):
        qrows, krows, var = window(blk)
        units, rows = [], []
        for r in range(n_mid_res):
            rows.append(pl.ds(blk * (R * n_mid_res) + r, R, stride=n_mid_res))
            units.append((qm_ref[r, qrows, :], km_ref[r, krows, :], vm_ref[r, krows, :],
                          tmid_ref[0, var], tmid_ref[1, var]))
        park(1, rows, _attend(units))
        return 0

    lax.fori_loop(0, n_mid // R, mid_body, 0)

    def far_body(it, _):
        units, rows = [], []
        for g in range(group):
            r = it * group + g
            rows.append(pl.ds(r, n_far, stride=n_far_res))
            units.append((qf_ref[r], kf_ref[r], vf_ref[r], tfar_ref[0], tfar_ref[1]))
        park(2, rows, _attend(units))
        return 0

    lax.fori_loop(0, n_far_res // group, far_body, 0)

    def merge_body(i, _):
        rows = pl.ds(pl.multiple_of(i * R, R), R)
        m = [part_ref[c, 0, rows, :] for c in range(3)]
        top = jnp.maximum(jnp.maximum(m[0], m[1]), m[2])
        w = [jnp.exp2(m[c] - top) for c in range(3)]
        total = sum(w[c] * part_ref[c, 1, rows, :] for c in range(3))
        acc = sum(w[c] * part_ref[c, 2, rows, :] for c in range(3))
        o_ref[rows, :] = (acc / total).astype(o_ref.dtype)
        return 0

    lax.fori_loop(0, S // R, merge_body, 0)


def _attention(qkv, qkv_mid, qkv_far, tables, B, S):
    T = qkv.shape[0]
    t_band, t_mid, t_far = tables
    d2, d3 = DILATED_GROUPS[1][1], DILATED_GROUPS[2][1]
    assert S % (ATT_BLOCK * 4) == 0 and S // d2 >= ATT_KEYS
    nat = lambda c: pl.BlockSpec((S, LANES), lambda hp, b: (b, c * HEAD_PAIRS + hp))
    mid = lambda c: pl.BlockSpec((None, d2, S // d2, LANES), lambda hp, b: (b, 0, 0, c * HEAD_PAIRS + hp))
    far = lambda c: pl.BlockSpec((None, d3, S // d3, LANES), lambda hp, b: (b, 0, 0, c * HEAD_PAIRS + hp))
    return pl.pallas_call(
        _attn_kernel,
        grid=(HEAD_PAIRS, B),
        in_specs=[nat(0), nat(1), nat(2), mid(0), mid(1), mid(2), far(0), far(1), far(2),
                  pl.BlockSpec((2,) + t_band.shape[1:], lambda hp, b: (hp, 0, 0, 0)),
                  pl.BlockSpec((2,) + t_mid.shape[1:], lambda hp, b: (hp, 0, 0, 0)),
                  pl.BlockSpec((2,) + t_far.shape[1:], lambda hp, b: (hp, 0, 0))],
        out_specs=pl.BlockSpec((S, LANES), lambda hp, b: (b, hp)),
        out_shape=jax.ShapeDtypeStruct((T, ATT_WIDTH), BF16),
        scratch_shapes=[pltpu.VMEM((3, 3, S, LANES), F32)],
        compiler_params=_cparams(("arbitrary", "arbitrary")),
        name="dilated_attn",
    )(qkv, qkv, qkv, qkv_mid, qkv_mid, qkv_mid, qkv_far, qkv_far, qkv_far, t_band, t_mid, t_far)


def _split2(x):
    hi = x.astype(BF16)
    lo = (x - hi.astype(F32)).astype(BF16)
    return hi, lo


def _head_sum(x, m2):
    hi, lo = _split2(x)
    return jnp.dot(jnp.concatenate([hi, lo], axis=1), m2, preferred_element_type=F32)


def _rwkv_kernel(*refs, has_vres, n_chunks, n_batch):
    if has_vres:
        (rwm_ref, rwl_ref, vf_ref, vec_ref, w2_ref, a2_ref, g2_ref, v1_ref, v2_ref,
         o_ref, state_ref, r_s, k_s, v_s, lw_s, kk_s, kb_s, y_s,
         ar_s, inv_s, bk_s, nrbk_s, akv_s, dec_s) = refs
    else:
        (rwm_ref, rwl_ref, vec_ref, w2_ref, a2_ref, g2_ref,
         o_ref, vf_out_ref, state_ref, r_s, k_s, v_s, lw_s, kk_s, kb_s, y_s,
         ar_s, inv_s, bk_s, nrbk_s, akv_s, dec_s) = refs
    C = CHUNK
    W = RWKV_WIDTH
    rows_per_batch = rwm_ref.shape[1]

    def stacked(ref, cols=slice(None)):
        return jnp.concatenate([ref[b, :, cols] for b in range(n_batch)], axis=0)

    def unstack(ref, x, cols=slice(None)):
        for b in range(n_batch):
            ref[b, :, cols] = x[b * rows_per_batch:(b + 1) * rows_per_batch].astype(ref.dtype)

    @pl.when(pl.program_id(1) == 0)
    def _():
        state_ref[...] = jnp.zeros_like(state_ref)

    ri = lax.broadcasted_iota(jnp.int32, (2 * LANES, LANES), 0)
    ci = lax.broadcasted_iota(jnp.int32, (2 * LANES, LANES), 1)
    m2 = ((ri % LANES) // HEAD_DIM == ci // HEAD_DIM).astype(BF16)

    vec = vec_ref[...]
    w0, a0, k_k, k_a, r_k, ln_w, ln_b, v0 = [vec[i:i + 1, :] for i in range(8)]

    r = stacked(rwm_ref, slice(0, W))
    k = stacked(rwm_ref, slice(W, 2 * W))
    v = stacked(rwm_ref, slice(2 * W, 3 * W))
    lora = stacked(rwl_ref, slice(0, LANES))
    w_raw = w0 + _mm(jnp.tanh(lora), w2_ref[...])
    lw_s[...] = -math.exp(-0.5) * _sigmoid(w_raw)
    a = _sigmoid(a0 + _mm(lora, a2_ref[...]))
    g = _mm(_sigmoid(stacked(rwl_ref, slice(LANES, RWL_COLS))), g2_ref[...])
    if has_vres:
        mix = _sigmoid(v0 + _mm(_mm(v, v1_ref[...]), v2_ref[...]))
        v = v + (stacked(vf_ref) - v) * mix
    else:
        unstack(vf_out_ref, v)
    kk = k * k_k
    for p in range(HEAD_PAIRS):
        ls = slice(p * LANES, (p + 1) * LANES)
        kkp = kk[:, ls]
        kkp = kkp * jnp.minimum(lax.rsqrt(_head_sum(kkp * kkp, m2)), 1e12)
        kk_s[:, ls] = kkp
        kb_s[:, ls] = kkp * a[:, ls]
    kmod = k * (1.0 + (a - 1.0) * k_a)
    r_s[...] = r
    k_s[...] = kmod
    v_s[...] = v

    ri = lax.broadcasted_iota(jnp.int32, (LANES, LANES), 0)
    ci = lax.broadcasted_iota(jnp.int32, (LANES, LANES), 1)
    same_head = (ri // C) == (ci // C)
    strict = same_head & ((ri % C) > (ci % C))
    incl = same_head & ((ri % C) >= (ci % C))
    eye = (ri == ci).astype(F32)
    lane = lax.broadcasted_iota(jnp.int32, (C, LANES), 1)
    head0 = lane < HEAD_DIM
    tri_r = lax.broadcasted_iota(jnp.int32, (C, 3 * C), 0)
    tri_c = lax.broadcasted_iota(jnp.int32, (C, 3 * C), 1)
    tri3 = ((tri_c % C) <= tri_r).astype(BF16)

    lane_slices = [slice(p * LANES, (p + 1) * LANES) for p in range(HEAD_PAIRS)]

    def cumsum(x):
        hi = x.astype(BF16)
        rem = x - hi.astype(F32)
        mid = rem.astype(BF16)
        lo = (rem - mid.astype(F32)).astype(BF16)
        return jnp.dot(tri3, jnp.concatenate([hi, mid, lo], axis=0), preferred_element_type=F32)

    def per_head_rows(x, y):
        zero = jnp.zeros_like(x)
        return jnp.concatenate([jnp.where(head0, x, zero), jnp.where(head0, zero, x),
                                jnp.where(head0, y, zero), jnp.where(head0, zero, y)], axis=0)

    def intra_body(it, _):
        units = [(it * chunk_group + g, p) for g in range(chunk_group) for p in range(HEAD_PAIRS)]
        idx = range(len(units))
        rows = [pl.ds(pl.multiple_of(c * C, C), C) for c, _ in units]
        ls = [lane_slices[p] for _, p in units]
        rc = [r_s[rows[u], ls[u]] for u in idx]
        kc = [k_s[rows[u], ls[u]] for u in idx]
        vc = [v_s[rows[u], ls[u]] for u in idx]
        lw = [lw_s[rows[u], ls[u]] for u in idx]
        kkc = [kk_s[rows[u], ls[u]] for u in idx]
        kbc = [kb_s[rows[u], ls[u]] for u in idx]
        cum = [cumsum(lw[u]) for u in idx]
        total = [cum[u][C - 1:C, :] for u in idx]
        g_inv = [jnp.exp(-cum[u]) for u in idx]
        a_t = [-kkc[u] * jnp.exp(cum[u] - lw[u]) for u in idx]
        r_t = [rc[u] * jnp.exp(cum[u]) for u in idx]
        b_t = [kbc[u] * g_inv[u] for u in idx]
        k_t = [kc[u] * g_inv[u] for u in idx]
        to_end = [jnp.exp(total[u] - cum[u]) for u in idx]
        gram = [_mm_nt(per_head_rows(a_t[u], r_t[u]), per_head_rows(b_t[u], k_t[u])) for u in idx]
        n_ab = [jnp.where(strict, gram[u][:LANES, :LANES], 0.0) for u in idx]
        n_ak = [jnp.where(strict, gram[u][:LANES, LANES:], 0.0) for u in idx]
        n_rb = [jnp.where(incl, gram[u][LANES:, :LANES], 0.0) for u in idx]
        n_rk = [jnp.where(incl, gram[u][LANES:, LANES:], 0.0) for u in idx]
        inv = [eye + n_ab[u] for u in idx]
        pw = [_mm(n_ab[u], n_ab[u]) for u in idx]
        n_rounds = int(math.log2(C)) - 1
        for i in range(n_rounds):
            if i + 1 < n_rounds:
                both = [_mm(pw[u], jnp.concatenate([inv[u], pw[u]], axis=1)) for u in idx]
                inv = [inv[u] + both[u][:, :LANES] for u in idx]
                pw = [both[u][:, LANES:] for u in idx]
            else:
                inv = [inv[u] + _mm(pw[u], inv[u]) for u in idx]
        for u, (c, p) in enumerate(units):
            ar_s[c, p] = jnp.concatenate([a_t[u], r_t[u]], axis=0).astype(BF16)
            inv_s[c, p] = inv[u].astype(BF16)
            akv_s[c, p] = _mm(n_ak[u], jnp.concatenate([vc[u], vc[u]], axis=0))
            nrbk_s[c, p] = jnp.concatenate([n_rb[u], n_rk[u]], axis=1).astype(BF16)
            bk_s[c, p] = jnp.concatenate([kbc[u] * to_end[u], kc[u] * to_end[u]], axis=0).astype(BF16)
            dec_s[c, p] = jnp.broadcast_to(jnp.exp(total[u]), (8, LANES))
        return 0

    chunks_per_batch = n_chunks // n_batch

    def state_body(step, _):
        units = [(b, b * chunks_per_batch + step, p) for b in range(n_batch) for p in range(HEAD_PAIRS)]
        idx = range(len(units))
        rows = [pl.ds(pl.multiple_of(c * C, C), C) for _, c, _ in units]
        ls = [lane_slices[p] for _, _, p in units]
        vc = [v_s[rows[u], ls[u]] for u in idx]
        st = [state_ref[b, p] for b, _, p in units]
        ah = [_mm_nt(ar_s[c, p], st[u]) for u, (_, c, p) in enumerate(units)]
        u_stack = [_mm(inv_s[c, p], jnp.concatenate([ah[u][:C], ah[u][:C]], axis=0) + akv_s[c, p])
                   for u, (_, c, p) in enumerate(units)]
        y_stack = [jnp.concatenate([ah[u][C:], ah[u][C:]], axis=0)
                   + _mm(nrbk_s[c, p], jnp.concatenate([u_stack[u], vc[u], vc[u]], axis=0))
                   for u, (_, c, p) in enumerate(units)]
        for u in idx:
            y_s[rows[u], ls[u]] = jnp.where(head0, y_stack[u][:C], y_stack[u][C:])
        uv_t = [jnp.concatenate([jnp.where(head0, u_stack[u][:C], u_stack[u][C:]), vc[u]], axis=0).T
                for u in idx]
        st_new = [st[u] * dec_s[c, p][0:1, :] + _mm(uv_t[u], bk_s[c, p]) for u, (_, c, p) in enumerate(units)]
        for u, (b, _, p) in enumerate(units):
            state_ref[b, p] = jnp.where(same_head, st_new[u], 0.0)
        return 0

    chunk_group = 2 if n_chunks % 2 == 0 else 1
    lax.fori_loop(0, n_chunks // chunk_group, intra_body, 0)
    lax.fori_loop(0, chunks_per_batch, state_body, 0)

    for p in range(HEAD_PAIRS):
        ls = slice(p * LANES, (p + 1) * LANES)
        y = y_s[:, ls]
        mean = _head_sum(y, m2) * (1.0 / HEAD_DIM)
        d = y - mean
        var = _head_sum(d * d, m2) * (1.0 / HEAD_DIM)
        yn = d * lax.rsqrt(var + GN_EPS) * ln_w[:, ls] + ln_b[:, ls]
        bonus = _head_sum(r_s[:, ls] * k_s[:, ls] * r_k[:, ls], m2) * v_s[:, ls]
        unstack(o_ref, (yn + bonus) * g[:, ls], ls)


def _rwkv(rwm, rwl, v_first, vec, w2p, a2p, g2p, v1p, v2p, B, S):
    T = rwm.shape[0]
    W = RWKV_WIDTH
    nb = 2 if B % 2 == 0 else 1
    tb = _tile(S, 512 // nb)
    n_chunks = nb * tb // CHUNK
    has_vres = v_first is not None
    stack = lambda a: a.reshape(B // nb, nb, S, a.shape[-1])
    row = lambda cols: pl.BlockSpec((None, nb, tb, cols), lambda b, i: (b, 0, i, 0))
    full = lambda b, i: (0, 0)
    in_specs = [row(RWM_COLS), row(RWL_COLS)]
    args = [stack(rwm), stack(rwl)]
    if has_vres:
        in_specs.append(row(W))
        args.append(stack(v_first))
    in_specs += [pl.BlockSpec(vec.shape, full), pl.BlockSpec(w2p.shape, full),
                 pl.BlockSpec(a2p.shape, full), pl.BlockSpec(g2p.shape, full)]
    args += [vec, w2p, a2p, g2p]
    if has_vres:
        in_specs += [pl.BlockSpec(v1p.shape, full), pl.BlockSpec(v2p.shape, full)]
        args += [v1p, v2p]
    out_specs = [row(W)]
    out_shape = [jax.ShapeDtypeStruct((B // nb, nb, S, W), BF16)]
    if not has_vres:
        out_specs.append(row(W))
        out_shape.append(jax.ShapeDtypeStruct((B // nb, nb, S, W), F32))
    outs = pl.pallas_call(
        functools.partial(_rwkv_kernel, has_vres=has_vres, n_chunks=n_chunks, n_batch=nb),
        grid=(B // nb, S // tb),
        in_specs=in_specs,
        out_specs=out_specs,
        out_shape=out_shape,
        scratch_shapes=([pltpu.VMEM((nb, HEAD_PAIRS, LANES, LANES), F32)] + [pltpu.VMEM((nb * tb, W), F32)] * 7
                        + [pltpu.VMEM((n_chunks, HEAD_PAIRS, 2 * CHUNK, LANES), BF16)] * 3
                        + [pltpu.VMEM((n_chunks, HEAD_PAIRS, 2 * CHUNK, 2 * LANES), BF16),
                           pltpu.VMEM((n_chunks, HEAD_PAIRS, 2 * CHUNK, LANES), F32),
                           pltpu.VMEM((n_chunks, HEAD_PAIRS, 8, LANES), F32)]),
        compiler_params=_cparams(("arbitrary", "arbitrary")),
        name="rwkv7",
    )(*args)
    outs = [o.reshape(T, W) for o in outs]
    return (outs[0], v_first) if has_vres else (outs[0], outs[1])


TOKEN_TILE = (8, LANES)


def _store_token_tiles(ref, x):
    n = x.shape[0]
    for s in range(TOKEN_TILE[0]):
        ref[pl.ds(s, n, stride=TOKEN_TILE[0]), :] = x[:, s * LANES:(s + 1) * LANES]


def _load_token_tiles(ref, n, start=0):
    return jnp.concatenate([ref[pl.ds(start + s, n, stride=TOKEN_TILE[0]), :] for s in range(TOKEN_TILE[0])],
                           axis=1)


def _outproj_kernel(*refs, moe):
    if moe:
        (x_ref, att_ref, rw_ref, mod_ref, g_ref, w_ref, rw_w_ref, rw_b_ref,
         x1_ref, h_ref, info_ref) = refs
    else:
        x_ref, att_ref, rw_ref, mod_ref, g_ref, w_ref, x1_ref, h_ref = refs
    mix = (jnp.dot(att_ref[...], w_ref[0, :ATT_WIDTH, :], preferred_element_type=F32)
           + jnp.dot(rw_ref[...], w_ref[0, ATT_WIDTH:, :], preferred_element_type=F32))
    x1 = x_ref[...] + mod_ref[0, 2:3, :] * mix
    x1_ref[...] = x1
    h = _rms_mod(x1, g_ref[...], mod_ref[0, 4:5, :], mod_ref[0, 3:4, :])
    if moe:
        _store_token_tiles(h_ref, h)
    else:
        h_ref[...] = h.astype(h_ref.dtype)
    if moe:
        h_hi, h_lo = _split2(h)
        w_hi, w_lo = _split2(rw_w_ref[...])
        logits = (jnp.dot(h_hi, w_hi, preferred_element_type=F32)
                  + jnp.dot(h_hi, w_lo, preferred_element_type=F32)
                  + jnp.dot(h_lo, w_hi, preferred_element_type=F32)) + rw_b_ref[...]
        lane = lax.broadcasted_iota(jnp.int32, logits.shape, 1)
        logits = jnp.where(lane < N_EXPERTS, logits, -jnp.inf)
        m1 = jnp.max(logits, axis=-1, keepdims=True)
        i1 = jnp.min(jnp.where(logits == m1, lane, LANES), axis=-1, keepdims=True)
        rest = jnp.where(lane == i1, -jnp.inf, logits)
        m2 = jnp.max(rest, axis=-1, keepdims=True)
        i2 = jnp.min(jnp.where(rest == m2, lane, LANES), axis=-1, keepdims=True)
        e = jnp.exp(m2 - m1)
        g1 = 1.0 / (1.0 + e)
        g2 = e / (1.0 + e)
        info = jnp.where(lane == 0, i1.astype(F32),
                         jnp.where(lane == 1, i2.astype(F32),
                                   jnp.where(lane == 2, g1, jnp.where(lane == 3, g2, 0.0))))
        info_ref[...] = info


def _outproj(x, att, rw, mod, g, w_out_b, B, S, l, router=None):
    T, D = x.shape
    tm = _tile(S, 512)
    nS = S // tm
    moe = router is not None
    row = lambda b, i: (b * nS + i, 0)
    full = lambda b, i: (0, 0)
    in_specs = [pl.BlockSpec((tm, D), row), pl.BlockSpec((tm, ATT_WIDTH), row),
                pl.BlockSpec((tm, RWKV_WIDTH), row),
                pl.BlockSpec((1, 6, D), lambda b, i: (l * B + b, 0, 0)),
                pl.BlockSpec((1, D), full),
                pl.BlockSpec((1, D, D), lambda b, i: (l, 0, 0))]
    args = [x, att, rw, mod, g, w_out_b]
    if moe:
        out_specs = [pl.BlockSpec((tm, D), row), pl.BlockSpec((tm * TOKEN_TILE[0], LANES), row)]
        out_shape = [jax.ShapeDtypeStruct((T, D), F32), jax.ShapeDtypeStruct((T * TOKEN_TILE[0], LANES), F32)]
    else:
        out_specs = [pl.BlockSpec((tm, D), row), pl.BlockSpec((tm, D), row)]
        out_shape = [jax.ShapeDtypeStruct((T, D), F32), jax.ShapeDtypeStruct((T, D), BF16)]
    if moe:
        in_specs += [pl.BlockSpec((D, LANES), full), pl.BlockSpec((1, LANES), full)]
        args += list(router)
        out_specs.append(pl.BlockSpec((tm, LANES), row))
        out_shape.append(jax.ShapeDtypeStruct((T, LANES), F32))
    return pl.pallas_call(
        functools.partial(_outproj_kernel, moe=moe),
        grid=(B, nS),
        in_specs=in_specs,
        out_specs=out_specs,
        out_shape=out_shape,
        compiler_params=_cparams(("arbitrary", "arbitrary")),
        name="outproj",
    )(*args)


def _ffn_kernel(x_ref, h_ref, mod_ref, wg_ref, wu_ref, wd_ref, o_ref, acc_ref):
    f = pl.program_id(1)

    @pl.when(f == 0)
    def _():
        acc_ref[...] = jnp.zeros_like(acc_ref)

    hb = h_ref[...]
    gate = jnp.dot(hb, wg_ref[0], preferred_element_type=F32)
    up = jnp.dot(hb, wu_ref[0], preferred_element_type=F32)
    mid = (_silu(gate) * up).astype(BF16)
    acc_ref[...] += jnp.dot(mid, wd_ref[0], preferred_element_type=F32)

    @pl.when(f == pl.num_programs(1) - 1)
    def _():
        o_ref[...] = x_ref[...] + mod_ref[0, 5:6, :] * acc_ref[...]


def _ffn_dense(x1, h, mod, wg, wu, wd, B, S, l, li):
    T, D = x1.shape
    F = wg.shape[-1]
    tm = _tile(S, 512)
    tf = _tile(F, 1408)
    per_seq = S // tm
    return pl.pallas_call(
        _ffn_kernel,
        grid=(T // tm, F // tf),
        in_specs=[pl.BlockSpec((tm, D), lambda i, f: (i, 0)),
                  pl.BlockSpec((tm, D), lambda i, f: (i, 0)),
                  pl.BlockSpec((1, 6, D), lambda i, f: (l * B + i // per_seq, 0, 0)),
                  pl.BlockSpec((1, D, tf), lambda i, f: (li, 0, f)),
                  pl.BlockSpec((1, D, tf), lambda i, f: (li, 0, f)),
                  pl.BlockSpec((1, tf, D), lambda i, f: (li, f, 0))],
        out_specs=pl.BlockSpec((tm, D), lambda i, f: (i, 0)),
        out_shape=jax.ShapeDtypeStruct((T, D), F32),
        scratch_shapes=[pltpu.VMEM((tm, D), F32)],
        compiler_params=_cparams(("arbitrary", "arbitrary")),
        name="ffn_dense",
    )(x1, h, mod, wg, wu, wd)


def _experts_kernel(te_ref, nu_ref, tok_ref, h_ref, wg_ref, wu_ref, wd_ref, y_ref,
                    land_ref, xb_ref, acc_ref, sem):
    i = pl.program_id(0)
    f = pl.program_id(1)
    tm = xb_ref.shape[0]
    rows_per_token = TOKEN_TILE[0]
    n_used = nu_ref[0]

    def start_gather(tile):
        base = tile * tm

        def body(j, _):
            src = pl.multiple_of(tok_ref[base + j] * rows_per_token, rows_per_token)
            dst = pl.multiple_of(j * rows_per_token, rows_per_token)
            pltpu.make_async_copy(h_ref.at[pl.ds(src, rows_per_token)],
                                  land_ref.at[pl.ds(dst, rows_per_token)], sem).start()
            return 0

        lax.fori_loop(0, tm, body, 0, unroll=DMA_ISSUE_UNROLL)

    @pl.when(i < n_used)
    def _():
        @pl.when(f == 0)
        def _():
            @pl.when(i == 0)
            def _():
                start_gather(0)

            pltpu.make_async_copy(h_ref.at[pl.ds(0, tm * rows_per_token)], land_ref, sem).wait()
            xb_ref[...] = _load_token_tiles(land_ref, tm).astype(BF16)
            acc_ref[...] = jnp.zeros_like(acc_ref)

            @pl.when(i + 1 < n_used)
            def _():
                start_gather(i + 1)

        xb = xb_ref[...]
        gate = jnp.dot(xb, wg_ref[0, 0], preferred_element_type=F32)
        up = jnp.dot(xb, wu_ref[0, 0], preferred_element_type=F32)
        mid = (_silu(gate) * up).astype(BF16)
        acc_ref[...] += jnp.dot(mid, wd_ref[0, 0], preferred_element_type=F32)

        @pl.when(f == pl.num_programs(1) - 1)
        def _():
            _store_token_tiles(y_ref, acc_ref[...])

    @pl.when((i >= n_used) & (f == 0))
    def _():
        y_ref[...] = jnp.zeros_like(y_ref)


def _experts(h, row_tok, tile_e, n_used, wg, wu, wd, li, tm):
    n_rows = row_tok.shape[0]
    D = wg.shape[-2]
    rpt = TOKEN_TILE[0]
    assert h.shape[0] >= tm * rpt
    F = wg.shape[-1]
    tf = _tile(F, 1792)
    nf = F // tf

    def rows(i, f, te, nu, tok):
        return (i, 0)

    def wcol(i, f, te, nu, tok):
        return (li, te[jnp.minimum(i, nu[0] - 1)], 0, jnp.where(i < nu[0], f, nf - 1))

    def wrow(i, f, te, nu, tok):
        return (li, te[jnp.minimum(i, nu[0] - 1)], jnp.where(i < nu[0], f, nf - 1), 0)

    return pl.pallas_call(
        _experts_kernel,
        grid_spec=pltpu.PrefetchScalarGridSpec(
            num_scalar_prefetch=3,
            grid=(n_rows // tm, nf),
            in_specs=[pl.BlockSpec(memory_space=pl.ANY),
                      pl.BlockSpec((1, 1, D, tf), wcol),
                      pl.BlockSpec((1, 1, D, tf), wcol),
                      pl.BlockSpec((1, 1, tf, D), wrow)],
            out_specs=pl.BlockSpec((tm * rpt, LANES), rows),
            scratch_shapes=[pltpu.VMEM((tm * rpt, LANES), F32), pltpu.VMEM((tm, D), BF16),
                            pltpu.VMEM((tm, D), F32), pltpu.SemaphoreType.DMA(())]),
        out_shape=jax.ShapeDtypeStruct((n_rows * rpt, LANES), F32),
        compiler_params=_cparams(("arbitrary", "arbitrary")),
        name="moe_experts",
    )(tile_e, n_used, row_tok, h, wg, wu, wd)


def _combine_kernel(pos_ref, y_ref, x_ref, info_ref, mod_ref, fg_ref, o_ref, buf, sem, *, tc, final):
    i = pl.program_id(0)
    slot = i % 2
    rows_per_token = TOKEN_TILE[0]

    def start_gather(tile, dst_slot):
        base = tile * (2 * tc)

        def body(j, _):
            src = pl.multiple_of(pos_ref[base + j] * rows_per_token, rows_per_token)
            dst = pl.multiple_of(j * rows_per_token, rows_per_token)
            pltpu.make_async_copy(y_ref.at[pl.ds(src, rows_per_token)],
                                  buf.at[dst_slot, pl.ds(dst, rows_per_token)], sem.at[dst_slot]).start()
            return 0

        lax.fori_loop(0, 2 * tc, body, 0, unroll=DMA_ISSUE_UNROLL)

    @pl.when(i == 0)
    def _():
        start_gather(0, 0)

    @pl.when(i + 1 < pl.num_programs(0))
    def _():
        start_gather(i + 1, 1 - slot)

    pltpu.make_async_copy(y_ref.at[pl.ds(0, 2 * tc * rows_per_token)], buf.at[slot], sem.at[slot]).wait()
    info = info_ref[...]
    ff = (info[:, 2:3] * _load_token_tiles(buf.at[slot], tc)
          + info[:, 3:4] * _load_token_tiles(buf.at[slot], tc, tc * rows_per_token))
    out = x_ref[...] + mod_ref[0, 5:6, :] * ff
    if final:
        ms = jnp.mean(out * out, axis=-1, keepdims=True)
        out = out * lax.rsqrt(ms + RMS_EPS) * fg_ref[...]
    o_ref[...] = out


def _combine(pos, y_rows, x1, info, mod, final_g, B, S, l, final):
    T, D = x1.shape
    tc = _tile(S, 256)
    per_seq = S // tc
    return pl.pallas_call(
        functools.partial(_combine_kernel, tc=tc, final=final),
        grid_spec=pltpu.PrefetchScalarGridSpec(
            num_scalar_prefetch=1,
            grid=(T // tc,),
            in_specs=[pl.BlockSpec(memory_space=pl.ANY),
                      pl.BlockSpec((tc, D), lambda i, pos: (i, 0)),
                      pl.BlockSpec((tc, LANES), lambda i, pos: (i, 0)),
                      pl.BlockSpec((1, 6, D), lambda i, pos: (l * B + i // per_seq, 0, 0)),
                      pl.BlockSpec((1, D), lambda i, pos: (0, 0))],
            out_specs=pl.BlockSpec((tc, D), lambda i, pos: (i, 0)),
            scratch_shapes=[pltpu.VMEM((2, 2 * tc * TOKEN_TILE[0], LANES), F32), pltpu.SemaphoreType.DMA((2,))]),
        out_shape=jax.ShapeDtypeStruct((T, D), F32),
        compiler_params=_cparams(("arbitrary",)),
        name="moe_combine",
    )(pos, y_rows, x1, info, mod, final_g)


def _moe(x1, h, info, mod, wg, wu, wd, final_g, B, S, l, li, final):
    T, D = x1.shape
    TK = T * TOP_K
    tm = _tile(TK // N_EXPERTS, 512)
    n_rows = TK + (N_EXPERTS - 1) * tm
    flat_e = info[:, :TOP_K].astype(jnp.int32).reshape(TK)
    onehot = (flat_e[:, None] == jnp.arange(N_EXPERTS, dtype=jnp.int32)[None, :]).astype(jnp.int32)
    csum = jnp.cumsum(onehot, axis=0)
    counts = csum[-1]
    rank = jnp.sum(onehot * csum, axis=1) - 1
    padded = (counts + tm - 1) // tm * tm
    pad_ends = jnp.cumsum(padded)
    pad_starts = pad_ends - padded
    dest = pad_starts[flat_e] + rank
    row_tok = jnp.zeros((n_rows,), jnp.int32).at[dest].set(jnp.arange(TK, dtype=jnp.int32) // TOP_K)
    tile_start = jnp.arange(n_rows // tm, dtype=jnp.int32) * tm
    tile_e = jnp.minimum(jnp.searchsorted(pad_ends, tile_start, side='right'), N_EXPERTS - 1).astype(jnp.int32)
    n_used_rows = pad_ends[-1:].astype(jnp.int32)
    n_used_tiles = n_used_rows // tm
    tc = _tile(S, 256)
    pos = dest.reshape(T // tc, tc, TOP_K).transpose(0, 2, 1).reshape(TK).astype(jnp.int32)

    y_rows = _experts(h, row_tok, tile_e, n_used_tiles, wg, wu, wd, li, tm)
    return _combine(pos, y_rows, x1, info, mod, final_g, B, S, l, final)


def kernel(x, c, w_ada, b_ada, norm1_g, norm2_g, final_g, w_in, w_out, rel_bias, rwkv_mu, rwkv_w0, rwkv_w2, rwkv_a0, rwkv_a2, rwkv_g2, rwkv_k_k, rwkv_k_a, rwkv_r_k, rwkv_ln_w, rwkv_ln_b, rwkv_v0, rwkv_v1, rwkv_v2, ffn_w_gate, ffn_w_up, ffn_w_down, moe_router_w, moe_router_b, moe_w_gate, moe_w_up, moe_w_down):
    B, S, D = x.shape
    L = w_in.shape[0]
    T = B * S
    W = RWKV_WIDTH

    mod = _ada_mod(c, w_ada, b_ada).reshape(L * B, 6, D)
    w_in_b = jnp.pad(w_in, ((0, 0), (0, 0), (0, N_IN_PAD - N_IN))).astype(BF16)
    w_out_b = w_out.astype(BF16)
    ffn_w = [w.astype(BF16) for w in (ffn_w_gate, ffn_w_up, ffn_w_down)]
    moe_w = [w.astype(BF16) for w in (moe_w_gate, moe_w_up, moe_w_down)]
    mu_p = jnp.pad(rwkv_mu, ((0, 0), (0, RWM_COLS + RWL_COLS - rwkv_mu.shape[1])))
    att_tables = _att_tables(rel_bias, S)
    zeros_w = jnp.zeros((1, W), F32)
    final_g2 = final_g.reshape(1, D)

    xf = x.reshape(T, D)
    v_first = None
    for l in range(L):
        qkv, qkv_mid, qkv_far, rwm, rwl = _inproj(xf, mod, norm1_g[l].reshape(1, D), w_in_b,
                                                   mu_p[l].reshape(1, -1), B, S, l)
        att = _attention(qkv, qkv_mid, qkv_far, att_tables, B, S)
        v0 = rwkv_v0[l - 1].reshape(1, W) if l > 0 else zeros_w
        vec = jnp.concatenate([rwkv_w0[l].reshape(1, W), rwkv_a0[l].reshape(1, W), rwkv_k_k[l].reshape(1, W),
                               rwkv_k_a[l].reshape(1, W), rwkv_r_k[l].reshape(1, W), rwkv_ln_w[l].reshape(1, W),
                               rwkv_ln_b[l].reshape(1, W), v0], axis=0)
        w2p = jnp.pad(rwkv_w2[l], ((0, LANES - D_DECAY_LORA), (0, 0)))
        a2p = jnp.pad(rwkv_a2[l], ((D_DECAY_LORA, 0), (0, 0)))
        g2p = jnp.pad(rwkv_g2[l], ((0, RWL_COLS - LANES - D_GATE_LORA), (0, 0)))
        if l > 0:
            v1p = jnp.pad(rwkv_v1[l - 1], ((0, 0), (0, LANES - D_MV_LORA)))
            v2p = jnp.pad(rwkv_v2[l - 1], ((0, LANES - D_MV_LORA), (0, 0)))
        else:
            v1p = v2p = None
        rw, v_first = _rwkv(rwm, rwl, v_first, vec, w2p, a2p, g2p, v1p, v2p, B, S)
        li = l // 2
        if l % 2 == 0:
            x1, h = _outproj(xf, att, rw, mod, norm2_g[l].reshape(1, D), w_out_b, B, S, l)
            xf = _ffn_dense(x1, h, mod, *ffn_w, B, S, l, li)
        else:
            router = (jnp.pad(moe_router_w[li], ((0, 0), (0, LANES - N_EXPERTS))),
                      jnp.pad(moe_router_b[li], (0, LANES - N_EXPERTS)).reshape(1, LANES))
            x1, h, info = _outproj(xf, att, rw, mod, norm2_g[l].reshape(1, D), w_out_b, B, S, l, router)
            xf = _moe(x1, h, info, mod, *moe_w, final_g2, B, S, l, li,
                      final=(l == L - 1))
    if L % 2 == 1:
        raise NotImplementedError("final norm is fused into the last (expert) layer")
    return xf.reshape(B, S, D)
```

```python
import functools
import math

import numpy as np
import jax
import jax.numpy as jnp
from jax import lax
from jax.experimental import pallas as pl
from jax.experimental.pallas import tpu as pltpu

F32 = jnp.float32
BF16 = jnp.bfloat16

D_MODEL = 1024
HEAD_DIM = 64
ATT_WIDTH = 512
RWKV_WIDTH = 512
N_HEADS = 8
HEAD_PAIRS = 4
LANES = 128
DILATED_GROUPS = ((128, 1), (512, 4), (2048, 16))
N_BUCKETS = 32
MAX_DISTANCE = 2048
NEG_INF = -1e30
D_DECAY_LORA = 64
D_AAA_LORA = 64
D_MV_LORA = 32
D_GATE_LORA = 160
N_IN = 3 * ATT_WIDTH + 3 * RWKV_WIDTH + D_DECAY_LORA + D_AAA_LORA + D_GATE_LORA
QKV_COLS = 3 * ATT_WIDTH
RWM_COLS = 3 * RWKV_WIDTH
RWL_COLS = 384
N_IN_PAD = QKV_COLS + RWM_COLS + RWL_COLS
GN_EPS = HEAD_DIM * 1e-5
RMS_EPS = 1e-6
N_EXPERTS = 8
TOP_K = 2
CHUNK = 64
VMEM_LIMIT = 56 * 1024 * 1024
LOG2_E = math.log2(math.e)
Q_SCALE = LOG2_E / math.sqrt(HEAD_DIM)
DMA_ISSUE_UNROLL = 8


def _cparams(sem):
    return pltpu.CompilerParams(dimension_semantics=sem, vmem_limit_bytes=VMEM_LIMIT)


def _tile(n, pref):
    t = min(n, pref)
    while n % t:
        t //= 2
    return t


def _mm(a, b):
    return jnp.dot(a.astype(BF16), b.astype(BF16), preferred_element_type=F32)


def _mm_nt(a, b):
    return lax.dot_general(a.astype(BF16), b.astype(BF16), (((1,), (1,)), ((), ())),
                           preferred_element_type=F32)


def _sigmoid(x):
    return 1.0 / (1.0 + jnp.exp(-x))


def _silu(x):
    return x * _sigmoid(x)


def _ada_kernel(c_ref, w_ref, b_ref, o_ref):
    ca = _silu(c_ref[...])
    o_ref[0] = _mm(ca, w_ref[0]) + b_ref[0]


def _ada_mod(c, w_ada, b_ada):
    L, D, N = w_ada.shape
    B = c.shape[0]
    tn = _tile(N, 1536)
    return pl.pallas_call(
        _ada_kernel,
        grid=(L, N // tn),
        in_specs=[pl.BlockSpec((B, D), lambda l, j: (0, 0)),
                  pl.BlockSpec((1, D, tn), lambda l, j: (l, 0, j)),
                  pl.BlockSpec((1, 1, tn), lambda l, j: (l, 0, j))],
        out_specs=pl.BlockSpec((1, B, tn), lambda l, j: (l, 0, j)),
        out_shape=jax.ShapeDtypeStruct((L, B, N), F32),
        compiler_params=_cparams(("arbitrary", "arbitrary")),
        name="ada_mod",
    )(c, w_ada, b_ada.reshape(L, 1, N))


def _rms_mod(xf, g, scale, shift):
    ms = jnp.mean(xf * xf, axis=-1, keepdims=True)
    return (xf * lax.rsqrt(ms + RMS_EPS) * g) * (1.0 + scale) + shift


def _inproj_kernel(x_ref, mod_ref, g_ref, w_ref, mu_ref, qkv_ref, qkv_mid_ref, qkv_far_ref, rwm_ref, rwl_ref,
                   carry_ref, qkv_scr):
    i = pl.program_id(1)
    tm = x_ref.shape[0]
    h = _rms_mod(x_ref[...], g_ref[...], mod_ref[0, 1:2, :], mod_ref[0, 0:1, :])
    acc = jnp.dot(h.astype(BF16), w_ref[0], preferred_element_type=F32)
    for c in range(QKV_COLS // LANES):
        cols = slice(c * LANES, (c + 1) * LANES)
        blk = acc[:, cols] * Q_SCALE if (c + 1) * LANES <= ATT_WIDTH else acc[:, cols]
        qkv_ref[:, cols] = blk.astype(BF16)
        qkv_scr[c] = blk
        for out_ref in (qkv_mid_ref, qkv_far_ref):
            d, rows = out_ref.shape[0], out_ref.shape[1]
            for r in range(d):
                out_ref[r, :, cols] = qkv_scr[c, pl.ds(r, rows, stride=d), :].astype(BF16)
    p = acc[:, QKV_COLS:]
    first = jnp.where(i == 0, 0.0, carry_ref[...])
    row = lax.broadcasted_iota(jnp.int32, p.shape, 0)
    prev = jnp.where(row == 0, first, pltpu.roll(p, 1, 0))
    carry_ref[...] = p[tm - 1:tm, :]
    pm = p + mu_ref[...] * (prev - p)
    rwm_ref[...] = pm[:, :RWM_COLS]
    rwl_ref[...] = pm[:, RWM_COLS:]


def _inproj(x, mod, g, w_in_l, mu_l, B, S, l):
    T, D = x.shape
    tm = _tile(S, 512)
    nS = S // tm
    d2, d3 = DILATED_GROUPS[1][1], DILATED_GROUPS[2][1]
    return pl.pallas_call(
        _inproj_kernel,
        grid=(B, nS),
        in_specs=[pl.BlockSpec((tm, D), lambda b, i: (b * nS + i, 0)),
                  pl.BlockSpec((1, 6, D), lambda b, i: (l * B + b, 0, 0)),
                  pl.BlockSpec((1, D), lambda b, i: (0, 0)),
                  pl.BlockSpec((1, D, N_IN_PAD), lambda b, i: (l, 0, 0)),
                  pl.BlockSpec((1, RWM_COLS + RWL_COLS), lambda b, i: (0, 0))],
        out_specs=[pl.BlockSpec((tm, QKV_COLS), lambda b, i: (b * nS + i, 0)),
                   pl.BlockSpec((None, d2, tm // d2, QKV_COLS), lambda b, i: (b, 0, i, 0)),
                   pl.BlockSpec((None, d3, tm // d3, QKV_COLS), lambda b, i: (b, 0, i, 0)),
                   pl.BlockSpec((tm, RWM_COLS), lambda b, i: (b * nS + i, 0)),
                   pl.BlockSpec((tm, RWL_COLS), lambda b, i: (b * nS + i, 0))],
        out_shape=[jax.ShapeDtypeStruct((T, QKV_COLS), BF16),
                   jax.ShapeDtypeStruct((B, d2, S // d2, QKV_COLS), BF16),
                   jax.ShapeDtypeStruct((B, d3, S // d3, QKV_COLS), BF16),
                   jax.ShapeDtypeStruct((T, RWM_COLS), F32),
                   jax.ShapeDtypeStruct((T, RWL_COLS), F32)],
        scratch_shapes=[pltpu.VMEM((1, RWM_COLS + RWL_COLS), F32), pltpu.VMEM((QKV_COLS // LANES, tm, LANES), F32)],
        compiler_params=_cparams(("arbitrary", "arbitrary")),
        name="inproj",
    )(x, mod, g, w_in_l, mu_l)


def _t5_bucket(n):
    max_exact = N_BUCKETS // 2
    large = max_exact + (np.log(np.maximum(n, 1) / max_exact) / np.log(MAX_DISTANCE / max_exact)
                         * (N_BUCKETS - max_exact)).astype(np.int32)
    large = np.minimum(large, N_BUCKETS - 1)
    return np.where(n < max_exact, n, large).astype(np.int32)


ATT_BLOCK = 128
ATT_KEYS = 256


def _toeplitz(vec, c0, rows, cols):
    H, n = vec.shape
    period = rows + cols - 1
    w = jnp.concatenate([jnp.full((H, cols - 1), NEG_INF, F32), vec, jnp.full((H, rows), NEG_INF, F32)], axis=1)
    rw = w[:, ::-1]
    a = w.shape[1] - 1 - (c0 + cols - 1)
    z = jnp.concatenate([rw[:, a:a + cols], rw[:, a - (rows - 1):a]], axis=1)
    flat = jnp.tile(z, (1, rows))[:, :rows * (period - 1)]
    return flat.reshape(H, rows, period - 1)[..., :cols]


def _att_tables(rel_bias, S):
    (w1, d1), (w2, d2), (w3, d3) = DILATED_GROUPS
    assert d1 == 1 and w1 == ATT_KEYS - ATT_BLOCK and w2 // d2 == w1 and S % (d3 * 8) == 0
    off = np.arange(S)
    mult = np.zeros(S, np.int64)
    for w, d in DILATED_GROUPS:
        mult += ((off % d == 0) & (off <= w)).astype(np.int64)
    logm = np.where(mult > 0, np.log(np.maximum(mult, 1)), NEG_INF).astype(np.float32)
    per_off = (rel_bias[_t5_bucket(off)].T.astype(F32) + jnp.asarray(logm)[None, :]) * LOG2_E
    u2 = np.arange(S // d2)
    u3 = np.arange(S // d3)
    band = jnp.where(jnp.asarray(off <= w1)[None], per_off, NEG_INF)
    mid = jnp.where(jnp.asarray((u2 > w1 // d2) & (u2 <= w2 // d2))[None], per_off[:, ::d2], NEG_INF)
    far = jnp.where(jnp.asarray((u3 > w2 // d3) & (u3 <= w3 // d3))[None], per_off[:, ::d3], NEG_INF)
    shift = ATT_KEYS - ATT_BLOCK
    t_band = jnp.stack([_toeplitz(band, 0, ATT_BLOCK, ATT_KEYS), _toeplitz(band, shift, ATT_BLOCK, ATT_KEYS)], axis=1)
    t_mid = jnp.stack([_toeplitz(mid, 0, ATT_BLOCK, ATT_KEYS), _toeplitz(mid, shift, ATT_BLOCK, ATT_KEYS)], axis=1)
    t_far = _toeplitz(far, 0, S // d3, S // d3)
    return t_band, t_mid, t_far


def _attend(units):
    idx = range(len(units))
    heads = range(2)
    head0 = [lax.broadcasted_iota(jnp.int32, u[0].shape, 1) < HEAD_DIM for u in units]
    qh = [[jnp.where(head0[i], units[i][0], jnp.zeros_like(units[i][0])),
           jnp.where(head0[i], jnp.zeros_like(units[i][0]), units[i][0])] for i in idx]
    s = [[lax.dot_general(qh[i][h], units[i][1], (((1,), (1,)), ((), ())), preferred_element_type=F32)
          + units[i][3 + h] for h in heads] for i in idx]
    mx = [[jnp.max(s[i][h], axis=-1, keepdims=True) for h in heads] for i in idx]
    p = [[jnp.exp2(s[i][h] - mx[i][h]) for h in heads] for i in idx]
    ps = [[jnp.sum(p[i][h], axis=-1, keepdims=True) for h in heads] for i in idx]
    pv = [[jnp.dot(p[i][h].astype(BF16), units[i][2], preferred_element_type=F32) for h in heads] for i in idx]
    out = []
    for i in idx:
        shape = units[i][0].shape
        out.append((jnp.where(head0[i], jnp.broadcast_to(mx[i][0], shape), jnp.broadcast_to(mx[i][1], shape)),
                    jnp.where(head0[i], jnp.broadcast_to(ps[i][0], shape), jnp.broadcast_to(ps[i][1], shape)),
                    jnp.where(head0[i], pv[i][0], pv[i][1])))
    return out


def _attn_kernel(qa_ref, ka_ref, va_ref, qm_ref, km_ref, vm_ref, qf_ref, kf_ref, vf_ref,
                 tband_ref, tmid_ref, tfar_ref, o_ref, part_ref):
    R, K = ATT_BLOCK, ATT_KEYS
    S = qa_ref.shape[0]
    n_mid_res, n_mid = qm_ref.shape[0], qm_ref.shape[1]
    n_far_res, n_far = qf_ref.shape[0], qf_ref.shape[1]
    group = 4

    def window(blk):
        start = pl.multiple_of(jnp.maximum(blk * R - (K - R), 0), R)
        return pl.ds(pl.multiple_of(blk * R, R), R), pl.ds(start, K), jnp.minimum(blk, 1)

    def park(cls, rows, results):
        for nat, parts in zip(rows, results):
            for j in range(3):
                part_ref[cls, j, nat, :] = parts[j]

    def band_body(it, _):
        units, rows = [], []
        for g in range(group):
            qrows, krows, var = window(it * group + g)
            rows.append(qrows)
            units.append((qa_ref[qrows, :], ka_ref[krows, :], va_ref[krows, :],
                          tband_ref[0, var], tband_ref[1, var]))
        park(0, rows, _attend(units))
        return 0

    lax.fori_loop(0, S // (R * group), band_body, 0)

    def mid_body(blk, _):
        qrows, krows, var = window(blk)
        units, rows = [], []
        for r in range(n_mid_res):
            rows.append(pl.ds(blk * (R * n_mid_res) + r, R, stride=n_mid_res))
            units.append((qm_ref[r, qrows, :], km_ref[r, krows, :], vm_ref[r, krows, :],
                          tmid_ref[0, var], tmid_ref[1, var]))
        park(1, rows, _attend(units))
        return 0

    lax.fori_loop(0, n_mid // R, mid_body, 0)

    def far_body(it, _):
        units, rows = [], []
        for g in range(group):
            r = it * group + g
            rows.append(pl.ds(r, n_far, stride=n_far_res))
            units.append((qf_ref[r], kf_ref[r], vf_ref[r], tfar_ref[0], tfar_ref[1]))
        park(2, rows, _attend(units))
        return 0

    lax.fori_loop(0, n_far_res // group, far_body, 0)

    def merge_body(i, _):
        rows = pl.ds(pl.multiple_of(i * R, R), R)
        m = [part_ref[c, 0, rows, :] for c in range(3)]
        top = jnp.maximum(jnp.maximum(m[0], m[1]), m[2])
        w = [jnp.exp2(m[c] - top) for c in range(3)]
        total = sum(w[c] * part_ref[c, 1, rows, :] for c in range(3))
        acc = sum(w[c] * part_ref[c, 2, rows, :] for c in range(3))
        o_ref[rows, :] = (acc / total).astype(o_ref.dtype)
        return 0

    lax.fori_loop(0, S // R, merge_body, 0)


def _attention(qkv, qkv_mid, qkv_far, tables, B, S):
    T = qkv.shape[0]
    t_band, t_mid, t_far = tables
    d2, d3 = DILATED_GROUPS[1][1], DILATED_GROUPS[2][1]
    assert S % (ATT_BLOCK * 4) == 0 and S // d2 >= ATT_KEYS
    nat = lambda c: pl.BlockSpec((S, LANES), lambda hp, b: (b, c * HEAD_PAIRS + hp))
    mid = lambda c: pl.BlockSpec((None, d2, S // d2, LANES), lambda hp, b: (b, 0, 0, c * HEAD_PAIRS + hp))
    far = lambda c: pl.BlockSpec((None, d3, S // d3, LANES), lambda hp, b: (b, 0, 0, c * HEAD_PAIRS + hp))
    return pl.pallas_call(
        _attn_kernel,
        grid=(HEAD_PAIRS, B),
        in_specs=[nat(0), nat(1), nat(2), mid(0), mid(1), mid(2), far(0), far(1), far(2),
                  pl.BlockSpec((2,) + t_band.shape[1:], lambda hp, b: (hp, 0, 0, 0)),
                  pl.BlockSpec((2,) + t_mid.shape[1:], lambda hp, b: (hp, 0, 0, 0)),
                  pl.BlockSpec((2,) + t_far.shape[1:], lambda hp, b: (hp, 0, 0))],
        out_specs=pl.BlockSpec((S, LANES), lambda hp, b: (b, hp)),
        out_shape=jax.ShapeDtypeStruct((T, ATT_WIDTH), BF16),
        scratch_shapes=[pltpu.VMEM((3, 3, S, LANES), F32)],
        compiler_params=_cparams(("arbitrary", "arbitrary")),
        name="dilated_attn",
    )(qkv, qkv, qkv, qkv_mid, qkv_mid, qkv_mid, qkv_far, qkv_far, qkv_far, t_band, t_mid, t_far)


def _split2(x):
    hi = x.astype(BF16)
    lo = (x - hi.astype(F32)).astype(BF16)
    return hi, lo


def _head_sum(x, m2):
    hi, lo = _split2(x)
    return jnp.dot(jnp.concatenate([hi, lo], axis=1), m2, preferred_element_type=F32)


def _rwkv_kernel(*refs, has_vres, n_chunks, n_batch):
    if has_vres:
        (rwm_ref, rwl_ref, vf_ref, vec_ref, w2_ref, a2_ref, g2_ref, v1_ref, v2_ref,
         o_ref, state_ref, r_s, k_s, v_s, lw_s, kk_s, kb_s, y_s,
         ar_s, inv_s, bk_s, nrbk_s, akv_s, dec_s) = refs
    else:
        (rwm_ref, rwl_ref, vec_ref, w2_ref, a2_ref, g2_ref,
         o_ref, vf_out_ref, state_ref, r_s, k_s, v_s, lw_s, kk_s, kb_s, y_s,
         ar_s, inv_s, bk_s, nrbk_s, akv_s, dec_s) = refs
    C = CHUNK
    W = RWKV_WIDTH
    rows_per_batch = rwm_ref.shape[1]

    def stacked(ref, cols=slice(None)):
        return jnp.concatenate([ref[b, :, cols] for b in range(n_batch)], axis=0)

    def unstack(ref, x, cols=slice(None)):
        for b in range(n_batch):
            ref[b, :, cols] = x[b * rows_per_batch:(b + 1) * rows_per_batch].astype(ref.dtype)

    @pl.when(pl.program_id(1) == 0)
    def _():
        state_ref[...] = jnp.zeros_like(state_ref)

    ri = lax.broadcasted_iota(jnp.int32, (2 * LANES, LANES), 0)
    ci = lax.broadcasted_iota(jnp.int32, (2 * LANES, LANES), 1)
    m2 = ((ri % LANES) // HEAD_DIM == ci // HEAD_DIM).astype(BF16)

    vec = vec_ref[...]
    w0, a0, k_k, k_a, r_k, ln_w, ln_b, v0 = [vec[i:i + 1, :] for i in range(8)]

    r = stacked(rwm_ref, slice(0, W))
    k = stacked(rwm_ref, slice(W, 2 * W))
    v = stacked(rwm_ref, slice(2 * W, 3 * W))
    lora = stacked(rwl_ref, slice(0, LANES))
    w_raw = w0 + _mm(jnp.tanh(lora), w2_ref[...])
    lw_s[...] = -math.exp(-0.5) * _sigmoid(w_raw)
    a = _sigmoid(a0 + _mm(lora, a2_ref[...]))
    g = _mm(_sigmoid(stacked(rwl_ref, slice(LANES, RWL_COLS))), g2_ref[...])
    if has_vres:
        mix = _sigmoid(v0 + _mm(_mm(v, v1_ref[...]), v2_ref[...]))
        v = v + (stacked(vf_ref) - v) * mix
    else:
        unstack(vf_out_ref, v)
    kk = k * k_k
    for p in range(HEAD_PAIRS):
        ls = slice(p * LANES, (p + 1) * LANES)
        kkp = kk[:, ls]
        kkp = kkp * jnp.minimum(lax.rsqrt(_head_sum(kkp * kkp, m2)), 1e12)
        kk_s[:, ls] = kkp
        kb_s[:, ls] = kkp * a[:, ls]
    kmod = k * (1.0 + (a - 1.0) * k_a)
    r_s[...] = r
    k_s[...] = kmod
    v_s[...] = v

    ri = lax.broadcasted_iota(jnp.int32, (LANES, LANES), 0)
    ci = lax.broadcasted_iota(jnp.int32, (LANES, LANES), 1)
    same_head = (ri // C) == (ci // C)
    strict = same_head & ((ri % C) > (ci % C))
    incl = same_head & ((ri % C) >= (ci % C))
    eye = (ri == ci).astype(F32)
    lane = lax.broadcasted_iota(jnp.int32, (C, LANES), 1)
    head0 = lane < HEAD_DIM
    tri_r = lax.broadcasted_iota(jnp.int32, (C, 3 * C), 0)
    tri_c = lax.broadcasted_iota(jnp.int32, (C, 3 * C), 1)
    tri3 = ((tri_c % C) <= tri_r).astype(BF16)

    lane_slices = [slice(p * LANES, (p + 1) * LANES) for p in range(HEAD_PAIRS)]

    def cumsum(x):
        hi = x.astype(BF16)
        rem = x - hi.astype(F32)
        mid = rem.astype(BF16)
        lo = (rem - mid.astype(F32)).astype(BF16)
        return jnp.dot(tri3, jnp.concatenate([hi, mid, lo], axis=0), preferred_element_type=F32)

    def per_head_rows(x, y):
        zero = jnp.zeros_like(x)
        return jnp.concatenate([jnp.where(head0, x, zero), jnp.where(head0, zero, x),
                                jnp.where(head0, y, zero), jnp.where(head0, zero, y)], axis=0)

    def intra_body(it, _):
        units = [(it * chunk_group + g, p) for g in range(chunk_group) for p in range(HEAD_PAIRS)]
        idx = range(len(units))
        rows = [pl.ds(pl.multiple_of(c * C, C), C) for c, _ in units]
        ls = [lane_slices[p] for _, p in units]
        rc = [r_s[rows[u], ls[u]] for u in idx]
        kc = [k_s[rows[u], ls[u]] for u in idx]
        vc = [v_s[rows[u], ls[u]] for u in idx]
        lw = [lw_s[rows[u], ls[u]] for u in idx]
        kkc = [kk_s[rows[u], ls[u]] for u in idx]
        kbc = [kb_s[rows[u], ls[u]] for u in idx]
        cum = [cumsum(lw[u]) for u in idx]
        total = [cum[u][C - 1:C, :] for u in idx]
        g_inv = [jnp.exp(-cum[u]) for u in idx]
        a_t = [-kkc[u] * jnp.exp(cum[u] - lw[u]) for u in idx]
        r_t = [rc[u] * jnp.exp(cum[u]) for u in idx]
        b_t = [kbc[u] * g_inv[u] for u in idx]
        k_t = [kc[u] * g_inv[u] for u in idx]
        to_end = [jnp.exp(total[u] - cum[u]) for u in idx]
        gram = [_mm_nt(per_head_rows(a_t[u], r_t[u]), per_head_rows(b_t[u], k_t[u])) for u in idx]
        n_ab = [jnp.where(strict, gram[u][:LANES, :LANES], 0.0) for u in idx]
        n_ak = [jnp.where(strict, gram[u][:LANES, LANES:], 0.0) for u in idx]
        n_rb = [jnp.where(incl, gram[u][LANES:, :LANES], 0.0) for u in idx]
        n_rk = [jnp.where(incl, gram[u][LANES:, LANES:], 0.0) for u in idx]
        inv = [eye + n_ab[u] for u in idx]
        pw = [_mm(n_ab[u], n_ab[u]) for u in idx]
        n_rounds = int(math.log2(C)) - 1
        for i in range(n_rounds):
            if i + 1 < n_rounds:
                both = [_mm(pw[u], jnp.concatenate([inv[u], pw[u]], axis=1)) for u in idx]
                inv = [inv[u] + both[u][:, :LANES] for u in idx]
                pw = [both[u][:, LANES:] for u in idx]
            else:
                inv = [inv[u] + _mm(pw[u], inv[u]) for u in idx]
        for u, (c, p) in enumerate(units):
            ar_s[c, p] = jnp.concatenate([a_t[u], r_t[u]], axis=0).astype(BF16)
            inv_s[c, p] = inv[u].astype(BF16)
            akv_s[c, p] = _mm(n_ak[u], jnp.concatenate([vc[u], vc[u]], axis=0))
            nrbk_s[c, p] = jnp.concatenate([n_rb[u], n_rk[u]], axis=1).astype(BF16)
            bk_s[c, p] = jnp.concatenate([kbc[u] * to_end[u], kc[u] * to_end[u]], axis=0).astype(BF16)
            dec_s[c, p] = jnp.broadcast_to(jnp.exp(total[u]), (8, LANES))
        return 0

    chunks_per_batch = n_chunks // n_batch

    def state_body(step, _):
        units = [(b, b * chunks_per_batch + step, p) for b in range(n_batch) for p in range(HEAD_PAIRS)]
        idx = range(len(units))
        rows = [pl.ds(pl.multiple_of(c * C, C), C) for _, c, _ in units]
        ls = [lane_slices[p] for _, _, p in units]
        vc = [v_s[rows[u], ls[u]] for u in idx]
        st = [state_ref[b, p] for b, _, p in units]
        ah = [_mm_nt(ar_s[c, p], st[u]) for u, (_, c, p) in enumerate(units)]
        u_stack = [_mm(inv_s[c, p], jnp.concatenate([ah[u][:C], ah[u][:C]], axis=0) + akv_s[c, p])
                   for u, (_, c, p) in enumerate(units)]
        y_stack = [jnp.concatenate([ah[u][C:], ah[u][C:]], axis=0)
                   + _mm(nrbk_s[c, p], jnp.concatenate([u_stack[u], vc[u], vc[u]], axis=0))
                   for u, (_, c, p) in enumerate(units)]
        for u in idx:
            y_s[rows[u], ls[u]] = jnp.where(head0, y_stack[u][:C], y_stack[u][C:])
        uv_t = [jnp.concatenate([jnp.where(head0, u_stack[u][:C], u_stack[u][C:]), vc[u]], axis=0).T
                for u in idx]
        st_new = [st[u] * dec_s[c, p][0:1, :] + _mm(uv_t[u], bk_s[c, p]) for u, (_, c, p) in enumerate(units)]
        for u, (b, _, p) in enumerate(units):
            state_ref[b, p] = jnp.where(same_head, st_new[u], 0.0)
        return 0

    chunk_group = 4 if n_chunks % 4 == 0 else (2 if n_chunks % 2 == 0 else 1)
    lax.fori_loop(0, n_chunks // chunk_group, intra_body, 0)
    lax.fori_loop(0, chunks_per_batch, state_body, 0)

    for p in range(HEAD_PAIRS):
        ls = slice(p * LANES, (p + 1) * LANES)
        y = y_s[:, ls]
        mean = _head_sum(y, m2) * (1.0 / HEAD_DIM)
        d = y - mean
        var = _head_sum(d * d, m2) * (1.0 / HEAD_DIM)
        yn = d * lax.rsqrt(var + GN_EPS) * ln_w[:, ls] + ln_b[:, ls]
        bonus = _head_sum(r_s[:, ls] * k_s[:, ls] * r_k[:, ls], m2) * v_s[:, ls]
        unstack(o_ref, (yn + bonus) * g[:, ls], ls)


def _rwkv(rwm, rwl, v_first, vec, w2p, a2p, g2p, v1p, v2p, B, S):
    T = rwm.shape[0]
    W = RWKV_WIDTH
    nb = 4 if B % 4 == 0 else (2 if B % 2 == 0 else 1)
    tb = _tile(S, 512 // nb)
    n_chunks = nb * tb // CHUNK
    has_vres = v_first is not None
    stack = lambda a: a.reshape(B // nb, nb, S, a.shape[-1])
    row = lambda cols: pl.BlockSpec((None, nb, tb, cols), lambda b, i: (b, 0, i, 0))
    full = lambda b, i: (0, 0)
    in_specs = [row(RWM_COLS), row(RWL_COLS)]
    args = [stack(rwm), stack(rwl)]
    if has_vres:
        in_specs.append(row(W))
        args.append(stack(v_first))
    in_specs += [pl.BlockSpec(vec.shape, full), pl.BlockSpec(w2p.shape, full),
                 pl.BlockSpec(a2p.shape, full), pl.BlockSpec(g2p.shape, full)]
    args += [vec, w2p, a2p, g2p]
    if has_vres:
        in_specs += [pl.BlockSpec(v1p.shape, full), pl.BlockSpec(v2p.shape, full)]
        args += [v1p, v2p]
    out_specs = [row(W)]
    out_shape = [jax.ShapeDtypeStruct((B // nb, nb, S, W), BF16)]
    if not has_vres:
        out_specs.append(row(W))
        out_shape.append(jax.ShapeDtypeStruct((B // nb, nb, S, W), F32))
    outs = pl.pallas_call(
        functools.partial(_rwkv_kernel, has_vres=has_vres, n_chunks=n_chunks, n_batch=nb),
        grid=(B // nb, S // tb),
        in_specs=in_specs,
        out_specs=out_specs,
        out_shape=out_shape,
        scratch_shapes=([pltpu.VMEM((nb, HEAD_PAIRS, LANES, LANES), F32)] + [pltpu.VMEM((nb * tb, W), F32)] * 7
                        + [pltpu.VMEM((n_chunks, HEAD_PAIRS, 2 * CHUNK, LANES), BF16)] * 3
                        + [pltpu.VMEM((n_chunks, HEAD_PAIRS, 2 * CHUNK, 2 * LANES), BF16),
                           pltpu.VMEM((n_chunks, HEAD_PAIRS, 2 * CHUNK, LANES), F32),
                           pltpu.VMEM((n_chunks, HEAD_PAIRS, 8, LANES), F32)]),
        compiler_params=_cparams(("arbitrary", "arbitrary")),
        name="rwkv7",
    )(*args)
    outs = [o.reshape(T, W) for o in outs]
    return (outs[0], v_first) if has_vres else (outs[0], outs[1])


TOKEN_TILE = (8, LANES)


def _store_token_tiles(ref, x):
    n = x.shape[0]
    for s in range(TOKEN_TILE[0]):
        ref[pl.ds(s, n, stride=TOKEN_TILE[0]), :] = x[:, s * LANES:(s + 1) * LANES]


def _load_token_tiles(ref, n, start=0):
    return jnp.concatenate([ref[pl.ds(start + s, n, stride=TOKEN_TILE[0]), :] for s in range(TOKEN_TILE[0])],
                           axis=1)


def _outproj_kernel(*refs, moe):
    if moe:
        (x_ref, att_ref, rw_ref, mod_ref, g_ref, w_ref, rw_w_ref, rw_b_ref,
         x1_ref, h_ref, info_ref) = refs
    else:
        x_ref, att_ref, rw_ref, mod_ref, g_ref, w_ref, x1_ref, h_ref = refs
    mix = (jnp.dot(att_ref[...], w_ref[0, :ATT_WIDTH, :], preferred_element_type=F32)
           + jnp.dot(rw_ref[...], w_ref[0, ATT_WIDTH:, :], preferred_element_type=F32))
    x1 = x_ref[...] + mod_ref[0, 2:3, :] * mix
    x1_ref[...] = x1
    h = _rms_mod(x1, g_ref[...], mod_ref[0, 4:5, :], mod_ref[0, 3:4, :])
    if moe:
        _store_token_tiles(h_ref, h)
    else:
        h_ref[...] = h.astype(h_ref.dtype)
    if moe:
        h_hi, h_lo = _split2(h)
        w_hi, w_lo = _split2(rw_w_ref[...])
        logits = (jnp.dot(h_hi, w_hi, preferred_element_type=F32)
                  + jnp.dot(h_hi, w_lo, preferred_element_type=F32)
                  + jnp.dot(h_lo, w_hi, preferred_element_type=F32)) + rw_b_ref[...]
        lane = lax.broadcasted_iota(jnp.int32, logits.shape, 1)
        logits = jnp.where(lane < N_EXPERTS, logits, -jnp.inf)
        m1 = jnp.max(logits, axis=-1, keepdims=True)
        i1 = jnp.min(jnp.where(logits == m1, lane, LANES), axis=-1, keepdims=True)
        rest = jnp.where(lane == i1, -jnp.inf, logits)
        m2 = jnp.max(rest, axis=-1, keepdims=True)
        i2 = jnp.min(jnp.where(rest == m2, lane, LANES), axis=-1, keepdims=True)
        e = jnp.exp(m2 - m1)
        g1 = 1.0 / (1.0 + e)
        g2 = e / (1.0 + e)
        info = jnp.where(lane == 0, i1.astype(F32),
                         jnp.where(lane == 1, i2.astype(F32),
                                   jnp.where(lane == 2, g1, jnp.where(lane == 3, g2, 0.0))))
        info_ref[...] = info


def _outproj(x, att, rw, mod, g, w_out_b, B, S, l, router=None):
    T, D = x.shape
    tm = _tile(S, 512)
    nS = S // tm
    moe = router is not None
    row = lambda b, i: (b * nS + i, 0)
    full = lambda b, i: (0, 0)
    in_specs = [pl.BlockSpec((tm, D), row), pl.BlockSpec((tm, ATT_WIDTH), row),
                pl.BlockSpec((tm, RWKV_WIDTH), row),
                pl.BlockSpec((1, 6, D), lambda b, i: (l * B + b, 0, 0)),
                pl.BlockSpec((1, D), full),
                pl.BlockSpec((1, D, D), lambda b, i: (l, 0, 0))]
    args = [x, att, rw, mod, g, w_out_b]
    if moe:
        out_specs = [pl.BlockSpec((tm, D), row), pl.BlockSpec((tm * TOKEN_TILE[0], LANES), row)]
        out_shape = [jax.ShapeDtypeStruct((T, D), F32), jax.ShapeDtypeStruct((T * TOKEN_TILE[0], LANES), F32)]
    else:
        out_specs = [pl.BlockSpec((tm, D), row), pl.BlockSpec((tm, D), row)]
        out_shape = [jax.ShapeDtypeStruct((T, D), F32), jax.ShapeDtypeStruct((T, D), BF16)]
    if moe:
        in_specs += [pl.BlockSpec((D, LANES), full), pl.BlockSpec((1, LANES), full)]
        args += list(router)
        out_specs.append(pl.BlockSpec((tm, LANES), row))
        out_shape.append(jax.ShapeDtypeStruct((T, LANES), F32))
    return pl.pallas_call(
        functools.partial(_outproj_kernel, moe=moe),
        grid=(B, nS),
        in_specs=in_specs,
        out_specs=out_specs,
        out_shape=out_shape,
        compiler_params=_cparams(("arbitrary", "arbitrary")),
        name="outproj",
    )(*args)


def _ffn_kernel(x_ref, h_ref, mod_ref, wg_ref, wu_ref, wd_ref, o_ref, acc_ref):
    f = pl.program_id(1)

    @pl.when(f == 0)
    def _():
        acc_ref[...] = jnp.zeros_like(acc_ref)

    hb = h_ref[...]
    gate = jnp.dot(hb, wg_ref[0], preferred_element_type=F32)
    up = jnp.dot(hb, wu_ref[0], preferred_element_type=F32)
    mid = (_silu(gate) * up).astype(BF16)
    acc_ref[...] += jnp.dot(mid, wd_ref[0], preferred_element_type=F32)

    @pl.when(f == pl.num_programs(1) - 1)
    def _():
        o_ref[...] = x_ref[...] + mod_ref[0, 5:6, :] * acc_ref[...]


def _ffn_dense(x1, h, mod, wg, wu, wd, B, S, l, li):
    T, D = x1.shape
    F = wg.shape[-1]
    tm = _tile(S, 512)
    tf = _tile(F, 1408)
    per_seq = S // tm
    return pl.pallas_call(
        _ffn_kernel,
        grid=(T // tm, F // tf),
        in_specs=[pl.BlockSpec((tm, D), lambda i, f: (i, 0)),
                  pl.BlockSpec((tm, D), lambda i, f: (i, 0)),
                  pl.BlockSpec((1, 6, D), lambda i, f: (l * B + i // per_seq, 0, 0)),
                  pl.BlockSpec((1, D, tf), lambda i, f: (li, 0, f)),
                  pl.BlockSpec((1, D, tf), lambda i, f: (li, 0, f)),
                  pl.BlockSpec((1, tf, D), lambda i, f: (li, f, 0))],
        out_specs=pl.BlockSpec((tm, D), lambda i, f: (i, 0)),
        out_shape=jax.ShapeDtypeStruct((T, D), F32),
        scratch_shapes=[pltpu.VMEM((tm, D), F32)],
        compiler_params=_cparams(("arbitrary", "arbitrary")),
        name="ffn_dense",
    )(x1, h, mod, wg, wu, wd)


def _experts_kernel(te_ref, nu_ref, tok_ref, h_ref, wg_ref, wu_ref, wd_ref, y_ref,
                    land_ref, xb_ref, acc_ref, sem):
    i = pl.program_id(0)
    f = pl.program_id(1)
    tm = xb_ref.shape[0]
    rows_per_token = TOKEN_TILE[0]
    n_used = nu_ref[0]

    def start_gather(tile):
        base = tile * tm

        def body(j, _):
            src = pl.multiple_of(tok_ref[base + j] * rows_per_token, rows_per_token)
            dst = pl.multiple_of(j * rows_per_token, rows_per_token)
            pltpu.make_async_copy(h_ref.at[pl.ds(src, rows_per_token)],
                                  land_ref.at[pl.ds(dst, rows_per_token)], sem).start()
            return 0

        lax.fori_loop(0, tm, body, 0, unroll=DMA_ISSUE_UNROLL)

    @pl.when(i < n_used)
    def _():
        @pl.when(f == 0)
        def _():
            @pl.when(i == 0)
            def _():
                start_gather(0)

            pltpu.make_async_copy(h_ref.at[pl.ds(0, tm * rows_per_token)], land_ref, sem).wait()
            xb_ref[...] = _load_token_tiles(land_ref, tm).astype(BF16)
            acc_ref[...] = jnp.zeros_like(acc_ref)

            @pl.when(i + 1 < n_used)
            def _():
                start_gather(i + 1)

        xb = xb_ref[...]
        gate = jnp.dot(xb, wg_ref[0, 0], preferred_element_type=F32)
        up = jnp.dot(xb, wu_ref[0, 0], preferred_element_type=F32)
        mid = (_silu(gate) * up).astype(BF16)
        acc_ref[...] += jnp.dot(mid, wd_ref[0, 0], preferred_element_type=F32)

        @pl.when(f == pl.num_programs(1) - 1)
        def _():
            _store_token_tiles(y_ref, acc_ref[...])

    @pl.when((i >= n_used) & (f == 0))
    def _():
        y_ref[...] = jnp.zeros_like(y_ref)


def _experts(h, row_tok, tile_e, n_used, wg, wu, wd, li, tm):
    n_rows = row_tok.shape[0]
    D = wg.shape[-2]
    rpt = TOKEN_TILE[0]
    assert h.shape[0] >= tm * rpt
    F = wg.shape[-1]
    tf = _tile(F, 1792)
    nf = F // tf

    def rows(i, f, te, nu, tok):
        return (i, 0)

    def wcol(i, f, te, nu, tok):
        return (li, te[jnp.minimum(i, nu[0] - 1)], 0, jnp.where(i < nu[0], f, nf - 1))

    def wrow(i, f, te, nu, tok):
        return (li, te[jnp.minimum(i, nu[0] - 1)], jnp.where(i < nu[0], f, nf - 1), 0)

    return pl.pallas_call(
        _experts_kernel,
        grid_spec=pltpu.PrefetchScalarGridSpec(
            num_scalar_prefetch=3,
            grid=(n_rows // tm, nf),
            in_specs=[pl.BlockSpec(memory_space=pl.ANY),
                      pl.BlockSpec((1, 1, D, tf), wcol),
                      pl.BlockSpec((1, 1, D, tf), wcol),
                      pl.BlockSpec((1, 1, tf, D), wrow)],
            out_specs=pl.BlockSpec((tm * rpt, LANES), rows),
            scratch_shapes=[pltpu.VMEM((tm * rpt, LANES), F32), pltpu.VMEM((tm, D), BF16),
                            pltpu.VMEM((tm, D), F32), pltpu.SemaphoreType.DMA(())]),
        out_shape=jax.ShapeDtypeStruct((n_rows * rpt, LANES), F32),
        compiler_params=_cparams(("arbitrary", "arbitrary")),
        name="moe_experts",
    )(tile_e, n_used, row_tok, h, wg, wu, wd)


def _combine_kernel(pos_ref, y_ref, x_ref, info_ref, mod_ref, fg_ref, o_ref, buf, sem, *, tc, final):
    i = pl.program_id(0)
    slot = i % 2
    rows_per_token = TOKEN_TILE[0]

    def start_gather(tile, dst_slot):
        base = tile * (2 * tc)

        def body(j, _):
            src = pl.multiple_of(pos_ref[base + j] * rows_per_token, rows_per_token)
            dst = pl.multiple_of(j * rows_per_token, rows_per_token)
            pltpu.make_async_copy(y_ref.at[pl.ds(src, rows_per_token)],
                                  buf.at[dst_slot, pl.ds(dst, rows_per_token)], sem.at[dst_slot]).start()
            return 0

        lax.fori_loop(0, 2 * tc, body, 0, unroll=DMA_ISSUE_UNROLL)

    @pl.when(i == 0)
    def _():
        start_gather(0, 0)

    @pl.when(i + 1 < pl.num_programs(0))
    def _():
        start_gather(i + 1, 1 - slot)

    pltpu.make_async_copy(y_ref.at[pl.ds(0, 2 * tc * rows_per_token)], buf.at[slot], sem.at[slot]).wait()
    info = info_ref[...]
    ff = (info[:, 2:3] * _load_token_tiles(buf.at[slot], tc)
          + info[:, 3:4] * _load_token_tiles(buf.at[slot], tc, tc * rows_per_token))
    out = x_ref[...] + mod_ref[0, 5:6, :] * ff
    if final:
        ms = jnp.mean(out * out, axis=-1, keepdims=True)
        out = out * lax.rsqrt(ms + RMS_EPS) * fg_ref[...]
    o_ref[...] = out


def _combine(pos, y_rows, x1, info, mod, final_g, B, S, l, final):
    T, D = x1.shape
    tc = _tile(S, 256)
    per_seq = S // tc
    return pl.pallas_call(
        functools.partial(_combine_kernel, tc=tc, final=final),
        grid_spec=pltpu.PrefetchScalarGridSpec(
            num_scalar_prefetch=1,
            grid=(T // tc,),
            in_specs=[pl.BlockSpec(memory_space=pl.ANY),
                      pl.BlockSpec((tc, D), lambda i, pos: (i, 0)),
                      pl.BlockSpec((tc, LANES), lambda i, pos: (i, 0)),
                      pl.BlockSpec((1, 6, D), lambda i, pos: (l * B + i // per_seq, 0, 0)),
                      pl.BlockSpec((1, D), lambda i, pos: (0, 0))],
            out_specs=pl.BlockSpec((tc, D), lambda i, pos: (i, 0)),
            scratch_shapes=[pltpu.VMEM((2, 2 * tc * TOKEN_TILE[0], LANES), F32), pltpu.SemaphoreType.DMA((2,))]),
        out_shape=jax.ShapeDtypeStruct((T, D), F32),
        compiler_params=_cparams(("arbitrary",)),
        name="moe_combine",
    )(pos, y_rows, x1, info, mod, final_g)


def _moe(x1, h, info, mod, wg, wu, wd, final_g, B, S, l, li, final):
    T, D = x1.shape
    TK = T * TOP_K
    tm = _tile(TK // N_EXPERTS, 512)
    n_rows = TK + (N_EXPERTS - 1) * tm
    flat_e = info[:, :TOP_K].astype(jnp.int32).reshape(TK)
    onehot = (flat_e[:, None] == jnp.arange(N_EXPERTS, dtype=jnp.int32)[None, :]).astype(jnp.int32)
    csum = jnp.cumsum(onehot, axis=0)
    counts = csum[-1]
    rank = jnp.sum(onehot * csum, axis=1) - 1
    padded = (counts + tm - 1) // tm * tm
    pad_ends = jnp.cumsum(padded)
    pad_starts = pad_ends - padded
    dest = pad_starts[flat_e] + rank
    row_tok = jnp.zeros((n_rows,), jnp.int32).at[dest].set(jnp.arange(TK, dtype=jnp.int32) // TOP_K)
    tile_start = jnp.arange(n_rows // tm, dtype=jnp.int32) * tm
    tile_e = jnp.minimum(jnp.searchsorted(pad_ends, tile_start, side='right'), N_EXPERTS - 1).astype(jnp.int32)
    n_used_rows = pad_ends[-1:].astype(jnp.int32)
    n_used_tiles = n_used_rows // tm
    tc = _tile(S, 256)
    pos = dest.reshape(T // tc, tc, TOP_K).transpose(0, 2, 1).reshape(TK).astype(jnp.int32)

    y_rows = _experts(h, row_tok, tile_e, n_used_tiles, wg, wu, wd, li, tm)
    return _combine(pos, y_rows, x1, info, mod, final_g, B, S, l, final)


def kernel(x, c, w_ada, b_ada, norm1_g, norm2_g, final_g, w_in, w_out, rel_bias, rwkv_mu, rwkv_w0, rwkv_w2, rwkv_a0, rwkv_a2, rwkv_g2, rwkv_k_k, rwkv_k_a, rwkv_r_k, rwkv_ln_w, rwkv_ln_b, rwkv_v0, rwkv_v1, rwkv_v2, ffn_w_gate, ffn_w_up, ffn_w_down, moe_router_w, moe_router_b, moe_w_gate, moe_w_up, moe_w_down):
    B, S, D = x.shape
    L = w_in.shape[0]
    T = B * S
    W = RWKV_WIDTH

    mod = _ada_mod(c, w_ada, b_ada).reshape(L * B, 6, D)
    w_in_b = jnp.pad(w_in, ((0, 0), (0, 0), (0, N_IN_PAD - N_IN))).astype(BF16)
    w_out_b = w_out.astype(BF16)
    ffn_w = [w.astype(BF16) for w in (ffn_w_gate, ffn_w_up, ffn_w_down)]
    moe_w = [w.astype(BF16) for w in (moe_w_gate, moe_w_up, moe_w_down)]
    mu_p = jnp.pad(rwkv_mu, ((0, 0), (0, RWM_COLS + RWL_COLS - rwkv_mu.shape[1])))
    att_tables = _att_tables(rel_bias, S)
    zeros_w = jnp.zeros((1, W), F32)
    final_g2 = final_g.reshape(1, D)

    xf = x.reshape(T, D)
    v_first = None
    for l in range(L):
        qkv, qkv_mid, qkv_far, rwm, rwl = _inproj(xf, mod, norm1_g[l].reshape(1, D), w_in_b,
                                                   mu_p[l].reshape(1, -1), B, S, l)
        att = _attention(qkv, qkv_mid, qkv_far, att_tables, B, S)
        v0 = rwkv_v0[l - 1].reshape(1, W) if l > 0 else zeros_w
        vec = jnp.concatenate([rwkv_w0[l].reshape(1, W), rwkv_a0[l].reshape(1, W), rwkv_k_k[l].reshape(1, W),
                               rwkv_k_a[l].reshape(1, W), rwkv_r_k[l].reshape(1, W), rwkv_ln_w[l].reshape(1, W),
                               rwkv_ln_b[l].reshape(1, W), v0], axis=0)
        w2p = jnp.pad(rwkv_w2[l], ((0, LANES - D_DECAY_LORA), (0, 0)))
        a2p = jnp.pad(rwkv_a2[l], ((D_DECAY_LORA, 0), (0, 0)))
        g2p = jnp.pad(rwkv_g2[l], ((0, RWL_COLS - LANES - D_GATE_LORA), (0, 0)))
        if l > 0:
            v1p = jnp.pad(rwkv_v1[l - 1], ((0, 0), (0, LANES - D_MV_LORA)))
            v2p = jnp.pad(rwkv_v2[l - 1], ((0, LANES - D_MV_LORA), (0, 0)))
        else:
            v1p = v2p = None
        rw, v_first = _rwkv(rwm, rwl, v_first, vec, w2p, a2p, g2p, v1p, v2p, B, S)
        li = l // 2
        if l % 2 == 0:
            x1, h = _outproj(xf, att, rw, mod, norm2_g[l].reshape(1, D), w_out_b, B, S, l)
            xf = _ffn_dense(x1, h, mod, *ffn_w, B, S, l, li)
        else:
            router = (jnp.pad(moe_router_w[li], ((0, 0), (0, LANES - N_EXPERTS))),
                      jnp.pad(moe_router_b[li], (0, LANES - N_EXPERTS)).reshape(1, LANES))
            x1, h, info = _outproj(xf, att, rw, mod, norm2_g[l].reshape(1, D), w_out_b, B, S, l, router)
            xf = _moe(x1, h, info, mod, *moe_w, final_g2, B, S, l, li,
                      final=(l == L - 1))
    if L % 2 == 1:
        raise NotImplementedError("final norm is fused into the last (expert) layer")
    return xf.reshape(B, S, D)
```

```python
import functools
import math

import numpy as np
import jax
import jax.numpy as jnp
from jax import lax
from jax.experimental import pallas as pl
from jax.experimental.pallas import tpu as pltpu

F32 = jnp.float32
BF16 = jnp.bfloat16

D_MODEL = 1024
HEAD_DIM = 64
ATT_WIDTH = 512
RWKV_WIDTH = 512
N_HEADS = 8
HEAD_PAIRS = 4
LANES = 128
DILATED_GROUPS = ((128, 1), (512, 4), (2048, 16))
N_BUCKETS = 32
MAX_DISTANCE = 2048
NEG_INF = -1e30
D_DECAY_LORA = 64
D_AAA_LORA = 64
D_MV_LORA = 32
D_GATE_LORA = 160
N_IN = 3 * ATT_WIDTH + 3 * RWKV_WIDTH + D_DECAY_LORA + D_AAA_LORA + D_GATE_LORA
QKV_COLS = 3 * ATT_WIDTH
RWM_COLS = 3 * RWKV_WIDTH
RWL_COLS = 384
N_IN_PAD = QKV_COLS + RWM_COLS + RWL_COLS
GN_EPS = HEAD_DIM * 1e-5
RMS_EPS = 1e-6
N_EXPERTS = 8
TOP_K = 2
CHUNK = 64
VMEM_LIMIT = 56 * 1024 * 1024
LOG2_E = math.log2(math.e)
Q_SCALE = LOG2_E / math.sqrt(HEAD_DIM)
DMA_ISSUE_UNROLL = 8


def _cparams(sem):
    return pltpu.CompilerParams(dimension_semantics=sem, vmem_limit_bytes=VMEM_LIMIT)


def _tile(n, pref):
    t = min(n, pref)
    while n % t:
        t //= 2
    return t


def _mm(a, b):
    return jnp.dot(a.astype(BF16), b.astype(BF16), preferred_element_type=F32)


def _mm_nt(a, b):
    return lax.dot_general(a.astype(BF16), b.astype(BF16), (((1,), (1,)), ((), ())),
                           preferred_element_type=F32)


def _sigmoid(x):
    return 1.0 / (1.0 + jnp.exp(-x))


def _silu(x):
    return x * _sigmoid(x)


def _ada_kernel(c_ref, w_ref, b_ref, o_ref):
    ca = _silu(c_ref[...])
    o_ref[0] = _mm(ca, w_ref[0]) + b_ref[0]


def _ada_mod(c, w_ada, b_ada):
    L, D, N = w_ada.shape
    B = c.shape[0]
    tn = _tile(N, 1536)
    return pl.pallas_call(
        _ada_kernel,
        grid=(L, N // tn),
        in_specs=[pl.BlockSpec((B, D), lambda l, j: (0, 0)),
                  pl.BlockSpec((1, D, tn), lambda l, j: (l, 0, j)),
                  pl.BlockSpec((1, 1, tn), lambda l, j: (l, 0, j))],
        out_specs=pl.BlockSpec((1, B, tn), lambda l, j: (l, 0, j)),
        out_shape=jax.ShapeDtypeStruct((L, B, N), F32),
        compiler_params=_cparams(("arbitrary", "arbitrary")),
        name="ada_mod",
    )(c, w_ada, b_ada.reshape(L, 1, N))


def _rms_mod(xf, g, scale, shift):
    ms = jnp.mean(xf * xf, axis=-1, keepdims=True)
    return (xf * lax.rsqrt(ms + RMS_EPS) * g) * (1.0 + scale) + shift


def _inproj_kernel(x_ref, mod_ref, g_ref, w_ref, mu_ref, qkv_ref, qkv_mid_ref, qkv_far_ref, rwm_ref, rwl_ref,
                   carry_ref, qkv_scr):
    i = pl.program_id(1)
    tm = x_ref.shape[0]
    h = _rms_mod(x_ref[...], g_ref[...], mod_ref[0, 1:2, :], mod_ref[0, 0:1, :])
    acc = jnp.dot(h.astype(BF16), w_ref[0], preferred_element_type=F32)
    for c in range(QKV_COLS // LANES):
        cols = slice(c * LANES, (c + 1) * LANES)
        blk = acc[:, cols] * Q_SCALE if (c + 1) * LANES <= ATT_WIDTH else acc[:, cols]
        qkv_ref[:, cols] = blk.astype(BF16)
        qkv_scr[c] = blk
        for out_ref in (qkv_mid_ref, qkv_far_ref):
            d, rows = out_ref.shape[0], out_ref.shape[1]
            for r in range(d):
                out_ref[r, :, cols] = qkv_scr[c, pl.ds(r, rows, stride=d), :].astype(BF16)
    p = acc[:, QKV_COLS:]
    first = jnp.where(i == 0, 0.0, carry_ref[...])
    row = lax.broadcasted_iota(jnp.int32, p.shape, 0)
    prev = jnp.where(row == 0, first, pltpu.roll(p, 1, 0))
    carry_ref[...] = p[tm - 1:tm, :]
    pm = p + mu_ref[...] * (prev - p)
    rwm_ref[...] = pm[:, :RWM_COLS]
    rwl_ref[...] = pm[:, RWM_COLS:]


def _inproj(x, mod, g, w_in_l, mu_l, B, S, l):
    T, D = x.shape
    tm = _tile(S, 512)
    nS = S // tm
    d2, d3 = DILATED_GROUPS[1][1], DILATED_GROUPS[2][1]
    return pl.pallas_call(
        _inproj_kernel,
        grid=(B, nS),
        in_specs=[pl.BlockSpec((tm, D), lambda b, i: (b * nS + i, 0)),
                  pl.BlockSpec((1, 6, D), lambda b, i: (l * B + b, 0, 0)),
                  pl.BlockSpec((1, D), lambda b, i: (0, 0)),
                  pl.BlockSpec((1, D, N_IN_PAD), lambda b, i: (l, 0, 0)),
                  pl.BlockSpec((1, RWM_COLS + RWL_COLS), lambda b, i: (0, 0))],
        out_specs=[pl.BlockSpec((tm, QKV_COLS), lambda b, i: (b * nS + i, 0)),
                   pl.BlockSpec((None, d2, tm // d2, QKV_COLS), lambda b, i: (b, 0, i, 0)),
                   pl.BlockSpec((None, d3, tm // d3, QKV_COLS), lambda b, i: (b, 0, i, 0)),
                   pl.BlockSpec((tm, RWM_COLS), lambda b, i: (b * nS + i, 0)),
                   pl.BlockSpec((tm, RWL_COLS), lambda b, i: (b * nS + i, 0))],
        out_shape=[jax.ShapeDtypeStruct((T, QKV_COLS), BF16),
                   jax.ShapeDtypeStruct((B, d2, S // d2, QKV_COLS), BF16),
                   jax.ShapeDtypeStruct((B, d3, S // d3, QKV_COLS), BF16),
                   jax.ShapeDtypeStruct((T, RWM_COLS), F32),
                   jax.ShapeDtypeStruct((T, RWL_COLS), F32)],
        scratch_shapes=[pltpu.VMEM((1, RWM_COLS + RWL_COLS), F32), pltpu.VMEM((QKV_COLS // LANES, tm, LANES), F32)],
        compiler_params=_cparams(("arbitrary", "arbitrary")),
        name="inproj",
    )(x, mod, g, w_in_l, mu_l)


def _t5_bucket(n):
    max_exact = N_BUCKETS // 2
    large = max_exact + (np.log(np.maximum(n, 1) / max_exact) / np.log(MAX_DISTANCE / max_exact)
                         * (N_BUCKETS - max_exact)).astype(np.int32)
    large = np.minimum(large, N_BUCKETS - 1)
    return np.where(n < max_exact, n, large).astype(np.int32)


ATT_BLOCK = 128
ATT_KEYS = 256


def _toeplitz(vec, c0, rows, cols):
    H, n = vec.shape
    period = rows + cols - 1
    w = jnp.concatenate([jnp.full((H, cols - 1), NEG_INF, F32), vec, jnp.full((H, rows), NEG_INF, F32)], axis=1)
    rw = w[:, ::-1]
    a = w.shape[1] - 1 - (c0 + cols - 1)
    z = jnp.concatenate([rw[:, a:a + cols], rw[:, a - (rows - 1):a]], axis=1)
    flat = jnp.tile(z, (1, rows))[:, :rows * (period - 1)]
    return flat.reshape(H, rows, period - 1)[..., :cols]


def _att_tables(rel_bias, S):
    (w1, d1), (w2, d2), (w3, d3) = DILATED_GROUPS
    assert d1 == 1 and w1 == ATT_KEYS - ATT_BLOCK and w2 // d2 == w1 and S % (d3 * 8) == 0
    off = np.arange(S)
    mult = np.zeros(S, np.int64)
    for w, d in DILATED_GROUPS:
        mult += ((off % d == 0) & (off <= w)).astype(np.int64)
    logm = np.where(mult > 0, np.log(np.maximum(mult, 1)), NEG_INF).astype(np.float32)
    per_off = (rel_bias[_t5_bucket(off)].T.astype(F32) + jnp.asarray(logm)[None, :]) * LOG2_E
    u2 = np.arange(S // d2)
    u3 = np.arange(S // d3)
    band = jnp.where(jnp.asarray(off <= w1)[None], per_off, NEG_INF)
    mid = jnp.where(jnp.asarray((u2 > w1 // d2) & (u2 <= w2 // d2))[None], per_off[:, ::d2], NEG_INF)
    far = jnp.where(jnp.asarray((u3 > w2 // d3) & (u3 <= w3 // d3))[None], per_off[:, ::d3], NEG_INF)
    shift = ATT_KEYS - ATT_BLOCK
    t_band = jnp.stack([_toeplitz(band, 0, ATT_BLOCK, ATT_KEYS), _toeplitz(band, shift, ATT_BLOCK, ATT_KEYS)], axis=1)
    t_mid = jnp.stack([_toeplitz(mid, 0, ATT_BLOCK, ATT_KEYS), _toeplitz(mid, shift, ATT_BLOCK, ATT_KEYS)], axis=1)
    t_far = _toeplitz(far, 0, S // d3, S // d3)
    return t_band, t_mid, t_far


def _attend(units):
    idx = range(len(units))
    heads = range(2)
    head0 = [lax.broadcasted_iota(jnp.int32, u[0].shape, 1) < HEAD_DIM for u in units]
    qh = [[jnp.where(head0[i], units[i][0], jnp.zeros_like(units[i][0])),
           jnp.where(head0[i], jnp.zeros_like(units[i][0]), units[i][0])] for i in idx]
    s = [[lax.dot_general(qh[i][h], units[i][1], (((1,), (1,)), ((), ())), preferred_element_type=F32)
          + units[i][3 + h] for h in heads] for i in idx]
    mx = [[jnp.max(s[i][h], axis=-1, keepdims=True) for h in heads] for i in idx]
    p = [[jnp.exp2(s[i][h] - mx[i][h]) for h in heads] for i in idx]
    ps = [[jnp.sum(p[i][h], axis=-1, keepdims=True) for h in heads] for i in idx]
    pv = [[jnp.dot(p[i][h].astype(BF16), units[i][2], preferred_element_type=F32) for h in heads] for i in idx]
    out = []
    for i in idx:
        shape = units[i][0].shape
        out.append((jnp.where(head0[i], jnp.broadcast_to(mx[i][0], shape), jnp.broadcast_to(mx[i][1], shape)),
                    jnp.where(head0[i], jnp.broadcast_to(ps[i][0], shape), jnp.broadcast_to(ps[i][1], shape)),
                    jnp.where(head0[i], pv[i][0], pv[i][1])))
    return out


def _attn_kernel(qa_ref, ka_ref, va_ref, qm_ref, km_ref, vm_ref, qf_ref, kf_ref, vf_ref,
                 tband_ref, tmid_ref, tfar_ref, o_ref, part_ref):
    R, K = ATT_BLOCK, ATT_KEYS
    S = qa_ref.shape[0]
    n_mid_res, n_mid = qm_ref.shape[0], qm_ref.shape[1]
    n_far_res, n_far = qf_ref.shape[0], qf_ref.shape[1]
    group = 8

    def window(blk):
        start = pl.multiple_of(jnp.maximum(blk * R - (K - R), 0), R)
        return pl.ds(pl.multiple_of(blk * R, R), R), pl.ds(start, K), jnp.minimum(blk, 1)

    def park(cls, rows, results):
        for nat, parts in zip(rows, results):
            for j in range(3):
                part_ref[cls, j, nat, :] = parts[j]

    def band_body(it, _):
        units, rows = [], []
        for g in range(group):
            qrows, krows, var = window(it * group + g)
            rows.append(qrows)
            units.append((qa_ref[qrows, :], ka_ref[krows, :], va_ref[krows, :],
                          tband_ref[0, var], tband_ref[1, var]))
        park(0, rows, _attend(units))
        return 0

    lax.fori_loop(0, S // (R * group), band_body, 0)

    blocks_per_step = group // n_mid_res

    def mid_body(it, _):
        units, rows = [], []
        for g in range(blocks_per_step):
            blk = it * blocks_per_step + g
            qrows, krows, var = window(blk)
            for r in range(n_mid_res):
                rows.append(pl.ds(blk * (R * n_mid_res) + r, R, stride=n_mid_res))
                units.append((qm_ref[r, qrows, :], km_ref[r, krows, :], vm_ref[r, krows, :],
                              tmid_ref[0, var], tmid_ref[1, var]))
        park(1, rows, _attend(units))
        return 0

    lax.fori_loop(0, n_mid // (R * blocks_per_step), mid_body, 0)

    def far_body(it, _):
        units, rows = [], []
        for g in range(group):
            r = it * group + g
            rows.append(pl.ds(r, n_far, stride=n_far_res))
            units.append((qf_ref[r], kf_ref[r], vf_ref[r], tfar_ref[0], tfar_ref[1]))
        park(2, rows, _attend(units))
        return 0

    lax.fori_loop(0, n_far_res // group, far_body, 0)

    def merge_body(i, _):
        rows = pl.ds(pl.multiple_of(i * R, R), R)
        m = [part_ref[c, 0, rows, :] for c in range(3)]
        top = jnp.maximum(jnp.maximum(m[0], m[1]), m[2])
        w = [jnp.exp2(m[c] - top) for c in range(3)]
        total = sum(w[c] * part_ref[c, 1, rows, :] for c in range(3))
        acc = sum(w[c] * part_ref[c, 2, rows, :] for c in range(3))
        o_ref[rows, :] = (acc / total).astype(o_ref.dtype)
        return 0

    lax.fori_loop(0, S // R, merge_body, 0)


def _attention(qkv, qkv_mid, qkv_far, tables, B, S):
    T = qkv.shape[0]
    t_band, t_mid, t_far = tables
    d2, d3 = DILATED_GROUPS[1][1], DILATED_GROUPS[2][1]
    assert S % (ATT_BLOCK * 8) == 0 and S // d2 >= ATT_KEYS
    nat = lambda c: pl.BlockSpec((S, LANES), lambda hp, b: (b, c * HEAD_PAIRS + hp))
    mid = lambda c: pl.BlockSpec((None, d2, S // d2, LANES), lambda hp, b: (b, 0, 0, c * HEAD_PAIRS + hp))
    far = lambda c: pl.BlockSpec((None, d3, S // d3, LANES), lambda hp, b: (b, 0, 0, c * HEAD_PAIRS + hp))
    return pl.pallas_call(
        _attn_kernel,
        grid=(HEAD_PAIRS, B),
        in_specs=[nat(0), nat(1), nat(2), mid(0), mid(1), mid(2), far(0), far(1), far(2),
                  pl.BlockSpec((2,) + t_band.shape[1:], lambda hp, b: (hp, 0, 0, 0)),
                  pl.BlockSpec((2,) + t_mid.shape[1:], lambda hp, b: (hp, 0, 0, 0)),
                  pl.BlockSpec((2,) + t_far.shape[1:], lambda hp, b: (hp, 0, 0))],
        out_specs=pl.BlockSpec((S, LANES), lambda hp, b: (b, hp)),
        out_shape=jax.ShapeDtypeStruct((T, ATT_WIDTH), BF16),
        scratch_shapes=[pltpu.VMEM((3, 3, S, LANES), F32)],
        compiler_params=_cparams(("arbitrary", "arbitrary")),
        name="dilated_attn",
    )(qkv, qkv, qkv, qkv_mid, qkv_mid, qkv_mid, qkv_far, qkv_far, qkv_far, t_band, t_mid, t_far)


def _split2(x):
    hi = x.astype(BF16)
    lo = (x - hi.astype(F32)).astype(BF16)
    return hi, lo


def _head_sum(x, m2):
    hi, lo = _split2(x)
    return jnp.dot(jnp.concatenate([hi, lo], axis=1), m2, preferred_element_type=F32)


def _rwkv_kernel(*refs, has_vres, n_chunks, n_batch):
    if has_vres:
        (rwm_ref, rwl_ref, vf_ref, vec_ref, w2_ref, a2_ref, g2_ref, v1_ref, v2_ref,
         o_ref, state_ref, r_s, k_s, v_s, lw_s, kk_s, kb_s, y_s,
         ar_s, inv_s, bk_s, nrbk_s, akv_s, dec_s) = refs
    else:
        (rwm_ref, rwl_ref, vec_ref, w2_ref, a2_ref, g2_ref,
         o_ref, vf_out_ref, state_ref, r_s, k_s, v_s, lw_s, kk_s, kb_s, y_s,
         ar_s, inv_s, bk_s, nrbk_s, akv_s, dec_s) = refs
    C = CHUNK
    W = RWKV_WIDTH
    rows_per_batch = rwm_ref.shape[1]

    def stacked(ref, cols=slice(None)):
        return jnp.concatenate([ref[b, :, cols] for b in range(n_batch)], axis=0)

    def unstack(ref, x, cols=slice(None)):
        for b in range(n_batch):
            ref[b, :, cols] = x[b * rows_per_batch:(b + 1) * rows_per_batch].astype(ref.dtype)

    @pl.when(pl.program_id(1) == 0)
    def _():
        state_ref[...] = jnp.zeros_like(state_ref)

    ri = lax.broadcasted_iota(jnp.int32, (2 * LANES, LANES), 0)
    ci = lax.broadcasted_iota(jnp.int32, (2 * LANES, LANES), 1)
    m2 = ((ri % LANES) // HEAD_DIM == ci // HEAD_DIM).astype(BF16)

    vec = vec_ref[...]
    w0, a0, k_k, k_a, r_k, ln_w, ln_b, v0 = [vec[i:i + 1, :] for i in range(8)]

    r = stacked(rwm_ref, slice(0, W))
    k = stacked(rwm_ref, slice(W, 2 * W))
    v = stacked(rwm_ref, slice(2 * W, 3 * W))
    lora = stacked(rwl_ref, slice(0, LANES))
    w_raw = w0 + _mm(jnp.tanh(lora), w2_ref[...])
    lw_s[...] = -math.exp(-0.5) * _sigmoid(w_raw)
    a = _sigmoid(a0 + _mm(lora, a2_ref[...]))
    g = _mm(_sigmoid(stacked(rwl_ref, slice(LANES, RWL_COLS))), g2_ref[...])
    if has_vres:
        mix = _sigmoid(v0 + _mm(_mm(v, v1_ref[...]), v2_ref[...]))
        v = v + (stacked(vf_ref) - v) * mix
    else:
        unstack(vf_out_ref, v)
    kk = k * k_k
    for p in range(HEAD_PAIRS):
        ls = slice(p * LANES, (p + 1) * LANES)
        kkp = kk[:, ls]
        kkp = kkp * jnp.minimum(lax.rsqrt(_head_sum(kkp * kkp, m2)), 1e12)
        kk_s[:, ls] = kkp
        kb_s[:, ls] = kkp * a[:, ls]
    kmod = k * (1.0 + (a - 1.0) * k_a)
    r_s[...] = r
    k_s[...] = kmod
    v_s[...] = v

    ri = lax.broadcasted_iota(jnp.int32, (LANES, LANES), 0)
    ci = lax.broadcasted_iota(jnp.int32, (LANES, LANES), 1)
    same_head = (ri // C) == (ci // C)
    strict = same_head & ((ri % C) > (ci % C))
    incl = same_head & ((ri % C) >= (ci % C))
    eye = (ri == ci).astype(F32)
    lane = lax.broadcasted_iota(jnp.int32, (C, LANES), 1)
    head0 = lane < HEAD_DIM
    tri_r = lax.broadcasted_iota(jnp.int32, (C, 3 * C), 0)
    tri_c = lax.broadcasted_iota(jnp.int32, (C, 3 * C), 1)
    tri3 = ((tri_c % C) <= tri_r).astype(BF16)

    lane_slices = [slice(p * LANES, (p + 1) * LANES) for p in range(HEAD_PAIRS)]

    def cumsum(x):
        hi = x.astype(BF16)
        rem = x - hi.astype(F32)
        mid = rem.astype(BF16)
        lo = (rem - mid.astype(F32)).astype(BF16)
        return jnp.dot(tri3, jnp.concatenate([hi, mid, lo], axis=0), preferred_element_type=F32)

    def per_head_rows(x, y):
        zero = jnp.zeros_like(x)
        return jnp.concatenate([jnp.where(head0, x, zero), jnp.where(head0, zero, x),
                                jnp.where(head0, y, zero), jnp.where(head0, zero, y)], axis=0)

    def intra_body(it, _):
        units = [(it * chunk_group + g, p) for g in range(chunk_group) for p in range(HEAD_PAIRS)]
        idx = range(len(units))
        rows = [pl.ds(pl.multiple_of(c * C, C), C) for c, _ in units]
        ls = [lane_slices[p] for _, p in units]
        rc = [r_s[rows[u], ls[u]] for u in idx]
        kc = [k_s[rows[u], ls[u]] for u in idx]
        vc = [v_s[rows[u], ls[u]] for u in idx]
        lw = [lw_s[rows[u], ls[u]] for u in idx]
        kkc = [kk_s[rows[u], ls[u]] for u in idx]
        kbc = [kb_s[rows[u], ls[u]] for u in idx]
        cum = [cumsum(lw[u]) for u in idx]
        total = [cum[u][C - 1:C, :] for u in idx]
        g_inv = [jnp.exp(-cum[u]) for u in idx]
        a_t = [-kkc[u] * jnp.exp(cum[u] - lw[u]) for u in idx]
        r_t = [rc[u] * jnp.exp(cum[u]) for u in idx]
        b_t = [kbc[u] * g_inv[u] for u in idx]
        k_t = [kc[u] * g_inv[u] for u in idx]
        to_end = [jnp.exp(total[u] - cum[u]) for u in idx]
        gram = [_mm_nt(per_head_rows(a_t[u], r_t[u]), per_head_rows(b_t[u], k_t[u])) for u in idx]
        n_ab = [jnp.where(strict, gram[u][:LANES, :LANES], 0.0) for u in idx]
        n_ak = [jnp.where(strict, gram[u][:LANES, LANES:], 0.0) for u in idx]
        n_rb = [jnp.where(incl, gram[u][LANES:, :LANES], 0.0) for u in idx]
        n_rk = [jnp.where(incl, gram[u][LANES:, LANES:], 0.0) for u in idx]
        inv = [eye + n_ab[u] for u in idx]
        pw = [_mm(n_ab[u], n_ab[u]) for u in idx]
        n_rounds = int(math.log2(C)) - 1
        for i in range(n_rounds):
            if i + 1 < n_rounds:
                both = [_mm(pw[u], jnp.concatenate([inv[u], pw[u]], axis=1)) for u in idx]
                inv = [inv[u] + both[u][:, :LANES] for u in idx]
                pw = [both[u][:, LANES:] for u in idx]
            else:
                inv = [inv[u] + _mm(pw[u], inv[u]) for u in idx]
        for u, (c, p) in enumerate(units):
            ar_s[c, p] = jnp.concatenate([a_t[u], r_t[u]], axis=0).astype(BF16)
            inv_s[c, p] = inv[u].astype(BF16)
            akv_s[c, p] = _mm(n_ak[u], jnp.concatenate([vc[u], vc[u]], axis=0))
            nrbk_s[c, p] = jnp.concatenate([n_rb[u], n_rk[u]], axis=1).astype(BF16)
            bk_s[c, p] = jnp.concatenate([kbc[u] * to_end[u], kc[u] * to_end[u]], axis=0).astype(BF16)
            dec_s[c, p] = jnp.broadcast_to(jnp.exp(total[u]), (8, LANES))
        return 0

    chunks_per_batch = n_chunks // n_batch

    def state_body(step, _):
        units = [(b, b * chunks_per_batch + step, p) for b in range(n_batch) for p in range(HEAD_PAIRS)]
        idx = range(len(units))
        rows = [pl.ds(pl.multiple_of(c * C, C), C) for _, c, _ in units]
        ls = [lane_slices[p] for _, _, p in units]
        vc = [v_s[rows[u], ls[u]] for u in idx]
        st = [state_ref[b, p] for b, _, p in units]
        ah = [_mm_nt(ar_s[c, p], st[u]) for u, (_, c, p) in enumerate(units)]
        u_stack = [_mm(inv_s[c, p], jnp.concatenate([ah[u][:C], ah[u][:C]], axis=0) + akv_s[c, p])
                   for u, (_, c, p) in enumerate(units)]
        y_stack = [jnp.concatenate([ah[u][C:], ah[u][C:]], axis=0)
                   + _mm(nrbk_s[c, p], jnp.concatenate([u_stack[u], vc[u], vc[u]], axis=0))
                   for u, (_, c, p) in enumerate(units)]
        for u in idx:
            y_s[rows[u], ls[u]] = jnp.where(head0, y_stack[u][:C], y_stack[u][C:])
        uv_t = [jnp.concatenate([jnp.where(head0, u_stack[u][:C], u_stack[u][C:]), vc[u]], axis=0).T
                for u in idx]
        st_new = [st[u] * dec_s[c, p][0:1, :] + _mm(uv_t[u], bk_s[c, p]) for u, (_, c, p) in enumerate(units)]
        for u, (b, _, p) in enumerate(units):
            state_ref[b, p] = jnp.where(same_head, st_new[u], 0.0)
        return 0

    chunk_group = 4 if n_chunks % 4 == 0 else (2 if n_chunks % 2 == 0 else 1)
    lax.fori_loop(0, n_chunks // chunk_group, intra_body, 0)
    lax.fori_loop(0, chunks_per_batch, state_body, 0)

    for p in range(HEAD_PAIRS):
        ls = slice(p * LANES, (p + 1) * LANES)
        y = y_s[:, ls]
        mean = _head_sum(y, m2) * (1.0 / HEAD_DIM)
        d = y - mean
        var = _head_sum(d * d, m2) * (1.0 / HEAD_DIM)
        yn = d * lax.rsqrt(var + GN_EPS) * ln_w[:, ls] + ln_b[:, ls]
        bonus = _head_sum(r_s[:, ls] * k_s[:, ls] * r_k[:, ls], m2) * v_s[:, ls]
        unstack(o_ref, (yn + bonus) * g[:, ls], ls)


def _rwkv(rwm, rwl, v_first, vec, w2p, a2p, g2p, v1p, v2p, B, S):
    T = rwm.shape[0]
    W = RWKV_WIDTH
    nb = 4 if B % 4 == 0 else (2 if B % 2 == 0 else 1)
    tb = _tile(S, 512 // nb)
    n_chunks = nb * tb // CHUNK
    has_vres = v_first is not None
    stack = lambda a: a.reshape(B // nb, nb, S, a.shape[-1])
    row = lambda cols: pl.BlockSpec((None, nb, tb, cols), lambda b, i: (b, 0, i, 0))
    full = lambda b, i: (0, 0)
    in_specs = [row(RWM_COLS), row(RWL_COLS)]
    args = [stack(rwm), stack(rwl)]
    if has_vres:
        in_specs.append(row(W))
        args.append(stack(v_first))
    in_specs += [pl.BlockSpec(vec.shape, full), pl.BlockSpec(w2p.shape, full),
                 pl.BlockSpec(a2p.shape, full), pl.BlockSpec(g2p.shape, full)]
    args += [vec, w2p, a2p, g2p]
    if has_vres:
        in_specs += [pl.BlockSpec(v1p.shape, full), pl.BlockSpec(v2p.shape, full)]
        args += [v1p, v2p]
    out_specs = [row(W)]
    out_shape = [jax.ShapeDtypeStruct((B // nb, nb, S, W), BF16)]
    if not has_vres:
        out_specs.append(row(W))
        out_shape.append(jax.ShapeDtypeStruct((B // nb, nb, S, W), F32))
    outs = pl.pallas_call(
        functools.partial(_rwkv_kernel, has_vres=has_vres, n_chunks=n_chunks, n_batch=nb),
        grid=(B // nb, S // tb),
        in_specs=in_specs,
        out_specs=out_specs,
        out_shape=out_shape,
        scratch_shapes=([pltpu.VMEM((nb, HEAD_PAIRS, LANES, LANES), F32)] + [pltpu.VMEM((nb * tb, W), F32)] * 7
                        + [pltpu.VMEM((n_chunks, HEAD_PAIRS, 2 * CHUNK, LANES), BF16)] * 3
                        + [pltpu.VMEM((n_chunks, HEAD_PAIRS, 2 * CHUNK, 2 * LANES), BF16),
                           pltpu.VMEM((n_chunks, HEAD_PAIRS, 2 * CHUNK, LANES), F32),
                           pltpu.VMEM((n_chunks, HEAD_PAIRS, 8, LANES), F32)]),
        compiler_params=_cparams(("arbitrary", "arbitrary")),
        name="rwkv7",
    )(*args)
    outs = [o.reshape(T, W) for o in outs]
    return (outs[0], v_first) if has_vres else (outs[0], outs[1])


TOKEN_TILE = (8, LANES)


def _store_token_tiles(ref, x):
    n = x.shape[0]
    for s in range(TOKEN_TILE[0]):
        ref[pl.ds(s, n, stride=TOKEN_TILE[0]), :] = x[:, s * LANES:(s + 1) * LANES]


def _load_token_tiles(ref, n, start=0):
    return jnp.concatenate([ref[pl.ds(start + s, n, stride=TOKEN_TILE[0]), :] for s in range(TOKEN_TILE[0])],
                           axis=1)


def _outproj_kernel(*refs, moe):
    if moe:
        (x_ref, att_ref, rw_ref, mod_ref, g_ref, w_ref, rw_w_ref, rw_b_ref,
         x1_ref, h_ref, info_ref) = refs
    else:
        x_ref, att_ref, rw_ref, mod_ref, g_ref, w_ref, x1_ref, h_ref = refs
    mix = (jnp.dot(att_ref[...], w_ref[0, :ATT_WIDTH, :], preferred_element_type=F32)
           + jnp.dot(rw_ref[...], w_ref[0, ATT_WIDTH:, :], preferred_element_type=F32))
    x1 = x_ref[...] + mod_ref[0, 2:3, :] * mix
    x1_ref[...] = x1
    h = _rms_mod(x1, g_ref[...], mod_ref[0, 4:5, :], mod_ref[0, 3:4, :])
    if moe:
        _store_token_tiles(h_ref, h)
    else:
        h_ref[...] = h.astype(h_ref.dtype)
    if moe:
        h_hi, h_lo = _split2(h)
        w_hi, w_lo = _split2(rw_w_ref[...])
        logits = (jnp.dot(h_hi, w_hi, preferred_element_type=F32)
                  + jnp.dot(h_hi, w_lo, preferred_element_type=F32)
                  + jnp.dot(h_lo, w_hi, preferred_element_type=F32)) + rw_b_ref[...]
        lane = lax.broadcasted_iota(jnp.int32, logits.shape, 1)
        logits = jnp.where(lane < N_EXPERTS, logits, -jnp.inf)
        m1 = jnp.max(logits, axis=-1, keepdims=True)
        i1 = jnp.min(jnp.where(logits == m1, lane, LANES), axis=-1, keepdims=True)
        rest = jnp.where(lane == i1, -jnp.inf, logits)
        m2 = jnp.max(rest, axis=-1, keepdims=True)
        i2 = jnp.min(jnp.where(rest == m2, lane, LANES), axis=-1, keepdims=True)
        e = jnp.exp(m2 - m1)
        g1 = 1.0 / (1.0 + e)
        g2 = e / (1.0 + e)
        info = jnp.where(lane == 0, i1.astype(F32),
                         jnp.where(lane == 1, i2.astype(F32),
                                   jnp.where(lane == 2, g1, jnp.where(lane == 3, g2, 0.0))))
        info_ref[...] = info


def _outproj(x, att, rw, mod, g, w_out_b, B, S, l, router=None):
    T, D = x.shape
    tm = _tile(S, 512)
    nS = S // tm
    moe = router is not None
    row = lambda b, i: (b * nS + i, 0)
    full = lambda b, i: (0, 0)
    in_specs = [pl.BlockSpec((tm, D), row), pl.BlockSpec((tm, ATT_WIDTH), row),
                pl.BlockSpec((tm, RWKV_WIDTH), row),
                pl.BlockSpec((1, 6, D), lambda b, i: (l * B + b, 0, 0)),
                pl.BlockSpec((1, D), full),
                pl.BlockSpec((1, D, D), lambda b, i: (l, 0, 0))]
    args = [x, att, rw, mod, g, w_out_b]
    if moe:
        out_specs = [pl.BlockSpec((tm, D), row), pl.BlockSpec((tm * TOKEN_TILE[0], LANES), row)]
        out_shape = [jax.ShapeDtypeStruct((T, D), F32), jax.ShapeDtypeStruct((T * TOKEN_TILE[0], LANES), F32)]
    else:
        out_specs = [pl.BlockSpec((tm, D), row), pl.BlockSpec((tm, D), row)]
        out_shape = [jax.ShapeDtypeStruct((T, D), F32), jax.ShapeDtypeStruct((T, D), BF16)]
    if moe:
        in_specs += [pl.BlockSpec((D, LANES), full), pl.BlockSpec((1, LANES), full)]
        args += list(router)
        out_specs.append(pl.BlockSpec((tm, LANES), row))
        out_shape.append(jax.ShapeDtypeStruct((T, LANES), F32))
    return pl.pallas_call(
        functools.partial(_outproj_kernel, moe=moe),
        grid=(B, nS),
        in_specs=in_specs,
        out_specs=out_specs,
        out_shape=out_shape,
        compiler_params=_cparams(("arbitrary", "arbitrary")),
        name="outproj",
    )(*args)


def _ffn_kernel(x_ref, h_ref, mod_ref, wg_ref, wu_ref, wd_ref, o_ref, acc_ref):
    f = pl.program_id(1)

    @pl.when(f == 0)
    def _():
        acc_ref[...] = jnp.zeros_like(acc_ref)

    hb = h_ref[...]
    gate = jnp.dot(hb, wg_ref[0], preferred_element_type=F32)
    up = jnp.dot(hb, wu_ref[0], preferred_element_type=F32)
    mid = (_silu(gate) * up).astype(BF16)
    acc_ref[...] += jnp.dot(mid, wd_ref[0], preferred_element_type=F32)

    @pl.when(f == pl.num_programs(1) - 1)
    def _():
        o_ref[...] = x_ref[...] + mod_ref[0, 5:6, :] * acc_ref[...]


def _ffn_dense(x1, h, mod, wg, wu, wd, B, S, l, li):
    T, D = x1.shape
    F = wg.shape[-1]
    tm = _tile(S, 512)
    tf = _tile(F, 1408)
    per_seq = S // tm
    return pl.pallas_call(
        _ffn_kernel,
        grid=(T // tm, F // tf),
        in_specs=[pl.BlockSpec((tm, D), lambda i, f: (i, 0)),
                  pl.BlockSpec((tm, D), lambda i, f: (i, 0)),
                  pl.BlockSpec((1, 6, D), lambda i, f: (l * B + i // per_seq, 0, 0)),
                  pl.BlockSpec((1, D, tf), lambda i, f: (li, 0, f)),
                  pl.BlockSpec((1, D, tf), lambda i, f: (li, 0, f)),
                  pl.BlockSpec((1, tf, D), lambda i, f: (li, f, 0))],
        out_specs=pl.BlockSpec((tm, D), lambda i, f: (i, 0)),
        out_shape=jax.ShapeDtypeStruct((T, D), F32),
        scratch_shapes=[pltpu.VMEM((tm, D), F32)],
        compiler_params=_cparams(("arbitrary", "arbitrary")),
        name="ffn_dense",
    )(x1, h, mod, wg, wu, wd)


def _experts_kernel(te_ref, nu_ref, tok_ref, h_ref, wg_ref, wu_ref, wd_ref, y_ref,
                    land_ref, xb_ref, acc_ref, sem):
    i = pl.program_id(0)
    f = pl.program_id(1)
    tm = xb_ref.shape[0]
    rows_per_token = TOKEN_TILE[0]
    n_used = nu_ref[0]

    def start_gather(tile):
        base = tile * tm

        def body(j, _):
            src = pl.multiple_of(tok_ref[base + j] * rows_per_token, rows_per_token)
            dst = pl.multiple_of(j * rows_per_token, rows_per_token)
            pltpu.make_async_copy(h_ref.at[pl.ds(src, rows_per_token)],
                                  land_ref.at[pl.ds(dst, rows_per_token)], sem).start()
            return 0

        lax.fori_loop(0, tm, body, 0, unroll=DMA_ISSUE_UNROLL)

    @pl.when(i < n_used)
    def _():
        @pl.when(f == 0)
        def _():
            @pl.when(i == 0)
            def _():
                start_gather(0)

            pltpu.make_async_copy(h_ref.at[pl.ds(0, tm * rows_per_token)], land_ref, sem).wait()
            xb_ref[...] = _load_token_tiles(land_ref, tm).astype(BF16)
            acc_ref[...] = jnp.zeros_like(acc_ref)

            @pl.when(i + 1 < n_used)
            def _():
                start_gather(i + 1)

        xb = xb_ref[...]
        gate = jnp.dot(xb, wg_ref[0, 0], preferred_element_type=F32)
        up = jnp.dot(xb, wu_ref[0, 0], preferred_element_type=F32)
        mid = (_silu(gate) * up).astype(BF16)
        acc_ref[...] += jnp.dot(mid, wd_ref[0, 0], preferred_element_type=F32)

        @pl.when(f == pl.num_programs(1) - 1)
        def _():
            _store_token_tiles(y_ref, acc_ref[...])

    @pl.when((i >= n_used) & (f == 0))
    def _():
        y_ref[...] = jnp.zeros_like(y_ref)


def _experts(h, row_tok, tile_e, n_used, wg, wu, wd, li, tm):
    n_rows = row_tok.shape[0]
    D = wg.shape[-2]
    rpt = TOKEN_TILE[0]
    assert h.shape[0] >= tm * rpt
    F = wg.shape[-1]
    tf = _tile(F, 1792)
    nf = F // tf

    def rows(i, f, te, nu, tok):
        return (i, 0)

    def wcol(i, f, te, nu, tok):
        return (li, te[jnp.minimum(i, nu[0] - 1)], 0, jnp.where(i < nu[0], f, nf - 1))

    def wrow(i, f, te, nu, tok):
        return (li, te[jnp.minimum(i, nu[0] - 1)], jnp.where(i < nu[0], f, nf - 1), 0)

    return pl.pallas_call(
        _experts_kernel,
        grid_spec=pltpu.PrefetchScalarGridSpec(
            num_scalar_prefetch=3,
            grid=(n_rows // tm, nf),
            in_specs=[pl.BlockSpec(memory_space=pl.ANY),
                      pl.BlockSpec((1, 1, D, tf), wcol),
                      pl.BlockSpec((1, 1, D, tf), wcol),
                      pl.BlockSpec((1, 1, tf, D), wrow)],
            out_specs=pl.BlockSpec((tm * rpt, LANES), rows),
            scratch_shapes=[pltpu.VMEM((tm * rpt, LANES), F32), pltpu.VMEM((tm, D), BF16),
                            pltpu.VMEM((tm, D), F32), pltpu.SemaphoreType.DMA(())]),
        out_shape=jax.ShapeDtypeStruct((n_rows * rpt, LANES), F32),
        compiler_params=_cparams(("arbitrary", "arbitrary")),
        name="moe_experts",
    )(tile_e, n_used, row_tok, h, wg, wu, wd)


def _combine_kernel(pos_ref, y_ref, x_ref, info_ref, mod_ref, fg_ref, o_ref, buf, sem, *, tc, final):
    i = pl.program_id(0)
    slot = i % 2
    rows_per_token = TOKEN_TILE[0]

    def start_gather(tile, dst_slot):
        base = tile * (2 * tc)

        def body(j, _):
            src = pl.multiple_of(pos_ref[base + j] * rows_per_token, rows_per_token)
            dst = pl.multiple_of(j * rows_per_token, rows_per_token)
            pltpu.make_async_copy(y_ref.at[pl.ds(src, rows_per_token)],
                                  buf.at[dst_slot, pl.ds(dst, rows_per_token)], sem.at[dst_slot]).start()
            return 0

        lax.fori_loop(0, 2 * tc, body, 0, unroll=DMA_ISSUE_UNROLL)

    @pl.when(i == 0)
    def _():
        start_gather(0, 0)

    @pl.when(i + 1 < pl.num_programs(0))
    def _():
        start_gather(i + 1, 1 - slot)

    pltpu.make_async_copy(y_ref.at[pl.ds(0, 2 * tc * rows_per_token)], buf.at[slot], sem.at[slot]).wait()
    info = info_ref[...]
    ff = (info[:, 2:3] * _load_token_tiles(buf.at[slot], tc)
          + info[:, 3:4] * _load_token_tiles(buf.at[slot], tc, tc * rows_per_token))
    out = x_ref[...] + mod_ref[0, 5:6, :] * ff
    if final:
        ms = jnp.mean(out * out, axis=-1, keepdims=True)
        out = out * lax.rsqrt(ms + RMS_EPS) * fg_ref[...]
    o_ref[...] = out


def _combine(pos, y_rows, x1, info, mod, final_g, B, S, l, final):
    T, D = x1.shape
    tc = _tile(S, 256)
    per_seq = S // tc
    return pl.pallas_call(
        functools.partial(_combine_kernel, tc=tc, final=final),
        grid_spec=pltpu.PrefetchScalarGridSpec(
            num_scalar_prefetch=1,
            grid=(T // tc,),
            in_specs=[pl.BlockSpec(memory_space=pl.ANY),
                      pl.BlockSpec((tc, D), lambda i, pos: (i, 0)),
                      pl.BlockSpec((tc, LANES), lambda i, pos: (i, 0)),
                      pl.BlockSpec((1, 6, D), lambda i, pos: (l * B + i // per_seq, 0, 0)),
                      pl.BlockSpec((1, D), lambda i, pos: (0, 0))],
            out_specs=pl.BlockSpec((tc, D), lambda i, pos: (i, 0)),
            scratch_shapes=[pltpu.VMEM((2, 2 * tc * TOKEN_TILE[0], LANES), F32), pltpu.SemaphoreType.DMA((2,))]),
        out_shape=jax.ShapeDtypeStruct((T, D), F32),
        compiler_params=_cparams(("arbitrary",)),
        name="moe_combine",
    )(pos, y_rows, x1, info, mod, final_g)


def _moe(x1, h, info, mod, wg, wu, wd, final_g, B, S, l, li, final):
    T, D = x1.shape
    TK = T * TOP_K
    tm = _tile(TK // N_EXPERTS, 512)
    n_rows = TK + (N_EXPERTS - 1) * tm
    flat_e = info[:, :TOP_K].astype(jnp.int32).reshape(TK)
    onehot = (flat_e[:, None] == jnp.arange(N_EXPERTS, dtype=jnp.int32)[None, :]).astype(jnp.int32)
    csum = jnp.cumsum(onehot, axis=0)
    counts = csum[-1]
    rank = jnp.sum(onehot * csum, axis=1) - 1
    padded = (counts + tm - 1) // tm * tm
    pad_ends = jnp.cumsum(padded)
    pad_starts = pad_ends - padded
    dest = pad_starts[flat_e] + rank
    row_tok = jnp.zeros((n_rows,), jnp.int32).at[dest].set(jnp.arange(TK, dtype=jnp.int32) // TOP_K)
    tile_start = jnp.arange(n_rows // tm, dtype=jnp.int32) * tm
    tile_e = jnp.minimum(jnp.searchsorted(pad_ends, tile_start, side='right'), N_EXPERTS - 1).astype(jnp.int32)
    n_used_rows = pad_ends[-1:].astype(jnp.int32)
    n_used_tiles = n_used_rows // tm
    tc = _tile(S, 256)
    pos = dest.reshape(T // tc, tc, TOP_K).transpose(0, 2, 1).reshape(TK).astype(jnp.int32)

    y_rows = _experts(h, row_tok, tile_e, n_used_tiles, wg, wu, wd, li, tm)
    return _combine(pos, y_rows, x1, info, mod, final_g, B, S, l, final)


def kernel(x, c, w_ada, b_ada, norm1_g, norm2_g, final_g, w_in, w_out, rel_bias, rwkv_mu, rwkv_w0, rwkv_w2, rwkv_a0, rwkv_a2, rwkv_g2, rwkv_k_k, rwkv_k_a, rwkv_r_k, rwkv_ln_w, rwkv_ln_b, rwkv_v0, rwkv_v1, rwkv_v2, ffn_w_gate, ffn_w_up, ffn_w_down, moe_router_w, moe_router_b, moe_w_gate, moe_w_up, moe_w_down):
    B, S, D = x.shape
    L = w_in.shape[0]
    T = B * S
    W = RWKV_WIDTH

    mod = _ada_mod(c, w_ada, b_ada).reshape(L * B, 6, D)
    w_in_b = jnp.pad(w_in, ((0, 0), (0, 0), (0, N_IN_PAD - N_IN))).astype(BF16)
    w_out_b = w_out.astype(BF16)
    ffn_w = [w.astype(BF16) for w in (ffn_w_gate, ffn_w_up, ffn_w_down)]
    moe_w = [w.astype(BF16) for w in (moe_w_gate, moe_w_up, moe_w_down)]
    mu_p = jnp.pad(rwkv_mu, ((0, 0), (0, RWM_COLS + RWL_COLS - rwkv_mu.shape[1])))
    att_tables = _att_tables(rel_bias, S)
    zeros_w = jnp.zeros((1, W), F32)
    final_g2 = final_g.reshape(1, D)

    xf = x.reshape(T, D)
    v_first = None
    for l in range(L):
        qkv, qkv_mid, qkv_far, rwm, rwl = _inproj(xf, mod, norm1_g[l].reshape(1, D), w_in_b,
                                                   mu_p[l].reshape(1, -1), B, S, l)
        att = _attention(qkv, qkv_mid, qkv_far, att_tables, B, S)
        v0 = rwkv_v0[l - 1].reshape(1, W) if l > 0 else zeros_w
        vec = jnp.concatenate([rwkv_w0[l].reshape(1, W), rwkv_a0[l].reshape(1, W), rwkv_k_k[l].reshape(1, W),
                               rwkv_k_a[l].reshape(1, W), rwkv_r_k[l].reshape(1, W), rwkv_ln_w[l].reshape(1, W),
                               rwkv_ln_b[l].reshape(1, W), v0], axis=0)
        w2p = jnp.pad(rwkv_w2[l], ((0, LANES - D_DECAY_LORA), (0, 0)))
        a2p = jnp.pad(rwkv_a2[l], ((D_DECAY_LORA, 0), (0, 0)))
        g2p = jnp.pad(rwkv_g2[l], ((0, RWL_COLS - LANES - D_GATE_LORA), (0, 0)))
        if l > 0:
            v1p = jnp.pad(rwkv_v1[l - 1], ((0, 0), (0, LANES - D_MV_LORA)))
            v2p = jnp.pad(rwkv_v2[l - 1], ((0, LANES - D_MV_LORA), (0, 0)))
        else:
            v1p = v2p = None
        rw, v_first = _rwkv(rwm, rwl, v_first, vec, w2p, a2p, g2p, v1p, v2p, B, S)
        li = l // 2
        if l % 2 == 0:
            x1, h = _outproj(xf, att, rw, mod, norm2_g[l].reshape(1, D), w_out_b, B, S, l)
            xf = _ffn_dense(x1, h, mod, *ffn_w, B, S, l, li)
        else:
            router = (jnp.pad(moe_router_w[li], ((0, 0), (0, LANES - N_EXPERTS))),
                      jnp.pad(moe_router_b[li], (0, LANES - N_EXPERTS)).reshape(1, LANES))
            x1, h, info = _outproj(xf, att, rw, mod, norm2_g[l].reshape(1, D), w_out_b, B, S, l, router)
            xf = _moe(x1, h, info, mod, *moe_w, final_g2, B, S, l, li,
                      final=(l == L - 1))
    if L % 2 == 1:
        raise NotImplementedError("final norm is fused into the last (expert) layer")
    return xf.reshape(B, S, D)
```

```python
import functools
import math

import numpy as np
import jax
import jax.numpy as jnp
from jax import lax
from jax.experimental import pallas as pl
from jax.experimental.pallas import tpu as pltpu

F32 = jnp.float32
BF16 = jnp.bfloat16

D_MODEL = 1024
HEAD_DIM = 64
ATT_WIDTH = 512
RWKV_WIDTH = 512
N_HEADS = 8
HEAD_PAIRS = 4
LANES = 128
DILATED_GROUPS = ((128, 1), (512, 4), (2048, 16))
N_BUCKETS = 32
MAX_DISTANCE = 2048
NEG_INF = -1e30
D_DECAY_LORA = 64
D_AAA_LORA = 64
D_MV_LORA = 32
D_GATE_LORA = 160
N_IN = 3 * ATT_WIDTH + 3 * RWKV_WIDTH + D_DECAY_LORA + D_AAA_LORA + D_GATE_LORA
QKV_COLS = 3 * ATT_WIDTH
RWM_COLS = 3 * RWKV_WIDTH
RWL_COLS = 384
N_IN_PAD = QKV_COLS + RWM_COLS + RWL_COLS
GN_EPS = HEAD_DIM * 1e-5
RMS_EPS = 1e-6
N_EXPERTS = 8
TOP_K = 2
CHUNK = 64
VMEM_LIMIT = 56 * 1024 * 1024
LOG2_E = math.log2(math.e)
Q_SCALE = LOG2_E / math.sqrt(HEAD_DIM)
ROW_TILE = 512
FFN_ROW_TILE = 256
FFN_COL_TILE = 2816
EXPERT_ROW_TILE = 512
EXPERT_COL_TILE = 1792
COMBINE_ROW_TILE = 256
DMA_ISSUE_UNROLL = 8


def _cparams(sem):
    return pltpu.CompilerParams(dimension_semantics=sem, vmem_limit_bytes=VMEM_LIMIT)


def _tile(n, pref):
    t = min(n, pref)
    while n % t:
        t //= 2
    return t


def _mm(a, b):
    return jnp.dot(a.astype(BF16), b.astype(BF16), preferred_element_type=F32)


def _mm_nt(a, b):
    return lax.dot_general(a.astype(BF16), b.astype(BF16), (((1,), (1,)), ((), ())),
                           preferred_element_type=F32)


def _sigmoid(x):
    return 1.0 / (1.0 + jnp.exp(-x))


def _silu(x):
    return x * _sigmoid(x)


def _ada_kernel(c_ref, w_ref, b_ref, o_ref):
    ca = _silu(c_ref[...])
    o_ref[0] = _mm(ca, w_ref[0]) + b_ref[0]


def _ada_mod(c, w_ada, b_ada):
    L, D, N = w_ada.shape
    B = c.shape[0]
    tn = _tile(N, 1536)
    return pl.pallas_call(
        _ada_kernel,
        grid=(L, N // tn),
        in_specs=[pl.BlockSpec((B, D), lambda l, j: (0, 0)),
                  pl.BlockSpec((1, D, tn), lambda l, j: (l, 0, j)),
                  pl.BlockSpec((1, 1, tn), lambda l, j: (l, 0, j))],
        out_specs=pl.BlockSpec((1, B, tn), lambda l, j: (l, 0, j)),
        out_shape=jax.ShapeDtypeStruct((L, B, N), F32),
        compiler_params=_cparams(("arbitrary", "arbitrary")),
        name="ada_mod",
    )(c, w_ada, b_ada.reshape(L, 1, N))


def _rms_mod(xf, g, scale, shift):
    ms = jnp.mean(xf * xf, axis=-1, keepdims=True)
    return (xf * lax.rsqrt(ms + RMS_EPS) * g) * (1.0 + scale) + shift


def _inproj_kernel(x_ref, mod_ref, g_ref, w_ref, mu_ref, qkv_ref, qkv_mid_ref, qkv_far_ref, rwm_ref, rwl_ref,
                   carry_ref, qkv_scr):
    i = pl.program_id(1)
    tm = x_ref.shape[0]
    h = _rms_mod(x_ref[...], g_ref[...], mod_ref[0, 1:2, :], mod_ref[0, 0:1, :])
    acc = jnp.dot(h.astype(BF16), w_ref[0], preferred_element_type=F32)
    for c in range(QKV_COLS // LANES):
        cols = slice(c * LANES, (c + 1) * LANES)
        blk = acc[:, cols] * Q_SCALE if (c + 1) * LANES <= ATT_WIDTH else acc[:, cols]
        qkv_ref[:, cols] = blk.astype(BF16)
        qkv_scr[c] = blk
        for out_ref in (qkv_mid_ref, qkv_far_ref):
            d, rows = out_ref.shape[0], out_ref.shape[1]
            for r in range(d):
                out_ref[r, :, cols] = qkv_scr[c, pl.ds(r, rows, stride=d), :].astype(BF16)
    p = acc[:, QKV_COLS:]
    first = jnp.where(i == 0, 0.0, carry_ref[...])
    row = lax.broadcasted_iota(jnp.int32, p.shape, 0)
    prev = jnp.where(row == 0, first, pltpu.roll(p, 1, 0))
    carry_ref[...] = p[tm - 1:tm, :]
    pm = p + mu_ref[...] * (prev - p)
    rwm_ref[...] = pm[:, :RWM_COLS]
    rwl_ref[...] = pm[:, RWM_COLS:]


def _inproj(x, mod, g, w_in_l, mu_l, B, S, l):
    T, D = x.shape
    tm = _tile(S, ROW_TILE)
    nS = S // tm
    d2, d3 = DILATED_GROUPS[1][1], DILATED_GROUPS[2][1]
    return pl.pallas_call(
        _inproj_kernel,
        grid=(B, nS),
        in_specs=[pl.BlockSpec((tm, D), lambda b, i: (b * nS + i, 0)),
                  pl.BlockSpec((1, 6, D), lambda b, i: (l * B + b, 0, 0)),
                  pl.BlockSpec((1, D), lambda b, i: (0, 0)),
                  pl.BlockSpec((1, D, N_IN_PAD), lambda b, i: (l, 0, 0)),
                  pl.BlockSpec((1, RWM_COLS + RWL_COLS), lambda b, i: (0, 0))],
        out_specs=[pl.BlockSpec((tm, QKV_COLS), lambda b, i: (b * nS + i, 0)),
                   pl.BlockSpec((None, d2, tm // d2, QKV_COLS), lambda b, i: (b, 0, i, 0)),
                   pl.BlockSpec((None, d3, tm // d3, QKV_COLS), lambda b, i: (b, 0, i, 0)),
                   pl.BlockSpec((tm, RWM_COLS), lambda b, i: (b * nS + i, 0)),
                   pl.BlockSpec((tm, RWL_COLS), lambda b, i: (b * nS + i, 0))],
        out_shape=[jax.ShapeDtypeStruct((T, QKV_COLS), BF16),
                   jax.ShapeDtypeStruct((B, d2, S // d2, QKV_COLS), BF16),
                   jax.ShapeDtypeStruct((B, d3, S // d3, QKV_COLS), BF16),
                   jax.ShapeDtypeStruct((T, RWM_COLS), F32),
                   jax.ShapeDtypeStruct((T, RWL_COLS), F32)],
        scratch_shapes=[pltpu.VMEM((1, RWM_COLS + RWL_COLS), F32), pltpu.VMEM((QKV_COLS // LANES, tm, LANES), F32)],
        compiler_params=_cparams(("arbitrary", "arbitrary")),
        name="inproj",
    )(x, mod, g, w_in_l, mu_l)


def _t5_bucket(n):
    max_exact = N_BUCKETS // 2
    large = max_exact + (np.log(np.maximum(n, 1) / max_exact) / np.log(MAX_DISTANCE / max_exact)
                         * (N_BUCKETS - max_exact)).astype(np.int32)
    large = np.minimum(large, N_BUCKETS - 1)
    return np.where(n < max_exact, n, large).astype(np.int32)


ATT_BLOCK = 128
ATT_KEYS = 256


def _toeplitz(vec, c0, rows, cols):
    H, n = vec.shape
    period = rows + cols - 1
    w = jnp.concatenate([jnp.full((H, cols - 1), NEG_INF, F32), vec, jnp.full((H, rows), NEG_INF, F32)], axis=1)
    rw = w[:, ::-1]
    a = w.shape[1] - 1 - (c0 + cols - 1)
    z = jnp.concatenate([rw[:, a:a + cols], rw[:, a - (rows - 1):a]], axis=1)
    flat = jnp.tile(z, (1, rows))[:, :rows * (period - 1)]
    return flat.reshape(H, rows, period - 1)[..., :cols]


def _att_tables(rel_bias, S):
    (w1, d1), (w2, d2), (w3, d3) = DILATED_GROUPS
    assert d1 == 1 and w1 == ATT_KEYS - ATT_BLOCK and w2 // d2 == w1 and S % (d3 * 8) == 0
    off = np.arange(S)
    mult = np.zeros(S, np.int64)
    for w, d in DILATED_GROUPS:
        mult += ((off % d == 0) & (off <= w)).astype(np.int64)
    logm = np.where(mult > 0, np.log(np.maximum(mult, 1)), NEG_INF).astype(np.float32)
    per_off = (rel_bias[_t5_bucket(off)].T.astype(F32) + jnp.asarray(logm)[None, :]) * LOG2_E
    u2 = np.arange(S // d2)
    u3 = np.arange(S // d3)
    band = jnp.where(jnp.asarray(off <= w1)[None], per_off, NEG_INF)
    mid = jnp.where(jnp.asarray((u2 > w1 // d2) & (u2 <= w2 // d2))[None], per_off[:, ::d2], NEG_INF)
    far = jnp.where(jnp.asarray((u3 > w2 // d3) & (u3 <= w3 // d3))[None], per_off[:, ::d3], NEG_INF)
    shift = ATT_KEYS - ATT_BLOCK
    t_band = jnp.stack([_toeplitz(band, 0, ATT_BLOCK, ATT_KEYS), _toeplitz(band, shift, ATT_BLOCK, ATT_KEYS)], axis=1)
    t_mid = jnp.stack([_toeplitz(mid, 0, ATT_BLOCK, ATT_KEYS), _toeplitz(mid, shift, ATT_BLOCK, ATT_KEYS)], axis=1)
    t_far = _toeplitz(far, 0, S // d3, S // d3)
    return t_band, t_mid, t_far


def _attend(units):
    idx = range(len(units))
    heads = range(2)
    head0 = [lax.broadcasted_iota(jnp.int32, u[0].shape, 1) < HEAD_DIM for u in units]
    qh = [[jnp.where(head0[i], units[i][0], jnp.zeros_like(units[i][0])),
           jnp.where(head0[i], jnp.zeros_like(units[i][0]), units[i][0])] for i in idx]
    s = [[lax.dot_general(qh[i][h], units[i][1], (((1,), (1,)), ((), ())), preferred_element_type=F32)
          + units[i][3 + h] for h in heads] for i in idx]
    mx = [[jnp.max(s[i][h], axis=-1, keepdims=True) for h in heads] for i in idx]
    p = [[jnp.exp2(s[i][h] - mx[i][h]) for h in heads] for i in idx]
    ps = [[jnp.sum(p[i][h], axis=-1, keepdims=True) for h in heads] for i in idx]
    pv = [[jnp.dot(p[i][h].astype(BF16), units[i][2], preferred_element_type=F32) for h in heads] for i in idx]
    out = []
    for i in idx:
        shape = units[i][0].shape
        out.append((jnp.where(head0[i], jnp.broadcast_to(mx[i][0], shape), jnp.broadcast_to(mx[i][1], shape)),
                    jnp.where(head0[i], jnp.broadcast_to(ps[i][0], shape), jnp.broadcast_to(ps[i][1], shape)),
                    jnp.where(head0[i], pv[i][0], pv[i][1])))
    return out


def _attn_kernel(qa_ref, ka_ref, va_ref, qm_ref, km_ref, vm_ref, qf_ref, kf_ref, vf_ref,
                 tband_ref, tmid_ref, tfar_ref, o_ref, part_ref):
    R, K = ATT_BLOCK, ATT_KEYS
    S = qa_ref.shape[0]
    n_mid_res, n_mid = qm_ref.shape[0], qm_ref.shape[1]
    n_far_res, n_far = qf_ref.shape[0], qf_ref.shape[1]
    group = 8

    def window(blk):
        start = pl.multiple_of(jnp.maximum(blk * R - (K - R), 0), R)
        return pl.ds(pl.multiple_of(blk * R, R), R), pl.ds(start, K), jnp.minimum(blk, 1)

    def park(cls, rows, results):
        for nat, parts in zip(rows, results):
            for j in range(3):
                part_ref[cls, j, nat, :] = parts[j]

    def band_body(it, _):
        units, rows = [], []
        for g in range(group):
            qrows, krows, var = window(it * group + g)
            rows.append(qrows)
            units.append((qa_ref[qrows, :], ka_ref[krows, :], va_ref[krows, :],
                          tband_ref[0, var], tband_ref[1, var]))
        park(0, rows, _attend(units))
        return 0

    lax.fori_loop(0, S // (R * group), band_body, 0)

    blocks_per_step = group // n_mid_res

    def mid_body(it, _):
        units, rows = [], []
        for g in range(blocks_per_step):
            blk = it * blocks_per_step + g
            qrows, krows, var = window(blk)
            for r in range(n_mid_res):
                rows.append(pl.ds(blk * (R * n_mid_res) + r, R, stride=n_mid_res))
                units.append((qm_ref[r, qrows, :], km_ref[r, krows, :], vm_ref[r, krows, :],
                              tmid_ref[0, var], tmid_ref[1, var]))
        park(1, rows, _attend(units))
        return 0

    lax.fori_loop(0, n_mid // (R * blocks_per_step), mid_body, 0)

    def far_body(it, _):
        units, rows = [], []
        for g in range(group):
            r = it * group + g
            rows.append(pl.ds(r, n_far, stride=n_far_res))
            units.append((qf_ref[r], kf_ref[r], vf_ref[r], tfar_ref[0], tfar_ref[1]))
        park(2, rows, _attend(units))
        return 0

    lax.fori_loop(0, n_far_res // group, far_body, 0)

    def merge_body(i, _):
        rows = pl.ds(pl.multiple_of(i * R, R), R)
        m = [part_ref[c, 0, rows, :] for c in range(3)]
        top = jnp.maximum(jnp.maximum(m[0], m[1]), m[2])
        w = [jnp.exp2(m[c] - top) for c in range(3)]
        total = sum(w[c] * part_ref[c, 1, rows, :] for c in range(3))
        acc = sum(w[c] * part_ref[c, 2, rows, :] for c in range(3))
        o_ref[rows, :] = (acc / total).astype(o_ref.dtype)
        return 0

    lax.fori_loop(0, S // R, merge_body, 0)


def _attention(qkv, qkv_mid, qkv_far, tables, B, S):
    T = qkv.shape[0]
    t_band, t_mid, t_far = tables
    d2, d3 = DILATED_GROUPS[1][1], DILATED_GROUPS[2][1]
    assert S % (ATT_BLOCK * 8) == 0 and S // d2 >= ATT_KEYS
    nat = lambda c: pl.BlockSpec((S, LANES), lambda hp, b: (b, c * HEAD_PAIRS + hp))
    mid = lambda c: pl.BlockSpec((None, d2, S // d2, LANES), lambda hp, b: (b, 0, 0, c * HEAD_PAIRS + hp))
    far = lambda c: pl.BlockSpec((None, d3, S // d3, LANES), lambda hp, b: (b, 0, 0, c * HEAD_PAIRS + hp))
    return pl.pallas_call(
        _attn_kernel,
        grid=(HEAD_PAIRS, B),
        in_specs=[nat(0), nat(1), nat(2), mid(0), mid(1), mid(2), far(0), far(1), far(2),
                  pl.BlockSpec((2,) + t_band.shape[1:], lambda hp, b: (hp, 0, 0, 0)),
                  pl.BlockSpec((2,) + t_mid.shape[1:], lambda hp, b: (hp, 0, 0, 0)),
                  pl.BlockSpec((2,) + t_far.shape[1:], lambda hp, b: (hp, 0, 0))],
        out_specs=pl.BlockSpec((S, LANES), lambda hp, b: (b, hp)),
        out_shape=jax.ShapeDtypeStruct((T, ATT_WIDTH), BF16),
        scratch_shapes=[pltpu.VMEM((3, 3, S, LANES), F32)],
        compiler_params=_cparams(("arbitrary", "arbitrary")),
        name="dilated_attn",
    )(qkv, qkv, qkv, qkv_mid, qkv_mid, qkv_mid, qkv_far, qkv_far, qkv_far, t_band, t_mid, t_far)


def _split2(x):
    hi = x.astype(BF16)
    lo = (x - hi.astype(F32)).astype(BF16)
    return hi, lo


def _head_sum(x, m2):
    hi, lo = _split2(x)
    return jnp.dot(jnp.concatenate([hi, lo], axis=1), m2, preferred_element_type=F32)


def _rwkv_kernel(*refs, has_vres, n_chunks, n_batch):
    if has_vres:
        (rwm_ref, rwl_ref, vf_ref, vec_ref, w2_ref, a2_ref, g2_ref, v1_ref, v2_ref,
         o_ref, state_ref, r_s, k_s, v_s, lw_s, kk_s, kb_s, y_s,
         ar_s, inv_s, bk_s, nrbk_s, akv_s, dec_s) = refs
    else:
        (rwm_ref, rwl_ref, vec_ref, w2_ref, a2_ref, g2_ref,
         o_ref, vf_out_ref, state_ref, r_s, k_s, v_s, lw_s, kk_s, kb_s, y_s,
         ar_s, inv_s, bk_s, nrbk_s, akv_s, dec_s) = refs
    C = CHUNK
    W = RWKV_WIDTH
    rows_per_batch = rwm_ref.shape[1]

    def stacked(ref, cols=slice(None)):
        return jnp.concatenate([ref[b, :, cols] for b in range(n_batch)], axis=0)

    def unstack(ref, x, cols=slice(None)):
        for b in range(n_batch):
            ref[b, :, cols] = x[b * rows_per_batch:(b + 1) * rows_per_batch].astype(ref.dtype)

    @pl.when(pl.program_id(1) == 0)
    def _():
        state_ref[...] = jnp.zeros_like(state_ref)

    ri = lax.broadcasted_iota(jnp.int32, (2 * LANES, LANES), 0)
    ci = lax.broadcasted_iota(jnp.int32, (2 * LANES, LANES), 1)
    m2 = ((ri % LANES) // HEAD_DIM == ci // HEAD_DIM).astype(BF16)

    vec = vec_ref[...]
    w0, a0, k_k, k_a, r_k, ln_w, ln_b, v0 = [vec[i:i + 1, :] for i in range(8)]

    r = stacked(rwm_ref, slice(0, W))
    k = stacked(rwm_ref, slice(W, 2 * W))
    v = stacked(rwm_ref, slice(2 * W, 3 * W))
    lora = stacked(rwl_ref, slice(0, LANES))
    w_raw = w0 + _mm(jnp.tanh(lora), w2_ref[...])
    lw_s[...] = -math.exp(-0.5) * _sigmoid(w_raw)
    a = _sigmoid(a0 + _mm(lora, a2_ref[...]))
    g = _mm(_sigmoid(stacked(rwl_ref, slice(LANES, RWL_COLS))), g2_ref[...])
    if has_vres:
        mix = _sigmoid(v0 + _mm(_mm(v, v1_ref[...]), v2_ref[...]))
        v = v + (stacked(vf_ref) - v) * mix
    else:
        unstack(vf_out_ref, v)
    kk = k * k_k
    for p in range(HEAD_PAIRS):
        ls = slice(p * LANES, (p + 1) * LANES)
        kkp = kk[:, ls]
        kkp = kkp * jnp.minimum(lax.rsqrt(_head_sum(kkp * kkp, m2)), 1e12)
        kk_s[:, ls] = kkp
        kb_s[:, ls] = kkp * a[:, ls]
    kmod = k * (1.0 + (a - 1.0) * k_a)
    r_s[...] = r
    k_s[...] = kmod
    v_s[...] = v

    ri = lax.broadcasted_iota(jnp.int32, (LANES, LANES), 0)
    ci = lax.broadcasted_iota(jnp.int32, (LANES, LANES), 1)
    same_head = (ri // C) == (ci // C)
    strict = same_head & ((ri % C) > (ci % C))
    incl = same_head & ((ri % C) >= (ci % C))
    eye = (ri == ci).astype(F32)
    lane = lax.broadcasted_iota(jnp.int32, (C, LANES), 1)
    head0 = lane < HEAD_DIM
    tri_r = lax.broadcasted_iota(jnp.int32, (C, 3 * C), 0)
    tri_c = lax.broadcasted_iota(jnp.int32, (C, 3 * C), 1)
    tri3 = ((tri_c % C) <= tri_r).astype(BF16)

    lane_slices = [slice(p * LANES, (p + 1) * LANES) for p in range(HEAD_PAIRS)]

    def cumsum(x):
        hi = x.astype(BF16)
        rem = x - hi.astype(F32)
        mid = rem.astype(BF16)
        lo = (rem - mid.astype(F32)).astype(BF16)
        return jnp.dot(tri3, jnp.concatenate([hi, mid, lo], axis=0), preferred_element_type=F32)

    def per_head_rows(x, y):
        zero = jnp.zeros_like(x)
        return jnp.concatenate([jnp.where(head0, x, zero), jnp.where(head0, zero, x),
                                jnp.where(head0, y, zero), jnp.where(head0, zero, y)], axis=0)

    def intra_body(it, _):
        units = [(it * chunk_group + g, p) for g in range(chunk_group) for p in range(HEAD_PAIRS)]
        idx = range(len(units))
        rows = [pl.ds(pl.multiple_of(c * C, C), C) for c, _ in units]
        ls = [lane_slices[p] for _, p in units]
        rc = [r_s[rows[u], ls[u]] for u in idx]
        kc = [k_s[rows[u], ls[u]] for u in idx]
        vc = [v_s[rows[u], ls[u]] for u in idx]
        lw = [lw_s[rows[u], ls[u]] for u in idx]
        kkc = [kk_s[rows[u], ls[u]] for u in idx]
        kbc = [kb_s[rows[u], ls[u]] for u in idx]
        cum = [cumsum(lw[u]) for u in idx]
        total = [cum[u][C - 1:C, :] for u in idx]
        g_inv = [jnp.exp(-cum[u]) for u in idx]
        a_t = [-kkc[u] * jnp.exp(cum[u] - lw[u]) for u in idx]
        r_t = [rc[u] * jnp.exp(cum[u]) for u in idx]
        b_t = [kbc[u] * g_inv[u] for u in idx]
        k_t = [kc[u] * g_inv[u] for u in idx]
        to_end = [jnp.exp(total[u] - cum[u]) for u in idx]
        gram = [_mm_nt(per_head_rows(a_t[u], r_t[u]), per_head_rows(b_t[u], k_t[u])) for u in idx]
        n_ab = [jnp.where(strict, gram[u][:LANES, :LANES], 0.0) for u in idx]
        n_ak = [jnp.where(strict, gram[u][:LANES, LANES:], 0.0) for u in idx]
        n_rb = [jnp.where(incl, gram[u][LANES:, :LANES], 0.0) for u in idx]
        n_rk = [jnp.where(incl, gram[u][LANES:, LANES:], 0.0) for u in idx]
        inv = [eye + n_ab[u] for u in idx]
        pw = [_mm(n_ab[u], n_ab[u]) for u in idx]
        n_rounds = int(math.log2(C)) - 1
        for i in range(n_rounds):
            if i + 1 < n_rounds:
                both = [_mm(pw[u], jnp.concatenate([inv[u], pw[u]], axis=1)) for u in idx]
                inv = [inv[u] + both[u][:, :LANES] for u in idx]
                pw = [both[u][:, LANES:] for u in idx]
            else:
                inv = [inv[u] + _mm(pw[u], inv[u]) for u in idx]
        for u, (c, p) in enumerate(units):
            ar_s[c, p] = jnp.concatenate([a_t[u], r_t[u]], axis=0).astype(BF16)
            inv_s[c, p] = inv[u].astype(BF16)
            akv_s[c, p] = _mm(n_ak[u], jnp.concatenate([vc[u], vc[u]], axis=0))
            nrbk_s[c, p] = jnp.concatenate([n_rb[u], n_rk[u]], axis=1).astype(BF16)
            bk_s[c, p] = jnp.concatenate([kbc[u] * to_end[u], kc[u] * to_end[u]], axis=0).astype(BF16)
            dec_s[c, p] = jnp.broadcast_to(jnp.exp(total[u]), (8, LANES))
        return 0

    chunks_per_batch = n_chunks // n_batch

    def state_body(step, _):
        units = [(b, b * chunks_per_batch + step, p) for b in range(n_batch) for p in range(HEAD_PAIRS)]
        idx = range(len(units))
        rows = [pl.ds(pl.multiple_of(c * C, C), C) for _, c, _ in units]
        ls = [lane_slices[p] for _, _, p in units]
        vc = [v_s[rows[u], ls[u]] for u in idx]
        st = [state_ref[b, p] for b, _, p in units]
        ah = [_mm_nt(ar_s[c, p], st[u]) for u, (_, c, p) in enumerate(units)]
        u_stack = [_mm(inv_s[c, p], jnp.concatenate([ah[u][:C], ah[u][:C]], axis=0) + akv_s[c, p])
                   for u, (_, c, p) in enumerate(units)]
        y_stack = [jnp.concatenate([ah[u][C:], ah[u][C:]], axis=0)
                   + _mm(nrbk_s[c, p], jnp.concatenate([u_stack[u], vc[u], vc[u]], axis=0))
                   for u, (_, c, p) in enumerate(units)]
        for u in idx:
            y_s[rows[u], ls[u]] = jnp.where(head0, y_stack[u][:C], y_stack[u][C:])
        uv_t = [jnp.concatenate([jnp.where(head0, u_stack[u][:C], u_stack[u][C:]), vc[u]], axis=0).T
                for u in idx]
        st_new = [st[u] * dec_s[c, p][0:1, :] + _mm(uv_t[u], bk_s[c, p]) for u, (_, c, p) in enumerate(units)]
        for u, (b, _, p) in enumerate(units):
            state_ref[b, p] = jnp.where(same_head, st_new[u], 0.0)
        return 0

    chunk_group = 4 if n_chunks % 4 == 0 else (2 if n_chunks % 2 == 0 else 1)
    lax.fori_loop(0, n_chunks // chunk_group, intra_body, 0)
    lax.fori_loop(0, chunks_per_batch, state_body, 0)

    for p in range(HEAD_PAIRS):
        ls = slice(p * LANES, (p + 1) * LANES)
        y = y_s[:, ls]
        mean = _head_sum(y, m2) * (1.0 / HEAD_DIM)
        d = y - mean
        var = _head_sum(d * d, m2) * (1.0 / HEAD_DIM)
        yn = d * lax.rsqrt(var + GN_EPS) * ln_w[:, ls] + ln_b[:, ls]
        bonus = _head_sum(r_s[:, ls] * k_s[:, ls] * r_k[:, ls], m2) * v_s[:, ls]
        unstack(o_ref, (yn + bonus) * g[:, ls], ls)


def _rwkv(rwm, rwl, v_first, vec, w2p, a2p, g2p, v1p, v2p, B, S):
    T = rwm.shape[0]
    W = RWKV_WIDTH
    nb = 4 if B % 4 == 0 else (2 if B % 2 == 0 else 1)
    tb = _tile(S, ROW_TILE // nb)
    n_chunks = nb * tb // CHUNK
    has_vres = v_first is not None
    stack = lambda a: a.reshape(B // nb, nb, S, a.shape[-1])
    row = lambda cols: pl.BlockSpec((None, nb, tb, cols), lambda b, i: (b, 0, i, 0))
    full = lambda b, i: (0, 0)
    in_specs = [row(RWM_COLS), row(RWL_COLS)]
    args = [stack(rwm), stack(rwl)]
    if has_vres:
        in_specs.append(row(W))
        args.append(stack(v_first))
    in_specs += [pl.BlockSpec(vec.shape, full), pl.BlockSpec(w2p.shape, full),
                 pl.BlockSpec(a2p.shape, full), pl.BlockSpec(g2p.shape, full)]
    args += [vec, w2p, a2p, g2p]
    if has_vres:
        in_specs += [pl.BlockSpec(v1p.shape, full), pl.BlockSpec(v2p.shape, full)]
        args += [v1p, v2p]
    out_specs = [row(W)]
    out_shape = [jax.ShapeDtypeStruct((B // nb, nb, S, W), BF16)]
    if not has_vres:
        out_specs.append(row(W))
        out_shape.append(jax.ShapeDtypeStruct((B // nb, nb, S, W), F32))
    outs = pl.pallas_call(
        functools.partial(_rwkv_kernel, has_vres=has_vres, n_chunks=n_chunks, n_batch=nb),
        grid=(B // nb, S // tb),
        in_specs=in_specs,
        out_specs=out_specs,
        out_shape=out_shape,
        scratch_shapes=([pltpu.VMEM((nb, HEAD_PAIRS, LANES, LANES), F32)] + [pltpu.VMEM((nb * tb, W), F32)] * 7
                        + [pltpu.VMEM((n_chunks, HEAD_PAIRS, 2 * CHUNK, LANES), BF16)] * 3
                        + [pltpu.VMEM((n_chunks, HEAD_PAIRS, 2 * CHUNK, 2 * LANES), BF16),
                           pltpu.VMEM((n_chunks, HEAD_PAIRS, 2 * CHUNK, LANES), F32),
                           pltpu.VMEM((n_chunks, HEAD_PAIRS, 8, LANES), F32)]),
        compiler_params=_cparams(("arbitrary", "arbitrary")),
        name="rwkv7",
    )(*args)
    outs = [o.reshape(T, W) for o in outs]
    return (outs[0], v_first) if has_vres else (outs[0], outs[1])


TOKEN_TILE = (8, LANES)


def _store_token_tiles(ref, x):
    n = x.shape[0]
    for s in range(TOKEN_TILE[0]):
        ref[pl.ds(s, n, stride=TOKEN_TILE[0]), :] = x[:, s * LANES:(s + 1) * LANES]


def _load_token_tiles(ref, n, start=0):
    return jnp.concatenate([ref[pl.ds(start + s, n, stride=TOKEN_TILE[0]), :] for s in range(TOKEN_TILE[0])],
                           axis=1)


def _outproj_kernel(*refs, moe):
    if moe:
        (x_ref, att_ref, rw_ref, mod_ref, g_ref, w_ref, rw_w_ref, rw_b_ref,
         x1_ref, h_ref, info_ref) = refs
    else:
        x_ref, att_ref, rw_ref, mod_ref, g_ref, w_ref, x1_ref, h_ref = refs
    mix = (jnp.dot(att_ref[...], w_ref[0, :ATT_WIDTH, :], preferred_element_type=F32)
           + jnp.dot(rw_ref[...], w_ref[0, ATT_WIDTH:, :], preferred_element_type=F32))
    x1 = x_ref[...] + mod_ref[0, 2:3, :] * mix
    x1_ref[...] = x1
    h = _rms_mod(x1, g_ref[...], mod_ref[0, 4:5, :], mod_ref[0, 3:4, :])
    if moe:
        _store_token_tiles(h_ref, h)
    else:
        h_ref[...] = h.astype(h_ref.dtype)
    if moe:
        h_hi, h_lo = _split2(h)
        w_hi, w_lo = _split2(rw_w_ref[...])
        logits = (jnp.dot(h_hi, w_hi, preferred_element_type=F32)
                  + jnp.dot(h_hi, w_lo, preferred_element_type=F32)
                  + jnp.dot(h_lo, w_hi, preferred_element_type=F32)) + rw_b_ref[...]
        lane = lax.broadcasted_iota(jnp.int32, logits.shape, 1)
        logits = jnp.where(lane < N_EXPERTS, logits, -jnp.inf)
        m1 = jnp.max(logits, axis=-1, keepdims=True)
        i1 = jnp.min(jnp.where(logits == m1, lane, LANES), axis=-1, keepdims=True)
        rest = jnp.where(lane == i1, -jnp.inf, logits)
        m2 = jnp.max(rest, axis=-1, keepdims=True)
        i2 = jnp.min(jnp.where(rest == m2, lane, LANES), axis=-1, keepdims=True)
        e = jnp.exp(m2 - m1)
        g1 = 1.0 / (1.0 + e)
        g2 = e / (1.0 + e)
        info = jnp.where(lane == 0, i1.astype(F32),
                         jnp.where(lane == 1, i2.astype(F32),
                                   jnp.where(lane == 2, g1, jnp.where(lane == 3, g2, 0.0))))
        info_ref[...] = info


def _outproj(x, att, rw, mod, g, w_out_b, B, S, l, router=None):
    T, D = x.shape
    tm = _tile(S, ROW_TILE)
    nS = S // tm
    moe = router is not None
    row = lambda b, i: (b * nS + i, 0)
    full = lambda b, i: (0, 0)
    in_specs = [pl.BlockSpec((tm, D), row), pl.BlockSpec((tm, ATT_WIDTH), row),
                pl.BlockSpec((tm, RWKV_WIDTH), row),
                pl.BlockSpec((1, 6, D), lambda b, i: (l * B + b, 0, 0)),
                pl.BlockSpec((1, D), full),
                pl.BlockSpec((1, D, D), lambda b, i: (l, 0, 0))]
    args = [x, att, rw, mod, g, w_out_b]
    if moe:
        out_specs = [pl.BlockSpec((tm, D), row), pl.BlockSpec((tm * TOKEN_TILE[0], LANES), row)]
        out_shape = [jax.ShapeDtypeStruct((T, D), F32), jax.ShapeDtypeStruct((T * TOKEN_TILE[0], LANES), F32)]
    else:
        out_specs = [pl.BlockSpec((tm, D), row), pl.BlockSpec((tm, D), row)]
        out_shape = [jax.ShapeDtypeStruct((T, D), F32), jax.ShapeDtypeStruct((T, D), BF16)]
    if moe:
        in_specs += [pl.BlockSpec((D, LANES), full), pl.BlockSpec((1, LANES), full)]
        args += list(router)
        out_specs.append(pl.BlockSpec((tm, LANES), row))
        out_shape.append(jax.ShapeDtypeStruct((T, LANES), F32))
    return pl.pallas_call(
        functools.partial(_outproj_kernel, moe=moe),
        grid=(B, nS),
        in_specs=in_specs,
        out_specs=out_specs,
        out_shape=out_shape,
        compiler_params=_cparams(("arbitrary", "arbitrary")),
        name="outproj",
    )(*args)


def _ffn_kernel(x_ref, h_ref, mod_ref, wg_ref, wu_ref, wd_ref, o_ref, acc_ref):
    f = pl.program_id(1)

    @pl.when(f == 0)
    def _():
        acc_ref[...] = jnp.zeros_like(acc_ref)

    hb = h_ref[...]
    gate = jnp.dot(hb, wg_ref[0], preferred_element_type=F32)
    up = jnp.dot(hb, wu_ref[0], preferred_element_type=F32)
    mid = (_silu(gate) * up).astype(BF16)
    acc_ref[...] += jnp.dot(mid, wd_ref[0], preferred_element_type=F32)

    @pl.when(f == pl.num_programs(1) - 1)
    def _():
        o_ref[...] = x_ref[...] + mod_ref[0, 5:6, :] * acc_ref[...]


def _ffn_dense(x1, h, mod, wg, wu, wd, B, S, l, li):
    T, D = x1.shape
    F = wg.shape[-1]
    tm = _tile(S, FFN_ROW_TILE)
    tf = _tile(F, FFN_COL_TILE)
    per_seq = S // tm
    return pl.pallas_call(
        _ffn_kernel,
        grid=(T // tm, F // tf),
        in_specs=[pl.BlockSpec((tm, D), lambda i, f: (i, 0)),
                  pl.BlockSpec((tm, D), lambda i, f: (i, 0)),
                  pl.BlockSpec((1, 6, D), lambda i, f: (l * B + i // per_seq, 0, 0)),
                  pl.BlockSpec((1, D, tf), lambda i, f: (li, 0, f)),
                  pl.BlockSpec((1, D, tf), lambda i, f: (li, 0, f)),
                  pl.BlockSpec((1, tf, D), lambda i, f: (li, f, 0))],
        out_specs=pl.BlockSpec((tm, D), lambda i, f: (i, 0)),
        out_shape=jax.ShapeDtypeStruct((T, D), F32),
        scratch_shapes=[pltpu.VMEM((tm, D), F32)],
        compiler_params=_cparams(("arbitrary", "arbitrary")),
        name="ffn_dense",
    )(x1, h, mod, wg, wu, wd)


def _experts_kernel(te_ref, nu_ref, tok_ref, h_ref, wg_ref, wu_ref, wd_ref, y_ref,
                    land_ref, xb_ref, acc_ref, sem):
    i = pl.program_id(0)
    f = pl.program_id(1)
    tm = xb_ref.shape[0]
    rows_per_token = TOKEN_TILE[0]
    n_used = nu_ref[0]

    def start_gather(tile):
        base = tile * tm

        def body(j, _):
            src = pl.multiple_of(tok_ref[base + j] * rows_per_token, rows_per_token)
            dst = pl.multiple_of(j * rows_per_token, rows_per_token)
            pltpu.make_async_copy(h_ref.at[pl.ds(src, rows_per_token)],
                                  land_ref.at[pl.ds(dst, rows_per_token)], sem).start()
            return 0

        lax.fori_loop(0, tm, body, 0, unroll=DMA_ISSUE_UNROLL)

    @pl.when(i < n_used)
    def _():
        @pl.when(f == 0)
        def _():
            @pl.when(i == 0)
            def _():
                start_gather(0)

            pltpu.make_async_copy(h_ref.at[pl.ds(0, tm * rows_per_token)], land_ref, sem).wait()
            xb_ref[...] = _load_token_tiles(land_ref, tm).astype(BF16)
            acc_ref[...] = jnp.zeros_like(acc_ref)

            @pl.when(i + 1 < n_used)
            def _():
                start_gather(i + 1)

        xb = xb_ref[...]
        gate = jnp.dot(xb, wg_ref[0, 0], preferred_element_type=F32)
        up = jnp.dot(xb, wu_ref[0, 0], preferred_element_type=F32)
        mid = (_silu(gate) * up).astype(BF16)
        acc_ref[...] += jnp.dot(mid, wd_ref[0, 0], preferred_element_type=F32)

        @pl.when(f == pl.num_programs(1) - 1)
        def _():
            _store_token_tiles(y_ref, acc_ref[...])

    @pl.when((i >= n_used) & (f == 0))
    def _():
        y_ref[...] = jnp.zeros_like(y_ref)


def _experts(h, row_tok, tile_e, n_used, wg, wu, wd, li, tm):
    n_rows = row_tok.shape[0]
    D = wg.shape[-2]
    rpt = TOKEN_TILE[0]
    assert h.shape[0] >= tm * rpt
    F = wg.shape[-1]
    tf = _tile(F, EXPERT_COL_TILE)
    nf = F // tf

    def rows(i, f, te, nu, tok):
        return (i, 0)

    def wcol(i, f, te, nu, tok):
        return (li, te[jnp.minimum(i, nu[0] - 1)], 0, jnp.where(i < nu[0], f, nf - 1))

    def wrow(i, f, te, nu, tok):
        return (li, te[jnp.minimum(i, nu[0] - 1)], jnp.where(i < nu[0], f, nf - 1), 0)

    return pl.pallas_call(
        _experts_kernel,
        grid_spec=pltpu.PrefetchScalarGridSpec(
            num_scalar_prefetch=3,
            grid=(n_rows // tm, nf),
            in_specs=[pl.BlockSpec(memory_space=pl.ANY),
                      pl.BlockSpec((1, 1, D, tf), wcol),
                      pl.BlockSpec((1, 1, D, tf), wcol),
                      pl.BlockSpec((1, 1, tf, D), wrow)],
            out_specs=pl.BlockSpec((tm * rpt, LANES), rows),
            scratch_shapes=[pltpu.VMEM((tm * rpt, LANES), F32), pltpu.VMEM((tm, D), BF16),
                            pltpu.VMEM((tm, D), F32), pltpu.SemaphoreType.DMA(())]),
        out_shape=jax.ShapeDtypeStruct((n_rows * rpt, LANES), F32),
        compiler_params=_cparams(("arbitrary", "arbitrary")),
        name="moe_experts",
    )(tile_e, n_used, row_tok, h, wg, wu, wd)


def _combine_kernel(pos_ref, y_ref, x_ref, info_ref, mod_ref, fg_ref, o_ref, buf, sem, *, tc, final):
    i = pl.program_id(0)
    slot = i % 2
    rows_per_token = TOKEN_TILE[0]

    def start_gather(tile, dst_slot):
        base = tile * (2 * tc)

        def body(j, _):
            src = pl.multiple_of(pos_ref[base + j] * rows_per_token, rows_per_token)
            dst = pl.multiple_of(j * rows_per_token, rows_per_token)
            pltpu.make_async_copy(y_ref.at[pl.ds(src, rows_per_token)],
                                  buf.at[dst_slot, pl.ds(dst, rows_per_token)], sem.at[dst_slot]).start()
            return 0

        lax.fori_loop(0, 2 * tc, body, 0, unroll=DMA_ISSUE_UNROLL)

    @pl.when(i == 0)
    def _():
        start_gather(0, 0)

    @pl.when(i + 1 < pl.num_programs(0))
    def _():
        start_gather(i + 1, 1 - slot)

    pltpu.make_async_copy(y_ref.at[pl.ds(0, 2 * tc * rows_per_token)], buf.at[slot], sem.at[slot]).wait()
    info = info_ref[...]
    ff = (info[:, 2:3] * _load_token_tiles(buf.at[slot], tc)
          + info[:, 3:4] * _load_token_tiles(buf.at[slot], tc, tc * rows_per_token))
    out = x_ref[...] + mod_ref[0, 5:6, :] * ff
    if final:
        ms = jnp.mean(out * out, axis=-1, keepdims=True)
        out = out * lax.rsqrt(ms + RMS_EPS) * fg_ref[...]
    o_ref[...] = out


def _combine(pos, y_rows, x1, info, mod, final_g, B, S, l, final):
    T, D = x1.shape
    tc = _tile(S, COMBINE_ROW_TILE)
    per_seq = S // tc
    return pl.pallas_call(
        functools.partial(_combine_kernel, tc=tc, final=final),
        grid_spec=pltpu.PrefetchScalarGridSpec(
            num_scalar_prefetch=1,
            grid=(T // tc,),
            in_specs=[pl.BlockSpec(memory_space=pl.ANY),
                      pl.BlockSpec((tc, D), lambda i, pos: (i, 0)),
                      pl.BlockSpec((tc, LANES), lambda i, pos: (i, 0)),
                      pl.BlockSpec((1, 6, D), lambda i, pos: (l * B + i // per_seq, 0, 0)),
                      pl.BlockSpec((1, D), lambda i, pos: (0, 0))],
            out_specs=pl.BlockSpec((tc, D), lambda i, pos: (i, 0)),
            scratch_shapes=[pltpu.VMEM((2, 2 * tc * TOKEN_TILE[0], LANES), F32), pltpu.SemaphoreType.DMA((2,))]),
        out_shape=jax.ShapeDtypeStruct((T, D), F32),
        compiler_params=_cparams(("arbitrary",)),
        name="moe_combine",
    )(pos, y_rows, x1, info, mod, final_g)


def _moe(x1, h, info, mod, wg, wu, wd, final_g, B, S, l, li, final):
    T, D = x1.shape
    TK = T * TOP_K
    tm = _tile(TK // N_EXPERTS, EXPERT_ROW_TILE)
    n_rows = TK + (N_EXPERTS - 1) * tm
    flat_e = info[:, :TOP_K].astype(jnp.int32).reshape(TK)
    onehot = (flat_e[:, None] == jnp.arange(N_EXPERTS, dtype=jnp.int32)[None, :]).astype(jnp.int32)
    csum = jnp.cumsum(onehot, axis=0)
    counts = csum[-1]
    rank = jnp.sum(onehot * csum, axis=1) - 1
    padded = (counts + tm - 1) // tm * tm
    pad_ends = jnp.cumsum(padded)
    pad_starts = pad_ends - padded
    dest = pad_starts[flat_e] + rank
    row_tok = jnp.zeros((n_rows,), jnp.int32).at[dest].set(jnp.arange(TK, dtype=jnp.int32) // TOP_K)
    tile_start = jnp.arange(n_rows // tm, dtype=jnp.int32) * tm
    tile_e = jnp.minimum(jnp.searchsorted(pad_ends, tile_start, side='right'), N_EXPERTS - 1).astype(jnp.int32)
    n_used_rows = pad_ends[-1:].astype(jnp.int32)
    n_used_tiles = n_used_rows // tm
    tc = _tile(S, COMBINE_ROW_TILE)
    pos = dest.reshape(T // tc, tc, TOP_K).transpose(0, 2, 1).reshape(TK).astype(jnp.int32)

    y_rows = _experts(h, row_tok, tile_e, n_used_tiles, wg, wu, wd, li, tm)
    return _combine(pos, y_rows, x1, info, mod, final_g, B, S, l, final)


def kernel(x, c, w_ada, b_ada, norm1_g, norm2_g, final_g, w_in, w_out, rel_bias, rwkv_mu, rwkv_w0, rwkv_w2, rwkv_a0, rwkv_a2, rwkv_g2, rwkv_k_k, rwkv_k_a, rwkv_r_k, rwkv_ln_w, rwkv_ln_b, rwkv_v0, rwkv_v1, rwkv_v2, ffn_w_gate, ffn_w_up, ffn_w_down, moe_router_w, moe_router_b, moe_w_gate, moe_w_up, moe_w_down):
    B, S, D = x.shape
    L = w_in.shape[0]
    T = B * S
    W = RWKV_WIDTH

    mod = _ada_mod(c, w_ada, b_ada).reshape(L * B, 6, D)
    w_in_b = jnp.pad(w_in, ((0, 0), (0, 0), (0, N_IN_PAD - N_IN))).astype(BF16)
    w_out_b = w_out.astype(BF16)
    ffn_w = [w.astype(BF16) for w in (ffn_w_gate, ffn_w_up, ffn_w_down)]
    moe_w = [w.astype(BF16) for w in (moe_w_gate, moe_w_up, moe_w_down)]
    mu_p = jnp.pad(rwkv_mu, ((0, 0), (0, RWM_COLS + RWL_COLS - rwkv_mu.shape[1])))
    att_tables = _att_tables(rel_bias, S)
    zeros_w = jnp.zeros((1, W), F32)
    final_g2 = final_g.reshape(1, D)

    xf = x.reshape(T, D)
    v_first = None
    for l in range(L):
        qkv, qkv_mid, qkv_far, rwm, rwl = _inproj(xf, mod, norm1_g[l].reshape(1, D), w_in_b,
                                                   mu_p[l].reshape(1, -1), B, S, l)
        att = _attention(qkv, qkv_mid, qkv_far, att_tables, B, S)
        v0 = rwkv_v0[l - 1].reshape(1, W) if l > 0 else zeros_w
        vec = jnp.concatenate([rwkv_w0[l].reshape(1, W), rwkv_a0[l].reshape(1, W), rwkv_k_k[l].reshape(1, W),
                               rwkv_k_a[l].reshape(1, W), rwkv_r_k[l].reshape(1, W), rwkv_ln_w[l].reshape(1, W),
                               rwkv_ln_b[l].reshape(1, W), v0], axis=0)
        w2p = jnp.pad(rwkv_w2[l], ((0, LANES - D_DECAY_LORA), (0, 0)))
        a2p = jnp.pad(rwkv_a2[l], ((D_DECAY_LORA, 0), (0, 0)))
        g2p = jnp.pad(rwkv_g2[l], ((0, RWL_COLS - LANES - D_GATE_LORA), (0, 0)))
        if l > 0:
            v1p = jnp.pad(rwkv_v1[l - 1], ((0, 0), (0, LANES - D_MV_LORA)))
            v2p = jnp.pad(rwkv_v2[l - 1], ((0, LANES - D_MV_LORA), (0, 0)))
        else:
            v1p = v2p = None
        rw, v_first = _rwkv(rwm, rwl, v_first, vec, w2p, a2p, g2p, v1p, v2p, B, S)
        li = l // 2
        if l % 2 == 0:
            x1, h = _outproj(xf, att, rw, mod, norm2_g[l].reshape(1, D), w_out_b, B, S, l)
            xf = _ffn_dense(x1, h, mod, *ffn_w, B, S, l, li)
        else:
            router = (jnp.pad(moe_router_w[li], ((0, 0), (0, LANES - N_EXPERTS))),
                      jnp.pad(moe_router_b[li], (0, LANES - N_EXPERTS)).reshape(1, LANES))
            x1, h, info = _outproj(xf, att, rw, mod, norm2_g[l].reshape(1, D), w_out_b, B, S, l, router)
            xf = _moe(x1, h, info, mod, *moe_w, final_g2, B, S, l, li,
                      final=(l == L - 1))
    if L % 2 == 1:
        raise NotImplementedError("final norm is fused into the last (expert) layer")
    return xf.reshape(B, S, D)
```

```python
import functools
import math

import numpy as np
import jax
import jax.numpy as jnp
from jax import lax
from jax.experimental import pallas as pl
from jax.experimental.pallas import tpu as pltpu

F32 = jnp.float32
BF16 = jnp.bfloat16

D_MODEL = 1024
HEAD_DIM = 64
ATT_WIDTH = 512
RWKV_WIDTH = 512
N_HEADS = 8
HEAD_PAIRS = 4
LANES = 128
DILATED_GROUPS = ((128, 1), (512, 4), (2048, 16))
N_BUCKETS = 32
MAX_DISTANCE = 2048
NEG_INF = -1e30
D_DECAY_LORA = 64
D_AAA_LORA = 64
D_MV_LORA = 32
D_GATE_LORA = 160
N_IN = 3 * ATT_WIDTH + 3 * RWKV_WIDTH + D_DECAY_LORA + D_AAA_LORA + D_GATE_LORA
QKV_COLS = 3 * ATT_WIDTH
RWM_COLS = 3 * RWKV_WIDTH
RWL_COLS = 384
N_IN_PAD = QKV_COLS + RWM_COLS + RWL_COLS
GN_EPS = HEAD_DIM * 1e-5
RMS_EPS = 1e-6
N_EXPERTS = 8
TOP_K = 2
CHUNK = 64
VMEM_LIMIT = 56 * 1024 * 1024
LOG2_E = math.log2(math.e)
Q_SCALE = LOG2_E / math.sqrt(HEAD_DIM)
ROW_TILE = 512
FFN_ROW_TILE = 256
FFN_COL_TILE = 2816
EXPERT_ROW_TILE = 512
EXPERT_COL_TILE = 1792
COMBINE_ROW_TILE = 256
DMA_ISSUE_UNROLL = 8


def _cparams(sem):
    return pltpu.CompilerParams(dimension_semantics=sem, vmem_limit_bytes=VMEM_LIMIT)


def _tile(n, pref):
    t = min(n, pref)
    while n % t:
        t //= 2
    return t


def _mm(a, b):
    return jnp.dot(a.astype(BF16), b.astype(BF16), preferred_element_type=F32)


def _mm_nt(a, b):
    return lax.dot_general(a.astype(BF16), b.astype(BF16), (((1,), (1,)), ((), ())),
                           preferred_element_type=F32)


def _sigmoid(x):
    return 1.0 / (1.0 + jnp.exp(-x))


def _silu(x):
    return x * _sigmoid(x)


def _ada_kernel(c_ref, w_ref, b_ref, o_ref):
    ca = _silu(c_ref[...])
    o_ref[0] = _mm(ca, w_ref[0]) + b_ref[0]


def _ada_mod(c, w_ada, b_ada):
    L, D, N = w_ada.shape
    B = c.shape[0]
    tn = _tile(N, 1536)
    return pl.pallas_call(
        _ada_kernel,
        grid=(L, N // tn),
        in_specs=[pl.BlockSpec((B, D), lambda l, j: (0, 0)),
                  pl.BlockSpec((1, D, tn), lambda l, j: (l, 0, j)),
                  pl.BlockSpec((1, 1, tn), lambda l, j: (l, 0, j))],
        out_specs=pl.BlockSpec((1, B, tn), lambda l, j: (l, 0, j)),
        out_shape=jax.ShapeDtypeStruct((L, B, N), F32),
        compiler_params=_cparams(("arbitrary", "arbitrary")),
        name="ada_mod",
    )(c, w_ada, b_ada.reshape(L, 1, N))


def _rms_mod(xf, g, scale, shift):
    ms = jnp.mean(xf * xf, axis=-1, keepdims=True)
    return (xf * lax.rsqrt(ms + RMS_EPS) * g) * (1.0 + scale) + shift


def _inproj_kernel(x_ref, mod_ref, g_ref, w_ref, mu_ref, qkv_ref, qkv_mid_ref, qkv_far_ref, rwm_ref, rwl_ref,
                   carry_ref, qkv_scr):
    i = pl.program_id(1)
    tm = x_ref.shape[0]
    h = _rms_mod(x_ref[...], g_ref[...], mod_ref[0, 1:2, :], mod_ref[0, 0:1, :])
    acc = jnp.dot(h.astype(BF16), w_ref[0], preferred_element_type=F32)
    for c in range(QKV_COLS // LANES):
        cols = slice(c * LANES, (c + 1) * LANES)
        blk = acc[:, cols] * Q_SCALE if (c + 1) * LANES <= ATT_WIDTH else acc[:, cols]
        qkv_ref[:, cols] = blk.astype(BF16)
        qkv_scr[c] = blk
        for out_ref in (qkv_mid_ref, qkv_far_ref):
            d, rows = out_ref.shape[0], out_ref.shape[1]
            for r in range(d):
                out_ref[r, :, cols] = qkv_scr[c, pl.ds(r, rows, stride=d), :].astype(BF16)
    p = acc[:, QKV_COLS:]
    first = jnp.where(i == 0, 0.0, carry_ref[...])
    row = lax.broadcasted_iota(jnp.int32, p.shape, 0)
    prev = jnp.where(row == 0, first, pltpu.roll(p, 1, 0))
    carry_ref[...] = p[tm - 1:tm, :]
    pm = p + mu_ref[...] * (prev - p)
    rwm_ref[...] = pm[:, :RWM_COLS]
    rwl_ref[...] = pm[:, RWM_COLS:]


def _inproj(x, mod, g, w_in_l, mu_l, B, S, l):
    T, D = x.shape
    tm = _tile(S, ROW_TILE)
    nS = S // tm
    d2, d3 = DILATED_GROUPS[1][1], DILATED_GROUPS[2][1]
    return pl.pallas_call(
        _inproj_kernel,
        grid=(B, nS),
        in_specs=[pl.BlockSpec((tm, D), lambda b, i: (b * nS + i, 0)),
                  pl.BlockSpec((1, 6, D), lambda b, i: (l * B + b, 0, 0)),
                  pl.BlockSpec((1, D), lambda b, i: (0, 0)),
                  pl.BlockSpec((1, D, N_IN_PAD), lambda b, i: (l, 0, 0)),
                  pl.BlockSpec((1, RWM_COLS + RWL_COLS), lambda b, i: (0, 0))],
        out_specs=[pl.BlockSpec((tm, QKV_COLS), lambda b, i: (b * nS + i, 0)),
                   pl.BlockSpec((None, d2, tm // d2, QKV_COLS), lambda b, i: (b, 0, i, 0)),
                   pl.BlockSpec((None, d3, tm // d3, QKV_COLS), lambda b, i: (b, 0, i, 0)),
                   pl.BlockSpec((tm, RWM_COLS), lambda b, i: (b * nS + i, 0)),
                   pl.BlockSpec((tm, RWL_COLS), lambda b, i: (b * nS + i, 0))],
        out_shape=[jax.ShapeDtypeStruct((T, QKV_COLS), BF16),
                   jax.ShapeDtypeStruct((B, d2, S // d2, QKV_COLS), BF16),
                   jax.ShapeDtypeStruct((B, d3, S // d3, QKV_COLS), BF16),
                   jax.ShapeDtypeStruct((T, RWM_COLS), F32),
                   jax.ShapeDtypeStruct((T, RWL_COLS), F32)],
        scratch_shapes=[pltpu.VMEM((1, RWM_COLS + RWL_COLS), F32), pltpu.VMEM((QKV_COLS // LANES, tm, LANES), F32)],
        compiler_params=_cparams(("arbitrary", "arbitrary")),
        name="inproj",
    )(x, mod, g, w_in_l, mu_l)


def _t5_bucket(n):
    max_exact = N_BUCKETS // 2
    large = max_exact + (np.log(np.maximum(n, 1) / max_exact) / np.log(MAX_DISTANCE / max_exact)
                         * (N_BUCKETS - max_exact)).astype(np.int32)
    large = np.minimum(large, N_BUCKETS - 1)
    return np.where(n < max_exact, n, large).astype(np.int32)


ATT_BLOCK = 128
ATT_KEYS = 256


def _toeplitz(vec, c0, rows, cols):
    H, n = vec.shape
    period = rows + cols - 1
    w = jnp.concatenate([jnp.full((H, cols - 1), NEG_INF, F32), vec, jnp.full((H, rows), NEG_INF, F32)], axis=1)
    rw = w[:, ::-1]
    a = w.shape[1] - 1 - (c0 + cols - 1)
    z = jnp.concatenate([rw[:, a:a + cols], rw[:, a - (rows - 1):a]], axis=1)
    flat = jnp.tile(z, (1, rows))[:, :rows * (period - 1)]
    return flat.reshape(H, rows, period - 1)[..., :cols]


def _att_tables(rel_bias, S):
    (w1, d1), (w2, d2), (w3, d3) = DILATED_GROUPS
    assert d1 == 1 and w1 == ATT_KEYS - ATT_BLOCK and w2 // d2 == w1 and S % (d3 * 8) == 0
    off = np.arange(S)
    mult = np.zeros(S, np.int64)
    for w, d in DILATED_GROUPS:
        mult += ((off % d == 0) & (off <= w)).astype(np.int64)
    logm = np.where(mult > 0, np.log(np.maximum(mult, 1)), NEG_INF).astype(np.float32)
    per_off = (rel_bias[_t5_bucket(off)].T.astype(F32) + jnp.asarray(logm)[None, :]) * LOG2_E
    u2 = np.arange(S // d2)
    u3 = np.arange(S // d3)
    band = jnp.where(jnp.asarray(off <= w1)[None], per_off, NEG_INF)
    mid = jnp.where(jnp.asarray((u2 > w1 // d2) & (u2 <= w2 // d2))[None], per_off[:, ::d2], NEG_INF)
    far = jnp.where(jnp.asarray((u3 > w2 // d3) & (u3 <= w3 // d3))[None], per_off[:, ::d3], NEG_INF)
    shift = ATT_KEYS - ATT_BLOCK
    t_band = jnp.stack([_toeplitz(band, 0, ATT_BLOCK, ATT_KEYS), _toeplitz(band, shift, ATT_BLOCK, ATT_KEYS)], axis=1)
    t_mid = jnp.stack([_toeplitz(mid, 0, ATT_BLOCK, ATT_KEYS), _toeplitz(mid, shift, ATT_BLOCK, ATT_KEYS)], axis=1)
    t_far = _toeplitz(far, 0, S // d3, S // d3)
    return t_band, t_mid, t_far


def _attend(units):
    idx = range(len(units))
    heads = range(2)
    head0 = [lax.broadcasted_iota(jnp.int32, u[0].shape, 1) < HEAD_DIM for u in units]
    qh = [[jnp.where(head0[i], units[i][0], jnp.zeros_like(units[i][0])),
           jnp.where(head0[i], jnp.zeros_like(units[i][0]), units[i][0])] for i in idx]
    s = [[lax.dot_general(qh[i][h], units[i][1], (((1,), (1,)), ((), ())), preferred_element_type=F32)
          + units[i][3 + h] for h in heads] for i in idx]
    mx = [[jnp.max(s[i][h], axis=-1, keepdims=True) for h in heads] for i in idx]
    p = [[jnp.exp2(s[i][h] - mx[i][h]) for h in heads] for i in idx]
    ps = [[jnp.sum(p[i][h], axis=-1, keepdims=True) for h in heads] for i in idx]
    pv = [[jnp.dot(p[i][h].astype(BF16), units[i][2], preferred_element_type=F32) for h in heads] for i in idx]
    out = []
    for i in idx:
        shape = units[i][0].shape
        out.append((jnp.where(head0[i], jnp.broadcast_to(mx[i][0], shape), jnp.broadcast_to(mx[i][1], shape)),
                    jnp.where(head0[i], jnp.broadcast_to(ps[i][0], shape), jnp.broadcast_to(ps[i][1], shape)),
                    jnp.where(head0[i], pv[i][0], pv[i][1])))
    return out


def _attn_kernel(qa_ref, ka_ref, va_ref, qm_ref, km_ref, vm_ref, qf_ref, kf_ref, vf_ref,
                 tband_ref, tmid_ref, tfar_ref, o_ref, part_ref):
    R, K = ATT_BLOCK, ATT_KEYS
    S = qa_ref.shape[0]
    n_mid_res, n_mid = qm_ref.shape[0], qm_ref.shape[1]
    n_far_res, n_far = qf_ref.shape[0], qf_ref.shape[1]
    group = 8

    def window(blk):
        start = pl.multiple_of(jnp.maximum(blk * R - (K - R), 0), R)
        return pl.ds(pl.multiple_of(blk * R, R), R), pl.ds(start, K), jnp.minimum(blk, 1)

    def park(cls, rows, results):
        for nat, parts in zip(rows, results):
            for j in range(3):
                part_ref[cls, j, nat, :] = parts[j]

    def band_body(it, _):
        units, rows = [], []
        for g in range(group):
            qrows, krows, var = window(it * group + g)
            rows.append(qrows)
            units.append((qa_ref[qrows, :], ka_ref[krows, :], va_ref[krows, :],
                          tband_ref[0, var], tband_ref[1, var]))
        park(0, rows, _attend(units))
        return 0

    lax.fori_loop(0, S // (R * group), band_body, 0)

    blocks_per_step = group // n_mid_res

    def mid_body(it, _):
        units, rows = [], []
        for g in range(blocks_per_step):
            blk = it * blocks_per_step + g
            qrows, krows, var = window(blk)
            for r in range(n_mid_res):
                rows.append(pl.ds(blk * (R * n_mid_res) + r, R, stride=n_mid_res))
                units.append((qm_ref[r, qrows, :], km_ref[r, krows, :], vm_ref[r, krows, :],
                              tmid_ref[0, var], tmid_ref[1, var]))
        park(1, rows, _attend(units))
        return 0

    lax.fori_loop(0, n_mid // (R * blocks_per_step), mid_body, 0)

    def far_body(it, _):
        units, rows = [], []
        for g in range(group):
            r = it * group + g
            rows.append(pl.ds(r, n_far, stride=n_far_res))
            units.append((qf_ref[r], kf_ref[r], vf_ref[r], tfar_ref[0], tfar_ref[1]))
        park(2, rows, _attend(units))
        return 0

    lax.fori_loop(0, n_far_res // group, far_body, 0)

    def merge_body(i, _):
        rows = pl.ds(pl.multiple_of(i * R, R), R)
        m = [part_ref[c, 0, rows, :] for c in range(3)]
        top = jnp.maximum(jnp.maximum(m[0], m[1]), m[2])
        w = [jnp.exp2(m[c] - top) for c in range(3)]
        total = sum(w[c] * part_ref[c, 1, rows, :] for c in range(3))
        acc = sum(w[c] * part_ref[c, 2, rows, :] for c in range(3))
        o_ref[rows, :] = (acc / total).astype(o_ref.dtype)
        return 0

    lax.fori_loop(0, S // R, merge_body, 0)


def _attention(qkv, qkv_mid, qkv_far, tables, B, S):
    T = qkv.shape[0]
    t_band, t_mid, t_far = tables
    d2, d3 = DILATED_GROUPS[1][1], DILATED_GROUPS[2][1]
    assert S % (ATT_BLOCK * 8) == 0 and S // d2 >= ATT_KEYS
    nat = lambda c: pl.BlockSpec((S, LANES), lambda hp, b: (b, c * HEAD_PAIRS + hp))
    mid = lambda c: pl.BlockSpec((None, d2, S // d2, LANES), lambda hp, b: (b, 0, 0, c * HEAD_PAIRS + hp))
    far = lambda c: pl.BlockSpec((None, d3, S // d3, LANES), lambda hp, b: (b, 0, 0, c * HEAD_PAIRS + hp))
    return pl.pallas_call(
        _attn_kernel,
        grid=(HEAD_PAIRS, B),
        in_specs=[nat(0), nat(1), nat(2), mid(0), mid(1), mid(2), far(0), far(1), far(2),
                  pl.BlockSpec((2,) + t_band.shape[1:], lambda hp, b: (hp, 0, 0, 0)),
                  pl.BlockSpec((2,) + t_mid.shape[1:], lambda hp, b: (hp, 0, 0, 0)),
                  pl.BlockSpec((2,) + t_far.shape[1:], lambda hp, b: (hp, 0, 0))],
        out_specs=pl.BlockSpec((S, LANES), lambda hp, b: (b, hp)),
        out_shape=jax.ShapeDtypeStruct((T, ATT_WIDTH), BF16),
        scratch_shapes=[pltpu.VMEM((3, 3, S, LANES), F32)],
        compiler_params=_cparams(("arbitrary", "arbitrary")),
        name="dilated_attn",
    )(qkv, qkv, qkv, qkv_mid, qkv_mid, qkv_mid, qkv_far, qkv_far, qkv_far, t_band, t_mid, t_far)


def _split2(x):
    hi = x.astype(BF16)
    lo = (x - hi.astype(F32)).astype(BF16)
    return hi, lo


def _head_sum(x, m2):
    hi, lo = _split2(x)
    return jnp.dot(jnp.concatenate([hi, lo], axis=1), m2, preferred_element_type=F32)


def _rwkv_kernel(*refs, has_vres, n_chunks, n_batch):
    if has_vres:
        (rwm_ref, rwl_ref, vf_ref, vec_ref, w2_ref, a2_ref, g2_ref, v1_ref, v2_ref,
         o_ref, state_ref, r_s, k_s, v_s, lw_s, kk_s, kb_s, y_s,
         ar_s, inv_s, bk_s, nrbk_s, akv_s, dec_s) = refs
    else:
        (rwm_ref, rwl_ref, vec_ref, w2_ref, a2_ref, g2_ref,
         o_ref, vf_out_ref, state_ref, r_s, k_s, v_s, lw_s, kk_s, kb_s, y_s,
         ar_s, inv_s, bk_s, nrbk_s, akv_s, dec_s) = refs
    C = CHUNK
    W = RWKV_WIDTH
    rows_per_batch = rwm_ref.shape[1]

    def stacked(ref, cols=slice(None)):
        return jnp.concatenate([ref[b, :, cols] for b in range(n_batch)], axis=0)

    def unstack(ref, x, cols=slice(None)):
        for b in range(n_batch):
            ref[b, :, cols] = x[b * rows_per_batch:(b + 1) * rows_per_batch].astype(ref.dtype)

    @pl.when(pl.program_id(1) == 0)
    def _():
        state_ref[...] = jnp.zeros_like(state_ref)

    ri = lax.broadcasted_iota(jnp.int32, (2 * LANES, LANES), 0)
    ci = lax.broadcasted_iota(jnp.int32, (2 * LANES, LANES), 1)
    m2 = ((ri % LANES) // HEAD_DIM == ci // HEAD_DIM).astype(BF16)

    vec = vec_ref[...]
    w0, a0, k_k, k_a, r_k, ln_w, ln_b, v0 = [vec[i:i + 1, :] for i in range(8)]

    r = stacked(rwm_ref, slice(0, W))
    k = stacked(rwm_ref, slice(W, 2 * W))
    v = stacked(rwm_ref, slice(2 * W, 3 * W))
    lora = stacked(rwl_ref, slice(0, LANES))
    w_raw = w0 + _mm(jnp.tanh(lora), w2_ref[...])
    lw_s[...] = -math.exp(-0.5) * _sigmoid(w_raw)
    a = _sigmoid(a0 + _mm(lora, a2_ref[...]))
    g = _mm(_sigmoid(stacked(rwl_ref, slice(LANES, RWL_COLS))), g2_ref[...])
    if has_vres:
        mix = _sigmoid(v0 + _mm(_mm(v, v1_ref[...]), v2_ref[...]))
        v = v + (stacked(vf_ref) - v) * mix
    else:
        unstack(vf_out_ref, v)
    kk = k * k_k
    for p in range(HEAD_PAIRS):
        ls = slice(p * LANES, (p + 1) * LANES)
        kkp = kk[:, ls]
        kkp = kkp * jnp.minimum(lax.rsqrt(_head_sum(kkp * kkp, m2)), 1e12)
        kk_s[:, ls] = kkp
        kb_s[:, ls] = kkp * a[:, ls]
    kmod = k * (1.0 + (a - 1.0) * k_a)
    r_s[...] = r
    k_s[...] = kmod
    v_s[...] = v

    ri = lax.broadcasted_iota(jnp.int32, (LANES, LANES), 0)
    ci = lax.broadcasted_iota(jnp.int32, (LANES, LANES), 1)
    same_head = (ri // C) == (ci // C)
    strict = same_head & ((ri % C) > (ci % C))
    incl = same_head & ((ri % C) >= (ci % C))
    eye = (ri == ci).astype(F32)
    lane = lax.broadcasted_iota(jnp.int32, (C, LANES), 1)
    head0 = lane < HEAD_DIM
    tri_r = lax.broadcasted_iota(jnp.int32, (C, 3 * C), 0)
    tri_c = lax.broadcasted_iota(jnp.int32, (C, 3 * C), 1)
    tri3 = ((tri_c % C) <= tri_r).astype(BF16)

    lane_slices = [slice(p * LANES, (p + 1) * LANES) for p in range(HEAD_PAIRS)]

    def cumsum(x):
        hi = x.astype(BF16)
        rem = x - hi.astype(F32)
        mid = rem.astype(BF16)
        lo = (rem - mid.astype(F32)).astype(BF16)
        return jnp.dot(tri3, jnp.concatenate([hi, mid, lo], axis=0), preferred_element_type=F32)

    def per_head_rows(x, y):
        zero = jnp.zeros_like(x)
        return jnp.concatenate([jnp.where(head0, x, zero), jnp.where(head0, zero, x),
                                jnp.where(head0, y, zero), jnp.where(head0, zero, y)], axis=0)

    def intra_body(it, _):
        units = [(it * chunk_group + g, p) for g in range(chunk_group) for p in range(HEAD_PAIRS)]
        idx = range(len(units))
        rows = [pl.ds(pl.multiple_of(c * C, C), C) for c, _ in units]
        ls = [lane_slices[p] for _, p in units]
        rc = [r_s[rows[u], ls[u]] for u in idx]
        kc = [k_s[rows[u], ls[u]] for u in idx]
        vc = [v_s[rows[u], ls[u]] for u in idx]
        lw = [lw_s[rows[u], ls[u]] for u in idx]
        kkc = [kk_s[rows[u], ls[u]] for u in idx]
        kbc = [kb_s[rows[u], ls[u]] for u in idx]
        cum = [cumsum(lw[u]) for u in idx]
        total = [cum[u][C - 1:C, :] for u in idx]
        g_inv = [jnp.exp(-cum[u]) for u in idx]
        a_t = [-kkc[u] * jnp.exp(cum[u] - lw[u]) for u in idx]
        r_t = [rc[u] * jnp.exp(cum[u]) for u in idx]
        b_t = [kbc[u] * g_inv[u] for u in idx]
        k_t = [kc[u] * g_inv[u] for u in idx]
        to_end = [jnp.exp(total[u] - cum[u]) for u in idx]
        gram = [_mm_nt(per_head_rows(a_t[u], r_t[u]), per_head_rows(b_t[u], k_t[u])) for u in idx]
        n_ab = [jnp.where(strict, gram[u][:LANES, :LANES], 0.0) for u in idx]
        n_ak = [jnp.where(strict, gram[u][:LANES, LANES:], 0.0) for u in idx]
        n_rb = [jnp.where(incl, gram[u][LANES:, :LANES], 0.0) for u in idx]
        n_rk = [jnp.where(incl, gram[u][LANES:, LANES:], 0.0) for u in idx]
        inv = [eye + n_ab[u] for u in idx]
        pw = [_mm(n_ab[u], n_ab[u]) for u in idx]
        n_rounds = int(math.log2(C)) - 1
        for i in range(n_rounds):
            if i + 1 < n_rounds:
                both = [_mm(pw[u], jnp.concatenate([inv[u], pw[u]], axis=1)) for u in idx]
                inv = [inv[u] + both[u][:, :LANES] for u in idx]
                pw = [both[u][:, LANES:] for u in idx]
            else:
                inv = [inv[u] + _mm(pw[u], inv[u]) for u in idx]
        for u, (c, p) in enumerate(units):
            ar_s[c, p] = jnp.concatenate([a_t[u], r_t[u]], axis=0).astype(BF16)
            inv_s[c, p] = inv[u].astype(BF16)
            akv_s[c, p] = _mm(n_ak[u], jnp.concatenate([vc[u], vc[u]], axis=0))
            nrbk_s[c, p] = jnp.concatenate([n_rb[u], n_rk[u]], axis=1).astype(BF16)
            bk_s[c, p] = jnp.concatenate([kbc[u] * to_end[u], kc[u] * to_end[u]], axis=0).astype(BF16)
            dec_s[c, p] = jnp.broadcast_to(jnp.exp(total[u]), (8, LANES))
        return 0

    chunks_per_batch = n_chunks // n_batch

    def state_body(step, _):
        units = [(b, b * chunks_per_batch + step, p) for b in range(n_batch) for p in range(HEAD_PAIRS)]
        idx = range(len(units))
        rows = [pl.ds(pl.multiple_of(c * C, C), C) for _, c, _ in units]
        ls = [lane_slices[p] for _, _, p in units]
        vc = [v_s[rows[u], ls[u]] for u in idx]
        st = [state_ref[b, p] for b, _, p in units]
        ah = [_mm_nt(ar_s[c, p], st[u]) for u, (_, c, p) in enumerate(units)]
        u_stack = [_mm(inv_s[c, p], jnp.concatenate([ah[u][:C], ah[u][:C]], axis=0) + akv_s[c, p])
                   for u, (_, c, p) in enumerate(units)]
        y_stack = [jnp.concatenate([ah[u][C:], ah[u][C:]], axis=0)
                   + _mm(nrbk_s[c, p], jnp.concatenate([u_stack[u], vc[u], vc[u]], axis=0))
                   for u, (_, c, p) in enumerate(units)]
        for u in idx:
            y_s[rows[u], ls[u]] = jnp.where(head0, y_stack[u][:C], y_stack[u][C:])
        uv_t = [jnp.concatenate([jnp.where(head0, u_stack[u][:C], u_stack[u][C:]), vc[u]], axis=0).T
                for u in idx]
        st_new = [st[u] * dec_s[c, p][0:1, :] + _mm(uv_t[u], bk_s[c, p]) for u, (_, c, p) in enumerate(units)]
        for u, (b, _, p) in enumerate(units):
            state_ref[b, p] = jnp.where(same_head, st_new[u], 0.0)
        return 0

    chunk_group = 4 if n_chunks % 4 == 0 else (2 if n_chunks % 2 == 0 else 1)
    lax.fori_loop(0, n_chunks // chunk_group, intra_body, 0)
    lax.fori_loop(0, chunks_per_batch, state_body, 0)

    for p in range(HEAD_PAIRS):
        ls = slice(p * LANES, (p + 1) * LANES)
        y = y_s[:, ls]
        mean = _head_sum(y, m2) * (1.0 / HEAD_DIM)
        d = y - mean
        var = _head_sum(d * d, m2) * (1.0 / HEAD_DIM)
        yn = d * lax.rsqrt(var + GN_EPS) * ln_w[:, ls] + ln_b[:, ls]
        bonus = _head_sum(r_s[:, ls] * k_s[:, ls] * r_k[:, ls], m2) * v_s[:, ls]
        unstack(o_ref, (yn + bonus) * g[:, ls], ls)


def _rwkv(rwm, rwl, v_first, vec, w2p, a2p, g2p, v1p, v2p, B, S):
    T = rwm.shape[0]
    W = RWKV_WIDTH
    nb = 4 if B % 4 == 0 else (2 if B % 2 == 0 else 1)
    tb = _tile(S, ROW_TILE // nb)
    n_chunks = nb * tb // CHUNK
    has_vres = v_first is not None
    stack = lambda a: a.reshape(B // nb, nb, S, a.shape[-1])
    row = lambda cols: pl.BlockSpec((None, nb, tb, cols), lambda b, i: (b, 0, i, 0))
    full = lambda b, i: (0, 0)
    in_specs = [row(RWM_COLS), row(RWL_COLS)]
    args = [stack(rwm), stack(rwl)]
    if has_vres:
        in_specs.append(row(W))
        args.append(stack(v_first))
    in_specs += [pl.BlockSpec(vec.shape, full), pl.BlockSpec(w2p.shape, full),
                 pl.BlockSpec(a2p.shape, full), pl.BlockSpec(g2p.shape, full)]
    args += [vec, w2p, a2p, g2p]
    if has_vres:
        in_specs += [pl.BlockSpec(v1p.shape, full), pl.BlockSpec(v2p.shape, full)]
        args += [v1p, v2p]
    out_specs = [row(W)]
    out_shape = [jax.ShapeDtypeStruct((B // nb, nb, S, W), BF16)]
    if not has_vres:
        out_specs.append(row(W))
        out_shape.append(jax.ShapeDtypeStruct((B // nb, nb, S, W), F32))
    outs = pl.pallas_call(
        functools.partial(_rwkv_kernel, has_vres=has_vres, n_chunks=n_chunks, n_batch=nb),
        grid=(B // nb, S // tb),
        in_specs=in_specs,
        out_specs=out_specs,
        out_shape=out_shape,
        scratch_shapes=([pltpu.VMEM((nb, HEAD_PAIRS, LANES, LANES), F32)] + [pltpu.VMEM((nb * tb, W), F32)] * 7
                        + [pltpu.VMEM((n_chunks, HEAD_PAIRS, 2 * CHUNK, LANES), BF16)] * 3
                        + [pltpu.VMEM((n_chunks, HEAD_PAIRS, 2 * CHUNK, 2 * LANES), BF16),
                           pltpu.VMEM((n_chunks, HEAD_PAIRS, 2 * CHUNK, LANES), F32),
                           pltpu.VMEM((n_chunks, HEAD_PAIRS, 8, LANES), F32)]),
        compiler_params=_cparams(("arbitrary", "arbitrary")),
        name="rwkv7",
    )(*args)
    outs = [o.reshape(T, W) for o in outs]
    return (outs[0], v_first) if has_vres else (outs[0], outs[1])


TOKEN_TILE = (8, LANES)


def _store_token_tiles(ref, x):
    n = x.shape[0]
    for s in range(TOKEN_TILE[0]):
        ref[pl.ds(s, n, stride=TOKEN_TILE[0]), :] = x[:, s * LANES:(s + 1) * LANES]


def _load_token_tiles(ref, n, start=0):
    return jnp.concatenate([ref[pl.ds(start + s, n, stride=TOKEN_TILE[0]), :] for s in range(TOKEN_TILE[0])],
                           axis=1)


def _outproj_kernel(*refs, moe):
    if moe:
        (x_ref, att_ref, rw_ref, mod_ref, g_ref, w_ref, rw_w_ref, rw_b_ref,
         x1_ref, h_ref, info_ref) = refs
    else:
        x_ref, att_ref, rw_ref, mod_ref, g_ref, w_ref, x1_ref, h_ref = refs
    mix = (jnp.dot(att_ref[...], w_ref[0, :ATT_WIDTH, :], preferred_element_type=F32)
           + jnp.dot(rw_ref[...], w_ref[0, ATT_WIDTH:, :], preferred_element_type=F32))
    x1 = x_ref[...] + mod_ref[0, 2:3, :] * mix
    x1_ref[...] = x1
    h = _rms_mod(x1, g_ref[...], mod_ref[0, 4:5, :], mod_ref[0, 3:4, :])
    if moe:
        _store_token_tiles(h_ref, h)
    else:
        h_ref[...] = h.astype(h_ref.dtype)
    if moe:
        h_hi, h_lo = _split2(h)
        w_hi, w_lo = _split2(rw_w_ref[...])
        logits = (jnp.dot(h_hi, w_hi, preferred_element_type=F32)
                  + jnp.dot(h_hi, w_lo, preferred_element_type=F32)
                  + jnp.dot(h_lo, w_hi, preferred_element_type=F32)) + rw_b_ref[...]
        lane = lax.broadcasted_iota(jnp.int32, logits.shape, 1)
        logits = jnp.where(lane < N_EXPERTS, logits, -jnp.inf)
        m1 = jnp.max(logits, axis=-1, keepdims=True)
        i1 = jnp.min(jnp.where(logits == m1, lane, LANES), axis=-1, keepdims=True)
        rest = jnp.where(lane == i1, -jnp.inf, logits)
        m2 = jnp.max(rest, axis=-1, keepdims=True)
        i2 = jnp.min(jnp.where(rest == m2, lane, LANES), axis=-1, keepdims=True)
        e = jnp.exp(m2 - m1)
        g1 = 1.0 / (1.0 + e)
        g2 = e / (1.0 + e)
        info = jnp.where(lane == 0, i1.astype(F32),
                         jnp.where(lane == 1, i2.astype(F32),
                                   jnp.where(lane == 2, g1, jnp.where(lane == 3, g2, 0.0))))
        info_ref[...] = info


def _outproj(x, att, rw, mod, g, w_out_b, B, S, l, router=None):
    T, D = x.shape
    tm = _tile(S, ROW_TILE)
    nS = S // tm
    moe = router is not None
    row = lambda b, i: (b * nS + i, 0)
    full = lambda b, i: (0, 0)
    in_specs = [pl.BlockSpec((tm, D), row), pl.BlockSpec((tm, ATT_WIDTH), row),
                pl.BlockSpec((tm, RWKV_WIDTH), row),
                pl.BlockSpec((1, 6, D), lambda b, i: (l * B + b, 0, 0)),
                pl.BlockSpec((1, D), full),
                pl.BlockSpec((1, D, D), lambda b, i: (l, 0, 0))]
    args = [x, att, rw, mod, g, w_out_b]
    if moe:
        out_specs = [pl.BlockSpec((tm, D), row), pl.BlockSpec((tm * TOKEN_TILE[0], LANES), row)]
        out_shape = [jax.ShapeDtypeStruct((T, D), F32), jax.ShapeDtypeStruct((T * TOKEN_TILE[0], LANES), F32)]
    else:
        out_specs = [pl.BlockSpec((tm, D), row), pl.BlockSpec((tm, D), row)]
        out_shape = [jax.ShapeDtypeStruct((T, D), F32), jax.ShapeDtypeStruct((T, D), BF16)]
    if moe:
        in_specs += [pl.BlockSpec((D, LANES), full), pl.BlockSpec((1, LANES), full)]
        args += list(router)
        out_specs.append(pl.BlockSpec((tm, LANES), row))
        out_shape.append(jax.ShapeDtypeStruct((T, LANES), F32))
    return pl.pallas_call(
        functools.partial(_outproj_kernel, moe=moe),
        grid=(B, nS),
        in_specs=in_specs,
        out_specs=out_specs,
        out_shape=out_shape,
        compiler_params=_cparams(("arbitrary", "arbitrary")),
        name="outproj",
    )(*args)


def _ffn_kernel(x_ref, h_ref, mod_ref, wg_ref, wu_ref, wd_ref, o_ref, acc_ref):
    f = pl.program_id(1)

    @pl.when(f == 0)
    def _():
        acc_ref[...] = jnp.zeros_like(acc_ref)

    hb = h_ref[...]
    gate = jnp.dot(hb, wg_ref[0], preferred_element_type=F32)
    up = jnp.dot(hb, wu_ref[0], preferred_element_type=F32)
    mid = (_silu(gate) * up).astype(BF16)
    acc_ref[...] += jnp.dot(mid, wd_ref[0], preferred_element_type=F32)

    @pl.when(f == pl.num_programs(1) - 1)
    def _():
        o_ref[...] = x_ref[...] + mod_ref[0, 5:6, :] * acc_ref[...]


def _ffn_dense(x1, h, mod, wg, wu, wd, B, S, l, li):
    T, D = x1.shape
    F = wg.shape[-1]
    tm = _tile(S, FFN_ROW_TILE)
    tf = _tile(F, FFN_COL_TILE)
    per_seq = S // tm
    return pl.pallas_call(
        _ffn_kernel,
        grid=(T // tm, F // tf),
        in_specs=[pl.BlockSpec((tm, D), lambda i, f: (i, 0)),
                  pl.BlockSpec((tm, D), lambda i, f: (i, 0)),
                  pl.BlockSpec((1, 6, D), lambda i, f: (l * B + i // per_seq, 0, 0)),
                  pl.BlockSpec((1, D, tf), lambda i, f: (li, 0, f)),
                  pl.BlockSpec((1, D, tf), lambda i, f: (li, 0, f)),
                  pl.BlockSpec((1, tf, D), lambda i, f: (li, f, 0))],
        out_specs=pl.BlockSpec((tm, D), lambda i, f: (i, 0)),
        out_shape=jax.ShapeDtypeStruct((T, D), F32),
        scratch_shapes=[pltpu.VMEM((tm, D), F32)],
        compiler_params=_cparams(("arbitrary", "arbitrary")),
        name="ffn_dense",
    )(x1, h, mod, wg, wu, wd)


def _experts_kernel(te_ref, nu_ref, tok_ref, h_ref, wg_ref, wu_ref, wd_ref, y_ref,
                    land_ref, xb_ref, acc_ref, sem):
    i = pl.program_id(0)
    f = pl.program_id(1)
    tm = xb_ref.shape[0]
    rows_per_token = TOKEN_TILE[0]
    n_used = nu_ref[0]

    def start_gather(tile):
        base = tile * tm

        def body(j, _):
            src = pl.multiple_of(tok_ref[base + j] * rows_per_token, rows_per_token)
            dst = pl.multiple_of(j * rows_per_token, rows_per_token)
            pltpu.make_async_copy(h_ref.at[pl.ds(src, rows_per_token)],
                                  land_ref.at[pl.ds(dst, rows_per_token)], sem).start()
            return 0

        lax.fori_loop(0, tm, body, 0, unroll=DMA_ISSUE_UNROLL)

    @pl.when(i < n_used)
    def _():
        @pl.when(f == 0)
        def _():
            @pl.when(i == 0)
            def _():
                start_gather(0)

            pltpu.make_async_copy(h_ref.at[pl.ds(0, tm * rows_per_token)], land_ref, sem).wait()
            xb_ref[...] = _load_token_tiles(land_ref, tm).astype(BF16)
            acc_ref[...] = jnp.zeros_like(acc_ref)

            @pl.when(i + 1 < n_used)
            def _():
                start_gather(i + 1)

        xb = xb_ref[...]
        gate = jnp.dot(xb, wg_ref[0, 0].astype(BF16), preferred_element_type=F32)
        up = jnp.dot(xb, wu_ref[0, 0].astype(BF16), preferred_element_type=F32)
        mid = (_silu(gate) * up).astype(BF16)
        acc_ref[...] += jnp.dot(mid, wd_ref[0, 0].astype(BF16), preferred_element_type=F32)

        @pl.when(f == pl.num_programs(1) - 1)
        def _():
            _store_token_tiles(y_ref, acc_ref[...])

    @pl.when((i >= n_used) & (f == 0))
    def _():
        y_ref[...] = jnp.zeros_like(y_ref)


def _experts(h, row_tok, tile_e, n_used, wg, wu, wd, li, tm):
    n_rows = row_tok.shape[0]
    D = wg.shape[-2]
    rpt = TOKEN_TILE[0]
    assert h.shape[0] >= tm * rpt
    F = wg.shape[-1]
    tf = _tile(F, EXPERT_COL_TILE)
    nf = F // tf

    def rows(i, f, te, nu, tok):
        return (i, 0)

    def wcol(i, f, te, nu, tok):
        return (li, te[jnp.minimum(i, nu[0] - 1)], 0, jnp.where(i < nu[0], f, nf - 1))

    def wrow(i, f, te, nu, tok):
        return (li, te[jnp.minimum(i, nu[0] - 1)], jnp.where(i < nu[0], f, nf - 1), 0)

    return pl.pallas_call(
        _experts_kernel,
        grid_spec=pltpu.PrefetchScalarGridSpec(
            num_scalar_prefetch=3,
            grid=(n_rows // tm, nf),
            in_specs=[pl.BlockSpec(memory_space=pl.ANY),
                      pl.BlockSpec((1, 1, D, tf), wcol),
                      pl.BlockSpec((1, 1, D, tf), wcol),
                      pl.BlockSpec((1, 1, tf, D), wrow)],
            out_specs=pl.BlockSpec((tm * rpt, LANES), rows),
            scratch_shapes=[pltpu.VMEM((tm * rpt, LANES), F32), pltpu.VMEM((tm, D), BF16),
                            pltpu.VMEM((tm, D), F32), pltpu.SemaphoreType.DMA(())]),
        out_shape=jax.ShapeDtypeStruct((n_rows * rpt, LANES), F32),
        compiler_params=_cparams(("arbitrary", "arbitrary")),
        name="moe_experts",
    )(tile_e, n_used, row_tok, h, wg, wu, wd)


def _combine_kernel(pos_ref, y_ref, x_ref, info_ref, mod_ref, fg_ref, o_ref, buf, sem, *, tc, final):
    i = pl.program_id(0)
    slot = i % 2
    rows_per_token = TOKEN_TILE[0]

    def start_gather(tile, dst_slot):
        base = tile * (2 * tc)

        def body(j, _):
            src = pl.multiple_of(pos_ref[base + j] * rows_per_token, rows_per_token)
            dst = pl.multiple_of(j * rows_per_token, rows_per_token)
            pltpu.make_async_copy(y_ref.at[pl.ds(src, rows_per_token)],
                                  buf.at[dst_slot, pl.ds(dst, rows_per_token)], sem.at[dst_slot]).start()
            return 0

        lax.fori_loop(0, 2 * tc, body, 0, unroll=DMA_ISSUE_UNROLL)

    @pl.when(i == 0)
    def _():
        start_gather(0, 0)

    @pl.when(i + 1 < pl.num_programs(0))
    def _():
        start_gather(i + 1, 1 - slot)

    pltpu.make_async_copy(y_ref.at[pl.ds(0, 2 * tc * rows_per_token)], buf.at[slot], sem.at[slot]).wait()
    info = info_ref[...]
    ff = (info[:, 2:3] * _load_token_tiles(buf.at[slot], tc)
          + info[:, 3:4] * _load_token_tiles(buf.at[slot], tc, tc * rows_per_token))
    out = x_ref[...] + mod_ref[0, 5:6, :] * ff
    if final:
        ms = jnp.mean(out * out, axis=-1, keepdims=True)
        out = out * lax.rsqrt(ms + RMS_EPS) * fg_ref[...]
    o_ref[...] = out


def _combine(pos, y_rows, x1, info, mod, final_g, B, S, l, final):
    T, D = x1.shape
    tc = _tile(S, COMBINE_ROW_TILE)
    per_seq = S // tc
    return pl.pallas_call(
        functools.partial(_combine_kernel, tc=tc, final=final),
        grid_spec=pltpu.PrefetchScalarGridSpec(
            num_scalar_prefetch=1,
            grid=(T // tc,),
            in_specs=[pl.BlockSpec(memory_space=pl.ANY),
                      pl.BlockSpec((tc, D), lambda i, pos: (i, 0)),
                      pl.BlockSpec((tc, LANES), lambda i, pos: (i, 0)),
                      pl.BlockSpec((1, 6, D), lambda i, pos: (l * B + i // per_seq, 0, 0)),
                      pl.BlockSpec((1, D), lambda i, pos: (0, 0))],
            out_specs=pl.BlockSpec((tc, D), lambda i, pos: (i, 0)),
            scratch_shapes=[pltpu.VMEM((2, 2 * tc * TOKEN_TILE[0], LANES), F32), pltpu.SemaphoreType.DMA((2,))]),
        out_shape=jax.ShapeDtypeStruct((T, D), F32),
        compiler_params=_cparams(("arbitrary",)),
        name="moe_combine",
    )(pos, y_rows, x1, info, mod, final_g)


def _moe(x1, h, info, mod, wg, wu, wd, final_g, B, S, l, li, final):
    T, D = x1.shape
    TK = T * TOP_K
    tm = _tile(TK // N_EXPERTS, EXPERT_ROW_TILE)
    n_rows = TK + (N_EXPERTS - 1) * tm
    flat_e = info[:, :TOP_K].astype(jnp.int32).reshape(TK)
    onehot = (flat_e[:, None] == jnp.arange(N_EXPERTS, dtype=jnp.int32)[None, :]).astype(jnp.int32)
    csum = jnp.cumsum(onehot, axis=0)
    counts = csum[-1]
    rank = jnp.sum(onehot * csum, axis=1) - 1
    padded = (counts + tm - 1) // tm * tm
    pad_ends = jnp.cumsum(padded)
    pad_starts = pad_ends - padded
    dest = pad_starts[flat_e] + rank
    row_tok = jnp.zeros((n_rows,), jnp.int32).at[dest].set(jnp.arange(TK, dtype=jnp.int32) // TOP_K)
    tile_start = jnp.arange(n_rows // tm, dtype=jnp.int32) * tm
    tile_e = jnp.minimum(jnp.searchsorted(pad_ends, tile_start, side='right'), N_EXPERTS - 1).astype(jnp.int32)
    n_used_rows = pad_ends[-1:].astype(jnp.int32)
    n_used_tiles = n_used_rows // tm
    tc = _tile(S, COMBINE_ROW_TILE)
    pos = dest.reshape(T // tc, tc, TOP_K).transpose(0, 2, 1).reshape(TK).astype(jnp.int32)

    y_rows = _experts(h, row_tok, tile_e, n_used_tiles, wg, wu, wd, li, tm)
    return _combine(pos, y_rows, x1, info, mod, final_g, B, S, l, final)


def kernel(x, c, w_ada, b_ada, norm1_g, norm2_g, final_g, w_in, w_out, rel_bias, rwkv_mu, rwkv_w0, rwkv_w2, rwkv_a0, rwkv_a2, rwkv_g2, rwkv_k_k, rwkv_k_a, rwkv_r_k, rwkv_ln_w, rwkv_ln_b, rwkv_v0, rwkv_v1, rwkv_v2, ffn_w_gate, ffn_w_up, ffn_w_down, moe_router_w, moe_router_b, moe_w_gate, moe_w_up, moe_w_down):
    B, S, D = x.shape
    L = w_in.shape[0]
    T = B * S
    W = RWKV_WIDTH

    mod = _ada_mod(c, w_ada, b_ada).reshape(L * B, 6, D)
    w_in_b = jnp.pad(w_in, ((0, 0), (0, 0), (0, N_IN_PAD - N_IN))).astype(BF16)
    w_out_b = w_out.astype(BF16)
    ffn_w = [w.astype(BF16) for w in (ffn_w_gate, ffn_w_up, ffn_w_down)]
    moe_w = [moe_w_gate, moe_w_up, moe_w_down]
    mu_p = jnp.pad(rwkv_mu, ((0, 0), (0, RWM_COLS + RWL_COLS - rwkv_mu.shape[1])))
    att_tables = _att_tables(rel_bias, S)
    zeros_w = jnp.zeros((1, W), F32)
    final_g2 = final_g.reshape(1, D)

    xf = x.reshape(T, D)
    v_first = None
    for l in range(L):
        qkv, qkv_mid, qkv_far, rwm, rwl = _inproj(xf, mod, norm1_g[l].reshape(1, D), w_in_b,
                                                   mu_p[l].reshape(1, -1), B, S, l)
        att = _attention(qkv, qkv_mid, qkv_far, att_tables, B, S)
        v0 = rwkv_v0[l - 1].reshape(1, W) if l > 0 else zeros_w
        vec = jnp.concatenate([rwkv_w0[l].reshape(1, W), rwkv_a0[l].reshape(1, W), rwkv_k_k[l].reshape(1, W),
                               rwkv_k_a[l].reshape(1, W), rwkv_r_k[l].reshape(1, W), rwkv_ln_w[l].reshape(1, W),
                               rwkv_ln_b[l].reshape(1, W), v0], axis=0)
        w2p = jnp.pad(rwkv_w2[l], ((0, LANES - D_DECAY_LORA), (0, 0)))
        a2p = jnp.pad(rwkv_a2[l], ((D_DECAY_LORA, 0), (0, 0)))
        g2p = jnp.pad(rwkv_g2[l], ((0, RWL_COLS - LANES - D_GATE_LORA), (0, 0)))
        if l > 0:
            v1p = jnp.pad(rwkv_v1[l - 1], ((0, 0), (0, LANES - D_MV_LORA)))
            v2p = jnp.pad(rwkv_v2[l - 1], ((0, LANES - D_MV_LORA), (0, 0)))
        else:
            v1p = v2p = None
        rw, v_first = _rwkv(rwm, rwl, v_first, vec, w2p, a2p, g2p, v1p, v2p, B, S)
        li = l // 2
        if l % 2 == 0:
            x1, h = _outproj(xf, att, rw, mod, norm2_g[l].reshape(1, D), w_out_b, B, S, l)
            xf = _ffn_dense(x1, h, mod, *ffn_w, B, S, l, li)
        else:
            router = (jnp.pad(moe_router_w[li], ((0, 0), (0, LANES - N_EXPERTS))),
                      jnp.pad(moe_router_b[li], (0, LANES - N_EXPERTS)).reshape(1, LANES))
            x1, h, info = _outproj(xf, att, rw, mod, norm2_g[l].reshape(1, D), w_out_b, B, S, l, router)
            xf = _moe(x1, h, info, mod, *moe_w, final_g2, B, S, l, li,
                      final=(l == L - 1))
    if L % 2 == 1:
        raise NotImplementedError("final norm is fused into the last (expert) layer")
    return xf.reshape(B, S, D)
```

```python
import functools
import math

import numpy as np
import jax
import jax.numpy as jnp
from jax import lax
from jax.experimental import pallas as pl
from jax.experimental.pallas import tpu as pltpu

F32 = jnp.float32
BF16 = jnp.bfloat16

HEAD_DIM = 64
ATT_WIDTH = 512
RWKV_WIDTH = 512
HEAD_PAIRS = 4
LANES = 128
DILATED_GROUPS = ((128, 1), (512, 4), (2048, 16))
N_BUCKETS = 32
MAX_DISTANCE = 2048
NEG_INF = -1e30
D_DECAY_LORA = 64
D_AAA_LORA = 64
D_MV_LORA = 32
D_GATE_LORA = 160
N_IN = 3 * ATT_WIDTH + 3 * RWKV_WIDTH + D_DECAY_LORA + D_AAA_LORA + D_GATE_LORA
QKV_COLS = 3 * ATT_WIDTH
RWM_COLS = 3 * RWKV_WIDTH
RWL_COLS = 384
N_IN_PAD = QKV_COLS + RWM_COLS + RWL_COLS
GN_EPS = HEAD_DIM * 1e-5
RMS_EPS = 1e-6
N_EXPERTS = 8
TOP_K = 2
CHUNK = 64
VMEM_LIMIT = 56 * 1024 * 1024
LOG2_E = math.log2(math.e)
Q_SCALE = LOG2_E / math.sqrt(HEAD_DIM)
ROW_TILE = 512
FFN_ROW_TILE = 256
FFN_COL_TILE = 2816
EXPERT_ROW_TILE = 512
EXPERT_COL_TILE = 1792
COMBINE_ROW_TILE = 256
DMA_ISSUE_UNROLL = 8


def _cparams(sem):
    return pltpu.CompilerParams(dimension_semantics=sem, vmem_limit_bytes=VMEM_LIMIT)


def _tile(n, pref):
    t = min(n, pref)
    while n % t:
        t //= 2
    return t


def _mm(a, b):
    return jnp.dot(a.astype(BF16), b.astype(BF16), preferred_element_type=F32)


def _mm_nt(a, b):
    return lax.dot_general(a.astype(BF16), b.astype(BF16), (((1,), (1,)), ((), ())),
                           preferred_element_type=F32)


def _sigmoid(x):
    return 1.0 / (1.0 + jnp.exp(-x))


def _silu(x):
    return x * _sigmoid(x)


def _ada_kernel(c_ref, w_ref, b_ref, o_ref):
    ca = _silu(c_ref[...])
    o_ref[0] = _mm(ca, w_ref[0]) + b_ref[0]


def _ada_mod(c, w_ada, b_ada):
    L, D, N = w_ada.shape
    B = c.shape[0]
    tn = _tile(N, 1536)
    return pl.pallas_call(
        _ada_kernel,
        grid=(L, N // tn),
        in_specs=[pl.BlockSpec((B, D), lambda l, j: (0, 0)),
                  pl.BlockSpec((1, D, tn), lambda l, j: (l, 0, j)),
                  pl.BlockSpec((1, 1, tn), lambda l, j: (l, 0, j))],
        out_specs=pl.BlockSpec((1, B, tn), lambda l, j: (l, 0, j)),
        out_shape=jax.ShapeDtypeStruct((L, B, N), F32),
        compiler_params=_cparams(("arbitrary", "arbitrary")),
        name="ada_mod",
    )(c, w_ada, b_ada.reshape(L, 1, N))


def _rms_mod(xf, g, scale, shift):
    ms = jnp.mean(xf * xf, axis=-1, keepdims=True)
    return (xf * lax.rsqrt(ms + RMS_EPS) * g) * (1.0 + scale) + shift


def _inproj_kernel(x_ref, mod_ref, g_ref, w_ref, mu_ref, qkv_ref, qkv_mid_ref, qkv_far_ref, rwm_ref, rwl_ref,
                   carry_ref, qkv_scr):
    i = pl.program_id(1)
    tm = x_ref.shape[0]
    h = _rms_mod(x_ref[...], g_ref[...], mod_ref[0, 1:2, :], mod_ref[0, 0:1, :])
    acc = jnp.dot(h.astype(BF16), w_ref[0], preferred_element_type=F32)
    for c in range(QKV_COLS // LANES):
        cols = slice(c * LANES, (c + 1) * LANES)
        blk = acc[:, cols] * Q_SCALE if (c + 1) * LANES <= ATT_WIDTH else acc[:, cols]
        qkv_ref[:, cols] = blk.astype(BF16)
        qkv_scr[c] = blk
        for out_ref in (qkv_mid_ref, qkv_far_ref):
            d, rows = out_ref.shape[0], out_ref.shape[1]
            for r in range(d):
                out_ref[r, :, cols] = qkv_scr[c, pl.ds(r, rows, stride=d), :].astype(BF16)
    p = acc[:, QKV_COLS:]
    first = jnp.where(i == 0, 0.0, carry_ref[...])
    row = lax.broadcasted_iota(jnp.int32, p.shape, 0)
    prev = jnp.where(row == 0, first, pltpu.roll(p, 1, 0))
    carry_ref[...] = p[tm - 1:tm, :]
    pm = p + mu_ref[...] * (prev - p)
    rwm_ref[...] = pm[:, :RWM_COLS]
    rwl_ref[...] = pm[:, RWM_COLS:]


def _inproj(x, mod, g, w_in_l, mu_l, B, S, l):
    T, D = x.shape
    tm = _tile(S, ROW_TILE)
    nS = S // tm
    d2, d3 = DILATED_GROUPS[1][1], DILATED_GROUPS[2][1]
    return pl.pallas_call(
        _inproj_kernel,
        grid=(B, nS),
        in_specs=[pl.BlockSpec((tm, D), lambda b, i: (b * nS + i, 0)),
                  pl.BlockSpec((1, 6, D), lambda b, i: (l * B + b, 0, 0)),
                  pl.BlockSpec((1, D), lambda b, i: (0, 0)),
                  pl.BlockSpec((1, D, N_IN_PAD), lambda b, i: (l, 0, 0)),
                  pl.BlockSpec((1, RWM_COLS + RWL_COLS), lambda b, i: (0, 0))],
        out_specs=[pl.BlockSpec((tm, QKV_COLS), lambda b, i: (b * nS + i, 0)),
                   pl.BlockSpec((None, d2, tm // d2, QKV_COLS), lambda b, i: (b, 0, i, 0)),
                   pl.BlockSpec((None, d3, tm // d3, QKV_COLS), lambda b, i: (b, 0, i, 0)),
                   pl.BlockSpec((tm, RWM_COLS), lambda b, i: (b * nS + i, 0)),
                   pl.BlockSpec((tm, RWL_COLS), lambda b, i: (b * nS + i, 0))],
        out_shape=[jax.ShapeDtypeStruct((T, QKV_COLS), BF16),
                   jax.ShapeDtypeStruct((B, d2, S // d2, QKV_COLS), BF16),
                   jax.ShapeDtypeStruct((B, d3, S // d3, QKV_COLS), BF16),
                   jax.ShapeDtypeStruct((T, RWM_COLS), F32),
                   jax.ShapeDtypeStruct((T, RWL_COLS), F32)],
        scratch_shapes=[pltpu.VMEM((1, RWM_COLS + RWL_COLS), F32), pltpu.VMEM((QKV_COLS // LANES, tm, LANES), F32)],
        compiler_params=_cparams(("arbitrary", "arbitrary")),
        name="inproj",
    )(x, mod, g, w_in_l, mu_l)


def _t5_bucket(n):
    max_exact = N_BUCKETS // 2
    large = max_exact + (np.log(np.maximum(n, 1) / max_exact) / np.log(MAX_DISTANCE / max_exact)
                         * (N_BUCKETS - max_exact)).astype(np.int32)
    large = np.minimum(large, N_BUCKETS - 1)
    return np.where(n < max_exact, n, large).astype(np.int32)


ATT_BLOCK = 128
ATT_KEYS = 256


def _toeplitz(vec, c0, rows, cols):
    H, n = vec.shape
    period = rows + cols - 1
    w = jnp.concatenate([jnp.full((H, cols - 1), NEG_INF, F32), vec, jnp.full((H, rows), NEG_INF, F32)], axis=1)
    rw = w[:, ::-1]
    a = w.shape[1] - 1 - (c0 + cols - 1)
    z = jnp.concatenate([rw[:, a:a + cols], rw[:, a - (rows - 1):a]], axis=1)
    flat = jnp.tile(z, (1, rows))[:, :rows * (period - 1)]
    return flat.reshape(H, rows, period - 1)[..., :cols]


def _att_tables(rel_bias, S):
    (w1, d1), (w2, d2), (w3, d3) = DILATED_GROUPS
    assert d1 == 1 and w1 == ATT_KEYS - ATT_BLOCK and w2 // d2 == w1 and S % (d3 * 8) == 0
    off = np.arange(S)
    mult = np.zeros(S, np.int64)
    for w, d in DILATED_GROUPS:
        mult += ((off % d == 0) & (off <= w)).astype(np.int64)
    logm = np.where(mult > 0, np.log(np.maximum(mult, 1)), NEG_INF).astype(np.float32)
    per_off = (rel_bias[_t5_bucket(off)].T.astype(F32) + jnp.asarray(logm)[None, :]) * LOG2_E
    u2 = np.arange(S // d2)
    u3 = np.arange(S // d3)
    band = jnp.where(jnp.asarray(off <= w1)[None], per_off, NEG_INF)
    mid = jnp.where(jnp.asarray((u2 > w1 // d2) & (u2 <= w2 // d2))[None], per_off[:, ::d2], NEG_INF)
    far = jnp.where(jnp.asarray((u3 > w2 // d3) & (u3 <= w3 // d3))[None], per_off[:, ::d3], NEG_INF)
    shift = ATT_KEYS - ATT_BLOCK
    t_band = jnp.stack([_toeplitz(band, 0, ATT_BLOCK, ATT_KEYS), _toeplitz(band, shift, ATT_BLOCK, ATT_KEYS)], axis=1)
    t_mid = jnp.stack([_toeplitz(mid, 0, ATT_BLOCK, ATT_KEYS), _toeplitz(mid, shift, ATT_BLOCK, ATT_KEYS)], axis=1)
    t_far = _toeplitz(far, 0, S // d3, S // d3)
    return t_band, t_mid, t_far


def _attend(units):
    idx = range(len(units))
    heads = range(2)
    head0 = [lax.broadcasted_iota(jnp.int32, u[0].shape, 1) < HEAD_DIM for u in units]
    qh = [[jnp.where(head0[i], units[i][0], jnp.zeros_like(units[i][0])),
           jnp.where(head0[i], jnp.zeros_like(units[i][0]), units[i][0])] for i in idx]
    s = [[lax.dot_general(qh[i][h], units[i][1], (((1,), (1,)), ((), ())), preferred_element_type=F32)
          + units[i][3 + h] for h in heads] for i in idx]
    mx = [[jnp.max(s[i][h], axis=-1, keepdims=True) for h in heads] for i in idx]
    p = [[jnp.exp2(s[i][h] - mx[i][h]) for h in heads] for i in idx]
    ps = [[jnp.sum(p[i][h], axis=-1, keepdims=True) for h in heads] for i in idx]
    pv = [[jnp.dot(p[i][h].astype(BF16), units[i][2], preferred_element_type=F32) for h in heads] for i in idx]
    out = []
    for i in idx:
        shape = units[i][0].shape
        out.append((jnp.where(head0[i], jnp.broadcast_to(mx[i][0], shape), jnp.broadcast_to(mx[i][1], shape)),
                    jnp.where(head0[i], jnp.broadcast_to(ps[i][0], shape), jnp.broadcast_to(ps[i][1], shape)),
                    jnp.where(head0[i], pv[i][0], pv[i][1])))
    return out


def _attn_kernel(qa_ref, ka_ref, va_ref, qm_ref, km_ref, vm_ref, qf_ref, kf_ref, vf_ref,
                 tband_ref, tmid_ref, tfar_ref, o_ref, part_ref):
    R, K = ATT_BLOCK, ATT_KEYS
    S = qa_ref.shape[0]
    n_mid_res, n_mid = qm_ref.shape[0], qm_ref.shape[1]
    n_far_res, n_far = qf_ref.shape[0], qf_ref.shape[1]
    group = 8

    def window(blk):
        start = pl.multiple_of(jnp.maximum(blk * R - (K - R), 0), R)
        return pl.ds(pl.multiple_of(blk * R, R), R), pl.ds(start, K), jnp.minimum(blk, 1)

    def park(cls, rows, results):
        for nat, parts in zip(rows, results):
            for j in range(3):
                part_ref[cls, j, nat, :] = parts[j]

    def band_body(it, _):
        units, rows = [], []
        for g in range(group):
            qrows, krows, var = window(it * group + g)
            rows.append(qrows)
            units.append((qa_ref[qrows, :], ka_ref[krows, :], va_ref[krows, :],
                          tband_ref[0, var], tband_ref[1, var]))
        park(0, rows, _attend(units))
        return 0

    lax.fori_loop(0, S // (R * group), band_body, 0)

    blocks_per_step = group // n_mid_res

    def mid_body(it, _):
        units, rows = [], []
        for g in range(blocks_per_step):
            blk = it * blocks_per_step + g
            qrows, krows, var = window(blk)
            for r in range(n_mid_res):
                rows.append(pl.ds(blk * (R * n_mid_res) + r, R, stride=n_mid_res))
                units.append((qm_ref[r, qrows, :], km_ref[r, krows, :], vm_ref[r, krows, :],
                              tmid_ref[0, var], tmid_ref[1, var]))
        park(1, rows, _attend(units))
        return 0

    lax.fori_loop(0, n_mid // (R * blocks_per_step), mid_body, 0)

    def far_body(it, _):
        units, rows = [], []
        for g in range(group):
            r = it * group + g
            rows.append(pl.ds(r, n_far, stride=n_far_res))
            units.append((qf_ref[r], kf_ref[r], vf_ref[r], tfar_ref[0], tfar_ref[1]))
        park(2, rows, _attend(units))
        return 0

    lax.fori_loop(0, n_far_res // group, far_body, 0)

    def merge_body(i, _):
        rows = pl.ds(pl.multiple_of(i * R, R), R)
        m = [part_ref[c, 0, rows, :] for c in range(3)]
        top = jnp.maximum(jnp.maximum(m[0], m[1]), m[2])
        w = [jnp.exp2(m[c] - top) for c in range(3)]
        total = sum(w[c] * part_ref[c, 1, rows, :] for c in range(3))
        acc = sum(w[c] * part_ref[c, 2, rows, :] for c in range(3))
        o_ref[rows, :] = (acc / total).astype(o_ref.dtype)
        return 0

    lax.fori_loop(0, S // R, merge_body, 0)


def _attention(qkv, qkv_mid, qkv_far, tables, B, S):
    T = qkv.shape[0]
    t_band, t_mid, t_far = tables
    d2, d3 = DILATED_GROUPS[1][1], DILATED_GROUPS[2][1]
    assert S % (ATT_BLOCK * 8) == 0 and S // d2 >= ATT_KEYS
    nat = lambda c: pl.BlockSpec((S, LANES), lambda hp, b: (b, c * HEAD_PAIRS + hp))
    mid = lambda c: pl.BlockSpec((None, d2, S // d2, LANES), lambda hp, b: (b, 0, 0, c * HEAD_PAIRS + hp))
    far = lambda c: pl.BlockSpec((None, d3, S // d3, LANES), lambda hp, b: (b, 0, 0, c * HEAD_PAIRS + hp))
    return pl.pallas_call(
        _attn_kernel,
        grid=(HEAD_PAIRS, B),
        in_specs=[nat(0), nat(1), nat(2), mid(0), mid(1), mid(2), far(0), far(1), far(2),
                  pl.BlockSpec((2,) + t_band.shape[1:], lambda hp, b: (hp, 0, 0, 0)),
                  pl.BlockSpec((2,) + t_mid.shape[1:], lambda hp, b: (hp, 0, 0, 0)),
                  pl.BlockSpec((2,) + t_far.shape[1:], lambda hp, b: (hp, 0, 0))],
        out_specs=pl.BlockSpec((S, LANES), lambda hp, b: (b, hp)),
        out_shape=jax.ShapeDtypeStruct((T, ATT_WIDTH), BF16),
        scratch_shapes=[pltpu.VMEM((3, 3, S, LANES), F32)],
        compiler_params=_cparams(("arbitrary", "arbitrary")),
        name="dilated_attn",
    )(qkv, qkv, qkv, qkv_mid, qkv_mid, qkv_mid, qkv_far, qkv_far, qkv_far, t_band, t_mid, t_far)


def _split2(x):
    hi = x.astype(BF16)
    lo = (x - hi.astype(F32)).astype(BF16)
    return hi, lo


def _head_sum(x, m2):
    hi, lo = _split2(x)
    return jnp.dot(jnp.concatenate([hi, lo], axis=1), m2, preferred_element_type=F32)


def _rwkv_kernel(*refs, has_vres, n_chunks, n_batch):
    if has_vres:
        (rwm_ref, rwl_ref, vf_ref, vec_ref, w2_ref, a2_ref, g2_ref, v1_ref, v2_ref,
         o_ref, state_ref, r_s, k_s, v_s, lw_s, kk_s, kb_s, y_s,
         ar_s, inv_s, bk_s, nrbk_s, akv_s, dec_s) = refs
    else:
        (rwm_ref, rwl_ref, vec_ref, w2_ref, a2_ref, g2_ref,
         o_ref, vf_out_ref, state_ref, r_s, k_s, v_s, lw_s, kk_s, kb_s, y_s,
         ar_s, inv_s, bk_s, nrbk_s, akv_s, dec_s) = refs
    C = CHUNK
    W = RWKV_WIDTH
    rows_per_batch = rwm_ref.shape[1]

    def stacked(ref, cols=slice(None)):
        return jnp.concatenate([ref[b, :, cols] for b in range(n_batch)], axis=0)

    def unstack(ref, x, cols=slice(None)):
        for b in range(n_batch):
            ref[b, :, cols] = x[b * rows_per_batch:(b + 1) * rows_per_batch].astype(ref.dtype)

    @pl.when(pl.program_id(1) == 0)
    def _():
        state_ref[...] = jnp.zeros_like(state_ref)

    ri = lax.broadcasted_iota(jnp.int32, (2 * LANES, LANES), 0)
    ci = lax.broadcasted_iota(jnp.int32, (2 * LANES, LANES), 1)
    m2 = ((ri % LANES) // HEAD_DIM == ci // HEAD_DIM).astype(BF16)

    vec = vec_ref[...]
    w0, a0, k_k, k_a, r_k, ln_w, ln_b, v0 = [vec[i:i + 1, :] for i in range(8)]

    r = stacked(rwm_ref, slice(0, W))
    k = stacked(rwm_ref, slice(W, 2 * W))
    v = stacked(rwm_ref, slice(2 * W, 3 * W))
    lora = stacked(rwl_ref, slice(0, LANES))
    w_raw = w0 + _mm(jnp.tanh(lora), w2_ref[...])
    lw_s[...] = -math.exp(-0.5) * _sigmoid(w_raw)
    a = _sigmoid(a0 + _mm(lora, a2_ref[...]))
    g = _mm(_sigmoid(stacked(rwl_ref, slice(LANES, RWL_COLS))), g2_ref[...])
    if has_vres:
        mix = _sigmoid(v0 + _mm(_mm(v, v1_ref[...]), v2_ref[...]))
        v = v + (stacked(vf_ref) - v) * mix
    else:
        unstack(vf_out_ref, v)
    kk = k * k_k
    for p in range(HEAD_PAIRS):
        ls = slice(p * LANES, (p + 1) * LANES)
        kkp = kk[:, ls]
        kkp = kkp * jnp.minimum(lax.rsqrt(_head_sum(kkp * kkp, m2)), 1e12)
        kk_s[:, ls] = kkp
        kb_s[:, ls] = kkp * a[:, ls]
    kmod = k * (1.0 + (a - 1.0) * k_a)
    r_s[...] = r
    k_s[...] = kmod
    v_s[...] = v

    ri = lax.broadcasted_iota(jnp.int32, (LANES, LANES), 0)
    ci = lax.broadcasted_iota(jnp.int32, (LANES, LANES), 1)
    same_head = (ri // C) == (ci // C)
    strict = same_head & ((ri % C) > (ci % C))
    incl = same_head & ((ri % C) >= (ci % C))
    eye = (ri == ci).astype(F32)
    lane = lax.broadcasted_iota(jnp.int32, (C, LANES), 1)
    head0 = lane < HEAD_DIM
    tri_r = lax.broadcasted_iota(jnp.int32, (C, 3 * C), 0)
    tri_c = lax.broadcasted_iota(jnp.int32, (C, 3 * C), 1)
    tri3 = ((tri_c % C) <= tri_r).astype(BF16)

    lane_slices = [slice(p * LANES, (p + 1) * LANES) for p in range(HEAD_PAIRS)]

    def cumsum(x):
        hi = x.astype(BF16)
        rem = x - hi.astype(F32)
        mid = rem.astype(BF16)
        lo = (rem - mid.astype(F32)).astype(BF16)
        return jnp.dot(tri3, jnp.concatenate([hi, mid, lo], axis=0), preferred_element_type=F32)

    def per_head_rows(x, y):
        zero = jnp.zeros_like(x)
        return jnp.concatenate([jnp.where(head0, x, zero), jnp.where(head0, zero, x),
                                jnp.where(head0, y, zero), jnp.where(head0, zero, y)], axis=0)

    def intra_body(it, _):
        units = [(it * chunk_group + g, p) for g in range(chunk_group) for p in range(HEAD_PAIRS)]
        idx = range(len(units))
        rows = [pl.ds(pl.multiple_of(c * C, C), C) for c, _ in units]
        ls = [lane_slices[p] for _, p in units]
        rc = [r_s[rows[u], ls[u]] for u in idx]
        kc = [k_s[rows[u], ls[u]] for u in idx]
        vc = [v_s[rows[u], ls[u]] for u in idx]
        lw = [lw_s[rows[u], ls[u]] for u in idx]
        kkc = [kk_s[rows[u], ls[u]] for u in idx]
        kbc = [kb_s[rows[u], ls[u]] for u in idx]
        cum = [cumsum(lw[u]) for u in idx]
        total = [cum[u][C - 1:C, :] for u in idx]
        g_inv = [jnp.exp(-cum[u]) for u in idx]
        a_t = [-kkc[u] * jnp.exp(cum[u] - lw[u]) for u in idx]
        r_t = [rc[u] * jnp.exp(cum[u]) for u in idx]
        b_t = [kbc[u] * g_inv[u] for u in idx]
        k_t = [kc[u] * g_inv[u] for u in idx]
        to_end = [jnp.exp(total[u] - cum[u]) for u in idx]
        gram = [_mm_nt(per_head_rows(a_t[u], r_t[u]), per_head_rows(b_t[u], k_t[u])) for u in idx]
        n_ab = [jnp.where(strict, gram[u][:LANES, :LANES], 0.0) for u in idx]
        n_ak = [jnp.where(strict, gram[u][:LANES, LANES:], 0.0) for u in idx]
        n_rb = [jnp.where(incl, gram[u][LANES:, :LANES], 0.0) for u in idx]
        n_rk = [jnp.where(incl, gram[u][LANES:, LANES:], 0.0) for u in idx]
        inv = [eye + n_ab[u] for u in idx]
        pw = [_mm(n_ab[u], n_ab[u]) for u in idx]
        n_rounds = int(math.log2(C)) - 1
        for i in range(n_rounds):
            if i + 1 < n_rounds:
                both = [_mm(pw[u], jnp.concatenate([inv[u], pw[u]], axis=1)) for u in idx]
                inv = [inv[u] + both[u][:, :LANES] for u in idx]
                pw = [both[u][:, LANES:] for u in idx]
            else:
                inv = [inv[u] + _mm(pw[u], inv[u]) for u in idx]
        for u, (c, p) in enumerate(units):
            ar_s[c, p] = jnp.concatenate([a_t[u], r_t[u]], axis=0).astype(BF16)
            inv_s[c, p] = inv[u].astype(BF16)
            akv_s[c, p] = _mm(n_ak[u], jnp.concatenate([vc[u], vc[u]], axis=0))
            nrbk_s[c, p] = jnp.concatenate([n_rb[u], n_rk[u]], axis=1).astype(BF16)
            bk_s[c, p] = jnp.concatenate([kbc[u] * to_end[u], kc[u] * to_end[u]], axis=0).astype(BF16)
            dec_s[c, p] = jnp.broadcast_to(jnp.exp(total[u]), (8, LANES))
        return 0

    chunks_per_batch = n_chunks // n_batch

    def state_body(step, _):
        units = [(b, b * chunks_per_batch + step, p) for b in range(n_batch) for p in range(HEAD_PAIRS)]
        idx = range(len(units))
        rows = [pl.ds(pl.multiple_of(c * C, C), C) for _, c, _ in units]
        ls = [lane_slices[p] for _, _, p in units]
        vc = [v_s[rows[u], ls[u]] for u in idx]
        st = [state_ref[b, p] for b, _, p in units]
        ah = [_mm_nt(ar_s[c, p], st[u]) for u, (_, c, p) in enumerate(units)]
        u_stack = [_mm(inv_s[c, p], jnp.concatenate([ah[u][:C], ah[u][:C]], axis=0) + akv_s[c, p])
                   for u, (_, c, p) in enumerate(units)]
        y_stack = [jnp.concatenate([ah[u][C:], ah[u][C:]], axis=0)
                   + _mm(nrbk_s[c, p], jnp.concatenate([u_stack[u], vc[u], vc[u]], axis=0))
                   for u, (_, c, p) in enumerate(units)]
        for u in idx:
            y_s[rows[u], ls[u]] = jnp.where(head0, y_stack[u][:C], y_stack[u][C:])
        uv_t = [jnp.concatenate([jnp.where(head0, u_stack[u][:C], u_stack[u][C:]), vc[u]], axis=0).T
                for u in idx]
        st_new = [st[u] * dec_s[c, p][0:1, :] + _mm(uv_t[u], bk_s[c, p]) for u, (_, c, p) in enumerate(units)]
        for u, (b, _, p) in enumerate(units):
            state_ref[b, p] = jnp.where(same_head, st_new[u], 0.0)
        return 0

    chunk_group = 4 if n_chunks % 4 == 0 else (2 if n_chunks % 2 == 0 else 1)
    lax.fori_loop(0, n_chunks // chunk_group, intra_body, 0)
    lax.fori_loop(0, chunks_per_batch, state_body, 0)

    for p in range(HEAD_PAIRS):
        ls = slice(p * LANES, (p + 1) * LANES)
        y = y_s[:, ls]
        mean = _head_sum(y, m2) * (1.0 / HEAD_DIM)
        d = y - mean
        var = _head_sum(d * d, m2) * (1.0 / HEAD_DIM)
        yn = d * lax.rsqrt(var + GN_EPS) * ln_w[:, ls] + ln_b[:, ls]
        bonus = _head_sum(r_s[:, ls] * k_s[:, ls] * r_k[:, ls], m2) * v_s[:, ls]
        unstack(o_ref, (yn + bonus) * g[:, ls], ls)


def _rwkv(rwm, rwl, v_first, vec, w2p, a2p, g2p, v1p, v2p, B, S):
    T = rwm.shape[0]
    W = RWKV_WIDTH
    nb = 4 if B % 4 == 0 else (2 if B % 2 == 0 else 1)
    tb = _tile(S, ROW_TILE // nb)
    n_chunks = nb * tb // CHUNK
    has_vres = v_first is not None
    stack = lambda a: a.reshape(B // nb, nb, S, a.shape[-1])
    row = lambda cols: pl.BlockSpec((None, nb, tb, cols), lambda b, i: (b, 0, i, 0))
    full = lambda b, i: (0, 0)
    in_specs = [row(RWM_COLS), row(RWL_COLS)]
    args = [stack(rwm), stack(rwl)]
    if has_vres:
        in_specs.append(row(W))
        args.append(stack(v_first))
    in_specs += [pl.BlockSpec(vec.shape, full), pl.BlockSpec(w2p.shape, full),
                 pl.BlockSpec(a2p.shape, full), pl.BlockSpec(g2p.shape, full)]
    args += [vec, w2p, a2p, g2p]
    if has_vres:
        in_specs += [pl.BlockSpec(v1p.shape, full), pl.BlockSpec(v2p.shape, full)]
        args += [v1p, v2p]
    out_specs = [row(W)]
    out_shape = [jax.ShapeDtypeStruct((B // nb, nb, S, W), BF16)]
    if not has_vres:
        out_specs.append(row(W))
        out_shape.append(jax.ShapeDtypeStruct((B // nb, nb, S, W), F32))
    outs = pl.pallas_call(
        functools.partial(_rwkv_kernel, has_vres=has_vres, n_chunks=n_chunks, n_batch=nb),
        grid=(B // nb, S // tb),
        in_specs=in_specs,
        out_specs=out_specs,
        out_shape=out_shape,
        scratch_shapes=([pltpu.VMEM((nb, HEAD_PAIRS, LANES, LANES), F32)] + [pltpu.VMEM((nb * tb, W), F32)] * 7
                        + [pltpu.VMEM((n_chunks, HEAD_PAIRS, 2 * CHUNK, LANES), BF16)] * 3
                        + [pltpu.VMEM((n_chunks, HEAD_PAIRS, 2 * CHUNK, 2 * LANES), BF16),
                           pltpu.VMEM((n_chunks, HEAD_PAIRS, 2 * CHUNK, LANES), F32),
                           pltpu.VMEM((n_chunks, HEAD_PAIRS, 8, LANES), F32)]),
        compiler_params=_cparams(("arbitrary", "arbitrary")),
        name="rwkv7",
    )(*args)
    outs = [o.reshape(T, W) for o in outs]
    return (outs[0], v_first) if has_vres else (outs[0], outs[1])


TOKEN_TILE = (8, LANES)


def _store_token_tiles(ref, x):
    n = x.shape[0]
    for s in range(TOKEN_TILE[0]):
        ref[pl.ds(s, n, stride=TOKEN_TILE[0]), :] = x[:, s * LANES:(s + 1) * LANES]


def _load_token_tiles(ref, n, start=0):
    return jnp.concatenate([ref[pl.ds(start + s, n, stride=TOKEN_TILE[0]), :] for s in range(TOKEN_TILE[0])],
                           axis=1)


def _outproj_kernel(*refs, moe):
    if moe:
        (x_ref, att_ref, rw_ref, mod_ref, g_ref, w_ref, rw_w_ref, rw_b_ref,
         x1_ref, h_ref, info_ref) = refs
    else:
        x_ref, att_ref, rw_ref, mod_ref, g_ref, w_ref, x1_ref, h_ref = refs
    mix = (jnp.dot(att_ref[...], w_ref[0, :ATT_WIDTH, :], preferred_element_type=F32)
           + jnp.dot(rw_ref[...], w_ref[0, ATT_WIDTH:, :], preferred_element_type=F32))
    x1 = x_ref[...] + mod_ref[0, 2:3, :] * mix
    x1_ref[...] = x1
    h = _rms_mod(x1, g_ref[...], mod_ref[0, 4:5, :], mod_ref[0, 3:4, :])
    if moe:
        _store_token_tiles(h_ref, h)
    else:
        h_ref[...] = h.astype(h_ref.dtype)
    if moe:
        h_hi, h_lo = _split2(h)
        w_hi, w_lo = _split2(rw_w_ref[...])
        logits = (jnp.dot(h_hi, w_hi, preferred_element_type=F32)
                  + jnp.dot(h_hi, w_lo, preferred_element_type=F32)
                  + jnp.dot(h_lo, w_hi, preferred_element_type=F32)) + rw_b_ref[...]
        lane = lax.broadcasted_iota(jnp.int32, logits.shape, 1)
        logits = jnp.where(lane < N_EXPERTS, logits, -jnp.inf)
        m1 = jnp.max(logits, axis=-1, keepdims=True)
        i1 = jnp.min(jnp.where(logits == m1, lane, LANES), axis=-1, keepdims=True)
        rest = jnp.where(lane == i1, -jnp.inf, logits)
        m2 = jnp.max(rest, axis=-1, keepdims=True)
        i2 = jnp.min(jnp.where(rest == m2, lane, LANES), axis=-1, keepdims=True)
        e = jnp.exp(m2 - m1)
        g1 = 1.0 / (1.0 + e)
        g2 = e / (1.0 + e)
        info = jnp.where(lane == 0, i1.astype(F32),
                         jnp.where(lane == 1, i2.astype(F32),
                                   jnp.where(lane == 2, g1, jnp.where(lane == 3, g2, 0.0))))
        info_ref[...] = info


def _outproj(x, att, rw, mod, g, w_out_b, B, S, l, router=None):
    T, D = x.shape
    tm = _tile(S, ROW_TILE)
    nS = S // tm
    moe = router is not None
    row = lambda b, i: (b * nS + i, 0)
    full = lambda b, i: (0, 0)
    in_specs = [pl.BlockSpec((tm, D), row), pl.BlockSpec((tm, ATT_WIDTH), row),
                pl.BlockSpec((tm, RWKV_WIDTH), row),
                pl.BlockSpec((1, 6, D), lambda b, i: (l * B + b, 0, 0)),
                pl.BlockSpec((1, D), full),
                pl.BlockSpec((1, D, D), lambda b, i: (l, 0, 0))]
    args = [x, att, rw, mod, g, w_out_b]
    if moe:
        out_specs = [pl.BlockSpec((tm, D), row), pl.BlockSpec((tm * TOKEN_TILE[0], LANES), row)]
        out_shape = [jax.ShapeDtypeStruct((T, D), F32), jax.ShapeDtypeStruct((T * TOKEN_TILE[0], LANES), F32)]
    else:
        out_specs = [pl.BlockSpec((tm, D), row), pl.BlockSpec((tm, D), row)]
        out_shape = [jax.ShapeDtypeStruct((T, D), F32), jax.ShapeDtypeStruct((T, D), BF16)]
    if moe:
        in_specs += [pl.BlockSpec((D, LANES), full), pl.BlockSpec((1, LANES), full)]
        args += list(router)
        out_specs.append(pl.BlockSpec((tm, LANES), row))
        out_shape.append(jax.ShapeDtypeStruct((T, LANES), F32))
    return pl.pallas_call(
        functools.partial(_outproj_kernel, moe=moe),
        grid=(B, nS),
        in_specs=in_specs,
        out_specs=out_specs,
        out_shape=out_shape,
        compiler_params=_cparams(("arbitrary", "arbitrary")),
        name="outproj",
    )(*args)


def _ffn_kernel(x_ref, h_ref, mod_ref, wg_ref, wu_ref, wd_ref, o_ref, acc_ref):
    f = pl.program_id(1)

    @pl.when(f == 0)
    def _():
        acc_ref[...] = jnp.zeros_like(acc_ref)

    hb = h_ref[...]
    gate = jnp.dot(hb, wg_ref[0], preferred_element_type=F32)
    up = jnp.dot(hb, wu_ref[0], preferred_element_type=F32)
    mid = (_silu(gate) * up).astype(BF16)
    acc_ref[...] += jnp.dot(mid, wd_ref[0], preferred_element_type=F32)

    @pl.when(f == pl.num_programs(1) - 1)
    def _():
        o_ref[...] = x_ref[...] + mod_ref[0, 5:6, :] * acc_ref[...]


def _ffn_dense(x1, h, mod, wg, wu, wd, B, S, l, li):
    T, D = x1.shape
    F = wg.shape[-1]
    tm = _tile(S, FFN_ROW_TILE)
    tf = _tile(F, FFN_COL_TILE)
    per_seq = S // tm
    return pl.pallas_call(
        _ffn_kernel,
        grid=(T // tm, F // tf),
        in_specs=[pl.BlockSpec((tm, D), lambda i, f: (i, 0)),
                  pl.BlockSpec((tm, D), lambda i, f: (i, 0)),
                  pl.BlockSpec((1, 6, D), lambda i, f: (l * B + i // per_seq, 0, 0)),
                  pl.BlockSpec((1, D, tf), lambda i, f: (li, 0, f)),
                  pl.BlockSpec((1, D, tf), lambda i, f: (li, 0, f)),
                  pl.BlockSpec((1, tf, D), lambda i, f: (li, f, 0))],
        out_specs=pl.BlockSpec((tm, D), lambda i, f: (i, 0)),
        out_shape=jax.ShapeDtypeStruct((T, D), F32),
        scratch_shapes=[pltpu.VMEM((tm, D), F32)],
        compiler_params=_cparams(("arbitrary", "arbitrary")),
        name="ffn_dense",
    )(x1, h, mod, wg, wu, wd)


def _experts_kernel(te_ref, nu_ref, tok_ref, h_ref, wg_ref, wu_ref, wd_ref, y_ref,
                    land_ref, xb_ref, acc_ref, sem):
    i = pl.program_id(0)
    f = pl.program_id(1)
    tm = xb_ref.shape[0]
    rows_per_token = TOKEN_TILE[0]
    n_used = nu_ref[0]

    def start_gather(tile):
        base = tile * tm

        def body(j, _):
            src = pl.multiple_of(tok_ref[base + j] * rows_per_token, rows_per_token)
            dst = pl.multiple_of(j * rows_per_token, rows_per_token)
            pltpu.make_async_copy(h_ref.at[pl.ds(src, rows_per_token)],
                                  land_ref.at[pl.ds(dst, rows_per_token)], sem).start()
            return 0

        lax.fori_loop(0, tm, body, 0, unroll=DMA_ISSUE_UNROLL)

    @pl.when(i < n_used)
    def _():
        @pl.when(f == 0)
        def _():
            @pl.when(i == 0)
            def _():
                start_gather(0)

            pltpu.make_async_copy(h_ref.at[pl.ds(0, tm * rows_per_token)], land_ref, sem).wait()
            xb_ref[...] = _load_token_tiles(land_ref, tm).astype(BF16)
            acc_ref[...] = jnp.zeros_like(acc_ref)

            @pl.when(i + 1 < n_used)
            def _():
                start_gather(i + 1)

        xb = xb_ref[...]
        gate = jnp.dot(xb, wg_ref[0, 0].astype(BF16), preferred_element_type=F32)
        up = jnp.dot(xb, wu_ref[0, 0].astype(BF16), preferred_element_type=F32)
        mid = (_silu(gate) * up).astype(BF16)
        acc_ref[...] += jnp.dot(mid, wd_ref[0, 0].astype(BF16), preferred_element_type=F32)

        @pl.when(f == pl.num_programs(1) - 1)
        def _():
            _store_token_tiles(y_ref, acc_ref[...])

    @pl.when((i >= n_used) & (f == 0))
    def _():
        y_ref[...] = jnp.zeros_like(y_ref)


def _experts(h, row_tok, tile_e, n_used, wg, wu, wd, li, tm):
    n_rows = row_tok.shape[0]
    D = wg.shape[-2]
    rpt = TOKEN_TILE[0]
    assert h.shape[0] >= tm * rpt
    F = wg.shape[-1]
    tf = _tile(F, EXPERT_COL_TILE)
    nf = F // tf

    def rows(i, f, te, nu, tok):
        return (i, 0)

    def col_block(i, f, nu):
        t = jnp.minimum(i, nu[0] - 1)
        step = jnp.where(i < nu[0], f, nf - 1)
        return jnp.where(t % 2 == 0, step, nf - 1 - step)

    def wcol(i, f, te, nu, tok):
        return (li, te[jnp.minimum(i, nu[0] - 1)], 0, col_block(i, f, nu))

    def wrow(i, f, te, nu, tok):
        return (li, te[jnp.minimum(i, nu[0] - 1)], col_block(i, f, nu), 0)

    return pl.pallas_call(
        _experts_kernel,
        grid_spec=pltpu.PrefetchScalarGridSpec(
            num_scalar_prefetch=3,
            grid=(n_rows // tm, nf),
            in_specs=[pl.BlockSpec(memory_space=pl.ANY),
                      pl.BlockSpec((1, 1, D, tf), wcol),
                      pl.BlockSpec((1, 1, D, tf), wcol),
                      pl.BlockSpec((1, 1, tf, D), wrow)],
            out_specs=pl.BlockSpec((tm * rpt, LANES), rows),
            scratch_shapes=[pltpu.VMEM((tm * rpt, LANES), F32), pltpu.VMEM((tm, D), BF16),
                            pltpu.VMEM((tm, D), F32), pltpu.SemaphoreType.DMA(())]),
        out_shape=jax.ShapeDtypeStruct((n_rows * rpt, LANES), F32),
        compiler_params=_cparams(("arbitrary", "arbitrary")),
        name="moe_experts",
    )(tile_e, n_used, row_tok, h, wg, wu, wd)


def _combine_kernel(pos_ref, y_ref, x_ref, info_ref, mod_ref, fg_ref, o_ref, buf, sem, *, tc, final):
    i = pl.program_id(0)
    slot = i % 2
    rows_per_token = TOKEN_TILE[0]

    def start_gather(tile, dst_slot):
        base = tile * (2 * tc)

        def body(j, _):
            src = pl.multiple_of(pos_ref[base + j] * rows_per_token, rows_per_token)
            dst = pl.multiple_of(j * rows_per_token, rows_per_token)
            pltpu.make_async_copy(y_ref.at[pl.ds(src, rows_per_token)],
                                  buf.at[dst_slot, pl.ds(dst, rows_per_token)], sem.at[dst_slot]).start()
            return 0

        lax.fori_loop(0, 2 * tc, body, 0, unroll=DMA_ISSUE_UNROLL)

    @pl.when(i == 0)
    def _():
        start_gather(0, 0)

    @pl.when(i + 1 < pl.num_programs(0))
    def _():
        start_gather(i + 1, 1 - slot)

    pltpu.make_async_copy(y_ref.at[pl.ds(0, 2 * tc * rows_per_token)], buf.at[slot], sem.at[slot]).wait()
    info = info_ref[...]
    ff = (info[:, 2:3] * _load_token_tiles(buf.at[slot], tc)
          + info[:, 3:4] * _load_token_tiles(buf.at[slot], tc, tc * rows_per_token))
    out = x_ref[...] + mod_ref[0, 5:6, :] * ff
    if final:
        ms = jnp.mean(out * out, axis=-1, keepdims=True)
        out = out * lax.rsqrt(ms + RMS_EPS) * fg_ref[...]
    o_ref[...] = out


def _combine(pos, y_rows, x1, info, mod, final_g, B, S, l, final):
    T, D = x1.shape
    tc = _tile(S, COMBINE_ROW_TILE)
    per_seq = S // tc
    return pl.pallas_call(
        functools.partial(_combine_kernel, tc=tc, final=final),
        grid_spec=pltpu.PrefetchScalarGridSpec(
            num_scalar_prefetch=1,
            grid=(T // tc,),
            in_specs=[pl.BlockSpec(memory_space=pl.ANY),
                      pl.BlockSpec((tc, D), lambda i, pos: (i, 0)),
                      pl.BlockSpec((tc, LANES), lambda i, pos: (i, 0)),
                      pl.BlockSpec((1, 6, D), lambda i, pos: (l * B + i // per_seq, 0, 0)),
                      pl.BlockSpec((1, D), lambda i, pos: (0, 0))],
            out_specs=pl.BlockSpec((tc, D), lambda i, pos: (i, 0)),
            scratch_shapes=[pltpu.VMEM((2, 2 * tc * TOKEN_TILE[0], LANES), F32), pltpu.SemaphoreType.DMA((2,))]),
        out_shape=jax.ShapeDtypeStruct((T, D), F32),
        compiler_params=_cparams(("arbitrary",)),
        name="moe_combine",
    )(pos, y_rows, x1, info, mod, final_g)


def _moe(x1, h, info, mod, wg, wu, wd, final_g, B, S, l, li, final):
    T, D = x1.shape
    TK = T * TOP_K
    tm = _tile(TK // N_EXPERTS, EXPERT_ROW_TILE)
    n_rows = TK + (N_EXPERTS - 1) * tm
    flat_e = info[:, :TOP_K].astype(jnp.int32).reshape(TK)
    onehot = (flat_e[:, None] == jnp.arange(N_EXPERTS, dtype=jnp.int32)[None, :]).astype(jnp.int32)
    csum = jnp.cumsum(onehot, axis=0)
    counts = csum[-1]
    rank = jnp.sum(onehot * csum, axis=1) - 1
    padded = (counts + tm - 1) // tm * tm
    pad_ends = jnp.cumsum(padded)
    pad_starts = pad_ends - padded
    dest = pad_starts[flat_e] + rank
    row_tok = jnp.zeros((n_rows,), jnp.int32).at[dest].set(jnp.arange(TK, dtype=jnp.int32) // TOP_K)
    tile_start = jnp.arange(n_rows // tm, dtype=jnp.int32) * tm
    tile_e = jnp.minimum(jnp.searchsorted(pad_ends, tile_start, side='right'), N_EXPERTS - 1).astype(jnp.int32)
    n_used_rows = pad_ends[-1:].astype(jnp.int32)
    n_used_tiles = n_used_rows // tm
    tc = _tile(S, COMBINE_ROW_TILE)
    pos = dest.reshape(T // tc, tc, TOP_K).transpose(0, 2, 1).reshape(TK).astype(jnp.int32)

    y_rows = _experts(h, row_tok, tile_e, n_used_tiles, wg, wu, wd, li, tm)
    return _combine(pos, y_rows, x1, info, mod, final_g, B, S, l, final)


def kernel(x, c, w_ada, b_ada, norm1_g, norm2_g, final_g, w_in, w_out, rel_bias, rwkv_mu, rwkv_w0, rwkv_w2, rwkv_a0, rwkv_a2, rwkv_g2, rwkv_k_k, rwkv_k_a, rwkv_r_k, rwkv_ln_w, rwkv_ln_b, rwkv_v0, rwkv_v1, rwkv_v2, ffn_w_gate, ffn_w_up, ffn_w_down, moe_router_w, moe_router_b, moe_w_gate, moe_w_up, moe_w_down):
    B, S, D = x.shape
    L = w_in.shape[0]
    T = B * S
    W = RWKV_WIDTH

    mod = _ada_mod(c, w_ada, b_ada).reshape(L * B, 6, D)
    w_in_b = jnp.pad(w_in, ((0, 0), (0, 0), (0, N_IN_PAD - N_IN))).astype(BF16)
    w_out_b = w_out.astype(BF16)
    ffn_w = [w.astype(BF16) for w in (ffn_w_gate, ffn_w_up, ffn_w_down)]
    moe_w = [moe_w_gate, moe_w_up, moe_w_down]
    mu_p = jnp.pad(rwkv_mu, ((0, 0), (0, RWM_COLS + RWL_COLS - rwkv_mu.shape[1])))
    att_tables = _att_tables(rel_bias, S)
    zeros_w = jnp.zeros((1, W), F32)
    final_g2 = final_g.reshape(1, D)

    xf = x.reshape(T, D)
    v_first = None
    for l in range(L):
        qkv, qkv_mid, qkv_far, rwm, rwl = _inproj(xf, mod, norm1_g[l].reshape(1, D), w_in_b,
                                                   mu_p[l].reshape(1, -1), B, S, l)
        att = _attention(qkv, qkv_mid, qkv_far, att_tables, B, S)
        v0 = rwkv_v0[l - 1].reshape(1, W) if l > 0 else zeros_w
        vec = jnp.concatenate([rwkv_w0[l].reshape(1, W), rwkv_a0[l].reshape(1, W), rwkv_k_k[l].reshape(1, W),
                               rwkv_k_a[l].reshape(1, W), rwkv_r_k[l].reshape(1, W), rwkv_ln_w[l].reshape(1, W),
                               rwkv_ln_b[l].reshape(1, W), v0], axis=0)
        w2p = jnp.pad(rwkv_w2[l], ((0, LANES - D_DECAY_LORA), (0, 0)))
        a2p = jnp.pad(rwkv_a2[l], ((D_DECAY_LORA, 0), (0, 0)))
        g2p = jnp.pad(rwkv_g2[l], ((0, RWL_COLS - LANES - D_GATE_LORA), (0, 0)))
        if l > 0:
            v1p = jnp.pad(rwkv_v1[l - 1], ((0, 0), (0, LANES - D_MV_LORA)))
            v2p = jnp.pad(rwkv_v2[l - 1], ((0, LANES - D_MV_LORA), (0, 0)))
        else:
            v1p = v2p = None
        rw, v_first = _rwkv(rwm, rwl, v_first, vec, w2p, a2p, g2p, v1p, v2p, B, S)
        li = l // 2
        if l % 2 == 0:
            x1, h = _outproj(xf, att, rw, mod, norm2_g[l].reshape(1, D), w_out_b, B, S, l)
            xf = _ffn_dense(x1, h, mod, *ffn_w, B, S, l, li)
        else:
            router = (jnp.pad(moe_router_w[li], ((0, 0), (0, LANES - N_EXPERTS))),
                      jnp.pad(moe_router_b[li], (0, LANES - N_EXPERTS)).reshape(1, LANES))
            x1, h, info = _outproj(xf, att, rw, mod, norm2_g[l].reshape(1, D), w_out_b, B, S, l, router)
            xf = _moe(x1, h, info, mod, *moe_w, final_g2, B, S, l, li,
                      final=(l == L - 1))
    if L % 2 == 1:
        raise NotImplementedError("final norm is fused into the last (expert) layer")
    return xf.reshape(B, S, D)
```

```python
import functools
import math

import numpy as np
import jax
import jax.numpy as jnp
from jax import lax
from jax.experimental import pallas as pl
from jax.experimental.pallas import tpu as pltpu

F32 = jnp.float32
BF16 = jnp.bfloat16

HEAD_DIM = 64
ATT_WIDTH = 512
RWKV_WIDTH = 512
HEAD_PAIRS = 4
LANES = 128
DILATED_GROUPS = ((128, 1), (512, 4), (2048, 16))
N_BUCKETS = 32
MAX_DISTANCE = 2048
NEG_INF = -1e30
D_DECAY_LORA = 64
D_AAA_LORA = 64
D_MV_LORA = 32
D_GATE_LORA = 160
N_IN = 3 * ATT_WIDTH + 3 * RWKV_WIDTH + D_DECAY_LORA + D_AAA_LORA + D_GATE_LORA
QKV_COLS = 3 * ATT_WIDTH
RWM_COLS = 3 * RWKV_WIDTH
RWL_COLS = 384
N_IN_PAD = QKV_COLS + RWM_COLS + RWL_COLS
GN_EPS = HEAD_DIM * 1e-5
RMS_EPS = 1e-6
N_EXPERTS = 8
TOP_K = 2
CHUNK = 64
VMEM_LIMIT = 56 * 1024 * 1024
LOG2_E = math.log2(math.e)
Q_SCALE = LOG2_E / math.sqrt(HEAD_DIM)
ROW_TILE = 512
FFN_ROW_TILE = 256
FFN_COL_TILE = 2816
EXPERT_ROW_TILE = 512
EXPERT_COL_TILE = 1792
COMBINE_ROW_TILE = 256
DMA_ISSUE_UNROLL = 8


def _cparams(sem):
    return pltpu.CompilerParams(dimension_semantics=sem, vmem_limit_bytes=VMEM_LIMIT)


def _tile(n, pref):
    t = min(n, pref)
    while n % t:
        t //= 2
    return t


def _mm(a, b):
    return jnp.dot(a.astype(BF16), b.astype(BF16), preferred_element_type=F32)


def _mm_nt(a, b):
    return lax.dot_general(a.astype(BF16), b.astype(BF16), (((1,), (1,)), ((), ())),
                           preferred_element_type=F32)


def _sigmoid(x):
    return 1.0 / (1.0 + jnp.exp(-x))


def _silu(x):
    return x * _sigmoid(x)


def _ada_kernel(c_ref, w_ref, b_ref, o_ref):
    ca = _silu(c_ref[...])
    o_ref[0] = _mm(ca, w_ref[0]) + b_ref[0]


def _ada_mod(c, w_ada, b_ada):
    L, D, N = w_ada.shape
    B = c.shape[0]
    tn = _tile(N, 1536)
    return pl.pallas_call(
        _ada_kernel,
        grid=(L, N // tn),
        in_specs=[pl.BlockSpec((B, D), lambda l, j: (0, 0)),
                  pl.BlockSpec((1, D, tn), lambda l, j: (l, 0, j)),
                  pl.BlockSpec((1, 1, tn), lambda l, j: (l, 0, j))],
        out_specs=pl.BlockSpec((1, B, tn), lambda l, j: (l, 0, j)),
        out_shape=jax.ShapeDtypeStruct((L, B, N), F32),
        compiler_params=_cparams(("arbitrary", "arbitrary")),
        name="ada_mod",
    )(c, w_ada, b_ada.reshape(L, 1, N))


def _rms_mod(xf, g, scale, shift):
    ms = jnp.mean(xf * xf, axis=-1, keepdims=True)
    return (xf * lax.rsqrt(ms + RMS_EPS) * g) * (1.0 + scale) + shift


def _inproj_kernel(x_ref, mod_ref, g_ref, w_ref, mu_ref, qkv_ref, qkv_mid_ref, qkv_far_ref, rwm_ref, rwl_ref,
                   carry_ref, qkv_scr):
    i = pl.program_id(1)
    tm = x_ref.shape[0]
    h = _rms_mod(x_ref[...], g_ref[...], mod_ref[0, 1:2, :], mod_ref[0, 0:1, :])
    acc = jnp.dot(h.astype(BF16), w_ref[0], preferred_element_type=F32)
    for c in range(QKV_COLS // LANES):
        cols = slice(c * LANES, (c + 1) * LANES)
        blk = acc[:, cols] * Q_SCALE if (c + 1) * LANES <= ATT_WIDTH else acc[:, cols]
        qkv_ref[:, cols] = blk.astype(BF16)
        qkv_scr[c] = blk
        for out_ref in (qkv_mid_ref, qkv_far_ref):
            d, rows = out_ref.shape[0], out_ref.shape[1]
            for r in range(d):
                out_ref[r, :, cols] = qkv_scr[c, pl.ds(r, rows, stride=d), :].astype(BF16)
    p = acc[:, QKV_COLS:]
    first = jnp.where(i == 0, 0.0, carry_ref[...])
    row = lax.broadcasted_iota(jnp.int32, p.shape, 0)
    prev = jnp.where(row == 0, first, pltpu.roll(p, 1, 0))
    carry_ref[...] = p[tm - 1:tm, :]
    pm = p + mu_ref[...] * (prev - p)
    rwm_ref[...] = pm[:, :RWM_COLS]
    rwl_ref[...] = pm[:, RWM_COLS:]


def _inproj(x, mod, g, w_in_l, mu_l, B, S, l):
    T, D = x.shape
    tm = _tile(S, ROW_TILE)
    nS = S // tm
    d2, d3 = DILATED_GROUPS[1][1], DILATED_GROUPS[2][1]
    return pl.pallas_call(
        _inproj_kernel,
        grid=(B, nS),
        in_specs=[pl.BlockSpec((tm, D), lambda b, i: (b * nS + i, 0)),
                  pl.BlockSpec((1, 6, D), lambda b, i: (l * B + b, 0, 0)),
                  pl.BlockSpec((1, D), lambda b, i: (0, 0)),
                  pl.BlockSpec((1, D, N_IN_PAD), lambda b, i: (l, 0, 0)),
                  pl.BlockSpec((1, RWM_COLS + RWL_COLS), lambda b, i: (0, 0))],
        out_specs=[pl.BlockSpec((tm, QKV_COLS), lambda b, i: (b * nS + i, 0)),
                   pl.BlockSpec((None, d2, tm // d2, QKV_COLS), lambda b, i: (b, 0, i, 0)),
                   pl.BlockSpec((None, d3, tm // d3, QKV_COLS), lambda b, i: (b, 0, i, 0)),
                   pl.BlockSpec((tm, RWM_COLS), lambda b, i: (b * nS + i, 0)),
                   pl.BlockSpec((tm, RWL_COLS), lambda b, i: (b * nS + i, 0))],
        out_shape=[jax.ShapeDtypeStruct((T, QKV_COLS), BF16),
                   jax.ShapeDtypeStruct((B, d2, S // d2, QKV_COLS), BF16),
                   jax.ShapeDtypeStruct((B, d3, S // d3, QKV_COLS), BF16),
                   jax.ShapeDtypeStruct((T, RWM_COLS), F32),
                   jax.ShapeDtypeStruct((T, RWL_COLS), F32)],
        scratch_shapes=[pltpu.VMEM((1, RWM_COLS + RWL_COLS), F32), pltpu.VMEM((QKV_COLS // LANES, tm, LANES), F32)],
        compiler_params=_cparams(("arbitrary", "arbitrary")),
        name="inproj",
    )(x, mod, g, w_in_l, mu_l)


def _t5_bucket(n):
    max_exact = N_BUCKETS // 2
    large = max_exact + (np.log(np.maximum(n, 1) / max_exact) / np.log(MAX_DISTANCE / max_exact)
                         * (N_BUCKETS - max_exact)).astype(np.int32)
    large = np.minimum(large, N_BUCKETS - 1)
    return np.where(n < max_exact, n, large).astype(np.int32)


ATT_BLOCK = 128
ATT_KEYS = 256


def _toeplitz(vec, c0, rows, cols):
    H, n = vec.shape
    period = rows + cols - 1
    w = jnp.concatenate([jnp.full((H, cols - 1), NEG_INF, F32), vec, jnp.full((H, rows), NEG_INF, F32)], axis=1)
    rw = w[:, ::-1]
    a = w.shape[1] - 1 - (c0 + cols - 1)
    z = jnp.concatenate([rw[:, a:a + cols], rw[:, a - (rows - 1):a]], axis=1)
    flat = jnp.tile(z, (1, rows))[:, :rows * (period - 1)]
    return flat.reshape(H, rows, period - 1)[..., :cols]


def _att_tables(rel_bias, S):
    (w1, d1), (w2, d2), (w3, d3) = DILATED_GROUPS
    assert d1 == 1 and w1 == ATT_KEYS - ATT_BLOCK and w2 // d2 == w1 and S % (d3 * 8) == 0
    off = np.arange(S)
    mult = np.zeros(S, np.int64)
    for w, d in DILATED_GROUPS:
        mult += ((off % d == 0) & (off <= w)).astype(np.int64)
    logm = np.where(mult > 0, np.log(np.maximum(mult, 1)), NEG_INF).astype(np.float32)
    per_off = (rel_bias[_t5_bucket(off)].T.astype(F32) + jnp.asarray(logm)[None, :]) * LOG2_E
    u2 = np.arange(S // d2)
    u3 = np.arange(S // d3)
    band = jnp.where(jnp.asarray(off <= w1)[None], per_off, NEG_INF)
    mid = jnp.where(jnp.asarray((u2 > w1 // d2) & (u2 <= w2 // d2))[None], per_off[:, ::d2], NEG_INF)
    far = jnp.where(jnp.asarray((u3 > w2 // d3) & (u3 <= w3 // d3))[None], per_off[:, ::d3], NEG_INF)
    shift = ATT_KEYS - ATT_BLOCK
    t_band = jnp.stack([_toeplitz(band, 0, ATT_BLOCK, ATT_KEYS), _toeplitz(band, shift, ATT_BLOCK, ATT_KEYS)], axis=1)
    t_mid = jnp.stack([_toeplitz(mid, 0, ATT_BLOCK, ATT_KEYS), _toeplitz(mid, shift, ATT_BLOCK, ATT_KEYS)], axis=1)
    t_far = _toeplitz(far, 0, S // d3, S // d3)
    return t_band, t_mid, t_far


def _attend(units):
    idx = range(len(units))
    heads = range(2)
    head0 = [lax.broadcasted_iota(jnp.int32, u[0].shape, 1) < HEAD_DIM for u in units]
    qh = [[jnp.where(head0[i], units[i][0], jnp.zeros_like(units[i][0])),
           jnp.where(head0[i], jnp.zeros_like(units[i][0]), units[i][0])] for i in idx]
    s = [[lax.dot_general(qh[i][h], units[i][1], (((1,), (1,)), ((), ())), preferred_element_type=F32)
          + units[i][3 + h] for h in heads] for i in idx]
    mx = [[jnp.max(s[i][h], axis=-1, keepdims=True) for h in heads] for i in idx]
    p = [[jnp.exp2(s[i][h] - mx[i][h]) for h in heads] for i in idx]
    ps = [[jnp.sum(p[i][h], axis=-1, keepdims=True) for h in heads] for i in idx]
    pv = [[jnp.dot(p[i][h].astype(BF16), units[i][2], preferred_element_type=F32) for h in heads] for i in idx]
    out = []
    for i in idx:
        shape = units[i][0].shape
        out.append((jnp.where(head0[i], jnp.broadcast_to(mx[i][0], shape), jnp.broadcast_to(mx[i][1], shape)),
                    jnp.where(head0[i], jnp.broadcast_to(ps[i][0], shape), jnp.broadcast_to(ps[i][1], shape)),
                    jnp.where(head0[i], pv[i][0], pv[i][1])))
    return out


def _attn_kernel(qa_ref, ka_ref, va_ref, qm_ref, km_ref, vm_ref, qf_ref, kf_ref, vf_ref,
                 tband_ref, tmid_ref, tfar_ref, o_ref, part_ref):
    R, K = ATT_BLOCK, ATT_KEYS
    S = qa_ref.shape[0]
    n_mid_res, n_mid = qm_ref.shape[0], qm_ref.shape[1]
    n_far_res, n_far = qf_ref.shape[0], qf_ref.shape[1]
    group = 16 if S % (16 * R) == 0 else 8

    def window(blk):
        start = pl.multiple_of(jnp.maximum(blk * R - (K - R), 0), R)
        return pl.ds(pl.multiple_of(blk * R, R), R), pl.ds(start, K), jnp.minimum(blk, 1)

    def park(cls, rows, results):
        for nat, parts in zip(rows, results):
            for j in range(3):
                part_ref[cls, j, nat, :] = parts[j]

    def band_body(it, _):
        units, rows = [], []
        for g in range(group):
            qrows, krows, var = window(it * group + g)
            rows.append(qrows)
            units.append((qa_ref[qrows, :], ka_ref[krows, :], va_ref[krows, :],
                          tband_ref[0, var], tband_ref[1, var]))
        park(0, rows, _attend(units))
        return 0

    lax.fori_loop(0, S // (R * group), band_body, 0)

    blocks_per_step = group // n_mid_res

    def mid_body(it, _):
        units, rows = [], []
        for g in range(blocks_per_step):
            blk = it * blocks_per_step + g
            qrows, krows, var = window(blk)
            for r in range(n_mid_res):
                rows.append(pl.ds(blk * (R * n_mid_res) + r, R, stride=n_mid_res))
                units.append((qm_ref[r, qrows, :], km_ref[r, krows, :], vm_ref[r, krows, :],
                              tmid_ref[0, var], tmid_ref[1, var]))
        park(1, rows, _attend(units))
        return 0

    lax.fori_loop(0, n_mid // (R * blocks_per_step), mid_body, 0)

    def far_body(it, _):
        units, rows = [], []
        for g in range(group):
            r = it * group + g
            rows.append(pl.ds(r, n_far, stride=n_far_res))
            units.append((qf_ref[r], kf_ref[r], vf_ref[r], tfar_ref[0], tfar_ref[1]))
        park(2, rows, _attend(units))
        return 0

    lax.fori_loop(0, n_far_res // group, far_body, 0)

    def merge_body(i, _):
        rows = pl.ds(pl.multiple_of(i * R, R), R)
        m = [part_ref[c, 0, rows, :] for c in range(3)]
        top = jnp.maximum(jnp.maximum(m[0], m[1]), m[2])
        w = [jnp.exp2(m[c] - top) for c in range(3)]
        total = sum(w[c] * part_ref[c, 1, rows, :] for c in range(3))
        acc = sum(w[c] * part_ref[c, 2, rows, :] for c in range(3))
        o_ref[rows, :] = (acc / total).astype(o_ref.dtype)
        return 0

    lax.fori_loop(0, S // R, merge_body, 0)


def _attention(qkv, qkv_mid, qkv_far, tables, B, S):
    T = qkv.shape[0]
    t_band, t_mid, t_far = tables
    d2, d3 = DILATED_GROUPS[1][1], DILATED_GROUPS[2][1]
    assert S % (ATT_BLOCK * 8) == 0 and S // d2 >= ATT_KEYS
    nat = lambda c: pl.BlockSpec((S, LANES), lambda hp, b: (b, c * HEAD_PAIRS + hp))
    mid = lambda c: pl.BlockSpec((None, d2, S // d2, LANES), lambda hp, b: (b, 0, 0, c * HEAD_PAIRS + hp))
    far = lambda c: pl.BlockSpec((None, d3, S // d3, LANES), lambda hp, b: (b, 0, 0, c * HEAD_PAIRS + hp))
    return pl.pallas_call(
        _attn_kernel,
        grid=(HEAD_PAIRS, B),
        in_specs=[nat(0), nat(1), nat(2), mid(0), mid(1), mid(2), far(0), far(1), far(2),
                  pl.BlockSpec((2,) + t_band.shape[1:], lambda hp, b: (hp, 0, 0, 0)),
                  pl.BlockSpec((2,) + t_mid.shape[1:], lambda hp, b: (hp, 0, 0, 0)),
                  pl.BlockSpec((2,) + t_far.shape[1:], lambda hp, b: (hp, 0, 0))],
        out_specs=pl.BlockSpec((S, LANES), lambda hp, b: (b, hp)),
        out_shape=jax.ShapeDtypeStruct((T, ATT_WIDTH), BF16),
        scratch_shapes=[pltpu.VMEM((3, 3, S, LANES), F32)],
        compiler_params=_cparams(("arbitrary", "arbitrary")),
        name="dilated_attn",
    )(qkv, qkv, qkv, qkv_mid, qkv_mid, qkv_mid, qkv_far, qkv_far, qkv_far, t_band, t_mid, t_far)


def _split2(x):
    hi = x.astype(BF16)
    lo = (x - hi.astype(F32)).astype(BF16)
    return hi, lo


def _head_sum(x, m2):
    hi, lo = _split2(x)
    return jnp.dot(jnp.concatenate([hi, lo], axis=1), m2, preferred_element_type=F32)


def _rwkv_kernel(*refs, has_vres, n_chunks, n_batch):
    if has_vres:
        (rwm_ref, rwl_ref, vf_ref, vec_ref, w2_ref, a2_ref, g2_ref, v1_ref, v2_ref,
         o_ref, state_ref, r_s, k_s, v_s, lw_s, kk_s, kb_s, y_s,
         ar_s, inv_s, bk_s, nrbk_s, akv_s, dec_s) = refs
    else:
        (rwm_ref, rwl_ref, vec_ref, w2_ref, a2_ref, g2_ref,
         o_ref, vf_out_ref, state_ref, r_s, k_s, v_s, lw_s, kk_s, kb_s, y_s,
         ar_s, inv_s, bk_s, nrbk_s, akv_s, dec_s) = refs
    C = CHUNK
    W = RWKV_WIDTH
    rows_per_batch = rwm_ref.shape[1]

    def stacked(ref, cols=slice(None)):
        return jnp.concatenate([ref[b, :, cols] for b in range(n_batch)], axis=0)

    def unstack(ref, x, cols=slice(None)):
        for b in range(n_batch):
            ref[b, :, cols] = x[b * rows_per_batch:(b + 1) * rows_per_batch].astype(ref.dtype)

    @pl.when(pl.program_id(1) == 0)
    def _():
        state_ref[...] = jnp.zeros_like(state_ref)

    ri = lax.broadcasted_iota(jnp.int32, (2 * LANES, LANES), 0)
    ci = lax.broadcasted_iota(jnp.int32, (2 * LANES, LANES), 1)
    m2 = ((ri % LANES) // HEAD_DIM == ci // HEAD_DIM).astype(BF16)

    vec = vec_ref[...]
    w0, a0, k_k, k_a, r_k, ln_w, ln_b, v0 = [vec[i:i + 1, :] for i in range(8)]

    r = stacked(rwm_ref, slice(0, W))
    k = stacked(rwm_ref, slice(W, 2 * W))
    v = stacked(rwm_ref, slice(2 * W, 3 * W))
    lora = stacked(rwl_ref, slice(0, LANES))
    w_raw = w0 + _mm(jnp.tanh(lora), w2_ref[...])
    lw_s[...] = -math.exp(-0.5) * _sigmoid(w_raw)
    a = _sigmoid(a0 + _mm(lora, a2_ref[...]))
    g = _mm(_sigmoid(stacked(rwl_ref, slice(LANES, RWL_COLS))), g2_ref[...])
    if has_vres:
        mix = _sigmoid(v0 + _mm(_mm(v, v1_ref[...]), v2_ref[...]))
        v = v + (stacked(vf_ref) - v) * mix
    else:
        unstack(vf_out_ref, v)
    kk = k * k_k
    for p in range(HEAD_PAIRS):
        ls = slice(p * LANES, (p + 1) * LANES)
        kkp = kk[:, ls]
        kkp = kkp * jnp.minimum(lax.rsqrt(_head_sum(kkp * kkp, m2)), 1e12)
        kk_s[:, ls] = kkp
        kb_s[:, ls] = kkp * a[:, ls]
    kmod = k * (1.0 + (a - 1.0) * k_a)
    r_s[...] = r
    k_s[...] = kmod
    v_s[...] = v

    ri = lax.broadcasted_iota(jnp.int32, (LANES, LANES), 0)
    ci = lax.broadcasted_iota(jnp.int32, (LANES, LANES), 1)
    same_head = (ri // C) == (ci // C)
    strict = same_head & ((ri % C) > (ci % C))
    incl = same_head & ((ri % C) >= (ci % C))
    eye = (ri == ci).astype(F32)
    lane = lax.broadcasted_iota(jnp.int32, (C, LANES), 1)
    head0 = lane < HEAD_DIM
    tri_r = lax.broadcasted_iota(jnp.int32, (C, 3 * C), 0)
    tri_c = lax.broadcasted_iota(jnp.int32, (C, 3 * C), 1)
    tri3 = ((tri_c % C) <= tri_r).astype(BF16)

    lane_slices = [slice(p * LANES, (p + 1) * LANES) for p in range(HEAD_PAIRS)]

    def cumsum(x):
        hi = x.astype(BF16)
        rem = x - hi.astype(F32)
        mid = rem.astype(BF16)
        lo = (rem - mid.astype(F32)).astype(BF16)
        return jnp.dot(tri3, jnp.concatenate([hi, mid, lo], axis=0), preferred_element_type=F32)

    def per_head_rows(x, y):
        zero = jnp.zeros_like(x)
        return jnp.concatenate([jnp.where(head0, x, zero), jnp.where(head0, zero, x),
                                jnp.where(head0, y, zero), jnp.where(head0, zero, y)], axis=0)

    def intra_body(it, _):
        units = [(it * chunk_group + g, p) for g in range(chunk_group) for p in range(HEAD_PAIRS)]
        idx = range(len(units))
        rows = [pl.ds(pl.multiple_of(c * C, C), C) for c, _ in units]
        ls = [lane_slices[p] for _, p in units]
        rc = [r_s[rows[u], ls[u]] for u in idx]
        kc = [k_s[rows[u], ls[u]] for u in idx]
        vc = [v_s[rows[u], ls[u]] for u in idx]
        lw = [lw_s[rows[u], ls[u]] for u in idx]
        kkc = [kk_s[rows[u], ls[u]] for u in idx]
        kbc = [kb_s[rows[u], ls[u]] for u in idx]
        cum = [cumsum(lw[u]) for u in idx]
        total = [cum[u][C - 1:C, :] for u in idx]
        g_inv = [jnp.exp(-cum[u]) for u in idx]
        a_t = [-kkc[u] * jnp.exp(cum[u] - lw[u]) for u in idx]
        r_t = [rc[u] * jnp.exp(cum[u]) for u in idx]
        b_t = [kbc[u] * g_inv[u] for u in idx]
        k_t = [kc[u] * g_inv[u] for u in idx]
        to_end = [jnp.exp(total[u] - cum[u]) for u in idx]
        gram = [_mm_nt(per_head_rows(a_t[u], r_t[u]), per_head_rows(b_t[u], k_t[u])) for u in idx]
        n_ab = [jnp.where(strict, gram[u][:LANES, :LANES], 0.0) for u in idx]
        n_ak = [jnp.where(strict, gram[u][:LANES, LANES:], 0.0) for u in idx]
        n_rb = [jnp.where(incl, gram[u][LANES:, :LANES], 0.0) for u in idx]
        n_rk = [jnp.where(incl, gram[u][LANES:, LANES:], 0.0) for u in idx]
        inv = [eye + n_ab[u] for u in idx]
        pw = [_mm(n_ab[u], n_ab[u]) for u in idx]
        n_rounds = int(math.log2(C)) - 1
        for i in range(n_rounds):
            if i + 1 < n_rounds:
                both = [_mm(pw[u], jnp.concatenate([inv[u], pw[u]], axis=1)) for u in idx]
                inv = [inv[u] + both[u][:, :LANES] for u in idx]
                pw = [both[u][:, LANES:] for u in idx]
            else:
                inv = [inv[u] + _mm(pw[u], inv[u]) for u in idx]
        for u, (c, p) in enumerate(units):
            ar_s[c, p] = jnp.concatenate([a_t[u], r_t[u]], axis=0).astype(BF16)
            inv_s[c, p] = inv[u].astype(BF16)
            akv_s[c, p] = _mm(n_ak[u], jnp.concatenate([vc[u], vc[u]], axis=0))
            nrbk_s[c, p] = jnp.concatenate([n_rb[u], n_rk[u]], axis=1).astype(BF16)
            bk_s[c, p] = jnp.concatenate([kbc[u] * to_end[u], kc[u] * to_end[u]], axis=0).astype(BF16)
            dec_s[c, p] = jnp.broadcast_to(jnp.exp(total[u]), (8, LANES))
        return 0

    chunks_per_batch = n_chunks // n_batch

    def state_body(step, _):
        units = [(b, b * chunks_per_batch + step, p) for b in range(n_batch) for p in range(HEAD_PAIRS)]
        idx = range(len(units))
        rows = [pl.ds(pl.multiple_of(c * C, C), C) for _, c, _ in units]
        ls = [lane_slices[p] for _, _, p in units]
        vc = [v_s[rows[u], ls[u]] for u in idx]
        st = [state_ref[b, p] for b, _, p in units]
        ah = [_mm_nt(ar_s[c, p], st[u]) for u, (_, c, p) in enumerate(units)]
        u_stack = [_mm(inv_s[c, p], jnp.concatenate([ah[u][:C], ah[u][:C]], axis=0) + akv_s[c, p])
                   for u, (_, c, p) in enumerate(units)]
        y_stack = [jnp.concatenate([ah[u][C:], ah[u][C:]], axis=0)
                   + _mm(nrbk_s[c, p], jnp.concatenate([u_stack[u], vc[u], vc[u]], axis=0))
                   for u, (_, c, p) in enumerate(units)]
        for u in idx:
            y_s[rows[u], ls[u]] = jnp.where(head0, y_stack[u][:C], y_stack[u][C:])
        uv_t = [jnp.concatenate([jnp.where(head0, u_stack[u][:C], u_stack[u][C:]), vc[u]], axis=0).T
                for u in idx]
        st_new = [st[u] * dec_s[c, p][0:1, :] + _mm(uv_t[u], bk_s[c, p]) for u, (_, c, p) in enumerate(units)]
        for u, (b, _, p) in enumerate(units):
            state_ref[b, p] = jnp.where(same_head, st_new[u], 0.0)
        return 0

    chunk_group = 4 if n_chunks % 4 == 0 else (2 if n_chunks % 2 == 0 else 1)
    lax.fori_loop(0, n_chunks // chunk_group, intra_body, 0)
    lax.fori_loop(0, chunks_per_batch, state_body, 0)

    for p in range(HEAD_PAIRS):
        ls = slice(p * LANES, (p + 1) * LANES)
        y = y_s[:, ls]
        mean = _head_sum(y, m2) * (1.0 / HEAD_DIM)
        d = y - mean
        var = _head_sum(d * d, m2) * (1.0 / HEAD_DIM)
        yn = d * lax.rsqrt(var + GN_EPS) * ln_w[:, ls] + ln_b[:, ls]
        bonus = _head_sum(r_s[:, ls] * k_s[:, ls] * r_k[:, ls], m2) * v_s[:, ls]
        unstack(o_ref, (yn + bonus) * g[:, ls], ls)


def _rwkv(rwm, rwl, v_first, vec, w2p, a2p, g2p, v1p, v2p, B, S):
    T = rwm.shape[0]
    W = RWKV_WIDTH
    nb = 4 if B % 4 == 0 else (2 if B % 2 == 0 else 1)
    tb = _tile(S, ROW_TILE // nb)
    n_chunks = nb * tb // CHUNK
    has_vres = v_first is not None
    stack = lambda a: a.reshape(B // nb, nb, S, a.shape[-1])
    row = lambda cols: pl.BlockSpec((None, nb, tb, cols), lambda b, i: (b, 0, i, 0))
    full = lambda b, i: (0, 0)
    in_specs = [row(RWM_COLS), row(RWL_COLS)]
    args = [stack(rwm), stack(rwl)]
    if has_vres:
        in_specs.append(row(W))
        args.append(stack(v_first))
    in_specs += [pl.BlockSpec(vec.shape, full), pl.BlockSpec(w2p.shape, full),
                 pl.BlockSpec(a2p.shape, full), pl.BlockSpec(g2p.shape, full)]
    args += [vec, w2p, a2p, g2p]
    if has_vres:
        in_specs += [pl.BlockSpec(v1p.shape, full), pl.BlockSpec(v2p.shape, full)]
        args += [v1p, v2p]
    out_specs = [row(W)]
    out_shape = [jax.ShapeDtypeStruct((B // nb, nb, S, W), BF16)]
    if not has_vres:
        out_specs.append(row(W))
        out_shape.append(jax.ShapeDtypeStruct((B // nb, nb, S, W), F32))
    outs = pl.pallas_call(
        functools.partial(_rwkv_kernel, has_vres=has_vres, n_chunks=n_chunks, n_batch=nb),
        grid=(B // nb, S // tb),
        in_specs=in_specs,
        out_specs=out_specs,
        out_shape=out_shape,
        scratch_shapes=([pltpu.VMEM((nb, HEAD_PAIRS, LANES, LANES), F32)] + [pltpu.VMEM((nb * tb, W), F32)] * 7
                        + [pltpu.VMEM((n_chunks, HEAD_PAIRS, 2 * CHUNK, LANES), BF16)] * 3
                        + [pltpu.VMEM((n_chunks, HEAD_PAIRS, 2 * CHUNK, 2 * LANES), BF16),
                           pltpu.VMEM((n_chunks, HEAD_PAIRS, 2 * CHUNK, LANES), F32),
                           pltpu.VMEM((n_chunks, HEAD_PAIRS, 8, LANES), F32)]),
        compiler_params=_cparams(("arbitrary", "arbitrary")),
        name="rwkv7",
    )(*args)
    outs = [o.reshape(T, W) for o in outs]
    return (outs[0], v_first) if has_vres else (outs[0], outs[1])


TOKEN_TILE = (8, LANES)


def _store_token_tiles(ref, x):
    n = x.shape[0]
    for s in range(TOKEN_TILE[0]):
        ref[pl.ds(s, n, stride=TOKEN_TILE[0]), :] = x[:, s * LANES:(s + 1) * LANES]


def _load_token_tiles(ref, n, start=0):
    return jnp.concatenate([ref[pl.ds(start + s, n, stride=TOKEN_TILE[0]), :] for s in range(TOKEN_TILE[0])],
                           axis=1)


def _outproj_kernel(*refs, moe):
    if moe:
        (x_ref, att_ref, rw_ref, mod_ref, g_ref, w_ref, rw_w_ref, rw_b_ref,
         x1_ref, h_ref, info_ref) = refs
    else:
        x_ref, att_ref, rw_ref, mod_ref, g_ref, w_ref, x1_ref, h_ref = refs
    mix = (jnp.dot(att_ref[...], w_ref[0, :ATT_WIDTH, :], preferred_element_type=F32)
           + jnp.dot(rw_ref[...], w_ref[0, ATT_WIDTH:, :], preferred_element_type=F32))
    x1 = x_ref[...] + mod_ref[0, 2:3, :] * mix
    x1_ref[...] = x1
    h = _rms_mod(x1, g_ref[...], mod_ref[0, 4:5, :], mod_ref[0, 3:4, :])
    if moe:
        _store_token_tiles(h_ref, h)
    else:
        h_ref[...] = h.astype(h_ref.dtype)
    if moe:
        h_hi, h_lo = _split2(h)
        w_hi, w_lo = _split2(rw_w_ref[...])
        logits = (jnp.dot(h_hi, w_hi, preferred_element_type=F32)
                  + jnp.dot(h_hi, w_lo, preferred_element_type=F32)
                  + jnp.dot(h_lo, w_hi, preferred_element_type=F32)) + rw_b_ref[...]
        lane = lax.broadcasted_iota(jnp.int32, logits.shape, 1)
        logits = jnp.where(lane < N_EXPERTS, logits, -jnp.inf)
        m1 = jnp.max(logits, axis=-1, keepdims=True)
        i1 = jnp.min(jnp.where(logits == m1, lane, LANES), axis=-1, keepdims=True)
        rest = jnp.where(lane == i1, -jnp.inf, logits)
        m2 = jnp.max(rest, axis=-1, keepdims=True)
        i2 = jnp.min(jnp.where(rest == m2, lane, LANES), axis=-1, keepdims=True)
        e = jnp.exp(m2 - m1)
        g1 = 1.0 / (1.0 + e)
        g2 = e / (1.0 + e)
        info = jnp.where(lane == 0, i1.astype(F32),
                         jnp.where(lane == 1, i2.astype(F32),
                                   jnp.where(lane == 2, g1, jnp.where(lane == 3, g2, 0.0))))
        info_ref[...] = info


def _outproj(x, att, rw, mod, g, w_out_b, B, S, l, router=None):
    T, D = x.shape
    tm = _tile(S, ROW_TILE)
    nS = S // tm
    moe = router is not None
    row = lambda b, i: (b * nS + i, 0)
    full = lambda b, i: (0, 0)
    in_specs = [pl.BlockSpec((tm, D), row), pl.BlockSpec((tm, ATT_WIDTH), row),
                pl.BlockSpec((tm, RWKV_WIDTH), row),
                pl.BlockSpec((1, 6, D), lambda b, i: (l * B + b, 0, 0)),
                pl.BlockSpec((1, D), full),
                pl.BlockSpec((1, D, D), lambda b, i: (l, 0, 0))]
    args = [x, att, rw, mod, g, w_out_b]
    if moe:
        out_specs = [pl.BlockSpec((tm, D), row), pl.BlockSpec((tm * TOKEN_TILE[0], LANES), row)]
        out_shape = [jax.ShapeDtypeStruct((T, D), F32), jax.ShapeDtypeStruct((T * TOKEN_TILE[0], LANES), F32)]
    else:
        out_specs = [pl.BlockSpec((tm, D), row), pl.BlockSpec((tm, D), row)]
        out_shape = [jax.ShapeDtypeStruct((T, D), F32), jax.ShapeDtypeStruct((T, D), BF16)]
    if moe:
        in_specs += [pl.BlockSpec((D, LANES), full), pl.BlockSpec((1, LANES), full)]
        args += list(router)
        out_specs.append(pl.BlockSpec((tm, LANES), row))
        out_shape.append(jax.ShapeDtypeStruct((T, LANES), F32))
    return pl.pallas_call(
        functools.partial(_outproj_kernel, moe=moe),
        grid=(B, nS),
        in_specs=in_specs,
        out_specs=out_specs,
        out_shape=out_shape,
        compiler_params=_cparams(("arbitrary", "arbitrary")),
        name="outproj",
    )(*args)


def _ffn_kernel(x_ref, h_ref, mod_ref, wg_ref, wu_ref, wd_ref, o_ref, acc_ref):
    f = pl.program_id(1)

    @pl.when(f == 0)
    def _():
        acc_ref[...] = jnp.zeros_like(acc_ref)

    hb = h_ref[...]
    gate = jnp.dot(hb, wg_ref[0], preferred_element_type=F32)
    up = jnp.dot(hb, wu_ref[0], preferred_element_type=F32)
    mid = (_silu(gate) * up).astype(BF16)
    acc_ref[...] += jnp.dot(mid, wd_ref[0], preferred_element_type=F32)

    @pl.when(f == pl.num_programs(1) - 1)
    def _():
        o_ref[...] = x_ref[...] + mod_ref[0, 5:6, :] * acc_ref[...]


def _ffn_dense(x1, h, mod, wg, wu, wd, B, S, l, li):
    T, D = x1.shape
    F = wg.shape[-1]
    tm = _tile(S, FFN_ROW_TILE)
    tf = _tile(F, FFN_COL_TILE)
    per_seq = S // tm
    return pl.pallas_call(
        _ffn_kernel,
        grid=(T // tm, F // tf),
        in_specs=[pl.BlockSpec((tm, D), lambda i, f: (i, 0)),
                  pl.BlockSpec((tm, D), lambda i, f: (i, 0)),
                  pl.BlockSpec((1, 6, D), lambda i, f: (l * B + i // per_seq, 0, 0)),
                  pl.BlockSpec((1, D, tf), lambda i, f: (li, 0, f)),
                  pl.BlockSpec((1, D, tf), lambda i, f: (li, 0, f)),
                  pl.BlockSpec((1, tf, D), lambda i, f: (li, f, 0))],
        out_specs=pl.BlockSpec((tm, D), lambda i, f: (i, 0)),
        out_shape=jax.ShapeDtypeStruct((T, D), F32),
        scratch_shapes=[pltpu.VMEM((tm, D), F32)],
        compiler_params=_cparams(("arbitrary", "arbitrary")),
        name="ffn_dense",
    )(x1, h, mod, wg, wu, wd)


def _experts_kernel(te_ref, nu_ref, tok_ref, h_ref, wg_ref, wu_ref, wd_ref, y_ref,
                    land_ref, xb_ref, acc_ref, sem):
    i = pl.program_id(0)
    f = pl.program_id(1)
    tm = xb_ref.shape[0]
    rows_per_token = TOKEN_TILE[0]
    n_used = nu_ref[0]

    def start_gather(tile):
        base = tile * tm

        def body(j, _):
            src = pl.multiple_of(tok_ref[base + j] * rows_per_token, rows_per_token)
            dst = pl.multiple_of(j * rows_per_token, rows_per_token)
            pltpu.make_async_copy(h_ref.at[pl.ds(src, rows_per_token)],
                                  land_ref.at[pl.ds(dst, rows_per_token)], sem).start()
            return 0

        lax.fori_loop(0, tm, body, 0, unroll=DMA_ISSUE_UNROLL)

    @pl.when(i < n_used)
    def _():
        @pl.when(f == 0)
        def _():
            @pl.when(i == 0)
            def _():
                start_gather(0)

            pltpu.make_async_copy(h_ref.at[pl.ds(0, tm * rows_per_token)], land_ref, sem).wait()
            xb_ref[...] = _load_token_tiles(land_ref, tm).astype(BF16)
            acc_ref[...] = jnp.zeros_like(acc_ref)

            @pl.when(i + 1 < n_used)
            def _():
                start_gather(i + 1)

        xb = xb_ref[...]
        gate = jnp.dot(xb, wg_ref[0, 0].astype(BF16), preferred_element_type=F32)
        up = jnp.dot(xb, wu_ref[0, 0].astype(BF16), preferred_element_type=F32)
        mid = (_silu(gate) * up).astype(BF16)
        acc_ref[...] += jnp.dot(mid, wd_ref[0, 0].astype(BF16), preferred_element_type=F32)

        @pl.when(f == pl.num_programs(1) - 1)
        def _():
            _store_token_tiles(y_ref, acc_ref[...])

    @pl.when((i >= n_used) & (f == 0))
    def _():
        y_ref[...] = jnp.zeros_like(y_ref)


def _experts(h, row_tok, tile_e, n_used, wg, wu, wd, li, tm):
    n_rows = row_tok.shape[0]
    D = wg.shape[-2]
    rpt = TOKEN_TILE[0]
    assert h.shape[0] >= tm * rpt
    F = wg.shape[-1]
    tf = _tile(F, EXPERT_COL_TILE)
    nf = F // tf

    def rows(i, f, te, nu, tok):
        return (i, 0)

    def col_block(i, f, nu):
        t = jnp.minimum(i, nu[0] - 1)
        step = jnp.where(i < nu[0], f, nf - 1)
        return jnp.where(t % 2 == 0, step, nf - 1 - step)

    def wcol(i, f, te, nu, tok):
        return (li, te[jnp.minimum(i, nu[0] - 1)], 0, col_block(i, f, nu))

    def wrow(i, f, te, nu, tok):
        return (li, te[jnp.minimum(i, nu[0] - 1)], col_block(i, f, nu), 0)

    return pl.pallas_call(
        _experts_kernel,
        grid_spec=pltpu.PrefetchScalarGridSpec(
            num_scalar_prefetch=3,
            grid=(n_rows // tm, nf),
            in_specs=[pl.BlockSpec(memory_space=pl.ANY),
                      pl.BlockSpec((1, 1, D, tf), wcol),
                      pl.BlockSpec((1, 1, D, tf), wcol),
                      pl.BlockSpec((1, 1, tf, D), wrow)],
            out_specs=pl.BlockSpec((tm * rpt, LANES), rows),
            scratch_shapes=[pltpu.VMEM((tm * rpt, LANES), F32), pltpu.VMEM((tm, D), BF16),
                            pltpu.VMEM((tm, D), F32), pltpu.SemaphoreType.DMA(())]),
        out_shape=jax.ShapeDtypeStruct((n_rows * rpt, LANES), F32),
        compiler_params=_cparams(("arbitrary", "arbitrary")),
        name="moe_experts",
    )(tile_e, n_used, row_tok, h, wg, wu, wd)


def _combine_kernel(pos_ref, y_ref, x_ref, info_ref, mod_ref, fg_ref, o_ref, buf, sem, *, tc, final):
    i = pl.program_id(0)
    slot = i % 2
    rows_per_token = TOKEN_TILE[0]

    def start_gather(tile, dst_slot):
        base = tile * (2 * tc)

        def body(j, _):
            src = pl.multiple_of(pos_ref[base + j] * rows_per_token, rows_per_token)
            dst = pl.multiple_of(j * rows_per_token, rows_per_token)
            pltpu.make_async_copy(y_ref.at[pl.ds(src, rows_per_token)],
                                  buf.at[dst_slot, pl.ds(dst, rows_per_token)], sem.at[dst_slot]).start()
            return 0

        lax.fori_loop(0, 2 * tc, body, 0, unroll=DMA_ISSUE_UNROLL)

    @pl.when(i == 0)
    def _():
        start_gather(0, 0)

    @pl.when(i + 1 < pl.num_programs(0))
    def _():
        start_gather(i + 1, 1 - slot)

    pltpu.make_async_copy(y_ref.at[pl.ds(0, 2 * tc * rows_per_token)], buf.at[slot], sem.at[slot]).wait()
    info = info_ref[...]
    ff = (info[:, 2:3] * _load_token_tiles(buf.at[slot], tc)
          + info[:, 3:4] * _load_token_tiles(buf.at[slot], tc, tc * rows_per_token))
    out = x_ref[...] + mod_ref[0, 5:6, :] * ff
    if final:
        ms = jnp.mean(out * out, axis=-1, keepdims=True)
        out = out * lax.rsqrt(ms + RMS_EPS) * fg_ref[...]
    o_ref[...] = out


def _combine(pos, y_rows, x1, info, mod, final_g, B, S, l, final):
    T, D = x1.shape
    tc = _tile(S, COMBINE_ROW_TILE)
    per_seq = S // tc
    return pl.pallas_call(
        functools.partial(_combine_kernel, tc=tc, final=final),
        grid_spec=pltpu.PrefetchScalarGridSpec(
            num_scalar_prefetch=1,
            grid=(T // tc,),
            in_specs=[pl.BlockSpec(memory_space=pl.ANY),
                      pl.BlockSpec((tc, D), lambda i, pos: (i, 0)),
                      pl.BlockSpec((tc, LANES), lambda i, pos: (i, 0)),
                      pl.BlockSpec((1, 6, D), lambda i, pos: (l * B + i // per_seq, 0, 0)),
                      pl.BlockSpec((1, D), lambda i, pos: (0, 0))],
            out_specs=pl.BlockSpec((tc, D), lambda i, pos: (i, 0)),
            scratch_shapes=[pltpu.VMEM((2, 2 * tc * TOKEN_TILE[0], LANES), F32), pltpu.SemaphoreType.DMA((2,))]),
        out_shape=jax.ShapeDtypeStruct((T, D), F32),
        compiler_params=_cparams(("arbitrary",)),
        name="moe_combine",
    )(pos, y_rows, x1, info, mod, final_g)


def _moe(x1, h, info, mod, wg, wu, wd, final_g, B, S, l, li, final):
    T, D = x1.shape
    TK = T * TOP_K
    tm = _tile(TK // N_EXPERTS, EXPERT_ROW_TILE)
    n_rows = TK + (N_EXPERTS - 1) * tm
    flat_e = info[:, :TOP_K].astype(jnp.int32).reshape(TK)
    onehot = (flat_e[:, None] == jnp.arange(N_EXPERTS, dtype=jnp.int32)[None, :]).astype(jnp.int32)
    csum = jnp.cumsum(onehot, axis=0)
    counts = csum[-1]
    rank = jnp.sum(onehot * csum, axis=1) - 1
    padded = (counts + tm - 1) // tm * tm
    pad_ends = jnp.cumsum(padded)
    pad_starts = pad_ends - padded
    dest = pad_starts[flat_e] + rank
    row_tok = jnp.zeros((n_rows,), jnp.int32).at[dest].set(jnp.arange(TK, dtype=jnp.int32) // TOP_K)
    tile_start = jnp.arange(n_rows // tm, dtype=jnp.int32) * tm
    tile_e = jnp.minimum(jnp.searchsorted(pad_ends, tile_start, side='right'), N_EXPERTS - 1).astype(jnp.int32)
    n_used_rows = pad_ends[-1:].astype(jnp.int32)
    n_used_tiles = n_used_rows // tm
    tc = _tile(S, COMBINE_ROW_TILE)
    pos = dest.reshape(T // tc, tc, TOP_K).transpose(0, 2, 1).reshape(TK).astype(jnp.int32)

    y_rows = _experts(h, row_tok, tile_e, n_used_tiles, wg, wu, wd, li, tm)
    return _combine(pos, y_rows, x1, info, mod, final_g, B, S, l, final)


def kernel(x, c, w_ada, b_ada, norm1_g, norm2_g, final_g, w_in, w_out, rel_bias, rwkv_mu, rwkv_w0, rwkv_w2, rwkv_a0, rwkv_a2, rwkv_g2, rwkv_k_k, rwkv_k_a, rwkv_r_k, rwkv_ln_w, rwkv_ln_b, rwkv_v0, rwkv_v1, rwkv_v2, ffn_w_gate, ffn_w_up, ffn_w_down, moe_router_w, moe_router_b, moe_w_gate, moe_w_up, moe_w_down):
    B, S, D = x.shape
    L = w_in.shape[0]
    T = B * S
    W = RWKV_WIDTH

    mod = _ada_mod(c, w_ada, b_ada).reshape(L * B, 6, D)
    w_in_b = jnp.pad(w_in, ((0, 0), (0, 0), (0, N_IN_PAD - N_IN))).astype(BF16)
    w_out_b = w_out.astype(BF16)
    ffn_w = [w.astype(BF16) for w in (ffn_w_gate, ffn_w_up, ffn_w_down)]
    moe_w = [moe_w_gate, moe_w_up, moe_w_down]
    mu_p = jnp.pad(rwkv_mu, ((0, 0), (0, RWM_COLS + RWL_COLS - rwkv_mu.shape[1])))
    att_tables = _att_tables(rel_bias, S)
    zeros_w = jnp.zeros((1, W), F32)
    final_g2 = final_g.reshape(1, D)

    xf = x.reshape(T, D)
    v_first = None
    for l in range(L):
        qkv, qkv_mid, qkv_far, rwm, rwl = _inproj(xf, mod, norm1_g[l].reshape(1, D), w_in_b,
                                                   mu_p[l].reshape(1, -1), B, S, l)
        att = _attention(qkv, qkv_mid, qkv_far, att_tables, B, S)
        v0 = rwkv_v0[l - 1].reshape(1, W) if l > 0 else zeros_w
        vec = jnp.concatenate([rwkv_w0[l].reshape(1, W), rwkv_a0[l].reshape(1, W), rwkv_k_k[l].reshape(1, W),
                               rwkv_k_a[l].reshape(1, W), rwkv_r_k[l].reshape(1, W), rwkv_ln_w[l].reshape(1, W),
                               rwkv_ln_b[l].reshape(1, W), v0], axis=0)
        w2p = jnp.pad(rwkv_w2[l], ((0, LANES - D_DECAY_LORA), (0, 0)))
        a2p = jnp.pad(rwkv_a2[l], ((D_DECAY_LORA, 0), (0, 0)))
        g2p = jnp.pad(rwkv_g2[l], ((0, RWL_COLS - LANES - D_GATE_LORA), (0, 0)))
        if l > 0:
            v1p = jnp.pad(rwkv_v1[l - 1], ((0, 0), (0, LANES - D_MV_LORA)))
            v2p = jnp.pad(rwkv_v2[l - 1], ((0, LANES - D_MV_LORA), (0, 0)))
        else:
            v1p = v2p = None
        rw, v_first = _rwkv(rwm, rwl, v_first, vec, w2p, a2p, g2p, v1p, v2p, B, S)
        li = l // 2
        if l % 2 == 0:
            x1, h = _outproj(xf, att, rw, mod, norm2_g[l].reshape(1, D), w_out_b, B, S, l)
            xf = _ffn_dense(x1, h, mod, *ffn_w, B, S, l, li)
        else:
            router = (jnp.pad(moe_router_w[li], ((0, 0), (0, LANES - N_EXPERTS))),
                      jnp.pad(moe_router_b[li], (0, LANES - N_EXPERTS)).reshape(1, LANES))
            x1, h, info = _outproj(xf, att, rw, mod, norm2_g[l].reshape(1, D), w_out_b, B, S, l, router)
            xf = _moe(x1, h, info, mod, *moe_w, final_g2, B, S, l, li,
                      final=(l == L - 1))
    if L % 2 == 1:
        raise NotImplementedError("final norm is fused into the last (expert) layer")
    return xf.reshape(B, S, D)
```

```python
import functools
import math

import numpy as np
import jax
import jax.numpy as jnp
from jax import lax
from jax.experimental import pallas as pl
from jax.experimental.pallas import tpu as pltpu

F32 = jnp.float32
BF16 = jnp.bfloat16

HEAD_DIM = 64
ATT_WIDTH = 512
RWKV_WIDTH = 512
HEAD_PAIRS = 4
LANES = 128
DILATED_GROUPS = ((128, 1), (512, 4), (2048, 16))
N_BUCKETS = 32
MAX_DISTANCE = 2048
NEG_INF = -1e30
D_DECAY_LORA = 64
D_AAA_LORA = 64
D_MV_LORA = 32
D_GATE_LORA = 160
N_IN = 3 * ATT_WIDTH + 3 * RWKV_WIDTH + D_DECAY_LORA + D_AAA_LORA + D_GATE_LORA
QKV_COLS = 3 * ATT_WIDTH
RWM_COLS = 3 * RWKV_WIDTH
RWL_COLS = 384
N_IN_PAD = QKV_COLS + RWM_COLS + RWL_COLS
GN_EPS = HEAD_DIM * 1e-5
RMS_EPS = 1e-6
N_EXPERTS = 8
TOP_K = 2
CHUNK = 64
VMEM_LIMIT = 56 * 1024 * 1024
LOG2_E = math.log2(math.e)
Q_SCALE = LOG2_E / math.sqrt(HEAD_DIM)
ROW_TILE = 512
FFN_ROW_TILE = 256
FFN_COL_TILE = 2816
EXPERT_ROW_TILE = 512
EXPERT_COL_TILE = 1792
COMBINE_ROW_TILE = 256
DMA_ISSUE_UNROLL = 8


def _cparams(sem):
    return pltpu.CompilerParams(dimension_semantics=sem, vmem_limit_bytes=VMEM_LIMIT)


def _tile(n, pref):
    t = min(n, pref)
    while n % t:
        t //= 2
    return t


def _mm(a, b):
    return jnp.dot(a.astype(BF16), b.astype(BF16), preferred_element_type=F32)


def _mm_nt(a, b):
    return lax.dot_general(a.astype(BF16), b.astype(BF16), (((1,), (1,)), ((), ())),
                           preferred_element_type=F32)


def _sigmoid(x):
    return 1.0 / (1.0 + jnp.exp(-x))


def _silu(x):
    return x * _sigmoid(x)


def _ada_kernel(c_ref, w_ref, b_ref, o_ref):
    ca = _silu(c_ref[...])
    o_ref[0] = _mm(ca, w_ref[0]) + b_ref[0]


def _ada_mod(c, w_ada, b_ada):
    L, D, N = w_ada.shape
    B = c.shape[0]
    tn = _tile(N, 1536)
    return pl.pallas_call(
        _ada_kernel,
        grid=(L, N // tn),
        in_specs=[pl.BlockSpec((B, D), lambda l, j: (0, 0)),
                  pl.BlockSpec((1, D, tn), lambda l, j: (l, 0, j)),
                  pl.BlockSpec((1, 1, tn), lambda l, j: (l, 0, j))],
        out_specs=pl.BlockSpec((1, B, tn), lambda l, j: (l, 0, j)),
        out_shape=jax.ShapeDtypeStruct((L, B, N), F32),
        compiler_params=_cparams(("arbitrary", "arbitrary")),
        name="ada_mod",
    )(c, w_ada, b_ada.reshape(L, 1, N))


def _rms_mod(xf, g, scale, shift):
    ms = jnp.mean(xf * xf, axis=-1, keepdims=True)
    return (xf * lax.rsqrt(ms + RMS_EPS) * g) * (1.0 + scale) + shift


def _inproj_kernel(x_ref, mod_ref, g_ref, w_ref, mu_ref, qkv_ref, qkv_mid_ref, qkv_far_ref, rwm_ref, rwl_ref,
                   carry_ref, qkv_scr):
    i = pl.program_id(1)
    tm = x_ref.shape[0]
    h = _rms_mod(x_ref[...], g_ref[...], mod_ref[0, 1:2, :], mod_ref[0, 0:1, :])
    acc = jnp.dot(h.astype(BF16), w_ref[0], preferred_element_type=F32)
    for c in range(QKV_COLS // LANES):
        cols = slice(c * LANES, (c + 1) * LANES)
        blk = acc[:, cols] * Q_SCALE if (c + 1) * LANES <= ATT_WIDTH else acc[:, cols]
        qkv_ref[:, cols] = blk.astype(BF16)
        qkv_scr[c] = blk
        for out_ref in (qkv_mid_ref, qkv_far_ref):
            d, rows = out_ref.shape[0], out_ref.shape[1]
            for r in range(d):
                out_ref[r, :, cols] = qkv_scr[c, pl.ds(r, rows, stride=d), :].astype(BF16)
    p = acc[:, QKV_COLS:]
    first = jnp.where(i == 0, 0.0, carry_ref[...])
    row = lax.broadcasted_iota(jnp.int32, p.shape, 0)
    prev = jnp.where(row == 0, first, pltpu.roll(p, 1, 0))
    carry_ref[...] = p[tm - 1:tm, :]
    pm = p + mu_ref[...] * (prev - p)
    rwm_ref[...] = pm[:, :RWM_COLS]
    rwl_ref[...] = pm[:, RWM_COLS:]


def _inproj(x, mod, g, w_in_l, mu_l, B, S, l):
    T, D = x.shape
    tm = _tile(S, ROW_TILE)
    nS = S // tm
    d2, d3 = DILATED_GROUPS[1][1], DILATED_GROUPS[2][1]
    return pl.pallas_call(
        _inproj_kernel,
        grid=(B, nS),
        in_specs=[pl.BlockSpec((tm, D), lambda b, i: (b * nS + i, 0)),
                  pl.BlockSpec((1, 6, D), lambda b, i: (l * B + b, 0, 0)),
                  pl.BlockSpec((1, D), lambda b, i: (0, 0)),
                  pl.BlockSpec((1, D, N_IN_PAD), lambda b, i: (l, 0, 0)),
                  pl.BlockSpec((1, RWM_COLS + RWL_COLS), lambda b, i: (0, 0))],
        out_specs=[pl.BlockSpec((tm, QKV_COLS), lambda b, i: (b * nS + i, 0)),
                   pl.BlockSpec((None, d2, tm // d2, QKV_COLS), lambda b, i: (b, 0, i, 0)),
                   pl.BlockSpec((None, d3, tm // d3, QKV_COLS), lambda b, i: (b, 0, i, 0)),
                   pl.BlockSpec((tm, RWM_COLS), lambda b, i: (b * nS + i, 0)),
                   pl.BlockSpec((tm, RWL_COLS), lambda b, i: (b * nS + i, 0))],
        out_shape=[jax.ShapeDtypeStruct((T, QKV_COLS), BF16),
                   jax.ShapeDtypeStruct((B, d2, S // d2, QKV_COLS), BF16),
                   jax.ShapeDtypeStruct((B, d3, S // d3, QKV_COLS), BF16),
                   jax.ShapeDtypeStruct((T, RWM_COLS), F32),
                   jax.ShapeDtypeStruct((T, RWL_COLS), F32)],
        scratch_shapes=[pltpu.VMEM((1, RWM_COLS + RWL_COLS), F32), pltpu.VMEM((QKV_COLS // LANES, tm, LANES), F32)],
        compiler_params=_cparams(("arbitrary", "arbitrary")),
        name="inproj",
    )(x, mod, g, w_in_l, mu_l)


def _t5_bucket(n):
    max_exact = N_BUCKETS // 2
    large = max_exact + (np.log(np.maximum(n, 1) / max_exact) / np.log(MAX_DISTANCE / max_exact)
                         * (N_BUCKETS - max_exact)).astype(np.int32)
    large = np.minimum(large, N_BUCKETS - 1)
    return np.where(n < max_exact, n, large).astype(np.int32)


ATT_BLOCK = 128
ATT_KEYS = 256


def _toeplitz(vec, c0, rows, cols):
    H, n = vec.shape
    period = rows + cols - 1
    w = jnp.concatenate([jnp.full((H, cols - 1), NEG_INF, F32), vec, jnp.full((H, rows), NEG_INF, F32)], axis=1)
    rw = w[:, ::-1]
    a = w.shape[1] - 1 - (c0 + cols - 1)
    z = jnp.concatenate([rw[:, a:a + cols], rw[:, a - (rows - 1):a]], axis=1)
    flat = jnp.tile(z, (1, rows))[:, :rows * (period - 1)]
    return flat.reshape(H, rows, period - 1)[..., :cols]


def _att_tables(rel_bias, S):
    (w1, d1), (w2, d2), (w3, d3) = DILATED_GROUPS
    assert d1 == 1 and w1 == ATT_KEYS - ATT_BLOCK and w2 // d2 == w1 and S % (d3 * 8) == 0
    off = np.arange(S)
    mult = np.zeros(S, np.int64)
    for w, d in DILATED_GROUPS:
        mult += ((off % d == 0) & (off <= w)).astype(np.int64)
    logm = np.where(mult > 0, np.log(np.maximum(mult, 1)), NEG_INF).astype(np.float32)
    per_off = (rel_bias[_t5_bucket(off)].T.astype(F32) + jnp.asarray(logm)[None, :]) * LOG2_E
    u2 = np.arange(S // d2)
    u3 = np.arange(S // d3)
    band = jnp.where(jnp.asarray(off <= w1)[None], per_off, NEG_INF)
    mid = jnp.where(jnp.asarray((u2 > w1 // d2) & (u2 <= w2 // d2))[None], per_off[:, ::d2], NEG_INF)
    far = jnp.where(jnp.asarray((u3 > w2 // d3) & (u3 <= w3 // d3))[None], per_off[:, ::d3], NEG_INF)
    shift = ATT_KEYS - ATT_BLOCK
    t_band = jnp.stack([_toeplitz(band, 0, ATT_BLOCK, ATT_KEYS), _toeplitz(band, shift, ATT_BLOCK, ATT_KEYS)], axis=1)
    t_mid = jnp.stack([_toeplitz(mid, 0, ATT_BLOCK, ATT_KEYS), _toeplitz(mid, shift, ATT_BLOCK, ATT_KEYS)], axis=1)
    t_far = _toeplitz(far, 0, S // d3, S // d3)
    return t_band, t_mid, t_far


def _attend(units):
    idx = range(len(units))
    heads = range(2)
    head0 = [lax.broadcasted_iota(jnp.int32, u[0].shape, 1) < HEAD_DIM for u in units]
    qh = [[jnp.where(head0[i], units[i][0], jnp.zeros_like(units[i][0])),
           jnp.where(head0[i], jnp.zeros_like(units[i][0]), units[i][0])] for i in idx]
    s = [[lax.dot_general(qh[i][h], units[i][1], (((1,), (1,)), ((), ())), preferred_element_type=F32)
          + units[i][3 + h] for h in heads] for i in idx]
    mx = [[jnp.max(s[i][h], axis=-1, keepdims=True) for h in heads] for i in idx]
    p = [[jnp.exp2(s[i][h] - mx[i][h]) for h in heads] for i in idx]
    ps = [[jnp.sum(p[i][h], axis=-1, keepdims=True) for h in heads] for i in idx]
    pv = [[jnp.dot(p[i][h].astype(BF16), units[i][2], preferred_element_type=F32) for h in heads] for i in idx]
    out = []
    for i in idx:
        shape = units[i][0].shape
        out.append((jnp.where(head0[i], jnp.broadcast_to(mx[i][0], shape), jnp.broadcast_to(mx[i][1], shape)),
                    jnp.where(head0[i], jnp.broadcast_to(ps[i][0], shape), jnp.broadcast_to(ps[i][1], shape)),
                    jnp.where(head0[i], pv[i][0], pv[i][1])))
    return out


def _attn_kernel(qa_ref, ka_ref, va_ref, qm_ref, km_ref, vm_ref, qf_ref, kf_ref, vf_ref,
                 tband_ref, tmid_ref, tfar_ref, o_ref, part_ref):
    R, K = ATT_BLOCK, ATT_KEYS
    S = qa_ref.shape[0]
    n_mid_res, n_mid = qm_ref.shape[0], qm_ref.shape[1]
    n_far_res, n_far = qf_ref.shape[0], qf_ref.shape[1]
    group = 16 if S % (16 * R) == 0 else 8

    def window(blk):
        start = pl.multiple_of(jnp.maximum(blk * R - (K - R), 0), R)
        return pl.ds(pl.multiple_of(blk * R, R), R), pl.ds(start, K), jnp.minimum(blk, 1)

    def park(cls, rows, results):
        for nat, parts in zip(rows, results):
            for j in range(3):
                part_ref[cls, j, nat, :] = parts[j]

    def band_body(it, _):
        units, rows = [], []
        for g in range(group):
            qrows, krows, var = window(it * group + g)
            rows.append(qrows)
            units.append((qa_ref[qrows, :], ka_ref[krows, :], va_ref[krows, :],
                          tband_ref[0, var], tband_ref[1, var]))
        park(0, rows, _attend(units))
        return 0

    lax.fori_loop(0, S // (R * group), band_body, 0)

    blocks_per_step = group // n_mid_res

    def mid_body(it, _):
        units, rows = [], []
        for g in range(blocks_per_step):
            blk = it * blocks_per_step + g
            qrows, krows, var = window(blk)
            for r in range(n_mid_res):
                rows.append(pl.ds(blk * (R * n_mid_res) + r, R, stride=n_mid_res))
                units.append((qm_ref[r, qrows, :], km_ref[r, krows, :], vm_ref[r, krows, :],
                              tmid_ref[0, var], tmid_ref[1, var]))
        park(1, rows, _attend(units))
        return 0

    lax.fori_loop(0, n_mid // (R * blocks_per_step), mid_body, 0)

    def far_body(it, _):
        units, rows = [], []
        for g in range(group):
            r = it * group + g
            rows.append(pl.ds(r, n_far, stride=n_far_res))
            units.append((qf_ref[r], kf_ref[r], vf_ref[r], tfar_ref[0], tfar_ref[1]))
        park(2, rows, _attend(units))
        return 0

    lax.fori_loop(0, n_far_res // group, far_body, 0)

    def merge_body(i, _):
        rows = pl.ds(pl.multiple_of(i * R, R), R)
        m = [part_ref[c, 0, rows, :] for c in range(3)]
        top = jnp.maximum(jnp.maximum(m[0], m[1]), m[2])
        w = [jnp.exp2(m[c] - top) for c in range(3)]
        total = sum(w[c] * part_ref[c, 1, rows, :] for c in range(3))
        acc = sum(w[c] * part_ref[c, 2, rows, :] for c in range(3))
        o_ref[rows, :] = (acc / total).astype(o_ref.dtype)
        return 0

    lax.fori_loop(0, S // R, merge_body, 0)


def _attention(qkv, qkv_mid, qkv_far, tables, B, S):
    T = qkv.shape[0]
    t_band, t_mid, t_far = tables
    d2, d3 = DILATED_GROUPS[1][1], DILATED_GROUPS[2][1]
    assert S % (ATT_BLOCK * 8) == 0 and S // d2 >= ATT_KEYS
    nat = lambda c: pl.BlockSpec((S, LANES), lambda hp, b: (b, c * HEAD_PAIRS + hp))
    mid = lambda c: pl.BlockSpec((None, d2, S // d2, LANES), lambda hp, b: (b, 0, 0, c * HEAD_PAIRS + hp))
    far = lambda c: pl.BlockSpec((None, d3, S // d3, LANES), lambda hp, b: (b, 0, 0, c * HEAD_PAIRS + hp))
    return pl.pallas_call(
        _attn_kernel,
        grid=(HEAD_PAIRS, B),
        in_specs=[nat(0), nat(1), nat(2), mid(0), mid(1), mid(2), far(0), far(1), far(2),
                  pl.BlockSpec((2,) + t_band.shape[1:], lambda hp, b: (hp, 0, 0, 0)),
                  pl.BlockSpec((2,) + t_mid.shape[1:], lambda hp, b: (hp, 0, 0, 0)),
                  pl.BlockSpec((2,) + t_far.shape[1:], lambda hp, b: (hp, 0, 0))],
        out_specs=pl.BlockSpec((S, LANES), lambda hp, b: (b, hp)),
        out_shape=jax.ShapeDtypeStruct((T, ATT_WIDTH), BF16),
        scratch_shapes=[pltpu.VMEM((3, 3, S, LANES), F32)],
        compiler_params=_cparams(("arbitrary", "arbitrary")),
        name="dilated_attn",
    )(qkv, qkv, qkv, qkv_mid, qkv_mid, qkv_mid, qkv_far, qkv_far, qkv_far, t_band, t_mid, t_far)


def _split2(x):
    hi = x.astype(BF16)
    lo = (x - hi.astype(F32)).astype(BF16)
    return hi, lo


def _head_sum(x, m2):
    hi, lo = _split2(x)
    return jnp.dot(jnp.concatenate([hi, lo], axis=1), m2, preferred_element_type=F32)


def _rwkv_kernel(*refs, has_vres, n_chunks, n_batch):
    if has_vres:
        (rwm_ref, rwl_ref, vf_ref, vec_ref, w2_ref, a2_ref, g2_ref, v1_ref, v2_ref,
         o_ref, state_ref, r_s, k_s, v_s, lw_s, kk_s, kb_s, y_s,
         ar_s, inv_s, bk_s, nrbk_s, akv_s, dec_s) = refs
    else:
        (rwm_ref, rwl_ref, vec_ref, w2_ref, a2_ref, g2_ref,
         o_ref, vf_out_ref, state_ref, r_s, k_s, v_s, lw_s, kk_s, kb_s, y_s,
         ar_s, inv_s, bk_s, nrbk_s, akv_s, dec_s) = refs
    C = CHUNK
    W = RWKV_WIDTH
    rows_per_batch = rwm_ref.shape[1]

    def stacked(ref, cols=slice(None)):
        return jnp.concatenate([ref[b, :, cols] for b in range(n_batch)], axis=0)

    def unstack(ref, x, cols=slice(None)):
        for b in range(n_batch):
            ref[b, :, cols] = x[b * rows_per_batch:(b + 1) * rows_per_batch].astype(ref.dtype)

    @pl.when(pl.program_id(1) == 0)
    def _():
        state_ref[...] = jnp.zeros_like(state_ref)

    ri = lax.broadcasted_iota(jnp.int32, (2 * LANES, LANES), 0)
    ci = lax.broadcasted_iota(jnp.int32, (2 * LANES, LANES), 1)
    m2 = ((ri % LANES) // HEAD_DIM == ci // HEAD_DIM).astype(BF16)

    vec = vec_ref[...]
    w0, a0, k_k, k_a, r_k, ln_w, ln_b, v0 = [vec[i:i + 1, :] for i in range(8)]

    r = stacked(rwm_ref, slice(0, W))
    k = stacked(rwm_ref, slice(W, 2 * W))
    v = stacked(rwm_ref, slice(2 * W, 3 * W))
    lora = stacked(rwl_ref, slice(0, LANES))
    w_raw = w0 + _mm(jnp.tanh(lora), w2_ref[...])
    lw_s[...] = -math.exp(-0.5) * _sigmoid(w_raw)
    a = _sigmoid(a0 + _mm(lora, a2_ref[...]))
    g = _mm(_sigmoid(stacked(rwl_ref, slice(LANES, RWL_COLS))), g2_ref[...])
    if has_vres:
        mix = _sigmoid(v0 + _mm(_mm(v, v1_ref[...]), v2_ref[...]))
        v = v + (stacked(vf_ref) - v) * mix
    else:
        unstack(vf_out_ref, v)
    kk = k * k_k
    for p in range(HEAD_PAIRS):
        ls = slice(p * LANES, (p + 1) * LANES)
        kkp = kk[:, ls]
        kkp = kkp * jnp.minimum(lax.rsqrt(_head_sum(kkp * kkp, m2)), 1e12)
        kk_s[:, ls] = kkp
        kb_s[:, ls] = kkp * a[:, ls]
    kmod = k * (1.0 + (a - 1.0) * k_a)
    r_s[...] = r
    k_s[...] = kmod
    v_s[...] = v

    ri = lax.broadcasted_iota(jnp.int32, (LANES, LANES), 0)
    ci = lax.broadcasted_iota(jnp.int32, (LANES, LANES), 1)
    same_head = (ri // C) == (ci // C)
    strict = same_head & ((ri % C) > (ci % C))
    incl = same_head & ((ri % C) >= (ci % C))
    eye = (ri == ci).astype(F32)
    lane = lax.broadcasted_iota(jnp.int32, (C, LANES), 1)
    head0 = lane < HEAD_DIM
    tri_r = lax.broadcasted_iota(jnp.int32, (C, 3 * C), 0)
    tri_c = lax.broadcasted_iota(jnp.int32, (C, 3 * C), 1)
    tri3 = ((tri_c % C) <= tri_r).astype(BF16)

    lane_slices = [slice(p * LANES, (p + 1) * LANES) for p in range(HEAD_PAIRS)]

    def cumsum(x):
        hi = x.astype(BF16)
        rem = x - hi.astype(F32)
        mid = rem.astype(BF16)
        lo = (rem - mid.astype(F32)).astype(BF16)
        return jnp.dot(tri3, jnp.concatenate([hi, mid, lo], axis=0), preferred_element_type=F32)

    def per_head_rows(x, y):
        zero = jnp.zeros_like(x)
        return jnp.concatenate([jnp.where(head0, x, zero), jnp.where(head0, zero, x),
                                jnp.where(head0, y, zero), jnp.where(head0, zero, y)], axis=0)

    def intra_body(it, _):
        units = [(it * chunk_group + g, p) for g in range(chunk_group) for p in range(HEAD_PAIRS)]
        idx = range(len(units))
        rows = [pl.ds(pl.multiple_of(c * C, C), C) for c, _ in units]
        ls = [lane_slices[p] for _, p in units]
        rc = [r_s[rows[u], ls[u]] for u in idx]
        kc = [k_s[rows[u], ls[u]] for u in idx]
        vc = [v_s[rows[u], ls[u]] for u in idx]
        lw = [lw_s[rows[u], ls[u]] for u in idx]
        kkc = [kk_s[rows[u], ls[u]] for u in idx]
        kbc = [kb_s[rows[u], ls[u]] for u in idx]
        cum = [cumsum(lw[u]) for u in idx]
        total = [cum[u][C - 1:C, :] for u in idx]
        g_inv = [jnp.exp(-cum[u]) for u in idx]
        a_t = [-kkc[u] * jnp.exp(cum[u] - lw[u]) for u in idx]
        r_t = [rc[u] * jnp.exp(cum[u]) for u in idx]
        b_t = [kbc[u] * g_inv[u] for u in idx]
        k_t = [kc[u] * g_inv[u] for u in idx]
        to_end = [jnp.exp(total[u] - cum[u]) for u in idx]
        gram = [_mm_nt(per_head_rows(a_t[u], r_t[u]), per_head_rows(b_t[u], k_t[u])) for u in idx]
        n_ab = [jnp.where(strict, gram[u][:LANES, :LANES], 0.0) for u in idx]
        n_ak = [jnp.where(strict, gram[u][:LANES, LANES:], 0.0) for u in idx]
        n_rb = [jnp.where(incl, gram[u][LANES:, :LANES], 0.0) for u in idx]
        n_rk = [jnp.where(incl, gram[u][LANES:, LANES:], 0.0) for u in idx]
        inv = [eye + n_ab[u] for u in idx]
        pw = [_mm(n_ab[u], n_ab[u]) for u in idx]
        n_rounds = int(math.log2(C)) - 1
        for i in range(n_rounds):
            if i + 1 < n_rounds:
                both = [_mm(pw[u], jnp.concatenate([inv[u], pw[u]], axis=1)) for u in idx]
                inv = [inv[u] + both[u][:, :LANES] for u in idx]
                pw = [both[u][:, LANES:] for u in idx]
            else:
                inv = [inv[u] + _mm(pw[u], inv[u]) for u in idx]
        for u, (c, p) in enumerate(units):
            ar_s[c, p] = jnp.concatenate([a_t[u], r_t[u]], axis=0).astype(BF16)
            inv_s[c, p] = inv[u].astype(BF16)
            akv_s[c, p] = _mm(n_ak[u], jnp.concatenate([vc[u], vc[u]], axis=0))
            nrbk_s[c, p] = jnp.concatenate([n_rb[u], n_rk[u]], axis=1).astype(BF16)
            bk_s[c, p] = jnp.concatenate([kbc[u] * to_end[u], kc[u] * to_end[u]], axis=0).astype(BF16)
            dec_s[c, p] = jnp.broadcast_to(jnp.exp(total[u]), (8, LANES))
        return 0

    chunks_per_batch = n_chunks // n_batch

    def state_body(step, _):
        units = [(b, b * chunks_per_batch + step, p) for b in range(n_batch) for p in range(HEAD_PAIRS)]
        idx = range(len(units))
        rows = [pl.ds(pl.multiple_of(c * C, C), C) for _, c, _ in units]
        ls = [lane_slices[p] for _, _, p in units]
        vc = [v_s[rows[u], ls[u]] for u in idx]
        st = [state_ref[b, p] for b, _, p in units]
        ah = [_mm_nt(ar_s[c, p], st[u]) for u, (_, c, p) in enumerate(units)]
        u_stack = [_mm(inv_s[c, p], jnp.concatenate([ah[u][:C], ah[u][:C]], axis=0) + akv_s[c, p])
                   for u, (_, c, p) in enumerate(units)]
        y_stack = [jnp.concatenate([ah[u][C:], ah[u][C:]], axis=0)
                   + _mm(nrbk_s[c, p], jnp.concatenate([u_stack[u], vc[u], vc[u]], axis=0))
                   for u, (_, c, p) in enumerate(units)]
        for u in idx:
            y_s[rows[u], ls[u]] = jnp.where(head0, y_stack[u][:C], y_stack[u][C:])
        uv_t = [jnp.concatenate([jnp.where(head0, u_stack[u][:C], u_stack[u][C:]), vc[u]], axis=0).T
                for u in idx]
        st_new = [st[u] * dec_s[c, p][0:1, :] + _mm(uv_t[u], bk_s[c, p]) for u, (_, c, p) in enumerate(units)]
        for u, (b, _, p) in enumerate(units):
            state_ref[b, p] = jnp.where(same_head, st_new[u], 0.0)
        return 0

    chunk_group = next(g for g in (8, 4, 2, 1) if n_chunks % g == 0)
    lax.fori_loop(0, n_chunks // chunk_group, intra_body, 0)
    lax.fori_loop(0, chunks_per_batch, state_body, 0)

    for p in range(HEAD_PAIRS):
        ls = slice(p * LANES, (p + 1) * LANES)
        y = y_s[:, ls]
        mean = _head_sum(y, m2) * (1.0 / HEAD_DIM)
        d = y - mean
        var = _head_sum(d * d, m2) * (1.0 / HEAD_DIM)
        yn = d * lax.rsqrt(var + GN_EPS) * ln_w[:, ls] + ln_b[:, ls]
        bonus = _head_sum(r_s[:, ls] * k_s[:, ls] * r_k[:, ls], m2) * v_s[:, ls]
        unstack(o_ref, (yn + bonus) * g[:, ls], ls)


def _rwkv(rwm, rwl, v_first, vec, w2p, a2p, g2p, v1p, v2p, B, S):
    T = rwm.shape[0]
    W = RWKV_WIDTH
    nb = next(n for n in (8, 4, 2, 1) if B % n == 0)
    tb = _tile(S, ROW_TILE // nb)
    n_chunks = nb * tb // CHUNK
    has_vres = v_first is not None
    stack = lambda a: a.reshape(B // nb, nb, S, a.shape[-1])
    row = lambda cols: pl.BlockSpec((None, nb, tb, cols), lambda b, i: (b, 0, i, 0))
    full = lambda b, i: (0, 0)
    in_specs = [row(RWM_COLS), row(RWL_COLS)]
    args = [stack(rwm), stack(rwl)]
    if has_vres:
        in_specs.append(row(W))
        args.append(stack(v_first))
    in_specs += [pl.BlockSpec(vec.shape, full), pl.BlockSpec(w2p.shape, full),
                 pl.BlockSpec(a2p.shape, full), pl.BlockSpec(g2p.shape, full)]
    args += [vec, w2p, a2p, g2p]
    if has_vres:
        in_specs += [pl.BlockSpec(v1p.shape, full), pl.BlockSpec(v2p.shape, full)]
        args += [v1p, v2p]
    out_specs = [row(W)]
    out_shape = [jax.ShapeDtypeStruct((B // nb, nb, S, W), BF16)]
    if not has_vres:
        out_specs.append(row(W))
        out_shape.append(jax.ShapeDtypeStruct((B // nb, nb, S, W), F32))
    outs = pl.pallas_call(
        functools.partial(_rwkv_kernel, has_vres=has_vres, n_chunks=n_chunks, n_batch=nb),
        grid=(B // nb, S // tb),
        in_specs=in_specs,
        out_specs=out_specs,
        out_shape=out_shape,
        scratch_shapes=([pltpu.VMEM((nb, HEAD_PAIRS, LANES, LANES), F32)] + [pltpu.VMEM((nb * tb, W), F32)] * 7
                        + [pltpu.VMEM((n_chunks, HEAD_PAIRS, 2 * CHUNK, LANES), BF16)] * 3
                        + [pltpu.VMEM((n_chunks, HEAD_PAIRS, 2 * CHUNK, 2 * LANES), BF16),
                           pltpu.VMEM((n_chunks, HEAD_PAIRS, 2 * CHUNK, LANES), F32),
                           pltpu.VMEM((n_chunks, HEAD_PAIRS, 8, LANES), F32)]),
        compiler_params=_cparams(("arbitrary", "arbitrary")),
        name="rwkv7",
    )(*args)
    outs = [o.reshape(T, W) for o in outs]
    return (outs[0], v_first) if has_vres else (outs[0], outs[1])


TOKEN_TILE = (8, LANES)


def _store_token_tiles(ref, x):
    n = x.shape[0]
    for s in range(TOKEN_TILE[0]):
        ref[pl.ds(s, n, stride=TOKEN_TILE[0]), :] = x[:, s * LANES:(s + 1) * LANES]


def _load_token_tiles(ref, n, start=0):
    return jnp.concatenate([ref[pl.ds(start + s, n, stride=TOKEN_TILE[0]), :] for s in range(TOKEN_TILE[0])],
                           axis=1)


def _outproj_kernel(*refs, moe):
    if moe:
        (x_ref, att_ref, rw_ref, mod_ref, g_ref, w_ref, rw_w_ref, rw_b_ref,
         x1_ref, h_ref, info_ref) = refs
    else:
        x_ref, att_ref, rw_ref, mod_ref, g_ref, w_ref, x1_ref, h_ref = refs
    mix = (jnp.dot(att_ref[...], w_ref[0, :ATT_WIDTH, :], preferred_element_type=F32)
           + jnp.dot(rw_ref[...], w_ref[0, ATT_WIDTH:, :], preferred_element_type=F32))
    x1 = x_ref[...] + mod_ref[0, 2:3, :] * mix
    x1_ref[...] = x1
    h = _rms_mod(x1, g_ref[...], mod_ref[0, 4:5, :], mod_ref[0, 3:4, :])
    if moe:
        _store_token_tiles(h_ref, h)
    else:
        h_ref[...] = h.astype(h_ref.dtype)
    if moe:
        h_hi, h_lo = _split2(h)
        w_hi, w_lo = _split2(rw_w_ref[...])
        logits = (jnp.dot(h_hi, w_hi, preferred_element_type=F32)
                  + jnp.dot(h_hi, w_lo, preferred_element_type=F32)
                  + jnp.dot(h_lo, w_hi, preferred_element_type=F32)) + rw_b_ref[...]
        lane = lax.broadcasted_iota(jnp.int32, logits.shape, 1)
        logits = jnp.where(lane < N_EXPERTS, logits, -jnp.inf)
        m1 = jnp.max(logits, axis=-1, keepdims=True)
        i1 = jnp.min(jnp.where(logits == m1, lane, LANES), axis=-1, keepdims=True)
        rest = jnp.where(lane == i1, -jnp.inf, logits)
        m2 = jnp.max(rest, axis=-1, keepdims=True)
        i2 = jnp.min(jnp.where(rest == m2, lane, LANES), axis=-1, keepdims=True)
        e = jnp.exp(m2 - m1)
        g1 = 1.0 / (1.0 + e)
        g2 = e / (1.0 + e)
        info = jnp.where(lane == 0, i1.astype(F32),
                         jnp.where(lane == 1, i2.astype(F32),
                                   jnp.where(lane == 2, g1, jnp.where(lane == 3, g2, 0.0))))
        info_ref[...] = info


def _outproj(x, att, rw, mod, g, w_out_b, B, S, l, router=None):
    T, D = x.shape
    tm = _tile(S, ROW_TILE)
    nS = S // tm
    moe = router is not None
    row = lambda b, i: (b * nS + i, 0)
    full = lambda b, i: (0, 0)
    in_specs = [pl.BlockSpec((tm, D), row), pl.BlockSpec((tm, ATT_WIDTH), row),
                pl.BlockSpec((tm, RWKV_WIDTH), row),
                pl.BlockSpec((1, 6, D), lambda b, i: (l * B + b, 0, 0)),
                pl.BlockSpec((1, D), full),
                pl.BlockSpec((1, D, D), lambda b, i: (l, 0, 0))]
    args = [x, att, rw, mod, g, w_out_b]
    if moe:
        out_specs = [pl.BlockSpec((tm, D), row), pl.BlockSpec((tm * TOKEN_TILE[0], LANES), row)]
        out_shape = [jax.ShapeDtypeStruct((T, D), F32), jax.ShapeDtypeStruct((T * TOKEN_TILE[0], LANES), F32)]
    else:
        out_specs = [pl.BlockSpec((tm, D), row), pl.BlockSpec((tm, D), row)]
        out_shape = [jax.ShapeDtypeStruct((T, D), F32), jax.ShapeDtypeStruct((T, D), BF16)]
    if moe:
        in_specs += [pl.BlockSpec((D, LANES), full), pl.BlockSpec((1, LANES), full)]
        args += list(router)
        out_specs.append(pl.BlockSpec((tm, LANES), row))
        out_shape.append(jax.ShapeDtypeStruct((T, LANES), F32))
    return pl.pallas_call(
        functools.partial(_outproj_kernel, moe=moe),
        grid=(B, nS),
        in_specs=in_specs,
        out_specs=out_specs,
        out_shape=out_shape,
        compiler_params=_cparams(("arbitrary", "arbitrary")),
        name="outproj",
    )(*args)


def _ffn_kernel(x_ref, h_ref, mod_ref, wg_ref, wu_ref, wd_ref, o_ref, acc_ref):
    f = pl.program_id(1)

    @pl.when(f == 0)
    def _():
        acc_ref[...] = jnp.zeros_like(acc_ref)

    hb = h_ref[...]
    gate = jnp.dot(hb, wg_ref[0], preferred_element_type=F32)
    up = jnp.dot(hb, wu_ref[0], preferred_element_type=F32)
    mid = (_silu(gate) * up).astype(BF16)
    acc_ref[...] += jnp.dot(mid, wd_ref[0], preferred_element_type=F32)

    @pl.when(f == pl.num_programs(1) - 1)
    def _():
        o_ref[...] = x_ref[...] + mod_ref[0, 5:6, :] * acc_ref[...]


def _ffn_dense(x1, h, mod, wg, wu, wd, B, S, l, li):
    T, D = x1.shape
    F = wg.shape[-1]
    tm = _tile(S, FFN_ROW_TILE)
    tf = _tile(F, FFN_COL_TILE)
    per_seq = S // tm
    return pl.pallas_call(
        _ffn_kernel,
        grid=(T // tm, F // tf),
        in_specs=[pl.BlockSpec((tm, D), lambda i, f: (i, 0)),
                  pl.BlockSpec((tm, D), lambda i, f: (i, 0)),
                  pl.BlockSpec((1, 6, D), lambda i, f: (l * B + i // per_seq, 0, 0)),
                  pl.BlockSpec((1, D, tf), lambda i, f: (li, 0, f)),
                  pl.BlockSpec((1, D, tf), lambda i, f: (li, 0, f)),
                  pl.BlockSpec((1, tf, D), lambda i, f: (li, f, 0))],
        out_specs=pl.BlockSpec((tm, D), lambda i, f: (i, 0)),
        out_shape=jax.ShapeDtypeStruct((T, D), F32),
        scratch_shapes=[pltpu.VMEM((tm, D), F32)],
        compiler_params=_cparams(("arbitrary", "arbitrary")),
        name="ffn_dense",
    )(x1, h, mod, wg, wu, wd)


def _experts_kernel(te_ref, nu_ref, tok_ref, h_ref, wg_ref, wu_ref, wd_ref, y_ref,
                    land_ref, xb_ref, acc_ref, sem):
    i = pl.program_id(0)
    f = pl.program_id(1)
    tm = xb_ref.shape[0]
    rows_per_token = TOKEN_TILE[0]
    n_used = nu_ref[0]

    def start_gather(tile):
        base = tile * tm

        def body(j, _):
            src = pl.multiple_of(tok_ref[base + j] * rows_per_token, rows_per_token)
            dst = pl.multiple_of(j * rows_per_token, rows_per_token)
            pltpu.make_async_copy(h_ref.at[pl.ds(src, rows_per_token)],
                                  land_ref.at[pl.ds(dst, rows_per_token)], sem).start()
            return 0

        lax.fori_loop(0, tm, body, 0, unroll=DMA_ISSUE_UNROLL)

    @pl.when(i < n_used)
    def _():
        @pl.when(f == 0)
        def _():
            @pl.when(i == 0)
            def _():
                start_gather(0)

            pltpu.make_async_copy(h_ref.at[pl.ds(0, tm * rows_per_token)], land_ref, sem).wait()
            xb_ref[...] = _load_token_tiles(land_ref, tm).astype(BF16)
            acc_ref[...] = jnp.zeros_like(acc_ref)

            @pl.when(i + 1 < n_used)
            def _():
                start_gather(i + 1)

        xb = xb_ref[...]
        gate = jnp.dot(xb, wg_ref[0, 0].astype(BF16), preferred_element_type=F32)
        up = jnp.dot(xb, wu_ref[0, 0].astype(BF16), preferred_element_type=F32)
        mid = (_silu(gate) * up).astype(BF16)
        acc_ref[...] += jnp.dot(mid, wd_ref[0, 0].astype(BF16), preferred_element_type=F32)

        @pl.when(f == pl.num_programs(1) - 1)
        def _():
            _store_token_tiles(y_ref, acc_ref[...])

    @pl.when((i >= n_used) & (f == 0))
    def _():
        y_ref[...] = jnp.zeros_like(y_ref)


def _experts(h, row_tok, tile_e, n_used, wg, wu, wd, li, tm):
    n_rows = row_tok.shape[0]
    D = wg.shape[-2]
    rpt = TOKEN_TILE[0]
    assert h.shape[0] >= tm * rpt
    F = wg.shape[-1]
    tf = _tile(F, EXPERT_COL_TILE)
    nf = F // tf

    def rows(i, f, te, nu, tok):
        return (i, 0)

    def col_block(i, f, nu):
        t = jnp.minimum(i, nu[0] - 1)
        step = jnp.where(i < nu[0], f, nf - 1)
        return jnp.where(t % 2 == 0, step, nf - 1 - step)

    def wcol(i, f, te, nu, tok):
        return (li, te[jnp.minimum(i, nu[0] - 1)], 0, col_block(i, f, nu))

    def wrow(i, f, te, nu, tok):
        return (li, te[jnp.minimum(i, nu[0] - 1)], col_block(i, f, nu), 0)

    return pl.pallas_call(
        _experts_kernel,
        grid_spec=pltpu.PrefetchScalarGridSpec(
            num_scalar_prefetch=3,
            grid=(n_rows // tm, nf),
            in_specs=[pl.BlockSpec(memory_space=pl.ANY),
                      pl.BlockSpec((1, 1, D, tf), wcol),
                      pl.BlockSpec((1, 1, D, tf), wcol),
                      pl.BlockSpec((1, 1, tf, D), wrow)],
            out_specs=pl.BlockSpec((tm * rpt, LANES), rows),
            scratch_shapes=[pltpu.VMEM((tm * rpt, LANES), F32), pltpu.VMEM((tm, D), BF16),
                            pltpu.VMEM((tm, D), F32), pltpu.SemaphoreType.DMA(())]),
        out_shape=jax.ShapeDtypeStruct((n_rows * rpt, LANES), F32),
        compiler_params=_cparams(("arbitrary", "arbitrary")),
        name="moe_experts",
    )(tile_e, n_used, row_tok, h, wg, wu, wd)


def _combine_kernel(pos_ref, y_ref, x_ref, info_ref, mod_ref, fg_ref, o_ref, buf, sem, *, tc, final):
    i = pl.program_id(0)
    slot = i % 2
    rows_per_token = TOKEN_TILE[0]

    def start_gather(tile, dst_slot):
        base = tile * (2 * tc)

        def body(j, _):
            src = pl.multiple_of(pos_ref[base + j] * rows_per_token, rows_per_token)
            dst = pl.multiple_of(j * rows_per_token, rows_per_token)
            pltpu.make_async_copy(y_ref.at[pl.ds(src, rows_per_token)],
                                  buf.at[dst_slot, pl.ds(dst, rows_per_token)], sem.at[dst_slot]).start()
            return 0

        lax.fori_loop(0, 2 * tc, body, 0, unroll=DMA_ISSUE_UNROLL)

    @pl.when(i == 0)
    def _():
        start_gather(0, 0)

    @pl.when(i + 1 < pl.num_programs(0))
    def _():
        start_gather(i + 1, 1 - slot)

    pltpu.make_async_copy(y_ref.at[pl.ds(0, 2 * tc * rows_per_token)], buf.at[slot], sem.at[slot]).wait()
    info = info_ref[...]
    ff = (info[:, 2:3] * _load_token_tiles(buf.at[slot], tc)
          + info[:, 3:4] * _load_token_tiles(buf.at[slot], tc, tc * rows_per_token))
    out = x_ref[...] + mod_ref[0, 5:6, :] * ff
    if final:
        ms = jnp.mean(out * out, axis=-1, keepdims=True)
        out = out * lax.rsqrt(ms + RMS_EPS) * fg_ref[...]
    o_ref[...] = out


def _combine(pos, y_rows, x1, info, mod, final_g, B, S, l, final):
    T, D = x1.shape
    tc = _tile(S, COMBINE_ROW_TILE)
    per_seq = S // tc
    return pl.pallas_call(
        functools.partial(_combine_kernel, tc=tc, final=final),
        grid_spec=pltpu.PrefetchScalarGridSpec(
            num_scalar_prefetch=1,
            grid=(T // tc,),
            in_specs=[pl.BlockSpec(memory_space=pl.ANY),
                      pl.BlockSpec((tc, D), lambda i, pos: (i, 0)),
                      pl.BlockSpec((tc, LANES), lambda i, pos: (i, 0)),
                      pl.BlockSpec((1, 6, D), lambda i, pos: (l * B + i // per_seq, 0, 0)),
                      pl.BlockSpec((1, D), lambda i, pos: (0, 0))],
            out_specs=pl.BlockSpec((tc, D), lambda i, pos: (i, 0)),
            scratch_shapes=[pltpu.VMEM((2, 2 * tc * TOKEN_TILE[0], LANES), F32), pltpu.SemaphoreType.DMA((2,))]),
        out_shape=jax.ShapeDtypeStruct((T, D), F32),
        compiler_params=_cparams(("arbitrary",)),
        name="moe_combine",
    )(pos, y_rows, x1, info, mod, final_g)


def _moe(x1, h, info, mod, wg, wu, wd, final_g, B, S, l, li, final):
    T, D = x1.shape
    TK = T * TOP_K
    tm = _tile(TK // N_EXPERTS, EXPERT_ROW_TILE)
    n_rows = TK + (N_EXPERTS - 1) * tm
    flat_e = info[:, :TOP_K].astype(jnp.int32).reshape(TK)
    onehot = (flat_e[:, None] == jnp.arange(N_EXPERTS, dtype=jnp.int32)[None, :]).astype(jnp.int32)
    csum = jnp.cumsum(onehot, axis=0)
    counts = csum[-1]
    rank = jnp.sum(onehot * csum, axis=1) - 1
    padded = (counts + tm - 1) // tm * tm
    pad_ends = jnp.cumsum(padded)
    pad_starts = pad_ends - padded
    dest = pad_starts[flat_e] + rank
    row_tok = jnp.zeros((n_rows,), jnp.int32).at[dest].set(jnp.arange(TK, dtype=jnp.int32) // TOP_K)
    tile_start = jnp.arange(n_rows // tm, dtype=jnp.int32) * tm
    tile_e = jnp.minimum(jnp.searchsorted(pad_ends, tile_start, side='right'), N_EXPERTS - 1).astype(jnp.int32)
    n_used_rows = pad_ends[-1:].astype(jnp.int32)
    n_used_tiles = n_used_rows // tm
    tc = _tile(S, COMBINE_ROW_TILE)
    pos = dest.reshape(T // tc, tc, TOP_K).transpose(0, 2, 1).reshape(TK).astype(jnp.int32)

    y_rows = _experts(h, row_tok, tile_e, n_used_tiles, wg, wu, wd, li, tm)
    return _combine(pos, y_rows, x1, info, mod, final_g, B, S, l, final)


def kernel(x, c, w_ada, b_ada, norm1_g, norm2_g, final_g, w_in, w_out, rel_bias, rwkv_mu, rwkv_w0, rwkv_w2, rwkv_a0, rwkv_a2, rwkv_g2, rwkv_k_k, rwkv_k_a, rwkv_r_k, rwkv_ln_w, rwkv_ln_b, rwkv_v0, rwkv_v1, rwkv_v2, ffn_w_gate, ffn_w_up, ffn_w_down, moe_router_w, moe_router_b, moe_w_gate, moe_w_up, moe_w_down):
    B, S, D = x.shape
    L = w_in.shape[0]
    T = B * S
    W = RWKV_WIDTH

    mod = _ada_mod(c, w_ada, b_ada).reshape(L * B, 6, D)
    w_in_b = jnp.pad(w_in, ((0, 0), (0, 0), (0, N_IN_PAD - N_IN))).astype(BF16)
    w_out_b = w_out.astype(BF16)
    ffn_w = [w.astype(BF16) for w in (ffn_w_gate, ffn_w_up, ffn_w_down)]
    moe_w = [moe_w_gate, moe_w_up, moe_w_down]
    mu_p = jnp.pad(rwkv_mu, ((0, 0), (0, RWM_COLS + RWL_COLS - rwkv_mu.shape[1])))
    att_tables = _att_tables(rel_bias, S)
    zeros_w = jnp.zeros((1, W), F32)
    final_g2 = final_g.reshape(1, D)

    xf = x.reshape(T, D)
    v_first = None
    for l in range(L):
        qkv, qkv_mid, qkv_far, rwm, rwl = _inproj(xf, mod, norm1_g[l].reshape(1, D), w_in_b,
                                                   mu_p[l].reshape(1, -1), B, S, l)
        att = _attention(qkv, qkv_mid, qkv_far, att_tables, B, S)
        v0 = rwkv_v0[l - 1].reshape(1, W) if l > 0 else zeros_w
        vec = jnp.concatenate([rwkv_w0[l].reshape(1, W), rwkv_a0[l].reshape(1, W), rwkv_k_k[l].reshape(1, W),
                               rwkv_k_a[l].reshape(1, W), rwkv_r_k[l].reshape(1, W), rwkv_ln_w[l].reshape(1, W),
                               rwkv_ln_b[l].reshape(1, W), v0], axis=0)
        w2p = jnp.pad(rwkv_w2[l], ((0, LANES - D_DECAY_LORA), (0, 0)))
        a2p = jnp.pad(rwkv_a2[l], ((D_DECAY_LORA, 0), (0, 0)))
        g2p = jnp.pad(rwkv_g2[l], ((0, RWL_COLS - LANES - D_GATE_LORA), (0, 0)))
        if l > 0:
            v1p = jnp.pad(rwkv_v1[l - 1], ((0, 0), (0, LANES - D_MV_LORA)))
            v2p = jnp.pad(rwkv_v2[l - 1], ((0, LANES - D_MV_LORA), (0, 0)))
        else:
            v1p = v2p = None
        rw, v_first = _rwkv(rwm, rwl, v_first, vec, w2p, a2p, g2p, v1p, v2p, B, S)
        li = l // 2
        if l % 2 == 0:
            x1, h = _outproj(xf, att, rw, mod, norm2_g[l].reshape(1, D), w_out_b, B, S, l)
            xf = _ffn_dense(x1, h, mod, *ffn_w, B, S, l, li)
        else:
            router = (jnp.pad(moe_router_w[li], ((0, 0), (0, LANES - N_EXPERTS))),
                      jnp.pad(moe_router_b[li], (0, LANES - N_EXPERTS)).reshape(1, LANES))
            x1, h, info = _outproj(xf, att, rw, mod, norm2_g[l].reshape(1, D), w_out_b, B, S, l, router)
            xf = _moe(x1, h, info, mod, *moe_w, final_g2, B, S, l, li,
                      final=(l == L - 1))
    if L % 2 == 1:
        raise NotImplementedError("final norm is fused into the last (expert) layer")
    return xf.reshape(B, S, D)
```

```python
import functools
import math

import numpy as np
import jax
import jax.numpy as jnp
from jax import lax
from jax.experimental import pallas as pl
from jax.experimental.pallas import tpu as pltpu

F32 = jnp.float32
BF16 = jnp.bfloat16

HEAD_DIM = 64
ATT_WIDTH = 512
RWKV_WIDTH = 512
HEAD_PAIRS = 4
LANES = 128
DILATED_GROUPS = ((128, 1), (512, 4), (2048, 16))
N_BUCKETS = 32
MAX_DISTANCE = 2048
NEG_INF = -1e30
D_DECAY_LORA = 64
D_AAA_LORA = 64
D_MV_LORA = 32
D_GATE_LORA = 160
N_IN = 3 * ATT_WIDTH + 3 * RWKV_WIDTH + D_DECAY_LORA + D_AAA_LORA + D_GATE_LORA
QKV_COLS = 3 * ATT_WIDTH
RWM_COLS = 3 * RWKV_WIDTH
RWL_COLS = 384
N_IN_PAD = QKV_COLS + RWM_COLS + RWL_COLS
GN_EPS = HEAD_DIM * 1e-5
RMS_EPS = 1e-6
N_EXPERTS = 8
TOP_K = 2
CHUNK = 64
VMEM_LIMIT = 56 * 1024 * 1024
LOG2_E = math.log2(math.e)
Q_SCALE = LOG2_E / math.sqrt(HEAD_DIM)
ROW_TILE = 512
FFN_ROW_TILE = 256
FFN_COL_TILE = 2816
EXPERT_ROW_TILE = 512
EXPERT_COL_TILE = 1792
COMBINE_ROW_TILE = 256
DMA_ISSUE_UNROLL = 8


def _cparams(sem):
    return pltpu.CompilerParams(dimension_semantics=sem, vmem_limit_bytes=VMEM_LIMIT)


def _tile(n, pref):
    t = min(n, pref)
    while n % t:
        t //= 2
    return t


def _mm(a, b):
    return jnp.dot(a.astype(BF16), b.astype(BF16), preferred_element_type=F32)


def _mm_nt(a, b):
    return lax.dot_general(a.astype(BF16), b.astype(BF16), (((1,), (1,)), ((), ())),
                           preferred_element_type=F32)


def _sigmoid(x):
    return 1.0 / (1.0 + jnp.exp(-x))


def _silu(x):
    return x * _sigmoid(x)


def _ada_kernel(c_ref, w_ref, b_ref, o_ref):
    ca = _silu(c_ref[...])
    o_ref[0] = _mm(ca, w_ref[0]) + b_ref[0]


def _ada_mod(c, w_ada, b_ada):
    L, D, N = w_ada.shape
    B = c.shape[0]
    tn = _tile(N, 1536)
    return pl.pallas_call(
        _ada_kernel,
        grid=(L, N // tn),
        in_specs=[pl.BlockSpec((B, D), lambda l, j: (0, 0)),
                  pl.BlockSpec((1, D, tn), lambda l, j: (l, 0, j)),
                  pl.BlockSpec((1, 1, tn), lambda l, j: (l, 0, j))],
        out_specs=pl.BlockSpec((1, B, tn), lambda l, j: (l, 0, j)),
        out_shape=jax.ShapeDtypeStruct((L, B, N), F32),
        compiler_params=_cparams(("arbitrary", "arbitrary")),
        name="ada_mod",
    )(c, w_ada, b_ada.reshape(L, 1, N))


def _rms_mod(xf, g, scale, shift):
    ms = jnp.mean(xf * xf, axis=-1, keepdims=True)
    return (xf * lax.rsqrt(ms + RMS_EPS) * g) * (1.0 + scale) + shift


def _inproj_kernel(x_ref, mod_ref, g_ref, w_ref, mu_ref, qkv_ref, qkv_mid_ref, qkv_far_ref, rwm_ref, rwl_ref,
                   carry_ref, qkv_scr):
    i = pl.program_id(1)
    tm = x_ref.shape[0]
    h = _rms_mod(x_ref[...], g_ref[...], mod_ref[0, 1:2, :], mod_ref[0, 0:1, :])
    acc = jnp.dot(h.astype(BF16), w_ref[0], preferred_element_type=F32)
    for c in range(QKV_COLS // LANES):
        cols = slice(c * LANES, (c + 1) * LANES)
        blk = acc[:, cols] * Q_SCALE if (c + 1) * LANES <= ATT_WIDTH else acc[:, cols]
        qkv_ref[:, cols] = blk.astype(BF16)
        qkv_scr[c] = blk
        for out_ref in (qkv_mid_ref, qkv_far_ref):
            d, rows = out_ref.shape[0], out_ref.shape[1]
            for r in range(d):
                out_ref[r, :, cols] = qkv_scr[c, pl.ds(r, rows, stride=d), :].astype(BF16)
    p = acc[:, QKV_COLS:]
    first = jnp.where(i == 0, 0.0, carry_ref[...])
    row = lax.broadcasted_iota(jnp.int32, p.shape, 0)
    prev = jnp.where(row == 0, first, pltpu.roll(p, 1, 0))
    carry_ref[...] = p[tm - 1:tm, :]
    pm = p + mu_ref[...] * (prev - p)
    rwm_ref[...] = pm[:, :RWM_COLS]
    rwl_ref[...] = pm[:, RWM_COLS:]


def _inproj(x, mod, g, w_in_l, mu_l, B, S, l):
    T, D = x.shape
    tm = _tile(S, ROW_TILE)
    nS = S // tm
    d2, d3 = DILATED_GROUPS[1][1], DILATED_GROUPS[2][1]
    return pl.pallas_call(
        _inproj_kernel,
        grid=(B, nS),
        in_specs=[pl.BlockSpec((tm, D), lambda b, i: (b * nS + i, 0)),
                  pl.BlockSpec((1, 6, D), lambda b, i: (l * B + b, 0, 0)),
                  pl.BlockSpec((1, D), lambda b, i: (0, 0)),
                  pl.BlockSpec((1, D, N_IN_PAD), lambda b, i: (l, 0, 0)),
                  pl.BlockSpec((1, RWM_COLS + RWL_COLS), lambda b, i: (0, 0))],
        out_specs=[pl.BlockSpec((tm, QKV_COLS), lambda b, i: (b * nS + i, 0)),
                   pl.BlockSpec((None, d2, tm // d2, QKV_COLS), lambda b, i: (b, 0, i, 0)),
                   pl.BlockSpec((None, d3, tm // d3, QKV_COLS), lambda b, i: (b, 0, i, 0)),
                   pl.BlockSpec((tm, RWM_COLS), lambda b, i: (b * nS + i, 0)),
                   pl.BlockSpec((tm, RWL_COLS), lambda b, i: (b * nS + i, 0))],
        out_shape=[jax.ShapeDtypeStruct((T, QKV_COLS), BF16),
                   jax.ShapeDtypeStruct((B, d2, S // d2, QKV_COLS), BF16),
                   jax.ShapeDtypeStruct((B, d3, S // d3, QKV_COLS), BF16),
                   jax.ShapeDtypeStruct((T, RWM_COLS), F32),
                   jax.ShapeDtypeStruct((T, RWL_COLS), F32)],
        scratch_shapes=[pltpu.VMEM((1, RWM_COLS + RWL_COLS), F32), pltpu.VMEM((QKV_COLS // LANES, tm, LANES), F32)],
        compiler_params=_cparams(("arbitrary", "arbitrary")),
        name="inproj",
    )(x, mod, g, w_in_l, mu_l)


def _t5_bucket(n):
    max_exact = N_BUCKETS // 2
    large = max_exact + (np.log(np.maximum(n, 1) / max_exact) / np.log(MAX_DISTANCE / max_exact)
                         * (N_BUCKETS - max_exact)).astype(np.int32)
    large = np.minimum(large, N_BUCKETS - 1)
    return np.where(n < max_exact, n, large).astype(np.int32)


ATT_BLOCK = 128
ATT_KEYS = 256


def _toeplitz(vec, c0, rows, cols):
    H, n = vec.shape
    period = rows + cols - 1
    w = jnp.concatenate([jnp.full((H, cols - 1), NEG_INF, F32), vec, jnp.full((H, rows), NEG_INF, F32)], axis=1)
    rw = w[:, ::-1]
    a = w.shape[1] - 1 - (c0 + cols - 1)
    z = jnp.concatenate([rw[:, a:a + cols], rw[:, a - (rows - 1):a]], axis=1)
    flat = jnp.tile(z, (1, rows))[:, :rows * (period - 1)]
    return flat.reshape(H, rows, period - 1)[..., :cols]


def _att_tables(rel_bias, S):
    (w1, d1), (w2, d2), (w3, d3) = DILATED_GROUPS
    assert d1 == 1 and w1 == ATT_KEYS - ATT_BLOCK and w2 // d2 == w1 and S % (d3 * 8) == 0
    off = np.arange(S)
    mult = np.zeros(S, np.int64)
    for w, d in DILATED_GROUPS:
        mult += ((off % d == 0) & (off <= w)).astype(np.int64)
    logm = np.where(mult > 0, np.log(np.maximum(mult, 1)), NEG_INF).astype(np.float32)
    per_off = (rel_bias[_t5_bucket(off)].T.astype(F32) + jnp.asarray(logm)[None, :]) * LOG2_E
    u2 = np.arange(S // d2)
    u3 = np.arange(S // d3)
    band = jnp.where(jnp.asarray(off <= w1)[None], per_off, NEG_INF)
    mid = jnp.where(jnp.asarray((u2 > w1 // d2) & (u2 <= w2 // d2))[None], per_off[:, ::d2], NEG_INF)
    far = jnp.where(jnp.asarray((u3 > w2 // d3) & (u3 <= w3 // d3))[None], per_off[:, ::d3], NEG_INF)
    shift = ATT_KEYS - ATT_BLOCK
    t_band = jnp.stack([_toeplitz(band, 0, ATT_BLOCK, ATT_KEYS), _toeplitz(band, shift, ATT_BLOCK, ATT_KEYS)], axis=1)
    t_mid = jnp.stack([_toeplitz(mid, 0, ATT_BLOCK, ATT_KEYS), _toeplitz(mid, shift, ATT_BLOCK, ATT_KEYS)], axis=1)
    t_far = _toeplitz(far, 0, S // d3, S // d3)
    return t_band, t_mid, t_far


def _attend(units):
    idx = range(len(units))
    heads = range(2)
    head0 = [lax.broadcasted_iota(jnp.int32, u[0].shape, 1) < HEAD_DIM for u in units]
    qh = [[jnp.where(head0[i], units[i][0], jnp.zeros_like(units[i][0])),
           jnp.where(head0[i], jnp.zeros_like(units[i][0]), units[i][0])] for i in idx]
    s = [[lax.dot_general(qh[i][h], units[i][1], (((1,), (1,)), ((), ())), preferred_element_type=F32)
          + units[i][3 + h] for h in heads] for i in idx]
    mx = [[jnp.max(s[i][h], axis=-1, keepdims=True) for h in heads] for i in idx]
    p = [[jnp.exp2(s[i][h] - mx[i][h]) for h in heads] for i in idx]
    ps = [[jnp.sum(p[i][h], axis=-1, keepdims=True) for h in heads] for i in idx]
    pv = [[jnp.dot(p[i][h].astype(BF16), units[i][2], preferred_element_type=F32) for h in heads] for i in idx]
    out = []
    for i in idx:
        shape = units[i][0].shape
        out.append((jnp.where(head0[i], jnp.broadcast_to(mx[i][0], shape), jnp.broadcast_to(mx[i][1], shape)),
                    jnp.where(head0[i], jnp.broadcast_to(ps[i][0], shape), jnp.broadcast_to(ps[i][1], shape)),
                    jnp.where(head0[i], pv[i][0], pv[i][1])))
    return out


def _attn_kernel(qa_ref, ka_ref, va_ref, qm_ref, km_ref, vm_ref, qf_ref, kf_ref, vf_ref,
                 tband_ref, tmid_ref, tfar_ref, o_ref, part_ref):
    R, K = ATT_BLOCK, ATT_KEYS
    S = qa_ref.shape[0]
    n_mid_res, n_mid = qm_ref.shape[0], qm_ref.shape[1]
    n_far_res, n_far = qf_ref.shape[0], qf_ref.shape[1]
    group = 16 if S % (16 * R) == 0 else 8

    def window(blk):
        start = pl.multiple_of(jnp.maximum(blk * R - (K - R), 0), R)
        return pl.ds(pl.multiple_of(blk * R, R), R), pl.ds(start, K), jnp.minimum(blk, 1)

    def park(cls, rows, results):
        for nat, parts in zip(rows, results):
            for j in range(3):
                part_ref[cls, j, nat, :] = parts[j]

    def band_body(it, _):
        units, rows = [], []
        for g in range(group):
            qrows, krows, var = window(it * group + g)
            rows.append(qrows)
            units.append((qa_ref[qrows, :], ka_ref[krows, :], va_ref[krows, :],
                          tband_ref[0, var], tband_ref[1, var]))
        park(0, rows, _attend(units))
        return 0

    lax.fori_loop(0, S // (R * group), band_body, 0)

    blocks_per_step = group // n_mid_res

    def mid_body(it, _):
        units, rows = [], []
        for g in range(blocks_per_step):
            blk = it * blocks_per_step + g
            qrows, krows, var = window(blk)
            for r in range(n_mid_res):
                rows.append(pl.ds(blk * (R * n_mid_res) + r, R, stride=n_mid_res))
                units.append((qm_ref[r, qrows, :], km_ref[r, krows, :], vm_ref[r, krows, :],
                              tmid_ref[0, var], tmid_ref[1, var]))
        park(1, rows, _attend(units))
        return 0

    lax.fori_loop(0, n_mid // (R * blocks_per_step), mid_body, 0)

    def far_body(it, _):
        units, rows = [], []
        for g in range(group):
            r = it * group + g
            rows.append(pl.ds(r, n_far, stride=n_far_res))
            units.append((qf_ref[r], kf_ref[r], vf_ref[r], tfar_ref[0], tfar_ref[1]))
        park(2, rows, _attend(units))
        return 0

    lax.fori_loop(0, n_far_res // group, far_body, 0)

    def merge_body(i, _):
        rows = pl.ds(pl.multiple_of(i * R, R), R)
        m = [part_ref[c, 0, rows, :] for c in range(3)]
        top = jnp.maximum(jnp.maximum(m[0], m[1]), m[2])
        w = [jnp.exp2(m[c] - top) for c in range(3)]
        total = sum(w[c] * part_ref[c, 1, rows, :] for c in range(3))
        acc = sum(w[c] * part_ref[c, 2, rows, :] for c in range(3))
        o_ref[rows, :] = (acc / total).astype(o_ref.dtype)
        return 0

    lax.fori_loop(0, S // R, merge_body, 0)


def _attention(qkv, qkv_mid, qkv_far, tables, B, S):
    T = qkv.shape[0]
    t_band, t_mid, t_far = tables
    d2, d3 = DILATED_GROUPS[1][1], DILATED_GROUPS[2][1]
    assert S % (ATT_BLOCK * 8) == 0 and S // d2 >= ATT_KEYS
    nat = lambda c: pl.BlockSpec((S, LANES), lambda hp, b: (b, c * HEAD_PAIRS + hp))
    mid = lambda c: pl.BlockSpec((None, d2, S // d2, LANES), lambda hp, b: (b, 0, 0, c * HEAD_PAIRS + hp))
    far = lambda c: pl.BlockSpec((None, d3, S // d3, LANES), lambda hp, b: (b, 0, 0, c * HEAD_PAIRS + hp))
    return pl.pallas_call(
        _attn_kernel,
        grid=(HEAD_PAIRS, B),
        in_specs=[nat(0), nat(1), nat(2), mid(0), mid(1), mid(2), far(0), far(1), far(2),
                  pl.BlockSpec((2,) + t_band.shape[1:], lambda hp, b: (hp, 0, 0, 0)),
                  pl.BlockSpec((2,) + t_mid.shape[1:], lambda hp, b: (hp, 0, 0, 0)),
                  pl.BlockSpec((2,) + t_far.shape[1:], lambda hp, b: (hp, 0, 0))],
        out_specs=pl.BlockSpec((S, LANES), lambda hp, b: (b, hp)),
        out_shape=jax.ShapeDtypeStruct((T, ATT_WIDTH), BF16),
        scratch_shapes=[pltpu.VMEM((3, 3, S, LANES), F32)],
        compiler_params=_cparams(("arbitrary", "arbitrary")),
        name="dilated_attn",
    )(qkv, qkv, qkv, qkv_mid, qkv_mid, qkv_mid, qkv_far, qkv_far, qkv_far, t_band, t_mid, t_far)


def _split2(x):
    hi = x.astype(BF16)
    lo = (x - hi.astype(F32)).astype(BF16)
    return hi, lo


def _head_sum(x, m2):
    hi, lo = _split2(x)
    return jnp.dot(jnp.concatenate([hi, lo], axis=1), m2, preferred_element_type=F32)


def _rwkv_kernel(*refs, has_vres, n_chunks, n_batch):
    if has_vres:
        (rwm_ref, rwl_ref, vf_ref, vec_ref, w2_ref, a2_ref, g2_ref, v1_ref, v2_ref,
         o_ref, state_ref, r_s, k_s, v_s, lw_s, kk_s, kb_s, y_s,
         ar_s, inv_s, bk_s, nrbk_s, akv_s, dec_s) = refs
    else:
        (rwm_ref, rwl_ref, vec_ref, w2_ref, a2_ref, g2_ref,
         o_ref, vf_out_ref, state_ref, r_s, k_s, v_s, lw_s, kk_s, kb_s, y_s,
         ar_s, inv_s, bk_s, nrbk_s, akv_s, dec_s) = refs
    C = CHUNK
    W = RWKV_WIDTH
    rows_per_batch = rwm_ref.shape[1]

    def stacked(ref, cols=slice(None)):
        return jnp.concatenate([ref[b, :, cols] for b in range(n_batch)], axis=0)

    def unstack(ref, x, cols=slice(None)):
        for b in range(n_batch):
            ref[b, :, cols] = x[b * rows_per_batch:(b + 1) * rows_per_batch].astype(ref.dtype)

    @pl.when(pl.program_id(1) == 0)
    def _():
        state_ref[...] = jnp.zeros_like(state_ref)

    ri = lax.broadcasted_iota(jnp.int32, (2 * LANES, LANES), 0)
    ci = lax.broadcasted_iota(jnp.int32, (2 * LANES, LANES), 1)
    m2 = ((ri % LANES) // HEAD_DIM == ci // HEAD_DIM).astype(BF16)

    vec = vec_ref[...]
    w0, a0, k_k, k_a, r_k, ln_w, ln_b, v0 = [vec[i:i + 1, :] for i in range(8)]

    r = stacked(rwm_ref, slice(0, W))
    k = stacked(rwm_ref, slice(W, 2 * W))
    v = stacked(rwm_ref, slice(2 * W, 3 * W))
    lora = stacked(rwl_ref, slice(0, LANES))
    w_raw = w0 + _mm(jnp.tanh(lora), w2_ref[...])
    lw_s[...] = -math.exp(-0.5) * _sigmoid(w_raw)
    a = _sigmoid(a0 + _mm(lora, a2_ref[...]))
    g = _mm(_sigmoid(stacked(rwl_ref, slice(LANES, RWL_COLS))), g2_ref[...])
    if has_vres:
        mix = _sigmoid(v0 + _mm(_mm(v, v1_ref[...]), v2_ref[...]))
        v = v + (stacked(vf_ref) - v) * mix
    else:
        unstack(vf_out_ref, v)
    kk = k * k_k
    for p in range(HEAD_PAIRS):
        ls = slice(p * LANES, (p + 1) * LANES)
        kkp = kk[:, ls]
        kkp = kkp * jnp.minimum(lax.rsqrt(_head_sum(kkp * kkp, m2)), 1e12)
        kk_s[:, ls] = kkp
        kb_s[:, ls] = kkp * a[:, ls]
    kmod = k * (1.0 + (a - 1.0) * k_a)
    r_s[...] = r
    k_s[...] = kmod
    v_s[...] = v

    ri = lax.broadcasted_iota(jnp.int32, (LANES, LANES), 0)
    ci = lax.broadcasted_iota(jnp.int32, (LANES, LANES), 1)
    same_head = (ri // C) == (ci // C)
    strict = same_head & ((ri % C) > (ci % C))
    incl = same_head & ((ri % C) >= (ci % C))
    eye = (ri == ci).astype(F32)
    lane = lax.broadcasted_iota(jnp.int32, (C, LANES), 1)
    head0 = lane < HEAD_DIM
    tri_r = lax.broadcasted_iota(jnp.int32, (C, 3 * C), 0)
    tri_c = lax.broadcasted_iota(jnp.int32, (C, 3 * C), 1)
    tri3 = ((tri_c % C) <= tri_r).astype(BF16)

    lane_slices = [slice(p * LANES, (p + 1) * LANES) for p in range(HEAD_PAIRS)]

    def cumsum(x):
        hi = x.astype(BF16)
        rem = x - hi.astype(F32)
        mid = rem.astype(BF16)
        lo = (rem - mid.astype(F32)).astype(BF16)
        return jnp.dot(tri3, jnp.concatenate([hi, mid, lo], axis=0), preferred_element_type=F32)

    def per_head_rows(x, y):
        zero = jnp.zeros_like(x)
        return jnp.concatenate([jnp.where(head0, x, zero), jnp.where(head0, zero, x),
                                jnp.where(head0, y, zero), jnp.where(head0, zero, y)], axis=0)

    def intra_body(it, _):
        units = [(it * chunk_group + g, p) for g in range(chunk_group) for p in range(HEAD_PAIRS)]
        idx = range(len(units))
        rows = [pl.ds(pl.multiple_of(c * C, C), C) for c, _ in units]
        ls = [lane_slices[p] for _, p in units]
        rc = [r_s[rows[u], ls[u]] for u in idx]
        kc = [k_s[rows[u], ls[u]] for u in idx]
        vc = [v_s[rows[u], ls[u]] for u in idx]
        lw = [lw_s[rows[u], ls[u]] for u in idx]
        kkc = [kk_s[rows[u], ls[u]] for u in idx]
        kbc = [kb_s[rows[u], ls[u]] for u in idx]
        cum = [cumsum(lw[u]) for u in idx]
        total = [cum[u][C - 1:C, :] for u in idx]
        g_inv = [jnp.exp(-cum[u]) for u in idx]
        a_t = [-kkc[u] * jnp.exp(cum[u] - lw[u]) for u in idx]
        r_t = [rc[u] * jnp.exp(cum[u]) for u in idx]
        b_t = [kbc[u] * g_inv[u] for u in idx]
        k_t = [kc[u] * g_inv[u] for u in idx]
        to_end = [jnp.exp(total[u] - cum[u]) for u in idx]
        gram = [_mm_nt(per_head_rows(a_t[u], r_t[u]), per_head_rows(b_t[u], k_t[u])) for u in idx]
        n_ab = [jnp.where(strict, gram[u][:LANES, :LANES], 0.0) for u in idx]
        n_ak = [jnp.where(strict, gram[u][:LANES, LANES:], 0.0) for u in idx]
        n_rb = [jnp.where(incl, gram[u][LANES:, :LANES], 0.0) for u in idx]
        n_rk = [jnp.where(incl, gram[u][LANES:, LANES:], 0.0) for u in idx]
        inv = [eye + n_ab[u] for u in idx]
        pw = [_mm(n_ab[u], n_ab[u]) for u in idx]
        n_rounds = int(math.log2(C)) - 1
        for i in range(n_rounds):
            if i + 1 < n_rounds:
                both = [_mm(pw[u], jnp.concatenate([inv[u], pw[u]], axis=1)) for u in idx]
                inv = [inv[u] + both[u][:, :LANES] for u in idx]
                pw = [both[u][:, LANES:] for u in idx]
            else:
                inv = [inv[u] + _mm(pw[u], inv[u]) for u in idx]
        for u, (c, p) in enumerate(units):
            ar_s[c, p] = jnp.concatenate([a_t[u], r_t[u]], axis=0).astype(BF16)
            inv_s[c, p] = inv[u].astype(BF16)
            akv_s[c, p] = _mm(n_ak[u], jnp.concatenate([vc[u], vc[u]], axis=0))
            nrbk_s[c, p] = jnp.concatenate([n_rb[u], n_rk[u]], axis=1).astype(BF16)
            bk_s[c, p] = jnp.concatenate([kbc[u] * to_end[u], kc[u] * to_end[u]], axis=0).astype(BF16)
            dec_s[c, p] = jnp.broadcast_to(jnp.exp(total[u]), (8, LANES))
        return 0

    chunks_per_batch = n_chunks // n_batch

    def state_body(step, _):
        units = [(b, b * chunks_per_batch + step, p) for b in range(n_batch) for p in range(HEAD_PAIRS)]
        idx = range(len(units))
        rows = [pl.ds(pl.multiple_of(c * C, C), C) for _, c, _ in units]
        ls = [lane_slices[p] for _, _, p in units]
        vc = [v_s[rows[u], ls[u]] for u in idx]
        st = [state_ref[b, p] for b, _, p in units]
        ah = [_mm_nt(ar_s[c, p], st[u]) for u, (_, c, p) in enumerate(units)]
        u_stack = [_mm(inv_s[c, p], jnp.concatenate([ah[u][:C], ah[u][:C]], axis=0) + akv_s[c, p])
                   for u, (_, c, p) in enumerate(units)]
        y_stack = [jnp.concatenate([ah[u][C:], ah[u][C:]], axis=0)
                   + _mm(nrbk_s[c, p], jnp.concatenate([u_stack[u], vc[u], vc[u]], axis=0))
                   for u, (_, c, p) in enumerate(units)]
        for u in idx:
            y_s[rows[u], ls[u]] = jnp.where(head0, y_stack[u][:C], y_stack[u][C:])
        uv_t = [jnp.concatenate([jnp.where(head0, u_stack[u][:C], u_stack[u][C:]), vc[u]], axis=0).T
                for u in idx]
        st_new = [st[u] * dec_s[c, p][0:1, :] + _mm(uv_t[u], bk_s[c, p]) for u, (_, c, p) in enumerate(units)]
        for u, (b, _, p) in enumerate(units):
            state_ref[b, p] = jnp.where(same_head, st_new[u], 0.0)
        return 0

    chunk_group = next(g for g in (8, 4, 2, 1) if n_chunks % g == 0)
    lax.fori_loop(0, n_chunks // chunk_group, intra_body, 0)
    lax.fori_loop(0, chunks_per_batch, state_body, 0)

    for p in range(HEAD_PAIRS):
        ls = slice(p * LANES, (p + 1) * LANES)
        y = y_s[:, ls]
        mean = _head_sum(y, m2) * (1.0 / HEAD_DIM)
        d = y - mean
        var = _head_sum(d * d, m2) * (1.0 / HEAD_DIM)
        yn = d * lax.rsqrt(var + GN_EPS) * ln_w[:, ls] + ln_b[:, ls]
        bonus = _head_sum(r_s[:, ls] * k_s[:, ls] * r_k[:, ls], m2) * v_s[:, ls]
        unstack(o_ref, (yn + bonus) * g[:, ls], ls)


def _rwkv(rwm, rwl, v_first, vec, w2p, a2p, g2p, v1p, v2p, B, S):
    T = rwm.shape[0]
    W = RWKV_WIDTH
    nb = next(n for n in (8, 4, 2, 1) if B % n == 0)
    tb = _tile(S, ROW_TILE // nb)
    n_chunks = nb * tb // CHUNK
    has_vres = v_first is not None
    stack = lambda a: a.reshape(B // nb, nb, S, a.shape[-1])
    row = lambda cols: pl.BlockSpec((None, nb, tb, cols), lambda b, i: (b, 0, i, 0))
    full = lambda b, i: (0, 0)
    in_specs = [row(RWM_COLS), row(RWL_COLS)]
    args = [stack(rwm), stack(rwl)]
    if has_vres:
        in_specs.append(row(W))
        args.append(stack(v_first))
    in_specs += [pl.BlockSpec(vec.shape, full), pl.BlockSpec(w2p.shape, full),
                 pl.BlockSpec(a2p.shape, full), pl.BlockSpec(g2p.shape, full)]
    args += [vec, w2p, a2p, g2p]
    if has_vres:
        in_specs += [pl.BlockSpec(v1p.shape, full), pl.BlockSpec(v2p.shape, full)]
        args += [v1p, v2p]
    out_specs = [row(W)]
    out_shape = [jax.ShapeDtypeStruct((B // nb, nb, S, W), BF16)]
    if not has_vres:
        out_specs.append(row(W))
        out_shape.append(jax.ShapeDtypeStruct((B // nb, nb, S, W), F32))
    outs = pl.pallas_call(
        functools.partial(_rwkv_kernel, has_vres=has_vres, n_chunks=n_chunks, n_batch=nb),
        grid=(B // nb, S // tb),
        in_specs=in_specs,
        out_specs=out_specs,
        out_shape=out_shape,
        scratch_shapes=([pltpu.VMEM((nb, HEAD_PAIRS, LANES, LANES), F32)] + [pltpu.VMEM((nb * tb, W), F32)] * 7
                        + [pltpu.VMEM((n_chunks, HEAD_PAIRS, 2 * CHUNK, LANES), BF16)] * 3
                        + [pltpu.VMEM((n_chunks, HEAD_PAIRS, 2 * CHUNK, 2 * LANES), BF16),
                           pltpu.VMEM((n_chunks, HEAD_PAIRS, 2 * CHUNK, LANES), F32),
                           pltpu.VMEM((n_chunks, HEAD_PAIRS, 8, LANES), F32)]),
        compiler_params=_cparams(("arbitrary", "arbitrary")),
        name="rwkv7",
    )(*args)
    outs = [o.reshape(T, W) for o in outs]
    return (outs[0], v_first) if has_vres else (outs[0], outs[1])


TOKEN_TILE = (8, LANES)


def _store_token_tiles(ref, x):
    n = x.shape[0]
    for s in range(TOKEN_TILE[0]):
        ref[pl.ds(s, n, stride=TOKEN_TILE[0]), :] = x[:, s * LANES:(s + 1) * LANES]


def _load_token_tiles(ref, n, start=0):
    return jnp.concatenate([ref[pl.ds(start + s, n, stride=TOKEN_TILE[0]), :] for s in range(TOKEN_TILE[0])],
                           axis=1)


def _outproj_kernel(*refs, moe):
    if moe:
        (x_ref, att_ref, rw_ref, mod_ref, g_ref, w_ref, rw_w_ref, rw_b_ref,
         x1_ref, h_ref, info_ref) = refs
    else:
        x_ref, att_ref, rw_ref, mod_ref, g_ref, w_ref, x1_ref, h_ref = refs
    mix = (jnp.dot(att_ref[...], w_ref[0, :ATT_WIDTH, :], preferred_element_type=F32)
           + jnp.dot(rw_ref[...], w_ref[0, ATT_WIDTH:, :], preferred_element_type=F32))
    x1 = x_ref[...] + mod_ref[0, 2:3, :] * mix
    x1_ref[...] = x1
    h = _rms_mod(x1, g_ref[...], mod_ref[0, 4:5, :], mod_ref[0, 3:4, :])
    if moe:
        _store_token_tiles(h_ref, h)
    else:
        h_ref[...] = h.astype(h_ref.dtype)
    if moe:
        h_hi, h_lo = _split2(h)
        w_hi, w_lo = _split2(rw_w_ref[...])
        logits = (jnp.dot(h_hi, w_hi, preferred_element_type=F32)
                  + jnp.dot(h_hi, w_lo, preferred_element_type=F32)
                  + jnp.dot(h_lo, w_hi, preferred_element_type=F32)) + rw_b_ref[...]
        lane = lax.broadcasted_iota(jnp.int32, logits.shape, 1)
        logits = jnp.where(lane < N_EXPERTS, logits, -jnp.inf)
        m1 = jnp.max(logits, axis=-1, keepdims=True)
        i1 = jnp.min(jnp.where(logits == m1, lane, LANES), axis=-1, keepdims=True)
        rest = jnp.where(lane == i1, -jnp.inf, logits)
        m2 = jnp.max(rest, axis=-1, keepdims=True)
        i2 = jnp.min(jnp.where(rest == m2, lane, LANES), axis=-1, keepdims=True)
        e = jnp.exp(m2 - m1)
        g1 = 1.0 / (1.0 + e)
        g2 = e / (1.0 + e)
        info = jnp.where(lane == 0, i1.astype(F32),
                         jnp.where(lane == 1, i2.astype(F32),
                                   jnp.where(lane == 2, g1, jnp.where(lane == 3, g2, 0.0))))
        info_ref[...] = info


def _outproj(x, att, rw, mod, g, w_out_b, B, S, l, router=None):
    T, D = x.shape
    tm = _tile(S, ROW_TILE)
    nS = S // tm
    moe = router is not None
    row = lambda b, i: (b * nS + i, 0)
    full = lambda b, i: (0, 0)
    in_specs = [pl.BlockSpec((tm, D), row), pl.BlockSpec((tm, ATT_WIDTH), row),
                pl.BlockSpec((tm, RWKV_WIDTH), row),
                pl.BlockSpec((1, 6, D), lambda b, i: (l * B + b, 0, 0)),
                pl.BlockSpec((1, D), full),
                pl.BlockSpec((1, D, D), lambda b, i: (l, 0, 0))]
    args = [x, att, rw, mod, g, w_out_b]
    if moe:
        out_specs = [pl.BlockSpec((tm, D), row), pl.BlockSpec((tm * TOKEN_TILE[0], LANES), row)]
        out_shape = [jax.ShapeDtypeStruct((T, D), F32), jax.ShapeDtypeStruct((T * TOKEN_TILE[0], LANES), F32)]
    else:
        out_specs = [pl.BlockSpec((tm, D), row), pl.BlockSpec((tm, D), row)]
        out_shape = [jax.ShapeDtypeStruct((T, D), F32), jax.ShapeDtypeStruct((T, D), BF16)]
    if moe:
        in_specs += [pl.BlockSpec((D, LANES), full), pl.BlockSpec((1, LANES), full)]
        args += list(router)
        out_specs.append(pl.BlockSpec((tm, LANES), row))
        out_shape.append(jax.ShapeDtypeStruct((T, LANES), F32))
    return pl.pallas_call(
        functools.partial(_outproj_kernel, moe=moe),
        grid=(B, nS),
        in_specs=in_specs,
        out_specs=out_specs,
        out_shape=out_shape,
        compiler_params=_cparams(("arbitrary", "arbitrary")),
        name="outproj",
    )(*args)


def _ffn_kernel(x_ref, h_ref, mod_ref, wg_ref, wu_ref, wd_ref, o_ref, acc_ref):
    f = pl.program_id(1)

    @pl.when(f == 0)
    def _():
        acc_ref[...] = jnp.zeros_like(acc_ref)

    hb = h_ref[...]
    gate = jnp.dot(hb, wg_ref[0], preferred_element_type=F32)
    up = jnp.dot(hb, wu_ref[0], preferred_element_type=F32)
    mid = (_silu(gate) * up).astype(BF16)
    acc_ref[...] += jnp.dot(mid, wd_ref[0], preferred_element_type=F32)

    @pl.when(f == pl.num_programs(1) - 1)
    def _():
        o_ref[...] = x_ref[...] + mod_ref[0, 5:6, :] * acc_ref[...]


def _ffn_dense(x1, h, mod, wg, wu, wd, B, S, l, li):
    T, D = x1.shape
    F = wg.shape[-1]
    tm = _tile(S, FFN_ROW_TILE)
    tf = _tile(F, FFN_COL_TILE)
    per_seq = S // tm
    return pl.pallas_call(
        _ffn_kernel,
        grid=(T // tm, F // tf),
        in_specs=[pl.BlockSpec((tm, D), lambda i, f: (i, 0)),
                  pl.BlockSpec((tm, D), lambda i, f: (i, 0)),
                  pl.BlockSpec((1, 6, D), lambda i, f: (l * B + i // per_seq, 0, 0)),
                  pl.BlockSpec((1, D, tf), lambda i, f: (li, 0, f)),
                  pl.BlockSpec((1, D, tf), lambda i, f: (li, 0, f)),
                  pl.BlockSpec((1, tf, D), lambda i, f: (li, f, 0))],
        out_specs=pl.BlockSpec((tm, D), lambda i, f: (i, 0)),
        out_shape=jax.ShapeDtypeStruct((T, D), F32),
        scratch_shapes=[pltpu.VMEM((tm, D), F32)],
        compiler_params=_cparams(("arbitrary", "arbitrary")),
        name="ffn_dense",
    )(x1, h, mod, wg, wu, wd)


def _experts_kernel(te_ref, nu_ref, tok_ref, h_ref, wg_ref, wu_ref, wd_ref, y_ref,
                    land_ref, xb_ref, acc_ref, sem):
    i = pl.program_id(0)
    f = pl.program_id(1)
    tm = xb_ref.shape[0]
    rows_per_token = TOKEN_TILE[0]
    n_used = nu_ref[0]

    def start_gather(tile):
        base = tile * tm

        def body(j, _):
            src = pl.multiple_of(tok_ref[base + j] * rows_per_token, rows_per_token)
            dst = pl.multiple_of(j * rows_per_token, rows_per_token)
            pltpu.make_async_copy(h_ref.at[pl.ds(src, rows_per_token)],
                                  land_ref.at[pl.ds(dst, rows_per_token)], sem).start()
            return 0

        lax.fori_loop(0, tm, body, 0, unroll=DMA_ISSUE_UNROLL)

    @pl.when(i < n_used)
    def _():
        @pl.when(f == 0)
        def _():
            @pl.when(i == 0)
            def _():
                start_gather(0)

            pltpu.make_async_copy(h_ref.at[pl.ds(0, tm * rows_per_token)], land_ref, sem).wait()
            xb_ref[...] = _load_token_tiles(land_ref, tm).astype(BF16)
            acc_ref[...] = jnp.zeros_like(acc_ref)

            @pl.when(i + 1 < n_used)
            def _():
                start_gather(i + 1)

        xb = xb_ref[...]
        gate = jnp.dot(xb, wg_ref[0, 0].astype(BF16), preferred_element_type=F32)
        up = jnp.dot(xb, wu_ref[0, 0].astype(BF16), preferred_element_type=F32)
        mid = (_silu(gate) * up).astype(BF16)
        acc_ref[...] += jnp.dot(mid, wd_ref[0, 0].astype(BF16), preferred_element_type=F32)

        @pl.when(f == pl.num_programs(1) - 1)
        def _():
            _store_token_tiles(y_ref, acc_ref[...])

    @pl.when((i >= n_used) & (f == 0))
    def _():
        y_ref[...] = jnp.zeros_like(y_ref)


def _experts(h, row_tok, tile_e, n_used, wg, wu, wd, li, tm):
    n_rows = row_tok.shape[0]
    D = wg.shape[-2]
    rpt = TOKEN_TILE[0]
    assert h.shape[0] >= tm * rpt
    F = wg.shape[-1]
    tf = _tile(F, EXPERT_COL_TILE)
    nf = F // tf

    def rows(i, f, te, nu, tok):
        return (i, 0)

    def col_block(i, f, nu):
        t = jnp.minimum(i, nu[0] - 1)
        step = jnp.where(i < nu[0], f, nf - 1)
        return jnp.where(t % 2 == 0, step, nf - 1 - step)

    def wcol(i, f, te, nu, tok):
        return (li, te[jnp.minimum(i, nu[0] - 1)], 0, col_block(i, f, nu))

    def wrow(i, f, te, nu, tok):
        return (li, te[jnp.minimum(i, nu[0] - 1)], col_block(i, f, nu), 0)

    return pl.pallas_call(
        _experts_kernel,
        grid_spec=pltpu.PrefetchScalarGridSpec(
            num_scalar_prefetch=3,
            grid=(n_rows // tm, nf),
            in_specs=[pl.BlockSpec(memory_space=pl.ANY),
                      pl.BlockSpec((1, 1, D, tf), wcol),
                      pl.BlockSpec((1, 1, D, tf), wcol),
                      pl.BlockSpec((1, 1, tf, D), wrow)],
            out_specs=pl.BlockSpec((tm * rpt, LANES), rows),
            scratch_shapes=[pltpu.VMEM((tm * rpt, LANES), F32), pltpu.VMEM((tm, D), BF16),
                            pltpu.VMEM((tm, D), F32), pltpu.SemaphoreType.DMA(())]),
        out_shape=jax.ShapeDtypeStruct((n_rows * rpt, LANES), F32),
        compiler_params=_cparams(("arbitrary", "arbitrary")),
        name="moe_experts",
    )(tile_e, n_used, row_tok, h, wg, wu, wd)


def _combine_kernel(pos_ref, y_ref, x_ref, info_ref, mod_ref, fg_ref, o_ref, buf, sem, *, tc, final):
    i = pl.program_id(0)
    slot = i % 2
    rows_per_token = TOKEN_TILE[0]

    def start_gather(tile, dst_slot):
        base = tile * (2 * tc)

        def body(g, _):
            for u in range(DMA_ISSUE_UNROLL):
                j = g * DMA_ISSUE_UNROLL + u
                src = pl.multiple_of(pos_ref[base + j] * rows_per_token, rows_per_token)
                dst = pl.multiple_of(j * rows_per_token, rows_per_token)
                pltpu.make_async_copy(y_ref.at[pl.ds(src, rows_per_token)],
                                      buf.at[dst_slot, pl.ds(dst, rows_per_token)],
                                      sem.at[dst_slot]).start(priority=u % 2)
            return 0

        lax.fori_loop(0, 2 * tc // DMA_ISSUE_UNROLL, body, 0)

    @pl.when(i == 0)
    def _():
        start_gather(0, 0)

    @pl.when(i + 1 < pl.num_programs(0))
    def _():
        start_gather(i + 1, 1 - slot)

    pltpu.make_async_copy(y_ref.at[pl.ds(0, 2 * tc * rows_per_token)], buf.at[slot], sem.at[slot]).wait()
    info = info_ref[...]
    ff = (info[:, 2:3] * _load_token_tiles(buf.at[slot], tc)
          + info[:, 3:4] * _load_token_tiles(buf.at[slot], tc, tc * rows_per_token))
    out = x_ref[...] + mod_ref[0, 5:6, :] * ff
    if final:
        ms = jnp.mean(out * out, axis=-1, keepdims=True)
        out = out * lax.rsqrt(ms + RMS_EPS) * fg_ref[...]
    o_ref[...] = out


def _combine(pos, y_rows, x1, info, mod, final_g, B, S, l, final):
    T, D = x1.shape
    tc = _tile(S, COMBINE_ROW_TILE)
    per_seq = S // tc
    return pl.pallas_call(
        functools.partial(_combine_kernel, tc=tc, final=final),
        grid_spec=pltpu.PrefetchScalarGridSpec(
            num_scalar_prefetch=1,
            grid=(T // tc,),
            in_specs=[pl.BlockSpec(memory_space=pl.ANY),
                      pl.BlockSpec((tc, D), lambda i, pos: (i, 0)),
                      pl.BlockSpec((tc, LANES), lambda i, pos: (i, 0)),
                      pl.BlockSpec((1, 6, D), lambda i, pos: (l * B + i // per_seq, 0, 0)),
                      pl.BlockSpec((1, D), lambda i, pos: (0, 0))],
            out_specs=pl.BlockSpec((tc, D), lambda i, pos: (i, 0)),
            scratch_shapes=[pltpu.VMEM((2, 2 * tc * TOKEN_TILE[0], LANES), F32), pltpu.SemaphoreType.DMA((2,))]),
        out_shape=jax.ShapeDtypeStruct((T, D), F32),
        compiler_params=_cparams(("arbitrary",)),
        name="moe_combine",
    )(pos, y_rows, x1, info, mod, final_g)


def _moe(x1, h, info, mod, wg, wu, wd, final_g, B, S, l, li, final):
    T, D = x1.shape
    TK = T * TOP_K
    tm = _tile(TK // N_EXPERTS, EXPERT_ROW_TILE)
    n_rows = TK + (N_EXPERTS - 1) * tm
    flat_e = info[:, :TOP_K].astype(jnp.int32).reshape(TK)
    onehot = (flat_e[:, None] == jnp.arange(N_EXPERTS, dtype=jnp.int32)[None, :]).astype(jnp.int32)
    csum = jnp.cumsum(onehot, axis=0)
    counts = csum[-1]
    rank = jnp.sum(onehot * csum, axis=1) - 1
    padded = (counts + tm - 1) // tm * tm
    pad_ends = jnp.cumsum(padded)
    pad_starts = pad_ends - padded
    dest = pad_starts[flat_e] + rank
    row_tok = jnp.zeros((n_rows,), jnp.int32).at[dest].set(jnp.arange(TK, dtype=jnp.int32) // TOP_K)
    tile_start = jnp.arange(n_rows // tm, dtype=jnp.int32) * tm
    tile_e = jnp.minimum(jnp.searchsorted(pad_ends, tile_start, side='right'), N_EXPERTS - 1).astype(jnp.int32)
    n_used_rows = pad_ends[-1:].astype(jnp.int32)
    n_used_tiles = n_used_rows // tm
    tc = _tile(S, COMBINE_ROW_TILE)
    pos = dest.reshape(T // tc, tc, TOP_K).transpose(0, 2, 1).reshape(TK).astype(jnp.int32)

    y_rows = _experts(h, row_tok, tile_e, n_used_tiles, wg, wu, wd, li, tm)
    return _combine(pos, y_rows, x1, info, mod, final_g, B, S, l, final)


def kernel(x, c, w_ada, b_ada, norm1_g, norm2_g, final_g, w_in, w_out, rel_bias, rwkv_mu, rwkv_w0, rwkv_w2, rwkv_a0, rwkv_a2, rwkv_g2, rwkv_k_k, rwkv_k_a, rwkv_r_k, rwkv_ln_w, rwkv_ln_b, rwkv_v0, rwkv_v1, rwkv_v2, ffn_w_gate, ffn_w_up, ffn_w_down, moe_router_w, moe_router_b, moe_w_gate, moe_w_up, moe_w_down):
    B, S, D = x.shape
    L = w_in.shape[0]
    T = B * S
    W = RWKV_WIDTH

    mod = _ada_mod(c, w_ada, b_ada).reshape(L * B, 6, D)
    w_in_b = jnp.pad(w_in, ((0, 0), (0, 0), (0, N_IN_PAD - N_IN))).astype(BF16)
    w_out_b = w_out.astype(BF16)
    ffn_w = [w.astype(BF16) for w in (ffn_w_gate, ffn_w_up, ffn_w_down)]
    moe_w = [moe_w_gate, moe_w_up, moe_w_down]
    mu_p = jnp.pad(rwkv_mu, ((0, 0), (0, RWM_COLS + RWL_COLS - rwkv_mu.shape[1])))
    att_tables = _att_tables(rel_bias, S)
    zeros_w = jnp.zeros((1, W), F32)
    final_g2 = final_g.reshape(1, D)

    xf = x.reshape(T, D)
    v_first = None
    for l in range(L):
        qkv, qkv_mid, qkv_far, rwm, rwl = _inproj(xf, mod, norm1_g[l].reshape(1, D), w_in_b,
                                                   mu_p[l].reshape(1, -1), B, S, l)
        att = _attention(qkv, qkv_mid, qkv_far, att_tables, B, S)
        v0 = rwkv_v0[l - 1].reshape(1, W) if l > 0 else zeros_w
        vec = jnp.concatenate([rwkv_w0[l].reshape(1, W), rwkv_a0[l].reshape(1, W), rwkv_k_k[l].reshape(1, W),
                               rwkv_k_a[l].reshape(1, W), rwkv_r_k[l].reshape(1, W), rwkv_ln_w[l].reshape(1, W),
                               rwkv_ln_b[l].reshape(1, W), v0], axis=0)
        w2p = jnp.pad(rwkv_w2[l], ((0, LANES - D_DECAY_LORA), (0, 0)))
        a2p = jnp.pad(rwkv_a2[l], ((D_DECAY_LORA, 0), (0, 0)))
        g2p = jnp.pad(rwkv_g2[l], ((0, RWL_COLS - LANES - D_GATE_LORA), (0, 0)))
        if l > 0:
            v1p = jnp.pad(rwkv_v1[l - 1], ((0, 0), (0, LANES - D_MV_LORA)))
            v2p = jnp.pad(rwkv_v2[l - 1], ((0, LANES - D_MV_LORA), (0, 0)))
        else:
            v1p = v2p = None
        rw, v_first = _rwkv(rwm, rwl, v_first, vec, w2p, a2p, g2p, v1p, v2p, B, S)
        li = l // 2
        if l % 2 == 0:
            x1, h = _outproj(xf, att, rw, mod, norm2_g[l].reshape(1, D), w_out_b, B, S, l)
            xf = _ffn_dense(x1, h, mod, *ffn_w, B, S, l, li)
        else:
            router = (jnp.pad(moe_router_w[li], ((0, 0), (0, LANES - N_EXPERTS))),
                      jnp.pad(moe_router_b[li], (0, LANES - N_EXPERTS)).reshape(1, LANES))
            x1, h, info = _outproj(xf, att, rw, mod, norm2_g[l].reshape(1, D), w_out_b, B, S, l, router)
            xf = _moe(x1, h, info, mod, *moe_w, final_g2, B, S, l, li,
                      final=(l == L - 1))
    if L % 2 == 1:
        raise NotImplementedError("final norm is fused into the last (expert) layer")
    return xf.reshape(B, S, D)
```
